```python
import jax
import jax.numpy as jnp
from jax import lax
import numpy as np

D_MODEL = 1024
BATCH = 1
SEQ = 16384
DEPTH = 2

GRID_W = 64
CTX_LEN = 256
N_MIXERS = 2
N_MOD = 6
EPS = 1e-6

D_RNN = 1280
RG_BLOCKS = 10
RG_BLOCK_W = D_RNN // RG_BLOCKS
CONV_W = 4
CONV_PAD_L = 2
CONV_PAD_R = CONV_W - 1 - CONV_PAD_L
RG_C = 8.0
RG_A_MIN = 0.9
RG_A_MAX = 0.999

ML_HEADS = 8
ML_DQK = D_MODEL // (2 * ML_HEADS)
ML_DV = D_MODEL // ML_HEADS
ML_CHUNK = 64
ML_QK_W = ML_HEADS * ML_DQK
ML_V_W = ML_HEADS * ML_DV
ML_IN_W = 2 * ML_QK_W + ML_V_W + D_MODEL + 4 * ML_HEADS
ML_FGATE_BIAS_LO = 3.0
ML_FGATE_BIAS_HI = 6.0

N_GROUPS = 4
EXPERTS_PER_GROUP = 8
N_EXPERTS = N_GROUPS * EXPERTS_PER_GROUP
TOP_K = 2
D_EXPERT = 512
MOE_BLOCK = 128

kernel_name = 'bidir_rglru_mlstm_hier_moe_prefix_dit'


def rms_norm(x, g):
    xf = x.astype(jnp.float32)
    y = xf * lax.rsqrt(jnp.mean(xf * xf, axis=-1, keepdims=True) + EPS)
    return (y * g.astype(jnp.float32)).astype(x.dtype)


def modulate(h, shift, scale):
    return h * (1.0 + scale) + shift


def adaln(cond, w, b):
    m = jax.nn.silu(cond) @ w + b
    return jnp.split(m, N_MOD, axis=-1)


def to_colmajor(t, rows):
    b, n, d = t.shape
    return t.reshape(b, rows, GRID_W, d).transpose(0, 2, 1, 3).reshape(b, n, d)


def from_colmajor(t, rows):
    b, n, d = t.shape
    return t.reshape(b, GRID_W, rows, d).transpose(0, 2, 1, 3).reshape(b, n, d)


def centred_dwconv(u, w, bias):
    n = u.shape[1]
    up = jnp.pad(u, ((0, 0), (CONV_PAD_L, CONV_PAD_R), (0, 0)))
    out = bias + up[:, 0:n] * w[0]
    for j in range(1, CONV_W):
        out = out + up[:, j:j + n] * w[j]
    return out


def rglru_coeffs(u, w_a, b_a, w_x, b_x, lam):
    b, n, _ = u.shape
    uf = u.astype(jnp.float32)
    ub = uf.reshape(b, n, RG_BLOCKS, RG_BLOCK_W)
    r = jax.nn.sigmoid(jnp.einsum('bnhi,hij->bnhj', ub, w_a.astype(jnp.float32)).reshape(b, n, D_RNN) + b_a.astype(jnp.float32))
    ig = jax.nn.sigmoid(jnp.einsum('bnhi,hij->bnhj', ub, w_x.astype(jnp.float32)).reshape(b, n, D_RNN) + b_x.astype(jnp.float32))
    log_a = -RG_C * r * jax.nn.softplus(-lam.astype(jnp.float32))
    a = jnp.exp(log_a)
    return a, jnp.sqrt(-jnp.expm1(2.0 * log_a)) * ig * uf


def linear_scan(a, b, h0, reverse):
    def combine(left, right):
        a_l, b_l = left
        a_r, b_r = right
        return a_l * a_r, a_r * b_l + b_r
    a_cum, b_cum = lax.associative_scan(combine, (a, b), reverse=reverse, axis=1)
    return b_cum + a_cum * h0[:, None, :]


def rglru_mixer(h, hc, w_in, conv_w, conv_b, w_a, b_a, w_x, b_x, lam, w_out, ctx_out):
    px = h @ w_in
    pc = hc @ w_in
    ux = centred_dwconv(px[..., D_RNN:], conv_w, conv_b)
    uc = centred_dwconv(pc[..., D_RNN:], conv_w, conv_b)
    zero = jnp.zeros((h.shape[0], D_RNN), jnp.float32)
    hx_dirs = []
    hc_dirs = []
    for d in range(2):
        rev = d == 1
        a_c, b_c = rglru_coeffs(uc, w_a[d], b_a[d], w_x[d], b_x[d], lam[d])
        hcd = linear_scan(a_c, b_c, zero, rev)
        h0 = hcd[:, 0] if rev else hcd[:, -1]
        a_l, b_l = rglru_coeffs(ux, w_a[d], b_a[d], w_x[d], b_x[d], lam[d])
        hx_dirs.append(linear_scan(a_l, b_l, h0, rev))
        hc_dirs.append(hcd)
    y_x = (jax.nn.gelu(px[..., :D_RNN]) * (hx_dirs[0] + hx_dirs[1]).astype(h.dtype)) @ w_out
    y_c = None
    if ctx_out:
        y_c = (jax.nn.gelu(pc[..., :D_RNN]) * (hc_dirs[0] + hc_dirs[1]).astype(h.dtype)) @ w_out
    return y_x, y_c


def mlstm_chunkwise(q, k, v, log_i, log_f, state):
    b, nh, n, _ = q.shape
    dv = v.shape[-1]
    nc = n // ML_CHUNK

    def chunks(t):
        return jnp.moveaxis(t.reshape((b, nh, nc, ML_CHUNK) + t.shape[3:]), 2, 0)

    lower = jnp.tril(jnp.ones((ML_CHUNK, ML_CHUNK), dtype=bool))

    def step(carry, inp):
        c_mat, n_vec, m = carry
        qc, kc, vc, li, lf = inp
        cum_f = jnp.cumsum(lf, axis=-1)
        log_w = jnp.where(lower, cum_f[..., :, None] - cum_f[..., None, :] + li[..., None, :], -jnp.inf)
        log_inter = cum_f + m[..., None]
        m_t = jnp.maximum(log_inter, jnp.max(log_w, axis=-1))
        w_intra = jnp.exp(log_w - m_t[..., None])
        w_inter = jnp.exp(log_inter - m_t)
        s = jnp.einsum('bhtd,bhsd->bhts', qc, kc) * w_intra
        num = w_inter[..., None] * jnp.einsum('bhtd,bhde->bhte', qc, c_mat) + jnp.einsum('bhts,bhse->bhte', s, vc)
        den = w_inter * jnp.einsum('bhtd,bhd->bht', qc, n_vec) + jnp.sum(s, axis=-1)
        h_out = num / jnp.maximum(jnp.abs(den), jnp.exp(-m_t))[..., None]
        m_new = m_t[..., -1]
        w_state = jnp.exp(cum_f[..., -1:] - cum_f + li - m_new[..., None])
        decay = jnp.exp(cum_f[..., -1] + m - m_new)
        kw = kc * w_state[..., None]
        c_new = decay[..., None, None] * c_mat + jnp.einsum('bhsd,bhse->bhde', kw, vc)
        n_new = decay[..., None] * n_vec + jnp.sum(kw, axis=2)
        return (c_new, n_new, m_new), h_out

    final, hs = lax.scan(step, state, (chunks(q), chunks(k), chunks(v), chunks(log_i), chunks(log_f)))
    return jnp.moveaxis(hs, 0, 2).reshape(b, nh, n, dv), final


def mlstm_mixer(h, hc, w_in, b_gates, norm_g, w_out, ctx_out):
    rows = h.shape[1] // GRID_W

    def project(t):
        p = (t @ w_in).astype(jnp.float32)
        b, n, _ = p.shape

        def heads(cols, dh):
            return cols.reshape(b, n, ML_HEADS, dh).transpose(0, 2, 1, 3)

        q = heads(p[..., :ML_QK_W], ML_DQK) * (ML_DQK ** -0.5)
        k = heads(p[..., ML_QK_W:2 * ML_QK_W], ML_DQK)
        v = heads(p[..., 2 * ML_QK_W:2 * ML_QK_W + ML_V_W], ML_DV)
        o = p[..., 2 * ML_QK_W + ML_V_W:2 * ML_QK_W + ML_V_W + D_MODEL]
        g = (p[..., ML_IN_W - 4 * ML_HEADS:] + b_gates.astype(jnp.float32)).reshape(b, n, 4, ML_HEADS).transpose(2, 0, 3, 1)
        return q, k, v, o, g

    qx, kx, vx, ox, gx = project(to_colmajor(h, rows))
    qc, kc, vc, oc, gc = project(hc)
    b = h.shape[0]
    zero = (jnp.zeros((b, ML_HEADS, ML_DQK, ML_DV), jnp.float32),
            jnp.zeros((b, ML_HEADS, ML_DQK), jnp.float32),
            jnp.zeros((b, ML_HEADS), jnp.float32))

    def flip(t):
        return jnp.flip(t, axis=2)

    hc_f, st_f = mlstm_chunkwise(qc, kc, vc, gc[0], jax.nn.log_sigmoid(gc[1]), zero)
    hx_f, _ = mlstm_chunkwise(qx, kx, vx, gx[0], jax.nn.log_sigmoid(gx[1]), st_f)
    hc_b, st_b = mlstm_chunkwise(flip(qc), flip(kc), flip(vc), flip(gc[2]), flip(jax.nn.log_sigmoid(gc[3])), zero)
    hx_b, _ = mlstm_chunkwise(flip(qx), flip(kx), flip(vx), flip(gx[2]), flip(jax.nn.log_sigmoid(gx[3])), st_b)

    def finish(h_f, h_b, o):
        hs = (h_f + h_b).transpose(0, 2, 1, 3)
        hn = hs * lax.rsqrt(jnp.mean(hs * hs, axis=-1, keepdims=True) + EPS) * norm_g.astype(jnp.float32).reshape(ML_HEADS, ML_DV)
        bb, n = hn.shape[:2]
        return (hn.reshape(bb, n, ML_V_W) * jax.nn.sigmoid(o)).astype(h.dtype) @ w_out

    y_x = from_colmajor(finish(hx_f, flip(hx_b), ox), rows)
    y_c = finish(hc_f, flip(hc_b), oc) if ctx_out else None
    return y_x, y_c


def hier_moe(h, w_grp, b_grp, w_exp, b_exp, w_gate, w_up, w_down):
    t_count, d = h.shape
    hf = h.astype(jnp.float32)
    grp_logits = hf @ w_grp.astype(jnp.float32) + b_grp.astype(jnp.float32)
    p_grp = jax.nn.softmax(grp_logits, axis=-1)
    _, grp = lax.top_k(grp_logits, 1)
    p_sel = jnp.take_along_axis(p_grp, grp, axis=-1)
    exp_logits = (hf @ w_exp.astype(jnp.float32) + b_exp.astype(jnp.float32)).reshape(t_count, N_GROUPS, EXPERTS_PER_GROUP)
    in_grp = jnp.take_along_axis(exp_logits, grp[:, :, None], axis=1)[:, 0]
    top_v, top_i = lax.top_k(in_grp, TOP_K)
    wts = jax.nn.softmax(top_v, axis=-1) * p_sel
    eid = grp * EXPERTS_PER_GROUP + top_i

    n_assign = t_count * TOP_K
    e_flat = eid.reshape(-1)
    order = jnp.argsort(e_flat)
    e_s = e_flat[order]
    tok_s = order // TOP_K
    w_s = wts.reshape(-1)[order]
    counts = jnp.bincount(e_flat, length=N_EXPERTS)
    padded = (counts + MOE_BLOCK - 1) // MOE_BLOCK * MOE_BLOCK
    pad_end = jnp.cumsum(padded)
    pad_start = pad_end - padded
    raw_start = jnp.cumsum(counts) - counts
    dest = pad_start[e_s] + jnp.arange(n_assign) - raw_start[e_s]
    n_rows = n_assign + N_EXPERTS * MOE_BLOCK
    n_blocks = n_rows // MOE_BLOCK
    buf = jnp.zeros((n_rows, d), h.dtype).at[dest].set(h[tok_s])
    blk_e = jnp.clip(jnp.searchsorted(pad_end, jnp.arange(n_blocks) * MOE_BLOCK, side='right'), 0, N_EXPERTS - 1)

    def expert_block(args):
        xb, e = args
        return (jax.nn.silu(xb @ w_gate[e]) * (xb @ w_up[e])) @ w_down[e]

    y_buf = lax.map(expert_block, (buf.reshape(n_blocks, MOE_BLOCK, d), blk_e))
    y_s = y_buf.reshape(n_rows, d)[dest]
    out = jax.ops.segment_sum(y_s.astype(jnp.float32) * w_s[:, None], tok_s, num_segments=t_count)
    return out.astype(h.dtype)


def setup_inputs(seed: int = 0) -> dict:
    key = jax.random.key(seed)
    keys = iter(jax.random.split(key, 96))

    def nrm(shape, scale):
        return jax.random.normal(next(keys), shape, jnp.float32) * scale

    d = D_MODEL
    inputs = {
        'x': nrm((BATCH, SEQ, d), 1.0),
        'c': nrm((BATCH, d), 1.0),
        'ctx': nrm((BATCH, CTX_LEN, d), 1.0),
        'c_ctx': nrm((d,), 1.0),
    }
    for i in range(DEPTH):
        p = 'l%d_' % i
        inputs[p + 'ada_w'] = nrm((d, N_MOD * d), 0.5 * d ** -0.5)
        inputs[p + 'ada_b'] = nrm((N_MOD * d,), 0.02)
        inputs[p + 'norm1_g'] = 1.0 + nrm((d,), 0.02)
        inputs[p + 'norm2_g'] = 1.0 + nrm((d,), 0.02)
        if i % N_MIXERS == 0:
            inputs[p + 'rg_w_in'] = nrm((d, 2 * D_RNN), d ** -0.5)
            inputs[p + 'rg_conv_w'] = nrm((CONV_W, D_RNN), CONV_W ** -0.5)
            inputs[p + 'rg_conv_b'] = nrm((D_RNN,), 0.02)
            inputs[p + 'rg_w_a'] = nrm((2, RG_BLOCKS, RG_BLOCK_W, RG_BLOCK_W), RG_BLOCK_W ** -0.5)
            inputs[p + 'rg_b_a'] = nrm((2, D_RNN), 0.02)
            inputs[p + 'rg_w_x'] = nrm((2, RG_BLOCKS, RG_BLOCK_W, RG_BLOCK_W), RG_BLOCK_W ** -0.5)
            inputs[p + 'rg_b_x'] = nrm((2, D_RNN), 0.02)
            a0 = jax.random.uniform(next(keys), (2, D_RNN), jnp.float32, RG_A_MIN ** (1.0 / RG_C), RG_A_MAX ** (1.0 / RG_C))
            inputs[p + 'rg_lambda'] = jnp.log(a0) - jnp.log1p(-a0)
            inputs[p + 'rg_w_out'] = nrm((D_RNN, d), D_RNN ** -0.5)
        else:
            inputs[p + 'ml_w_in'] = nrm((d, ML_IN_W), d ** -0.5)
            fb = jnp.linspace(ML_FGATE_BIAS_LO, ML_FGATE_BIAS_HI, ML_HEADS, dtype=jnp.float32)
            inputs[p + 'ml_b_gates'] = jnp.concatenate([nrm((ML_HEADS,), 0.1), fb + nrm((ML_HEADS,), 0.1),
                                                        nrm((ML_HEADS,), 0.1), fb + nrm((ML_HEADS,), 0.1)])
            inputs[p + 'ml_norm_g'] = 1.0 + nrm((ML_V_W,), 0.02)
            inputs[p + 'ml_w_out'] = nrm((ML_V_W, d), ML_V_W ** -0.5)
        inputs[p + 'moe_w_grp'] = nrm((d, N_GROUPS), d ** -0.5)
        inputs[p + 'moe_b_grp'] = nrm((N_GROUPS,), 0.01)
        inputs[p + 'moe_w_exp'] = nrm((d, N_EXPERTS), d ** -0.5)
        inputs[p + 'moe_b_exp'] = nrm((N_EXPERTS,), 0.01)
        inputs[p + 'moe_w_gate'] = nrm((N_EXPERTS, d, D_EXPERT), d ** -0.5)
        inputs[p + 'moe_w_up'] = nrm((N_EXPERTS, d, D_EXPERT), d ** -0.5)
        inputs[p + 'moe_w_down'] = nrm((N_EXPERTS, D_EXPERT, d), D_EXPERT ** -0.5)
    inputs['final_norm_g'] = 1.0 + nrm((d,), 0.02)
    return inputs


def reference(x, c, ctx, c_ctx,
              l0_ada_w, l0_ada_b, l0_norm1_g, l0_norm2_g,
              l0_rg_w_in, l0_rg_conv_w, l0_rg_conv_b, l0_rg_w_a, l0_rg_b_a, l0_rg_w_x, l0_rg_b_x, l0_rg_lambda, l0_rg_w_out,
              l0_moe_w_grp, l0_moe_b_grp, l0_moe_w_exp, l0_moe_b_exp, l0_moe_w_gate, l0_moe_w_up, l0_moe_w_down,
              l1_ada_w, l1_ada_b, l1_norm1_g, l1_norm2_g,
              l1_ml_w_in, l1_ml_b_gates, l1_ml_norm_g, l1_ml_w_out,
              l1_moe_w_grp, l1_moe_b_grp, l1_moe_w_exp, l1_moe_b_exp, l1_moe_w_gate, l1_moe_w_up, l1_moe_w_down,
              final_norm_g):
    layers = (
        (l0_ada_w, l0_ada_b, l0_norm1_g, l0_norm2_g,
         (l0_rg_w_in, l0_rg_conv_w, l0_rg_conv_b, l0_rg_w_a, l0_rg_b_a, l0_rg_w_x, l0_rg_b_x, l0_rg_lambda, l0_rg_w_out),
         (l0_moe_w_grp, l0_moe_b_grp, l0_moe_w_exp, l0_moe_b_exp, l0_moe_w_gate, l0_moe_w_up, l0_moe_w_down)),
        (l1_ada_w, l1_ada_b, l1_norm1_g, l1_norm2_g,
         (l1_ml_w_in, l1_ml_b_gates, l1_ml_norm_g, l1_ml_w_out),
         (l1_moe_w_grp, l1_moe_b_grp, l1_moe_w_exp, l1_moe_b_exp, l1_moe_w_gate, l1_moe_w_up, l1_moe_w_down)),
    )
    b, n, d = x.shape
    for i in range(DEPTH):
        ada_w, ada_b, g1, g2, mix_p, moe_p = layers[i]
        need_ctx = i < DEPTH - 1
        sh1, sc1, gt1, sh2, sc2, gt2 = [m[:, None, :] for m in adaln(c, ada_w, ada_b)]
        csh1, csc1, cgt1, csh2, csc2, cgt2 = adaln(c_ctx, ada_w, ada_b)
        hx = modulate(rms_norm(x, g1), sh1, sc1)
        hc = modulate(rms_norm(ctx, g1), csh1, csc1)
        if i % N_MIXERS == 0:
            yx, yc = rglru_mixer(hx, hc, *mix_p, ctx_out=need_ctx)
        else:
            yx, yc = mlstm_mixer(hx, hc, *mix_p, ctx_out=need_ctx)
        x = x + gt1 * yx
        h2 = modulate(rms_norm(x, g2), sh2, sc2)
        if need_ctx:
            ctx = ctx + cgt1 * yc
            hc2 = modulate(rms_norm(ctx, g2), csh2, csc2)
            tokens = jnp.concatenate([h2.reshape(b * n, d), hc2.reshape(-1, d)], axis=0)
            f = hier_moe(tokens, *moe_p)
            x = x + gt2 * f[:b * n].reshape(b, n, d)
            ctx = ctx + cgt2 * f[b * n:].reshape(b, -1, d)
        else:
            x = x + gt2 * hier_moe(h2.reshape(b * n, d), *moe_p).reshape(b, n, d)
    return rms_norm(x, final_norm_g)
```

```python
import functools

import jax
import jax.numpy as jnp
from jax import lax
from jax.experimental import pallas as pl
from jax.experimental.pallas import tpu as pltpu

D_MODEL = 1024
GRID_W = 64
N_MOD = 6
EPS = 1e-6

D_RNN = 1280
RG_BLOCKS = 10
RG_BLOCK_W = D_RNN // RG_BLOCKS
CONV_W = 4
CONV_PAD_L = 2
RG_C = 8.0

ML_HEADS = 8
ML_DQK = D_MODEL // (2 * ML_HEADS)
ML_DV = D_MODEL // ML_HEADS
ML_QK_W = ML_HEADS * ML_DQK
ML_V_W = ML_HEADS * ML_DV

N_GROUPS = 4
EXPERTS_PER_GROUP = 8
N_EXPERTS = N_GROUPS * EXPERTS_PER_GROUP
D_EXPERT = 512

TM = 256
TB = 256
SUBLANES = 8
LANES = 128
VMEM_LIMIT = 48 * 1024 * 1024

F32 = jnp.float32
BF16 = jnp.bfloat16
HI = lax.Precision.HIGHEST
NEG_INF = float("-inf")


def _params(n_axes=1):
    return pltpu.CompilerParams(dimension_semantics=("arbitrary",) * n_axes,
                                vmem_limit_bytes=VMEM_LIMIT)


def _rms(x, g):
    return x * lax.rsqrt(jnp.mean(x * x, axis=-1, keepdims=True) + EPS) * g


def _sigmoid(x):
    return 1.0 / (1.0 + jnp.exp(-x))


def _softplus(x):
    return jnp.maximum(x, 0.0) + jnp.log1p(jnp.exp(-jnp.abs(x)))


def _gelu_tanh(x):
    return 0.5 * x * (1.0 + jnp.tanh(0.7978845608028654 * (x + 0.044715 * (x * x * x))))


def _full(shape):
    return pl.BlockSpec(shape, lambda *_: (0,) * len(shape))


def _adaln_kernel(cond_ref, w_ref, b_ref, o_ref):
    c = cond_ref[...]
    s = c * _sigmoid(c)
    o_ref[...] = jnp.dot(s, w_ref[...], precision=HI, preferred_element_type=F32) + b_ref[...]


def _adaln(cond8, w, b):
    d = w.shape[0]
    return pl.pallas_call(
        _adaln_kernel,
        grid=(N_MOD,),
        in_specs=[_full((SUBLANES, d)),
                  pl.BlockSpec((d, d), lambda j: (0, j)),
                  pl.BlockSpec((1, d), lambda j: (0, j))],
        out_specs=pl.BlockSpec((SUBLANES, d), lambda j: (0, j)),
        out_shape=jax.ShapeDtypeStruct((SUBLANES, N_MOD * d), F32),
        compiler_params=_params(),
        name="adaln",
    )(cond8, w, b.reshape(1, -1))


def _mod_row(ref, is_ctx):
    return jnp.where(is_ctx, ref[1:2, :], ref[0:1, :])


def _inproj0_kernel(x_ref, ctx_ref, g_ref, sh_ref, sc_ref, w_ref, gate_ref, rec_ref):
    is_ctx = pl.program_id(0) == 0
    xin = jnp.where(is_ctx, ctx_ref[...], x_ref[...])
    h = _rms(xin, g_ref[...]) * (1.0 + _mod_row(sc_ref, is_ctx)) + _mod_row(sh_ref, is_ctx)
    p = jnp.dot(h.astype(BF16), w_ref[...], preferred_element_type=F32)
    gate_ref[...] = p[:, :D_RNN]
    rec_ref[...] = p[:, D_RNN:]


def _inproj0(x, ctx, g1, mod, w_in_bf):
    n, d = x.shape
    nt = n // TM + 1
    t = nt * TM
    return pl.pallas_call(
        _inproj0_kernel,
        grid=(nt,),
        in_specs=[pl.BlockSpec((TM, d), lambda i: (jnp.maximum(i - 1, 0), 0)),
                  _full((TM, d)),
                  _full((1, d)),
                  pl.BlockSpec((SUBLANES, d), lambda i: (0, 0)),
                  pl.BlockSpec((SUBLANES, d), lambda i: (0, 1)),
                  _full((d, 2 * D_RNN))],
        out_specs=[pl.BlockSpec((TM, D_RNN), lambda i: (i, 0)),
                   pl.BlockSpec((TM, D_RNN), lambda i: (i, 0))],
        out_shape=[jax.ShapeDtypeStruct((t, D_RNN), F32)] * 2,
        compiler_params=_params(),
        name="inproj0",
    )(x, ctx, g1.reshape(1, -1), mod, mod, w_in_bf)


def _scan_tile(reverse, s, nt):
    if not reverse:
        return s
    return jnp.where(s == 0, 0, nt - s)


def _rglru_kernel(reverse, nt, main_ref, prev_ref, next_ref, cw_ref, cb_ref, wcat_ref,
                  ba_ref, bx_ref, lam_ref, h_ref, carry_ref):
    s = pl.program_id(0)
    tile = _scan_tile(reverse, s, nt)

    @pl.when(s == 0)
    def _():
        carry_ref[...] = jnp.zeros_like(carry_ref)

    has_prev = tile >= 2
    has_next = jnp.logical_and(tile >= 1, tile <= nt - 2)
    sub = lax.broadcasted_iota(jnp.int32, (TM // SUBLANES, SUBLANES, LANES), 1)

    for j in range(RG_BLOCKS):
        ln = slice(j * RG_BLOCK_W, (j + 1) * RG_BLOCK_W)
        prev = jnp.where(has_prev, prev_ref[:, ln], 0.0)
        nxt = jnp.where(has_next, next_ref[:, ln], 0.0)
        ext = jnp.concatenate([prev, main_ref[:, ln], nxt], axis=0)
        base = SUBLANES - CONV_PAD_L
        u = cb_ref[:, ln] + ext[base:base + TM] * cw_ref[0:1, ln]
        for k in range(1, CONV_W):
            u = u + ext[base + k:base + k + TM] * cw_ref[k:k + 1, ln]

        g = jnp.dot(u.astype(BF16), wcat_ref[j], preferred_element_type=F32)
        r = _sigmoid(g[:, :RG_BLOCK_W] + ba_ref[:, ln])
        ig = _sigmoid(g[:, RG_BLOCK_W:] + bx_ref[:, ln])
        log_a = -RG_C * r * _softplus(-lam_ref[:, ln])
        a = jnp.exp(log_a)
        b = jnp.sqrt(-jnp.tanh(log_a) * (a * a + 1.0)) * ig * u

        a3 = a.reshape(TM // SUBLANES, SUBLANES, LANES)
        b3 = b.reshape(TM // SUBLANES, SUBLANES, LANES)
        for k in (1, 2, 4):
            shift = SUBLANES - k if reverse else k
            keep = (sub < SUBLANES - k) if reverse else (sub >= k)
            a_sh = pltpu.roll(a3, shift, 1)
            b_sh = pltpu.roll(b3, shift, 1)
            b3 = jnp.where(keep, a3 * b_sh + b3, b3)
            a3 = jnp.where(keep, a3 * a_sh, a3)

        h = carry_ref[0:1, ln]
        groups = range(TM // SUBLANES)
        outs = [None] * (TM // SUBLANES)
        for v in (reversed(groups) if reverse else groups):
            hv = b3[v] + a3[v] * h
            outs[v] = hv
            h = hv[0:1, :] if reverse else hv[SUBLANES - 1:SUBLANES, :]
        carry_ref[0:1, ln] = h
        h_ref[:, ln] = jnp.concatenate(outs, axis=0)


def _rglru_scan(rec, reverse, conv_w8, conv_b, wcat, b_a, b_x, lam):
    t = rec.shape[0]
    nt = t // TM
    hb = TM // SUBLANES
    n_hblk = t // SUBLANES

    def main_map(s):
        return (_scan_tile(reverse, s, nt), 0)

    def prev_map(s):
        return (jnp.maximum(_scan_tile(reverse, s, nt) * hb - 1, 0), 0)

    def next_map(s):
        return (jnp.minimum((_scan_tile(reverse, s, nt) + 1) * hb, n_hblk - 1), 0)

    row = lambda v: v.reshape(1, -1)
    return pl.pallas_call(
        functools.partial(_rglru_kernel, reverse, nt),
        grid=(nt,),
        in_specs=[pl.BlockSpec((TM, D_RNN), main_map),
                  pl.BlockSpec((SUBLANES, D_RNN), prev_map),
                  pl.BlockSpec((SUBLANES, D_RNN), next_map),
                  _full((SUBLANES, D_RNN)),
                  _full((1, D_RNN)),
                  _full((RG_BLOCKS, RG_BLOCK_W, 2 * RG_BLOCK_W)),
                  _full((1, D_RNN)), _full((1, D_RNN)), _full((1, D_RNN))],
        out_specs=pl.BlockSpec((TM, D_RNN), main_map),
        out_shape=jax.ShapeDtypeStruct((t, D_RNN), F32),
        scratch_shapes=[pltpu.VMEM((SUBLANES, D_RNN), F32)],
        compiler_params=_params(),
        name="rglru_bwd" if reverse else "rglru_fwd",
    )(rec, rec, rec, conv_w8, row(conv_b), wcat, row(b_a), row(b_x), row(lam))


def _route(x1, g2_ref, sh2, sc2, wrt_ref, br_ref, carry_ref, h2_ref, meta_ref, cnt_ref):
    step = pl.program_id(0)

    @pl.when(step == 0)
    def _():
        carry_ref[...] = jnp.zeros_like(carry_ref)

    h2 = _rms(x1, g2_ref[...]) * (1.0 + sc2) + sh2
    h2_ref[...] = h2
    logits = lax.dot_general(wrt_ref[...], h2, (((1,), (1,)), ((), ())),
                             precision=HI, preferred_element_type=F32) + br_ref[:, 0:1]
    tm = x1.shape[0]
    row8 = lax.broadcasted_iota(jnp.int32, (SUBLANES, tm), 0)
    grp_logits = jnp.where(row8 < N_GROUPS, logits[0:SUBLANES], NEG_INF)
    gmax = jnp.max(grp_logits, axis=0, keepdims=True)
    p_sel = 1.0 / jnp.sum(jnp.exp(grp_logits - gmax), axis=0, keepdims=True)
    grp = jnp.min(jnp.where(grp_logits == gmax, row8, SUBLANES), axis=0, keepdims=True)

    in_grp = logits[SUBLANES + (N_GROUPS - 1) * EXPERTS_PER_GROUP:SUBLANES + N_GROUPS * EXPERTS_PER_GROUP]
    for gi in range(N_GROUPS - 2, -1, -1):
        lo = SUBLANES + gi * EXPERTS_PER_GROUP
        in_grp = jnp.where(grp == gi, logits[lo:lo + EXPERTS_PER_GROUP], in_grp)
    v1 = jnp.max(in_grp, axis=0, keepdims=True)
    i1 = jnp.min(jnp.where(in_grp == v1, row8, EXPERTS_PER_GROUP), axis=0, keepdims=True)
    rest = jnp.where(row8 == i1, NEG_INF, in_grp)
    v2 = jnp.max(rest, axis=0, keepdims=True)
    i2 = jnp.min(jnp.where(rest == v2, row8, EXPERTS_PER_GROUP), axis=0, keepdims=True)
    e2 = jnp.exp(v2 - v1)
    w1 = p_sel / (1.0 + e2)
    w2 = p_sel * e2 / (1.0 + e2)
    eid = (grp * EXPERTS_PER_GROUP + i1, grp * EXPERTS_PER_GROUP + i2)

    rr = lax.broadcasted_iota(jnp.int32, (tm, tm), 0)
    cc = lax.broadcasted_iota(jnp.int32, (tm, tm), 1)
    strict_upper = jnp.where(rr < cc, 1.0, 0.0).astype(BF16)
    erow = lax.broadcasted_iota(jnp.int32, (N_EXPERTS, tm), 0)
    base = carry_ref[:, 0:1]
    ranks = []
    for k in range(2):
        onehot = jnp.where(erow == eid[k], 1.0, 0.0)
        pre = jnp.dot(onehot.astype(BF16), strict_upper, preferred_element_type=F32)
        ranks.append(jnp.sum(onehot * (base + pre), axis=0, keepdims=True))
        base = base + jnp.sum(onehot, axis=1, keepdims=True)
    carry_ref[...] = jnp.broadcast_to(base, carry_ref.shape)
    cnt_ref[...] = jnp.broadcast_to(base, cnt_ref.shape)

    meta_ref[0:1, :] = eid[0].astype(F32)
    meta_ref[1:2, :] = eid[1].astype(F32)
    meta_ref[2:3, :] = ranks[0]
    meta_ref[3:4, :] = ranks[1]
    meta_ref[4:5, :] = w1
    meta_ref[5:6, :] = w2
    meta_ref[6:8, :] = jnp.zeros((2, tm), F32)


_ROUTE_OUT_SPECS = lambda d: [pl.BlockSpec((TM, d), lambda i: (i, 0)),
                              pl.BlockSpec((TM, d), lambda i: (i, 0)),
                              pl.BlockSpec((SUBLANES, TM), lambda i: (0, i)),
                              _full((N_EXPERTS, LANES))]


def _route_out_shapes(t, d):
    return [jax.ShapeDtypeStruct((t, d), F32), jax.ShapeDtypeStruct((t, d), F32),
            jax.ShapeDtypeStruct((SUBLANES, t), F32), jax.ShapeDtypeStruct((N_EXPERTS, LANES), F32)]


def _router_weights(w_grp, b_grp, w_exp, b_exp):
    d = w_grp.shape[0]
    rows = SUBLANES + N_EXPERTS
    wrt = jnp.zeros((rows, d), F32).at[:N_GROUPS].set(w_grp.T).at[SUBLANES:].set(w_exp.T)
    br = jnp.zeros((rows,), F32).at[:N_GROUPS].set(b_grp).at[SUBLANES:].set(b_exp)
    return wrt, jnp.broadcast_to(br[:, None], (rows, LANES))


def _outproj0_kernel(x_ref, ctx_ref, gate_ref, hf_ref, hb_ref, w_ref, gt_ref, g2_ref, sh_ref, sc_ref,
                     wrt_ref, br_ref, x1_ref, h2_ref, meta_ref, cnt_ref, carry_ref):
    is_ctx = pl.program_id(0) == 0
    xin = jnp.where(is_ctx, ctx_ref[...], x_ref[...])
    y = _gelu_tanh(gate_ref[...]) * (hf_ref[...] + hb_ref[...])
    y = jnp.dot(y.astype(BF16), w_ref[...], preferred_element_type=F32)
    x1 = xin + _mod_row(gt_ref, is_ctx) * y
    x1_ref[...] = x1
    _route(x1, g2_ref, _mod_row(sh_ref, is_ctx), _mod_row(sc_ref, is_ctx), wrt_ref, br_ref,
           carry_ref, h2_ref, meta_ref, cnt_ref)


def _outproj0(x, ctx, gate, hf, hb, w_out_bf, mod, g2, wrt, br):
    n, d = x.shape
    nt = n // TM + 1
    t = nt * TM
    tok = pl.BlockSpec((TM, D_RNN), lambda i: (i, 0))
    modspec = lambda c: pl.BlockSpec((SUBLANES, d), lambda i: (0, c))
    return pl.pallas_call(
        _outproj0_kernel,
        grid=(nt,),
        in_specs=[pl.BlockSpec((TM, d), lambda i: (jnp.maximum(i - 1, 0), 0)),
                  _full((TM, d)),
                  tok, tok, tok,
                  _full((D_RNN, d)),
                  modspec(2), _full((1, d)), modspec(3), modspec(4),
                  _full(wrt.shape), _full(br.shape)],
        out_specs=_ROUTE_OUT_SPECS(d),
        out_shape=_route_out_shapes(t, d),
        scratch_shapes=[pltpu.VMEM((N_EXPERTS, LANES), F32)],
        compiler_params=_params(),
        name="outproj0_route",
    )(x, ctx, gate, hf, hb, w_out_bf, mod, g2.reshape(1, -1), mod, mod, wrt, br)


def _plan_kernel(n_tiles, nbp, cnt_ref, meta_ref, dest_ref, blk_ref):
    c = cnt_ref[...]
    padded = jnp.floor((c + (TB - 1)) * (1.0 / TB)) * TB
    r = lax.broadcasted_iota(jnp.int32, (N_EXPERTS, N_EXPERTS), 0)
    q = lax.broadcasted_iota(jnp.int32, (N_EXPERTS, N_EXPERTS), 1)
    lower = jnp.where(q <= r, 1.0, 0.0)
    pad_end = jnp.dot(lower, padded, precision=HI, preferred_element_type=F32)
    pad_start = pad_end - padded

    first_row = lax.broadcasted_iota(jnp.int32, (N_EXPERTS, nbp), 1).astype(F32) * TB
    owner = jnp.sum(jnp.where(pad_end[:, 0:1] <= first_row, 1.0, 0.0), axis=0, keepdims=True)
    blk_ref[0:1, :] = jnp.minimum(owner, N_EXPERTS - 1).astype(jnp.int32)
    n_used = pad_end[N_EXPERTS - 1:N_EXPERTS, 0:1] * (1.0 / TB)
    blk_ref[1:2, :] = jnp.broadcast_to(n_used, (1, nbp)).astype(jnp.int32)
    blk_ref[2:SUBLANES, :] = jnp.zeros((SUBLANES - 2, nbp), jnp.int32)

    erow = lax.broadcasted_iota(jnp.int32, (N_EXPERTS, TM), 0).astype(F32)

    def body(i, carry):
        ln = pl.ds(pl.multiple_of(i * TM, TM), TM)
        for k in range(2):
            onehot = jnp.where(erow == meta_ref[k:k + 1, ln], 1.0, 0.0)
            start = jnp.sum(onehot * pad_start[:, 0:1], axis=0, keepdims=True)
            dest_ref[k:k + 1, ln] = (start + meta_ref[2 + k:3 + k, ln]).astype(jnp.int32)
        dest_ref[2:SUBLANES, ln] = jnp.zeros((SUBLANES - 2, TM), jnp.int32)
        return carry

    lax.fori_loop(0, n_tiles, body, 0)


def _plan(cnt, meta):
    t = meta.shape[1]
    n_blocks = (2 * t + N_EXPERTS * TB) // TB
    nbp = -(-n_blocks // LANES) * LANES
    vm = pl.BlockSpec(memory_space=pltpu.VMEM)
    dest, blk = pl.pallas_call(
        functools.partial(_plan_kernel, t // TM, nbp),
        in_specs=[vm, vm],
        out_specs=[vm, vm],
        out_shape=[jax.ShapeDtypeStruct((SUBLANES, t), jnp.int32),
                   jax.ShapeDtypeStruct((SUBLANES, nbp), jnp.int32)],
        compiler_params=pltpu.CompilerParams(vmem_limit_bytes=VMEM_LIMIT),
        name="moe_plan",
    )(cnt, meta)
    return dest, blk, n_blocks


def _dispatch_kernel(h2_ref, dest_ref, init_ref, buf_ref, dsm_ref, sem_idx, sem_rows):
    del init_ref
    cp = pltpu.make_async_copy(dest_ref, dsm_ref, sem_idx)
    cp.start()
    cp.wait()

    def body(r, carry):
        for k in range(2):
            pltpu.make_async_copy(h2_ref.at[pl.ds(r, 1), :],
                                  buf_ref.at[pl.ds(dsm_ref[k, r], 1), :], sem_rows).start()
        return carry

    lax.fori_loop(0, TM, body, 0, unroll=8)
    for k in range(2):
        pltpu.make_async_copy(h2_ref, buf_ref.at[pl.ds(0, TM), :], sem_rows).wait()


def _dispatch(h2, dest, n_blocks):
    t, d = h2.shape
    n_rows = n_blocks * TB
    return pl.pallas_call(
        _dispatch_kernel,
        grid=(t // TM,),
        in_specs=[pl.BlockSpec((TM, d), lambda i: (i, 0)),
                  pl.BlockSpec((SUBLANES, TM), lambda i: (0, i)),
                  pl.BlockSpec(memory_space=pl.ANY)],
        out_specs=pl.BlockSpec(memory_space=pl.ANY),
        out_shape=jax.ShapeDtypeStruct((n_rows, d), F32),
        scratch_shapes=[pltpu.SMEM((SUBLANES, TM), jnp.int32),
                        pltpu.SemaphoreType.DMA, pltpu.SemaphoreType.DMA],
        input_output_aliases={2: 0},
        compiler_params=_params(),
        name="moe_dispatch",
    )(h2, dest, jnp.zeros((n_rows, d), F32))


def _experts_kernel(be_ref, nu_ref, x_ref, wg_ref, wu_ref, wd_ref, y_ref):
    @pl.when(pl.program_id(0) < nu_ref[0])
    def _():
        xb = x_ref[...].astype(BF16)
        g = jnp.dot(xb, wg_ref[0], preferred_element_type=F32)
        u = jnp.dot(xb, wu_ref[0], preferred_element_type=F32)
        a = (g * _sigmoid(g)) * u
        y_ref[...] = jnp.dot(a.astype(BF16), wd_ref[0], preferred_element_type=F32)

    @pl.when(pl.program_id(0) >= nu_ref[0])
    def _():
        y_ref[...] = jnp.zeros_like(y_ref)


def _experts(buf, blk_e, n_used, wg, wu, wd, n_blocks):
    d = buf.shape[1]
    last = lambda i, nu: jnp.minimum(i, nu[0] - 1)
    grid_spec = pltpu.PrefetchScalarGridSpec(
        num_scalar_prefetch=2,
        grid=(n_blocks,),
        in_specs=[pl.BlockSpec((TB, d), lambda i, be, nu: (last(i, nu), 0)),
                  pl.BlockSpec((1, d, D_EXPERT), lambda i, be, nu: (be[last(i, nu)], 0, 0)),
                  pl.BlockSpec((1, d, D_EXPERT), lambda i, be, nu: (be[last(i, nu)], 0, 0)),
                  pl.BlockSpec((1, D_EXPERT, d), lambda i, be, nu: (be[last(i, nu)], 0, 0))],
        out_specs=pl.BlockSpec((TB, d), lambda i, be, nu: (i, 0)),
    )
    return pl.pallas_call(
        _experts_kernel,
        grid_spec=grid_spec,
        out_shape=jax.ShapeDtypeStruct(buf.shape, F32),
        compiler_params=_params(),
        name="moe_experts",
    )(blk_e, n_used, buf, wg, wu, wd)


def _gather_rows(dest_ref, dsm_ref, ybuf_ref, rows_ref, sem_idx, sem_rows):
    cp = pltpu.make_async_copy(dest_ref, dsm_ref, sem_idx)
    cp.start()
    cp.wait()

    def body(r, carry):
        for k in range(2):
            pltpu.make_async_copy(ybuf_ref.at[pl.ds(dsm_ref[k, r], 1), :],
                                  rows_ref.at[k, pl.ds(r, 1), :], sem_rows).start()
        return carry

    lax.fori_loop(0, TM, body, 0, unroll=8)
    for k in range(2):
        pltpu.make_async_copy(ybuf_ref.at[pl.ds(0, TM), :], rows_ref.at[k], sem_rows).wait()


def _token_weights(meta_ref):
    meta = jnp.concatenate([meta_ref[...], jnp.zeros((LANES - SUBLANES, TM), F32)], axis=0)
    mt = meta.T
    return mt[:, 4:5], mt[:, 5:6]


def _combine0_kernel(x1_ref, dest_ref, meta_ref, gt_ref, ybuf_ref, x2_ref, ctx2_ref,
                     dsm_ref, rows_ref, sem_idx, sem_rows):
    step = pl.program_id(0)
    is_ctx = step == 0
    _gather_rows(dest_ref, dsm_ref, ybuf_ref, rows_ref, sem_idx, sem_rows)
    w0, w1 = _token_weights(meta_ref)
    out = x1_ref[...] + _mod_row(gt_ref, is_ctx) * (w0 * rows_ref[0] + w1 * rows_ref[1])
    x2_ref[...] = out

    @pl.when(is_ctx)
    def _():
        ctx2_ref[...] = out


def _combine0(x1, dest, meta, mod, ybuf):
    t, d = x1.shape
    nt = t // TM
    return pl.pallas_call(
        _combine0_kernel,
        grid=(nt,),
        in_specs=[pl.BlockSpec((TM, d), lambda i: (i, 0)),
                  pl.BlockSpec((SUBLANES, TM), lambda i: (0, i)),
                  pl.BlockSpec((SUBLANES, TM), lambda i: (0, i)),
                  pl.BlockSpec((SUBLANES, d), lambda i: (0, 5)),
                  pl.BlockSpec(memory_space=pl.ANY)],
        out_specs=[pl.BlockSpec((TM, d), lambda i: (jnp.maximum(i - 1, 0), 0)),
                   _full((TM, d))],
        out_shape=[jax.ShapeDtypeStruct((t - TM, d), F32), jax.ShapeDtypeStruct((TM, d), F32)],
        scratch_shapes=[pltpu.SMEM((SUBLANES, TM), jnp.int32),
                        pltpu.VMEM((2, TM, d), F32),
                        pltpu.SemaphoreType.DMA, pltpu.SemaphoreType.DMA],
        compiler_params=_params(),
        name="moe_combine0",
    )(x1, dest, meta, mod, ybuf)


def _combine1_kernel(x1_ref, dest_ref, meta_ref, gt_ref, gf_ref, ybuf_ref, o_ref,
                     dsm_ref, rows_ref, sem_idx, sem_rows):
    _gather_rows(dest_ref, dsm_ref, ybuf_ref, rows_ref, sem_idx, sem_rows)
    w0, w1 = _token_weights(meta_ref)
    out = x1_ref[...] + gt_ref[0:1, :] * (w0 * rows_ref[0] + w1 * rows_ref[1])
    o_ref[...] = _rms(out, gf_ref[...])


def _combine1(x1, dest, meta, mod, gf, ybuf):
    t, d = x1.shape
    rows = t // GRID_W
    out = pl.pallas_call(
        _combine1_kernel,
        grid=(t // TM,),
        in_specs=[pl.BlockSpec((TM, d), lambda i: (i, 0)),
                  pl.BlockSpec((SUBLANES, TM), lambda i: (0, i)),
                  pl.BlockSpec((SUBLANES, TM), lambda i: (0, i)),
                  pl.BlockSpec((SUBLANES, d), lambda i: (0, 5)),
                  _full((1, d)),
                  pl.BlockSpec(memory_space=pl.ANY)],
        out_specs=pl.BlockSpec((TM, d), lambda i: (0, i)),
        out_shape=jax.ShapeDtypeStruct((rows, GRID_W * d), F32),
        scratch_shapes=[pltpu.SMEM((SUBLANES, TM), jnp.int32),
                        pltpu.VMEM((2, TM, d), F32),
                        pltpu.SemaphoreType.DMA, pltpu.SemaphoreType.DMA],
        compiler_params=_params(),
        name="moe_combine1",
    )(x1, dest, meta, mod, gf.reshape(1, -1), ybuf)
    return out.reshape(t, d)


def _moe(h2, meta, cnt, w_gate_bf, w_up_bf, w_down_bf):
    dest, blk, n_blocks = _plan(cnt, meta)
    buf = _dispatch(h2, dest, n_blocks)
    ybuf = _experts(buf, blk[0], blk[1, :1], w_gate_bf, w_up_bf, w_down_bf, n_blocks)
    return dest, ybuf


def _inproj1_kernel(x_ref, ctx_ref, g_ref, sh_ref, sc_ref, wq_ref, wkt_ref, wv_ref, wo_ref,
                    wg_ref, wgt_ref, bg_ref, bgt_ref,
                    q_ref, kt_ref, v_ref, o_ref, gcol_ref, grow_ref):
    is_ctx = pl.program_id(0) == 0
    xin = jnp.where(is_ctx, ctx_ref[...], x_ref[...])
    h = _rms(xin, g_ref[...]) * (1.0 + _mod_row(sc_ref, is_ctx)) + _mod_row(sh_ref, is_ctx)
    hb = h.astype(BF16)
    q = jnp.dot(hb, wq_ref[...], preferred_element_type=F32) * (ML_DQK ** -0.5)
    q_ref[...] = q.astype(BF16)
    kt = lax.dot_general(wkt_ref[...], hb, (((1,), (1,)), ((), ())), preferred_element_type=F32)
    kt_ref[...] = kt.astype(BF16)
    v_ref[...] = jnp.dot(hb, wv_ref[...], preferred_element_type=F32).astype(BF16)
    o_ref[...] = jnp.dot(hb, wo_ref[...], preferred_element_type=F32)
    gcol_ref[...] = jnp.dot(h, wg_ref[...], precision=HI, preferred_element_type=F32) + bg_ref[...]
    grow_ref[...] = lax.dot_general(wgt_ref[...], h, (((1,), (1,)), ((), ())), precision=HI,
                                    preferred_element_type=F32) + bgt_ref[:, 0:1]


def _inproj1(x, ctx, g1, mod, w_in, b_gates):
    n, d = x.shape
    rows = n // GRID_W
    nt = GRID_W + 1
    t = nt * TM
    ng = 4 * ML_HEADS
    wq = w_in[:, :ML_QK_W].astype(BF16)
    wkt = w_in[:, ML_QK_W:2 * ML_QK_W].T.astype(BF16)
    wv = w_in[:, 2 * ML_QK_W:2 * ML_QK_W + ML_V_W].astype(BF16)
    wo = w_in[:, 2 * ML_QK_W + ML_V_W:2 * ML_QK_W + ML_V_W + d].astype(BF16)
    wg = jnp.zeros((d, LANES), F32).at[:, :ng].set(w_in[:, -ng:])
    wgt = w_in[:, -ng:].T
    bg = jnp.zeros((1, LANES), F32).at[0, :ng].set(b_gates)
    bgt = jnp.broadcast_to(b_gates[:, None], (ng, LANES))
    tok = lambda w: pl.BlockSpec((TM, w), lambda i: (i, 0))
    return pl.pallas_call(
        _inproj1_kernel,
        grid=(nt,),
        in_specs=[pl.BlockSpec((rows, d), lambda i: (0, jnp.maximum(i - 1, 0))),
                  _full((TM, d)),
                  _full((1, d)),
                  pl.BlockSpec((SUBLANES, d), lambda i: (0, 0)),
                  pl.BlockSpec((SUBLANES, d), lambda i: (0, 1)),
                  _full(wq.shape), _full(wkt.shape), _full(wv.shape), _full(wo.shape),
                  _full(wg.shape), _full(wgt.shape), _full(bg.shape), _full(bgt.shape)],
        out_specs=[tok(ML_QK_W),
                   pl.BlockSpec((ML_QK_W, TM), lambda i: (0, i)),
                   tok(ML_V_W), tok(d), tok(LANES),
                   pl.BlockSpec((ng, TM), lambda i: (0, i))],
        out_shape=[jax.ShapeDtypeStruct((t, ML_QK_W), BF16),
                   jax.ShapeDtypeStruct((ML_QK_W, t), BF16),
                   jax.ShapeDtypeStruct((t, ML_V_W), BF16),
                   jax.ShapeDtypeStruct((t, d), F32),
                   jax.ShapeDtypeStruct((t, LANES), F32),
                   jax.ShapeDtypeStruct((ng, t), F32)],
        compiler_params=_params(),
        name="inproj1",
    )(x.reshape(rows, GRID_W * d), ctx, g1.reshape(1, -1), mod, mod, wq, wkt, wv, wo, wg, wgt, bg, bgt)


def _log_sigmoid(x):
    return jnp.minimum(x, 0.0) - jnp.log1p(jnp.exp(-jnp.abs(x)))


def _mlstm_kernel(reverse, q_ref, kt_ref, v_ref, gcol_ref, grow_ref, h_ref, c_ref, m_ref):
    @pl.when(pl.program_id(0) == 0)
    def _():
        c_ref[...] = jnp.zeros_like(c_ref)
        m_ref[...] = jnp.zeros_like(m_ref)

    L = TM
    gi = 2 * ML_HEADS if reverse else 0
    gf = gi + ML_HEADS
    end = 0 if reverse else L - 1
    rr = lax.broadcasted_iota(jnp.int32, (L, L), 0)
    cc = lax.broadcasted_iota(jnp.int32, (L, L), 1)
    allowed = (cc >= rr) if reverse else (cc <= rr)
    tri = jnp.where(allowed, 1.0, 0.0)
    lf_col = _log_sigmoid(gcol_ref[...])
    lf_row = _log_sigmoid(grow_ref[gf:gf + ML_HEADS, :])
    cum_col = jnp.dot(tri, lf_col, precision=HI, preferred_element_type=F32)
    cum_row = lax.dot_general(lf_row, tri, (((1,), (1,)), ((), ())), precision=HI,
                              preferred_element_type=F32)
    ones_col = jnp.where(lax.broadcasted_iota(jnp.int32, (L, ML_DV), 1) == 0, 1.0, 0.0).astype(BF16)

    for hd in range(ML_HEADS):
        m_prev = m_ref[hd:hd + 1, 0:1]
        g_col = cum_col[:, gf + hd:gf + hd + 1]
        g_row = cum_row[hd:hd + 1, :]
        li_row = grow_ref[gi + hd:gi + hd + 1, :]
        total = g_col[end:end + 1, :]

        log_w = jnp.where(allowed, g_col - g_row + li_row, NEG_INF)
        log_inter = g_col + m_prev
        m_t = jnp.maximum(log_inter, jnp.max(log_w, axis=-1, keepdims=True))
        w_intra = jnp.exp(log_w - m_t)
        w_inter = jnp.exp(log_inter - m_t)

        qh = q_ref[:, hd * ML_DQK:(hd + 1) * ML_DQK]
        kth = kt_ref[hd * ML_DQK:(hd + 1) * ML_DQK, :]
        vext = jnp.concatenate([v_ref[:, hd * ML_DV:(hd + 1) * ML_DV], ones_col], axis=1)
        s = jnp.dot(qh, kth, preferred_element_type=F32) * w_intra
        state = c_ref[hd]
        tot = (w_inter * jnp.dot(qh, state.astype(BF16), preferred_element_type=F32)
               + jnp.dot(s.astype(BF16), vext, preferred_element_type=F32))
        den = tot[:, ML_DV:ML_DV + 1]
        h_ref[:, hd * ML_DV:(hd + 1) * ML_DV] = tot[:, :ML_DV] / jnp.maximum(jnp.abs(den), jnp.exp(-m_t))

        m_new = m_t[end:end + 1, :]
        w_state = jnp.exp(total - g_row + li_row - m_new)
        decay = jnp.exp(total + m_prev - m_new)
        kw = (kth.astype(F32) * w_state).astype(BF16)
        c_ref[hd] = decay * state + jnp.dot(kw, vext, preferred_element_type=F32)
        m_ref[hd:hd + 1, :] = jnp.broadcast_to(m_new, (1, LANES))


def _mlstm(q, kt, v, gcol, grow, reverse):
    t = q.shape[0]
    nt = t // TM
    tile = lambda s: _scan_tile(reverse, s, nt)
    return pl.pallas_call(
        functools.partial(_mlstm_kernel, reverse),
        grid=(nt,),
        in_specs=[pl.BlockSpec((TM, ML_QK_W), lambda s: (tile(s), 0)),
                  pl.BlockSpec((ML_QK_W, TM), lambda s: (0, tile(s))),
                  pl.BlockSpec((TM, ML_V_W), lambda s: (tile(s), 0)),
                  pl.BlockSpec((TM, LANES), lambda s: (tile(s), 0)),
                  pl.BlockSpec((4 * ML_HEADS, TM), lambda s: (0, tile(s)))],
        out_specs=pl.BlockSpec((TM, ML_V_W), lambda s: (tile(s), 0)),
        out_shape=jax.ShapeDtypeStruct((t, ML_V_W), F32),
        scratch_shapes=[pltpu.VMEM((ML_HEADS, ML_DQK, 2 * ML_DV), F32),
                        pltpu.VMEM((ML_HEADS, LANES), F32)],
        compiler_params=_params(),
        name="mlstm_bwd" if reverse else "mlstm_fwd",
    )(q, kt, v, gcol, grow)


def _finish1_kernel(x_ref, hf_ref, hb_ref, o_ref, ng_ref, w_ref, gt_ref, g2_ref, sh_ref, sc_ref,
                    wrt_ref, br_ref, x1_ref, h2_ref, meta_ref, cnt_ref, carry_ref):
    hs = hf_ref[...] + hb_ref[...]
    parts = []
    for hd in range(ML_HEADS):
        blk = hs[:, hd * ML_DV:(hd + 1) * ML_DV]
        parts.append(blk * lax.rsqrt(jnp.mean(blk * blk, axis=-1, keepdims=True) + EPS))
    hn = jnp.concatenate(parts, axis=1) * ng_ref[...]
    y = jnp.dot((hn * _sigmoid(o_ref[...])).astype(BF16), w_ref[...], preferred_element_type=F32)
    x1 = x_ref[...] + gt_ref[0:1, :] * y
    x1_ref[...] = x1
    _route(x1, g2_ref, sh_ref[0:1, :], sc_ref[0:1, :], wrt_ref, br_ref,
           carry_ref, h2_ref, meta_ref, cnt_ref)


def _finish1(x, hf, hb, o, norm_g, w_out_bf, mod, g2, wrt, br):
    n, d = x.shape
    rows = n // GRID_W
    lat = lambda w: pl.BlockSpec((TM, w), lambda i: (i + 1, 0))
    modspec = lambda c: pl.BlockSpec((SUBLANES, d), lambda i: (0, c))
    return pl.pallas_call(
        _finish1_kernel,
        grid=(GRID_W,),
        in_specs=[pl.BlockSpec((rows, d), lambda i: (0, i)),
                  lat(ML_V_W), lat(ML_V_W), lat(d),
                  _full((1, ML_V_W)), _full((ML_V_W, d)),
                  modspec(2), _full((1, d)), modspec(3), modspec(4),
                  _full(wrt.shape), _full(br.shape)],
        out_specs=_ROUTE_OUT_SPECS(d),
        out_shape=_route_out_shapes(n, d),
        scratch_shapes=[pltpu.VMEM((N_EXPERTS, LANES), F32)],
        compiler_params=_params(),
        name="finish1_route",
    )(x.reshape(rows, GRID_W * d), hf, hb, o, norm_g.reshape(1, -1), w_out_bf, mod,
      g2.reshape(1, -1), mod, mod, wrt, br)


def kernel(x, c, ctx, c_ctx,
           l0_ada_w, l0_ada_b, l0_norm1_g, l0_norm2_g,
           l0_rg_w_in, l0_rg_conv_w, l0_rg_conv_b, l0_rg_w_a, l0_rg_b_a, l0_rg_w_x, l0_rg_b_x,
           l0_rg_lambda, l0_rg_w_out,
           l0_moe_w_grp, l0_moe_b_grp, l0_moe_w_exp, l0_moe_b_exp, l0_moe_w_gate, l0_moe_w_up,
           l0_moe_w_down,
           l1_ada_w, l1_ada_b, l1_norm1_g, l1_norm2_g,
           l1_ml_w_in, l1_ml_b_gates, l1_ml_norm_g, l1_ml_w_out,
           l1_moe_w_grp, l1_moe_b_grp, l1_moe_w_exp, l1_moe_b_exp, l1_moe_w_gate, l1_moe_w_up,
           l1_moe_w_down,
           final_norm_g):
    assert x.shape[0] == 1 and ctx.shape[1] == TM and x.shape[1] == GRID_W * TM
    xs, cs = x[0], ctx[0]
    d = xs.shape[1]
    cond8 = jnp.zeros((SUBLANES, d), F32).at[0].set(c[0]).at[1].set(c_ctx)

    mod0 = _adaln(cond8, l0_ada_w, l0_ada_b)
    gate, rec = _inproj0(xs, cs, l0_norm1_g, mod0, l0_rg_w_in.astype(BF16))
    conv_w8 = jnp.zeros((SUBLANES, D_RNN), F32).at[:CONV_W].set(l0_rg_conv_w)
    h_dirs = []
    for dr in range(2):
        wcat = jnp.concatenate([l0_rg_w_a[dr], l0_rg_w_x[dr]], axis=-1).astype(BF16)
        h_dirs.append(_rglru_scan(rec, dr == 1, conv_w8, l0_rg_conv_b, wcat,
                                  l0_rg_b_a[dr], l0_rg_b_x[dr], l0_rg_lambda[dr]))
    wrt0, br0 = _router_weights(l0_moe_w_grp, l0_moe_b_grp, l0_moe_w_exp, l0_moe_b_exp)
    x1, h2, meta, cnt = _outproj0(xs, cs, gate, h_dirs[0], h_dirs[1], l0_rg_w_out.astype(BF16),
                                  mod0, l0_norm2_g, wrt0, br0)
    dest, ybuf = _moe(h2, meta, cnt, l0_moe_w_gate.astype(BF16), l0_moe_w_up.astype(BF16),
                      l0_moe_w_down.astype(BF16))
    x2, ctx2 = _combine0(x1, dest, meta, mod0, ybuf)

    mod1 = _adaln(cond8, l1_ada_w, l1_ada_b)
    q, kt, v, o, gcol, grow = _inproj1(x2, ctx2, l1_norm1_g, mod1, l1_ml_w_in, l1_ml_b_gates)
    hf = _mlstm(q, kt, v, gcol, grow, False)
    hb = _mlstm(q, kt, v, gcol, grow, True)
    wrt1, br1 = _router_weights(l1_moe_w_grp, l1_moe_b_grp, l1_moe_w_exp, l1_moe_b_exp)
    x1, h2, meta, cnt = _finish1(x2, hf, hb, o, l1_ml_norm_g, l1_ml_w_out.astype(BF16), mod1,
                                 l1_norm2_g, wrt1, br1)
    dest, ybuf = _moe(h2, meta, cnt, l1_moe_w_gate.astype(BF16), l1_moe_w_up.astype(BF16),
                      l1_moe_w_down.astype(BF16))
    out = _combine1(x1, dest, meta, mod1, final_norm_g, ybuf)
    return out[None]
```

```python
import functools

import jax
import jax.numpy as jnp
from jax import lax
from jax.experimental import pallas as pl
from jax.experimental.pallas import tpu as pltpu

D_MODEL = 1024
GRID_W = 64
N_MOD = 6
EPS = 1e-6

D_RNN = 1280
RG_BLOCKS = 10
RG_BLOCK_W = D_RNN // RG_BLOCKS
CONV_W = 4
CONV_PAD_L = 2
RG_C = 8.0

ML_HEADS = 8
ML_DQK = D_MODEL // (2 * ML_HEADS)
ML_DV = D_MODEL // ML_HEADS
ML_QK_W = ML_HEADS * ML_DQK
ML_V_W = ML_HEADS * ML_DV

N_GROUPS = 4
EXPERTS_PER_GROUP = 8
N_EXPERTS = N_GROUPS * EXPERTS_PER_GROUP
D_EXPERT = 512

TM = 256
TB = 256
SUBLANES = 8
LANES = 128
VMEM_LIMIT = 48 * 1024 * 1024

F32 = jnp.float32
BF16 = jnp.bfloat16
HI = lax.Precision.HIGHEST
NEG_INF = float("-inf")


def _params(n_axes=1):
    return pltpu.CompilerParams(dimension_semantics=("arbitrary",) * n_axes,
                                vmem_limit_bytes=VMEM_LIMIT)


def _rms(x, g):
    return x * lax.rsqrt(jnp.mean(x * x, axis=-1, keepdims=True) + EPS) * g


def _sigmoid(x):
    return 1.0 / (1.0 + jnp.exp(-x))


def _softplus(x):
    return jnp.maximum(x, 0.0) + jnp.log1p(jnp.exp(-jnp.abs(x)))


def _gelu_tanh(x):
    return 0.5 * x * (1.0 + jnp.tanh(0.7978845608028654 * (x + 0.044715 * (x * x * x))))


def _full(shape):
    return pl.BlockSpec(shape, lambda *_: (0,) * len(shape))


def _adaln_kernel(cond_ref, w_ref, b_ref, o_ref):
    c = cond_ref[...]
    s = c * _sigmoid(c)
    o_ref[...] = jnp.dot(s, w_ref[...], precision=HI, preferred_element_type=F32) + b_ref[...]


def _adaln(cond8, w, b):
    d = w.shape[0]
    return pl.pallas_call(
        _adaln_kernel,
        grid=(N_MOD,),
        in_specs=[_full((SUBLANES, d)),
                  pl.BlockSpec((d, d), lambda j: (0, j)),
                  pl.BlockSpec((1, d), lambda j: (0, j))],
        out_specs=pl.BlockSpec((SUBLANES, d), lambda j: (0, j)),
        out_shape=jax.ShapeDtypeStruct((SUBLANES, N_MOD * d), F32),
        compiler_params=_params(),
        name="adaln",
    )(cond8, w, b.reshape(1, -1))


def _mod_row(ref, is_ctx):
    return jnp.where(is_ctx, ref[1:2, :], ref[0:1, :])


def _inproj0_kernel(x_ref, ctx_ref, g_ref, sh_ref, sc_ref, w_ref, gate_ref, rec_ref):
    is_ctx = pl.program_id(0) == 0
    xin = jnp.where(is_ctx, ctx_ref[...], x_ref[...])
    h = _rms(xin, g_ref[...]) * (1.0 + _mod_row(sc_ref, is_ctx)) + _mod_row(sh_ref, is_ctx)
    p = jnp.dot(h.astype(BF16), w_ref[...], preferred_element_type=F32)
    gate_ref[...] = p[:, :D_RNN]
    rec_ref[...] = p[:, D_RNN:]


def _inproj0(x, ctx, g1, mod, w_in_bf):
    n, d = x.shape
    nt = n // TM + 1
    t = nt * TM
    return pl.pallas_call(
        _inproj0_kernel,
        grid=(nt,),
        in_specs=[pl.BlockSpec((TM, d), lambda i: (jnp.maximum(i - 1, 0), 0)),
                  _full((TM, d)),
                  _full((1, d)),
                  pl.BlockSpec((SUBLANES, d), lambda i: (0, 0)),
                  pl.BlockSpec((SUBLANES, d), lambda i: (0, 1)),
                  _full((d, 2 * D_RNN))],
        out_specs=[pl.BlockSpec((TM, D_RNN), lambda i: (i, 0)),
                   pl.BlockSpec((TM, D_RNN), lambda i: (i, 0))],
        out_shape=[jax.ShapeDtypeStruct((t, D_RNN), F32)] * 2,
        compiler_params=_params(),
        name="inproj0",
    )(x, ctx, g1.reshape(1, -1), mod, mod, w_in_bf)


def _scan_tile(reverse, s, nt):
    if not reverse:
        return s
    return jnp.where(s == 0, 0, nt - s)


def _rglru_kernel(reverse, nt, main_ref, prev_ref, next_ref, cw_ref, cb_ref, wcat_ref,
                  ba_ref, bx_ref, lam_ref, h_ref, carry_ref):
    s = pl.program_id(0)
    tile = _scan_tile(reverse, s, nt)

    @pl.when(s == 0)
    def _():
        carry_ref[...] = jnp.zeros_like(carry_ref)

    has_prev = tile >= 2
    has_next = jnp.logical_and(tile >= 1, tile <= nt - 2)
    sub = lax.broadcasted_iota(jnp.int32, (TM // SUBLANES, SUBLANES, LANES), 1)

    for j in range(RG_BLOCKS):
        ln = slice(j * RG_BLOCK_W, (j + 1) * RG_BLOCK_W)
        prev = jnp.where(has_prev, prev_ref[:, ln], 0.0)
        nxt = jnp.where(has_next, next_ref[:, ln], 0.0)
        ext = jnp.concatenate([prev, main_ref[:, ln], nxt], axis=0)
        base = SUBLANES - CONV_PAD_L
        u = cb_ref[:, ln] + ext[base:base + TM] * cw_ref[0:1, ln]
        for k in range(1, CONV_W):
            u = u + ext[base + k:base + k + TM] * cw_ref[k:k + 1, ln]

        g = jnp.dot(u.astype(BF16), wcat_ref[j], preferred_element_type=F32)
        r = _sigmoid(g[:, :RG_BLOCK_W] + ba_ref[:, ln])
        ig = _sigmoid(g[:, RG_BLOCK_W:] + bx_ref[:, ln])
        log_a = -RG_C * r * _softplus(-lam_ref[:, ln])
        a = jnp.exp(log_a)
        b = jnp.sqrt(-jnp.tanh(log_a) * (a * a + 1.0)) * ig * u

        a3 = a.reshape(TM // SUBLANES, SUBLANES, LANES)
        b3 = b.reshape(TM // SUBLANES, SUBLANES, LANES)
        for k in (1, 2, 4):
            shift = SUBLANES - k if reverse else k
            keep = (sub < SUBLANES - k) if reverse else (sub >= k)
            a_sh = pltpu.roll(a3, shift, 1)
            b_sh = pltpu.roll(b3, shift, 1)
            b3 = jnp.where(keep, a3 * b_sh + b3, b3)
            a3 = jnp.where(keep, a3 * a_sh, a3)

        h = carry_ref[0:1, ln]
        groups = range(TM // SUBLANES)
        outs = [None] * (TM // SUBLANES)
        for v in (reversed(groups) if reverse else groups):
            hv = b3[v] + a3[v] * h
            outs[v] = hv
            h = hv[0:1, :] if reverse else hv[SUBLANES - 1:SUBLANES, :]
        carry_ref[0:1, ln] = h
        h_ref[:, ln] = jnp.concatenate(outs, axis=0)


def _rglru_scan(rec, reverse, conv_w8, conv_b, wcat, b_a, b_x, lam):
    t = rec.shape[0]
    nt = t // TM
    hb = TM // SUBLANES
    n_hblk = t // SUBLANES

    def main_map(s):
        return (_scan_tile(reverse, s, nt), 0)

    def prev_map(s):
        return (jnp.maximum(_scan_tile(reverse, s, nt) * hb - 1, 0), 0)

    def next_map(s):
        return (jnp.minimum((_scan_tile(reverse, s, nt) + 1) * hb, n_hblk - 1), 0)

    row = lambda v: v.reshape(1, -1)
    return pl.pallas_call(
        functools.partial(_rglru_kernel, reverse, nt),
        grid=(nt,),
        in_specs=[pl.BlockSpec((TM, D_RNN), main_map),
                  pl.BlockSpec((SUBLANES, D_RNN), prev_map),
                  pl.BlockSpec((SUBLANES, D_RNN), next_map),
                  _full((SUBLANES, D_RNN)),
                  _full((1, D_RNN)),
                  _full((RG_BLOCKS, RG_BLOCK_W, 2 * RG_BLOCK_W)),
                  _full((1, D_RNN)), _full((1, D_RNN)), _full((1, D_RNN))],
        out_specs=pl.BlockSpec((TM, D_RNN), main_map),
        out_shape=jax.ShapeDtypeStruct((t, D_RNN), F32),
        scratch_shapes=[pltpu.VMEM((SUBLANES, D_RNN), F32)],
        compiler_params=_params(),
        name="rglru_bwd" if reverse else "rglru_fwd",
    )(rec, rec, rec, conv_w8, row(conv_b), wcat, row(b_a), row(b_x), row(lam))


def _route(x1, g2_ref, sh2, sc2, wrt_ref, br_ref, carry_ref, h2_ref, meta_ref, cnt_ref):
    step = pl.program_id(0)

    @pl.when(step == 0)
    def _():
        carry_ref[...] = jnp.zeros_like(carry_ref)

    h2 = _rms(x1, g2_ref[...]) * (1.0 + sc2) + sh2
    h2_ref[...] = h2
    logits = lax.dot_general(wrt_ref[...], h2, (((1,), (1,)), ((), ())),
                             precision=HI, preferred_element_type=F32) + br_ref[:, 0:1]
    tm = x1.shape[0]
    row8 = lax.broadcasted_iota(jnp.int32, (SUBLANES, tm), 0)
    grp_logits = jnp.where(row8 < N_GROUPS, logits[0:SUBLANES], NEG_INF)
    gmax = jnp.max(grp_logits, axis=0, keepdims=True)
    p_sel = 1.0 / jnp.sum(jnp.exp(grp_logits - gmax), axis=0, keepdims=True)
    grp = jnp.min(jnp.where(grp_logits == gmax, row8, SUBLANES), axis=0, keepdims=True)

    in_grp = logits[SUBLANES + (N_GROUPS - 1) * EXPERTS_PER_GROUP:SUBLANES + N_GROUPS * EXPERTS_PER_GROUP]
    for gi in range(N_GROUPS - 2, -1, -1):
        lo = SUBLANES + gi * EXPERTS_PER_GROUP
        in_grp = jnp.where(grp == gi, logits[lo:lo + EXPERTS_PER_GROUP], in_grp)
    v1 = jnp.max(in_grp, axis=0, keepdims=True)
    i1 = jnp.min(jnp.where(in_grp == v1, row8, EXPERTS_PER_GROUP), axis=0, keepdims=True)
    rest = jnp.where(row8 == i1, NEG_INF, in_grp)
    v2 = jnp.max(rest, axis=0, keepdims=True)
    i2 = jnp.min(jnp.where(rest == v2, row8, EXPERTS_PER_GROUP), axis=0, keepdims=True)
    e2 = jnp.exp(v2 - v1)
    w1 = p_sel / (1.0 + e2)
    w2 = p_sel * e2 / (1.0 + e2)
    eid = (grp * EXPERTS_PER_GROUP + i1, grp * EXPERTS_PER_GROUP + i2)

    rr = lax.broadcasted_iota(jnp.int32, (tm, tm), 0)
    cc = lax.broadcasted_iota(jnp.int32, (tm, tm), 1)
    strict_upper = jnp.where(rr < cc, 1.0, 0.0).astype(BF16)
    erow = lax.broadcasted_iota(jnp.int32, (N_EXPERTS, tm), 0)
    base = carry_ref[:, 0:1]
    ranks = []
    for k in range(2):
        onehot = jnp.where(erow == eid[k], 1.0, 0.0)
        pre = jnp.dot(onehot.astype(BF16), strict_upper, preferred_element_type=F32)
        ranks.append(jnp.sum(onehot * (base + pre), axis=0, keepdims=True))
        base = base + jnp.sum(onehot, axis=1, keepdims=True)
    carry_ref[...] = jnp.broadcast_to(base, carry_ref.shape)
    cnt_ref[...] = jnp.broadcast_to(base, cnt_ref.shape)

    meta_ref[0:1, :] = eid[0].astype(F32)
    meta_ref[1:2, :] = eid[1].astype(F32)
    meta_ref[2:3, :] = ranks[0]
    meta_ref[3:4, :] = ranks[1]
    meta_ref[4:5, :] = w1
    meta_ref[5:6, :] = w2
    meta_ref[6:8, :] = jnp.zeros((2, tm), F32)


_ROUTE_OUT_SPECS = lambda d: [pl.BlockSpec((TM, d), lambda i: (i, 0)),
                              pl.BlockSpec((TM, d), lambda i: (i, 0)),
                              pl.BlockSpec((SUBLANES, TM), lambda i: (0, i)),
                              _full((N_EXPERTS, LANES))]


def _route_out_shapes(t, d):
    return [jax.ShapeDtypeStruct((t, d), F32), jax.ShapeDtypeStruct((t, d), F32),
            jax.ShapeDtypeStruct((SUBLANES, t), F32), jax.ShapeDtypeStruct((N_EXPERTS, LANES), F32)]


def _router_weights(w_grp, b_grp, w_exp, b_exp):
    d = w_grp.shape[0]
    rows = SUBLANES + N_EXPERTS
    wrt = jnp.zeros((rows, d), F32).at[:N_GROUPS].set(w_grp.T).at[SUBLANES:].set(w_exp.T)
    br = jnp.zeros((rows,), F32).at[:N_GROUPS].set(b_grp).at[SUBLANES:].set(b_exp)
    return wrt, jnp.broadcast_to(br[:, None], (rows, LANES))


def _outproj0_kernel(x_ref, ctx_ref, gate_ref, hf_ref, hb_ref, w_ref, gt_ref, g2_ref, sh_ref, sc_ref,
                     wrt_ref, br_ref, x1_ref, h2_ref, meta_ref, cnt_ref, carry_ref):
    is_ctx = pl.program_id(0) == 0
    xin = jnp.where(is_ctx, ctx_ref[...], x_ref[...])
    y = _gelu_tanh(gate_ref[...]) * (hf_ref[...] + hb_ref[...])
    y = jnp.dot(y.astype(BF16), w_ref[...], preferred_element_type=F32)
    x1 = xin + _mod_row(gt_ref, is_ctx) * y
    x1_ref[...] = x1
    _route(x1, g2_ref, _mod_row(sh_ref, is_ctx), _mod_row(sc_ref, is_ctx), wrt_ref, br_ref,
           carry_ref, h2_ref, meta_ref, cnt_ref)


def _outproj0(x, ctx, gate, hf, hb, w_out_bf, mod, g2, wrt, br):
    n, d = x.shape
    nt = n // TM + 1
    t = nt * TM
    tok = pl.BlockSpec((TM, D_RNN), lambda i: (i, 0))
    modspec = lambda c: pl.BlockSpec((SUBLANES, d), lambda i: (0, c))
    return pl.pallas_call(
        _outproj0_kernel,
        grid=(nt,),
        in_specs=[pl.BlockSpec((TM, d), lambda i: (jnp.maximum(i - 1, 0), 0)),
                  _full((TM, d)),
                  tok, tok, tok,
                  _full((D_RNN, d)),
                  modspec(2), _full((1, d)), modspec(3), modspec(4),
                  _full(wrt.shape), _full(br.shape)],
        out_specs=_ROUTE_OUT_SPECS(d),
        out_shape=_route_out_shapes(t, d),
        scratch_shapes=[pltpu.VMEM((N_EXPERTS, LANES), F32)],
        compiler_params=_params(),
        name="outproj0_route",
    )(x, ctx, gate, hf, hb, w_out_bf, mod, g2.reshape(1, -1), mod, mod, wrt, br)


def _plan_kernel(n_tiles, nbp, cnt_ref, meta_ref, dest_ref, blk_ref):
    c = cnt_ref[...]
    padded = jnp.floor((c + (TB - 1)) * (1.0 / TB)) * TB
    r = lax.broadcasted_iota(jnp.int32, (N_EXPERTS, N_EXPERTS), 0)
    q = lax.broadcasted_iota(jnp.int32, (N_EXPERTS, N_EXPERTS), 1)
    lower = jnp.where(q <= r, 1.0, 0.0)
    pad_end = jnp.dot(lower, padded, precision=HI, preferred_element_type=F32)
    pad_start = pad_end - padded

    first_row = lax.broadcasted_iota(jnp.int32, (N_EXPERTS, nbp), 1).astype(F32) * TB
    owner = jnp.sum(jnp.where(pad_end[:, 0:1] <= first_row, 1.0, 0.0), axis=0, keepdims=True)
    blk_ref[0:1, :] = jnp.minimum(owner, N_EXPERTS - 1).astype(jnp.int32)
    n_used = pad_end[N_EXPERTS - 1:N_EXPERTS, 0:1] * (1.0 / TB)
    blk_ref[1:2, :] = jnp.broadcast_to(n_used, (1, nbp)).astype(jnp.int32)
    ends = jnp.concatenate([pad_end, jnp.zeros((LANES - N_EXPERTS, LANES), F32)], axis=0).T[0:1, :]
    blk_ref[2:3, :] = jnp.concatenate([ends, jnp.zeros((1, nbp - LANES), F32)], axis=1).astype(jnp.int32)
    blk_ref[3:SUBLANES, :] = jnp.zeros((SUBLANES - 3, nbp), jnp.int32)

    erow = lax.broadcasted_iota(jnp.int32, (N_EXPERTS, TM), 0).astype(F32)

    def body(i, carry):
        ln = pl.ds(pl.multiple_of(i * TM, TM), TM)
        for k in range(2):
            onehot = jnp.where(erow == meta_ref[k:k + 1, ln], 1.0, 0.0)
            start = jnp.sum(onehot * pad_start[:, 0:1], axis=0, keepdims=True)
            dest_ref[k:k + 1, ln] = (start + meta_ref[2 + k:3 + k, ln]).astype(jnp.int32)
        dest_ref[2:SUBLANES, ln] = jnp.zeros((SUBLANES - 2, TM), jnp.int32)
        return carry

    lax.fori_loop(0, n_tiles, body, 0)


def _plan(cnt, meta):
    t = meta.shape[1]
    n_blocks = (2 * t + N_EXPERTS * TB) // TB
    nbp = -(-n_blocks // LANES) * LANES
    vm = pl.BlockSpec(memory_space=pltpu.VMEM)
    dest, blk = pl.pallas_call(
        functools.partial(_plan_kernel, t // TM, nbp),
        in_specs=[vm, vm],
        out_specs=[vm, vm],
        out_shape=[jax.ShapeDtypeStruct((SUBLANES, t), jnp.int32),
                   jax.ShapeDtypeStruct((SUBLANES, nbp), jnp.int32)],
        compiler_params=pltpu.CompilerParams(vmem_limit_bytes=VMEM_LIMIT),
        name="moe_plan",
    )(cnt, meta)
    return dest, blk, n_blocks


def _row_wait(src_ref, dst_ref, sem):
    pltpu.make_async_copy(src_ref.at[pl.ds(0, TM), :], dst_ref.at[pl.ds(0, TM), :], sem).wait()


def _dispatch_kernel(n_blocks, pe_ref, nu_ref, dest_ref, h2_ref, buf_ref, dsm_ref, zero_ref,
                     sem_idx, sem_zero, sems):
    step = pl.program_id(0)
    nt = pl.num_programs(0)

    def zero_block(first_row):
        return pltpu.make_async_copy(zero_ref, buf_ref.at[pl.ds(pl.multiple_of(first_row, TB), TB), :],
                                     sem_zero)

    @pl.when(step == 0)
    def _():
        zero_ref[...] = jnp.zeros_like(zero_ref)

        def for_segments(fn):
            for e in range(N_EXPERTS):
                seg_start = pe_ref[e - 1] if e else 0

                @pl.when(pe_ref[e] > seg_start)
                def _():
                    fn(zero_block(pe_ref[e] - TB))

        for_segments(lambda cp: cp.start())
        lax.fori_loop(nu_ref[0], n_blocks, lambda b, c: (zero_block(b * TB).start(), c)[1], 0)
        for_segments(lambda cp: cp.wait())
        lax.fori_loop(nu_ref[0], n_blocks, lambda b, c: (zero_block(b * TB).wait(), c)[1], 0)

    cp = pltpu.make_async_copy(dest_ref, dsm_ref, sem_idx)
    cp.start()
    cp.wait()
    sem = sems.at[step % 2]
    base = step * TM

    def body(r, carry):
        for k in range(2):
            pltpu.make_async_copy(h2_ref.at[pl.ds(base + r, 1), :],
                                  buf_ref.at[pl.ds(dsm_ref[k, r], 1), :], sem).start(priority=k)
        return carry

    lax.fori_loop(0, TM, body, 0, unroll=8)

    @pl.when(step > 0)
    def _():
        for k in range(2):
            _row_wait(h2_ref, buf_ref, sems.at[(step + 1) % 2])

    @pl.when(step == nt - 1)
    def _():
        for k in range(2):
            _row_wait(h2_ref, buf_ref, sem)


def _dispatch(h2, dest, pad_end, n_used, n_blocks):
    t, d = h2.shape
    n_rows = n_blocks * TB
    grid_spec = pltpu.PrefetchScalarGridSpec(
        num_scalar_prefetch=2,
        grid=(t // TM,),
        in_specs=[pl.BlockSpec((SUBLANES, TM), lambda i, pe, nu: (0, i)),
                  pl.BlockSpec(memory_space=pl.ANY)],
        out_specs=pl.BlockSpec(memory_space=pl.ANY),
        scratch_shapes=[pltpu.SMEM((SUBLANES, TM), jnp.int32),
                        pltpu.VMEM((TB, d), F32),
                        pltpu.SemaphoreType.DMA, pltpu.SemaphoreType.DMA,
                        pltpu.SemaphoreType.DMA((2,))],
    )
    return pl.pallas_call(
        functools.partial(_dispatch_kernel, n_blocks),
        grid_spec=grid_spec,
        out_shape=jax.ShapeDtypeStruct((n_rows, d), F32),
        compiler_params=_params(),
        name="moe_dispatch",
    )(pad_end, n_used, dest, h2)


def _experts_kernel(be_ref, nu_ref, x_ref, wg_ref, wu_ref, wd_ref, y_ref, wg_bf, wu_bf, wd_bf):
    step = pl.program_id(0)
    used = step < nu_ref[0]
    new_expert = jnp.logical_or(step == 0, be_ref[step] != be_ref[jnp.maximum(step - 1, 0)])

    @pl.when(jnp.logical_and(used, new_expert))
    def _():
        wg_bf[...] = wg_ref[0].astype(BF16)
        wu_bf[...] = wu_ref[0].astype(BF16)
        wd_bf[...] = wd_ref[0].astype(BF16)

    @pl.when(used)
    def _():
        xb = x_ref[...].astype(BF16)
        g = jnp.dot(xb, wg_bf[...], preferred_element_type=F32)
        u = jnp.dot(xb, wu_bf[...], preferred_element_type=F32)
        a = (g * _sigmoid(g)) * u
        y_ref[...] = jnp.dot(a.astype(BF16), wd_bf[...], preferred_element_type=F32)

    @pl.when(jnp.logical_not(used))
    def _():
        y_ref[...] = jnp.zeros_like(y_ref)


def _experts(buf, blk_e, n_used, wg, wu, wd, n_blocks):
    d = buf.shape[1]
    last = lambda i, nu: jnp.minimum(i, nu[0] - 1)
    grid_spec = pltpu.PrefetchScalarGridSpec(
        num_scalar_prefetch=2,
        grid=(n_blocks,),
        in_specs=[pl.BlockSpec((TB, d), lambda i, be, nu: (last(i, nu), 0)),
                  pl.BlockSpec((1, d, D_EXPERT), lambda i, be, nu: (be[last(i, nu)], 0, 0)),
                  pl.BlockSpec((1, d, D_EXPERT), lambda i, be, nu: (be[last(i, nu)], 0, 0)),
                  pl.BlockSpec((1, D_EXPERT, d), lambda i, be, nu: (be[last(i, nu)], 0, 0))],
        out_specs=pl.BlockSpec((TB, d), lambda i, be, nu: (i, 0)),
        scratch_shapes=[pltpu.VMEM((d, D_EXPERT), BF16), pltpu.VMEM((d, D_EXPERT), BF16),
                        pltpu.VMEM((D_EXPERT, d), BF16)],
    )
    return pl.pallas_call(
        _experts_kernel,
        grid_spec=grid_spec,
        out_shape=jax.ShapeDtypeStruct(buf.shape, F32),
        compiler_params=_params(),
        name="moe_experts",
    )(blk_e, n_used, buf, wg, wu, wd)


def _gather_start(dest_ref, dsm_ref, ybuf_ref, rows_ref, slot, sem_idx, sem):
    cp = pltpu.make_async_copy(dest_ref, dsm_ref, sem_idx)
    cp.start()
    cp.wait()

    def body(r, carry):
        for k in range(2):
            pltpu.make_async_copy(ybuf_ref.at[pl.ds(dsm_ref[k, r], 1), :],
                                  rows_ref.at[slot, k, pl.ds(r, 1), :], sem).start(priority=k)
        return carry

    lax.fori_loop(0, TM, body, 0, unroll=8)


def _gathered_rows(dest_ref, next_ref, dsm_ref, ybuf_ref, rows_ref, sem_idx, sems):
    step = pl.program_id(0)
    nt = pl.num_programs(0)
    slot = step % 2

    @pl.when(step == 0)
    def _():
        _gather_start(dest_ref, dsm_ref, ybuf_ref, rows_ref, 0, sem_idx, sems.at[0])

    @pl.when(step + 1 < nt)
    def _():
        _gather_start(next_ref, dsm_ref, ybuf_ref, rows_ref, 1 - slot, sem_idx, sems.at[1 - slot])

    for k in range(2):
        _row_wait(ybuf_ref, rows_ref.at[slot, k], sems.at[slot])
    return rows_ref[slot, 0], rows_ref[slot, 1]


def _token_weights(meta_ref):
    meta = jnp.concatenate([meta_ref[...], jnp.zeros((LANES - SUBLANES, TM), F32)], axis=0)
    mt = meta.T
    return mt[:, 4:5], mt[:, 5:6]


_COMBINE_SCRATCH = lambda d: [pltpu.SMEM((SUBLANES, TM), jnp.int32),
                              pltpu.VMEM((2, 2, TM, d), F32),
                              pltpu.SemaphoreType.DMA, pltpu.SemaphoreType.DMA((2,))]


def _next_tile_spec(nt):
    return pl.BlockSpec((SUBLANES, TM), lambda i: (0, jnp.minimum(i + 1, nt - 1)))


def _combine0_kernel(x1_ref, dest_ref, next_ref, meta_ref, gt_ref, ybuf_ref, x2_ref, ctx2_ref,
                     dsm_ref, rows_ref, sem_idx, sems):
    is_ctx = pl.program_id(0) == 0
    y0, y1 = _gathered_rows(dest_ref, next_ref, dsm_ref, ybuf_ref, rows_ref, sem_idx, sems)
    w0, w1 = _token_weights(meta_ref)
    out = x1_ref[...] + _mod_row(gt_ref, is_ctx) * (w0 * y0 + w1 * y1)
    x2_ref[...] = out

    @pl.when(is_ctx)
    def _():
        ctx2_ref[...] = out


def _combine0(x1, dest, meta, mod, ybuf):
    t, d = x1.shape
    nt = t // TM
    return pl.pallas_call(
        _combine0_kernel,
        grid=(nt,),
        in_specs=[pl.BlockSpec((TM, d), lambda i: (i, 0)),
                  pl.BlockSpec((SUBLANES, TM), lambda i: (0, i)),
                  _next_tile_spec(nt),
                  pl.BlockSpec((SUBLANES, TM), lambda i: (0, i)),
                  pl.BlockSpec((SUBLANES, d), lambda i: (0, 5)),
                  pl.BlockSpec(memory_space=pl.ANY)],
        out_specs=[pl.BlockSpec((TM, d), lambda i: (jnp.maximum(i - 1, 0), 0)),
                   _full((TM, d))],
        out_shape=[jax.ShapeDtypeStruct((t - TM, d), F32), jax.ShapeDtypeStruct((TM, d), F32)],
        scratch_shapes=_COMBINE_SCRATCH(d),
        compiler_params=_params(),
        name="moe_combine0",
    )(x1, dest, dest, meta, mod, ybuf)


def _combine1_kernel(x1_ref, dest_ref, next_ref, meta_ref, gt_ref, gf_ref, ybuf_ref, o_ref,
                     dsm_ref, rows_ref, sem_idx, sems):
    y0, y1 = _gathered_rows(dest_ref, next_ref, dsm_ref, ybuf_ref, rows_ref, sem_idx, sems)
    w0, w1 = _token_weights(meta_ref)
    out = x1_ref[...] + gt_ref[0:1, :] * (w0 * y0 + w1 * y1)
    o_ref[...] = _rms(out, gf_ref[...])


def _combine1(x1, dest, meta, mod, gf, ybuf):
    t, d = x1.shape
    rows = t // GRID_W
    nt = t // TM
    out = pl.pallas_call(
        _combine1_kernel,
        grid=(nt,),
        in_specs=[pl.BlockSpec((TM, d), lambda i: (i, 0)),
                  pl.BlockSpec((SUBLANES, TM), lambda i: (0, i)),
                  _next_tile_spec(nt),
                  pl.BlockSpec((SUBLANES, TM), lambda i: (0, i)),
                  pl.BlockSpec((SUBLANES, d), lambda i: (0, 5)),
                  _full((1, d)),
                  pl.BlockSpec(memory_space=pl.ANY)],
        out_specs=pl.BlockSpec((TM, d), lambda i: (0, i)),
        out_shape=jax.ShapeDtypeStruct((rows, GRID_W * d), F32),
        scratch_shapes=_COMBINE_SCRATCH(d),
        compiler_params=_params(),
        name="moe_combine1",
    )(x1, dest, dest, meta, mod, gf.reshape(1, -1), ybuf)
    return out.reshape(t, d)


def _moe(h2, meta, cnt, w_gate, w_up, w_down):
    dest, blk, n_blocks = _plan(cnt, meta)
    n_used = blk[1, :1]
    buf = _dispatch(h2, dest, blk[2, :N_EXPERTS], n_used, n_blocks)
    ybuf = _experts(buf, blk[0], n_used, w_gate, w_up, w_down, n_blocks)
    return dest, ybuf


def _inproj1_kernel(x_ref, ctx_ref, g_ref, sh_ref, sc_ref, wq_ref, wkt_ref, wv_ref, wo_ref,
                    wg_ref, wgt_ref, bg_ref, bgt_ref,
                    q_ref, kt_ref, v_ref, o_ref, gcol_ref, grow_ref):
    is_ctx = pl.program_id(0) == 0
    xin = jnp.where(is_ctx, ctx_ref[...], x_ref[...])
    h = _rms(xin, g_ref[...]) * (1.0 + _mod_row(sc_ref, is_ctx)) + _mod_row(sh_ref, is_ctx)
    hb = h.astype(BF16)
    q = jnp.dot(hb, wq_ref[...], preferred_element_type=F32) * (ML_DQK ** -0.5)
    q_ref[...] = q.astype(BF16)
    kt = lax.dot_general(wkt_ref[...], hb, (((1,), (1,)), ((), ())), preferred_element_type=F32)
    kt_ref[...] = kt.astype(BF16)
    v_ref[...] = jnp.dot(hb, wv_ref[...], preferred_element_type=F32).astype(BF16)
    o_ref[...] = jnp.dot(hb, wo_ref[...], preferred_element_type=F32)
    gcol_ref[...] = jnp.dot(h, wg_ref[...], precision=HI, preferred_element_type=F32) + bg_ref[...]
    grow_ref[...] = lax.dot_general(wgt_ref[...], h, (((1,), (1,)), ((), ())), precision=HI,
                                    preferred_element_type=F32) + bgt_ref[:, 0:1]


def _inproj1(x, ctx, g1, mod, w_in, b_gates):
    n, d = x.shape
    rows = n // GRID_W
    nt = GRID_W + 1
    t = nt * TM
    ng = 4 * ML_HEADS
    wq = w_in[:, :ML_QK_W].astype(BF16)
    wkt = w_in[:, ML_QK_W:2 * ML_QK_W].T.astype(BF16)
    wv = w_in[:, 2 * ML_QK_W:2 * ML_QK_W + ML_V_W].astype(BF16)
    wo = w_in[:, 2 * ML_QK_W + ML_V_W:2 * ML_QK_W + ML_V_W + d].astype(BF16)
    wg = jnp.zeros((d, LANES), F32).at[:, :ng].set(w_in[:, -ng:])
    wgt = w_in[:, -ng:].T
    bg = jnp.zeros((1, LANES), F32).at[0, :ng].set(b_gates)
    bgt = jnp.broadcast_to(b_gates[:, None], (ng, LANES))
    tok = lambda w: pl.BlockSpec((TM, w), lambda i: (i, 0))
    return pl.pallas_call(
        _inproj1_kernel,
        grid=(nt,),
        in_specs=[pl.BlockSpec((rows, d), lambda i: (0, jnp.maximum(i - 1, 0))),
                  _full((TM, d)),
                  _full((1, d)),
                  pl.BlockSpec((SUBLANES, d), lambda i: (0, 0)),
                  pl.BlockSpec((SUBLANES, d), lambda i: (0, 1)),
                  _full(wq.shape), _full(wkt.shape), _full(wv.shape), _full(wo.shape),
                  _full(wg.shape), _full(wgt.shape), _full(bg.shape), _full(bgt.shape)],
        out_specs=[tok(ML_QK_W),
                   pl.BlockSpec((ML_QK_W, TM), lambda i: (0, i)),
                   tok(ML_V_W), tok(d), tok(LANES),
                   pl.BlockSpec((ng, TM), lambda i: (0, i))],
        out_shape=[jax.ShapeDtypeStruct((t, ML_QK_W), BF16),
                   jax.ShapeDtypeStruct((ML_QK_W, t), BF16),
                   jax.ShapeDtypeStruct((t, ML_V_W), BF16),
                   jax.ShapeDtypeStruct((t, d), F32),
                   jax.ShapeDtypeStruct((t, LANES), F32),
                   jax.ShapeDtypeStruct((ng, t), F32)],
        compiler_params=_params(),
        name="inproj1",
    )(x.reshape(rows, GRID_W * d), ctx, g1.reshape(1, -1), mod, mod, wq, wkt, wv, wo, wg, wgt, bg, bgt)


def _log_sigmoid(x):
    return jnp.minimum(x, 0.0) - jnp.log1p(jnp.exp(-jnp.abs(x)))


def _mlstm_kernel(reverse, q_ref, kt_ref, v_ref, gcol_ref, grow_ref, h_ref, c_ref, m_ref):
    @pl.when(pl.program_id(0) == 0)
    def _():
        c_ref[...] = jnp.zeros_like(c_ref)
        m_ref[...] = jnp.zeros_like(m_ref)

    L = TM
    gi = 2 * ML_HEADS if reverse else 0
    gf = gi + ML_HEADS
    end = 0 if reverse else L - 1
    rr = lax.broadcasted_iota(jnp.int32, (L, L), 0)
    cc = lax.broadcasted_iota(jnp.int32, (L, L), 1)
    allowed = (cc >= rr) if reverse else (cc <= rr)
    tri = jnp.where(allowed, 1.0, 0.0)
    lf_col = _log_sigmoid(gcol_ref[...])
    lf_row = _log_sigmoid(grow_ref[gf:gf + ML_HEADS, :])
    cum_col = jnp.dot(tri, lf_col, precision=HI, preferred_element_type=F32)
    cum_row = lax.dot_general(lf_row, tri, (((1,), (1,)), ((), ())), precision=HI,
                              preferred_element_type=F32)
    ones_col = jnp.where(lax.broadcasted_iota(jnp.int32, (L, ML_DV), 1) == 0, 1.0, 0.0).astype(BF16)

    for hd in range(ML_HEADS):
        m_prev = m_ref[hd:hd + 1, 0:1]
        g_col = cum_col[:, gf + hd:gf + hd + 1]
        g_row = cum_row[hd:hd + 1, :]
        li_row = grow_ref[gi + hd:gi + hd + 1, :]
        total = g_col[end:end + 1, :]

        log_w = jnp.where(allowed, g_col - g_row + li_row, NEG_INF)
        log_inter = g_col + m_prev
        m_t = jnp.maximum(log_inter, jnp.max(log_w, axis=-1, keepdims=True))
        w_intra = jnp.exp(log_w - m_t)
        w_inter = jnp.exp(log_inter - m_t)

        qh = q_ref[:, hd * ML_DQK:(hd + 1) * ML_DQK]
        kth = kt_ref[hd * ML_DQK:(hd + 1) * ML_DQK, :]
        vext = jnp.concatenate([v_ref[:, hd * ML_DV:(hd + 1) * ML_DV], ones_col], axis=1)
        s = jnp.dot(qh, kth, preferred_element_type=F32) * w_intra
        state = c_ref[hd]
        tot = (w_inter * jnp.dot(qh, state.astype(BF16), preferred_element_type=F32)
               + jnp.dot(s.astype(BF16), vext, preferred_element_type=F32))
        den = tot[:, ML_DV:ML_DV + 1]
        h_ref[:, hd * ML_DV:(hd + 1) * ML_DV] = tot[:, :ML_DV] / jnp.maximum(jnp.abs(den), jnp.exp(-m_t))

        m_new = m_t[end:end + 1, :]
        w_state = jnp.exp(total - g_row + li_row - m_new)
        decay = jnp.exp(total + m_prev - m_new)
        kw = (kth.astype(F32) * w_state).astype(BF16)
        c_ref[hd] = decay * state + jnp.dot(kw, vext, preferred_element_type=F32)
        m_ref[hd:hd + 1, :] = jnp.broadcast_to(m_new, (1, LANES))


def _mlstm(q, kt, v, gcol, grow, reverse):
    t = q.shape[0]
    nt = t // TM
    tile = lambda s: _scan_tile(reverse, s, nt)
    return pl.pallas_call(
        functools.partial(_mlstm_kernel, reverse),
        grid=(nt,),
        in_specs=[pl.BlockSpec((TM, ML_QK_W), lambda s: (tile(s), 0)),
                  pl.BlockSpec((ML_QK_W, TM), lambda s: (0, tile(s))),
                  pl.BlockSpec((TM, ML_V_W), lambda s: (tile(s), 0)),
                  pl.BlockSpec((TM, LANES), lambda s: (tile(s), 0)),
                  pl.BlockSpec((4 * ML_HEADS, TM), lambda s: (0, tile(s)))],
        out_specs=pl.BlockSpec((TM, ML_V_W), lambda s: (tile(s), 0)),
        out_shape=jax.ShapeDtypeStruct((t, ML_V_W), F32),
        scratch_shapes=[pltpu.VMEM((ML_HEADS, ML_DQK, 2 * ML_DV), F32),
                        pltpu.VMEM((ML_HEADS, LANES), F32)],
        compiler_params=_params(),
        name="mlstm_bwd" if reverse else "mlstm_fwd",
    )(q, kt, v, gcol, grow)


def _finish1_kernel(x_ref, hf_ref, hb_ref, o_ref, ng_ref, w_ref, gt_ref, g2_ref, sh_ref, sc_ref,
                    wrt_ref, br_ref, x1_ref, h2_ref, meta_ref, cnt_ref, carry_ref):
    hs = hf_ref[...] + hb_ref[...]
    parts = []
    for hd in range(ML_HEADS):
        blk = hs[:, hd * ML_DV:(hd + 1) * ML_DV]
        parts.append(blk * lax.rsqrt(jnp.mean(blk * blk, axis=-1, keepdims=True) + EPS))
    hn = jnp.concatenate(parts, axis=1) * ng_ref[...]
    y = jnp.dot((hn * _sigmoid(o_ref[...])).astype(BF16), w_ref[...], preferred_element_type=F32)
    x1 = x_ref[...] + gt_ref[0:1, :] * y
    x1_ref[...] = x1
    _route(x1, g2_ref, sh_ref[0:1, :], sc_ref[0:1, :], wrt_ref, br_ref,
           carry_ref, h2_ref, meta_ref, cnt_ref)


def _finish1(x, hf, hb, o, norm_g, w_out_bf, mod, g2, wrt, br):
    n, d = x.shape
    rows = n // GRID_W
    lat = lambda w: pl.BlockSpec((TM, w), lambda i: (i + 1, 0))
    modspec = lambda c: pl.BlockSpec((SUBLANES, d), lambda i: (0, c))
    return pl.pallas_call(
        _finish1_kernel,
        grid=(GRID_W,),
        in_specs=[pl.BlockSpec((rows, d), lambda i: (0, i)),
                  lat(ML_V_W), lat(ML_V_W), lat(d),
                  _full((1, ML_V_W)), _full((ML_V_W, d)),
                  modspec(2), _full((1, d)), modspec(3), modspec(4),
                  _full(wrt.shape), _full(br.shape)],
        out_specs=_ROUTE_OUT_SPECS(d),
        out_shape=_route_out_shapes(n, d),
        scratch_shapes=[pltpu.VMEM((N_EXPERTS, LANES), F32)],
        compiler_params=_params(),
        name="finish1_route",
    )(x.reshape(rows, GRID_W * d), hf, hb, o, norm_g.reshape(1, -1), w_out_bf, mod,
      g2.reshape(1, -1), mod, mod, wrt, br)


def kernel(x, c, ctx, c_ctx,
           l0_ada_w, l0_ada_b, l0_norm1_g, l0_norm2_g,
           l0_rg_w_in, l0_rg_conv_w, l0_rg_conv_b, l0_rg_w_a, l0_rg_b_a, l0_rg_w_x, l0_rg_b_x,
           l0_rg_lambda, l0_rg_w_out,
           l0_moe_w_grp, l0_moe_b_grp, l0_moe_w_exp, l0_moe_b_exp, l0_moe_w_gate, l0_moe_w_up,
           l0_moe_w_down,
           l1_ada_w, l1_ada_b, l1_norm1_g, l1_norm2_g,
           l1_ml_w_in, l1_ml_b_gates, l1_ml_norm_g, l1_ml_w_out,
           l1_moe_w_grp, l1_moe_b_grp, l1_moe_w_exp, l1_moe_b_exp, l1_moe_w_gate, l1_moe_w_up,
           l1_moe_w_down,
           final_norm_g):
    assert x.shape[0] == 1 and ctx.shape[1] == TM and x.shape[1] == GRID_W * TM
    xs, cs = x[0], ctx[0]
    d = xs.shape[1]
    cond8 = jnp.zeros((SUBLANES, d), F32).at[0].set(c[0]).at[1].set(c_ctx)

    mod0 = _adaln(cond8, l0_ada_w, l0_ada_b)
    gate, rec = _inproj0(xs, cs, l0_norm1_g, mod0, l0_rg_w_in.astype(BF16))
    conv_w8 = jnp.zeros((SUBLANES, D_RNN), F32).at[:CONV_W].set(l0_rg_conv_w)
    h_dirs = []
    for dr in range(2):
        wcat = jnp.concatenate([l0_rg_w_a[dr], l0_rg_w_x[dr]], axis=-1).astype(BF16)
        h_dirs.append(_rglru_scan(rec, dr == 1, conv_w8, l0_rg_conv_b, wcat,
                                  l0_rg_b_a[dr], l0_rg_b_x[dr], l0_rg_lambda[dr]))
    wrt0, br0 = _router_weights(l0_moe_w_grp, l0_moe_b_grp, l0_moe_w_exp, l0_moe_b_exp)
    x1, h2, meta, cnt = _outproj0(xs, cs, gate, h_dirs[0], h_dirs[1], l0_rg_w_out.astype(BF16),
                                  mod0, l0_norm2_g, wrt0, br0)
    dest, ybuf = _moe(h2, meta, cnt, l0_moe_w_gate, l0_moe_w_up, l0_moe_w_down)
    x2, ctx2 = _combine0(x1, dest, meta, mod0, ybuf)

    mod1 = _adaln(cond8, l1_ada_w, l1_ada_b)
    q, kt, v, o, gcol, grow = _inproj1(x2, ctx2, l1_norm1_g, mod1, l1_ml_w_in, l1_ml_b_gates)
    hf = _mlstm(q, kt, v, gcol, grow, False)
    hb = _mlstm(q, kt, v, gcol, grow, True)
    wrt1, br1 = _router_weights(l1_moe_w_grp, l1_moe_b_grp, l1_moe_w_exp, l1_moe_b_exp)
    x1, h2, meta, cnt = _finish1(x2, hf, hb, o, l1_ml_norm_g, l1_ml_w_out.astype(BF16), mod1,
                                 l1_norm2_g, wrt1, br1)
    dest, ybuf = _moe(h2, meta, cnt, l1_moe_w_gate, l1_moe_w_up, l1_moe_w_down)
    out = _combine1(x1, dest, meta, mod1, final_norm_g, ybuf)
    return out[None]
```

```python
import functools

import jax
import jax.numpy as jnp
from jax import lax
from jax.experimental import pallas as pl
from jax.experimental.pallas import tpu as pltpu

D_MODEL = 1024
GRID_W = 64
N_MOD = 6
EPS = 1e-6

D_RNN = 1280
RG_BLOCKS = 10
RG_BLOCK_W = D_RNN // RG_BLOCKS
CONV_W = 4
CONV_PAD_L = 2
RG_C = 8.0

ML_HEADS = 8
ML_DQK = D_MODEL // (2 * ML_HEADS)
ML_DV = D_MODEL // ML_HEADS
ML_QK_W = ML_HEADS * ML_DQK
ML_V_W = ML_HEADS * ML_DV

N_GROUPS = 4
EXPERTS_PER_GROUP = 8
N_EXPERTS = N_GROUPS * EXPERTS_PER_GROUP
D_EXPERT = 512

TM = 256
TB = 256
ROUTE_TILE_0 = 1280
ROUTE_TILE_1 = 1024
SUBLANES = 8
LANES = 128
VMEM_LIMIT = 48 * 1024 * 1024

F32 = jnp.float32
BF16 = jnp.bfloat16
HI = lax.Precision.HIGHEST
NEG_INF = float("-inf")


def _params(n_axes=1):
    return pltpu.CompilerParams(dimension_semantics=("arbitrary",) * n_axes,
                                vmem_limit_bytes=VMEM_LIMIT)


def _rms(x, g):
    return x * lax.rsqrt(jnp.mean(x * x, axis=-1, keepdims=True) + EPS) * g


def _sigmoid(x):
    return 1.0 / (1.0 + jnp.exp(-x))


def _softplus(x):
    return jnp.maximum(x, 0.0) + jnp.log1p(jnp.exp(-jnp.abs(x)))


def _gelu_tanh(x):
    return 0.5 * x * (1.0 + jnp.tanh(0.7978845608028654 * (x + 0.044715 * (x * x * x))))


def _full(shape):
    return pl.BlockSpec(shape, lambda *_: (0,) * len(shape))


def _adaln_kernel(cond_ref, w_ref, b_ref, o_ref):
    c = cond_ref[...]
    s = c * _sigmoid(c)
    o_ref[...] = jnp.dot(s, w_ref[...], precision=HI, preferred_element_type=F32) + b_ref[...]


def _adaln(cond8, w, b):
    d = w.shape[0]
    return pl.pallas_call(
        _adaln_kernel,
        grid=(N_MOD,),
        in_specs=[_full((SUBLANES, d)),
                  pl.BlockSpec((d, d), lambda j: (0, j)),
                  pl.BlockSpec((1, d), lambda j: (0, j))],
        out_specs=pl.BlockSpec((SUBLANES, d), lambda j: (0, j)),
        out_shape=jax.ShapeDtypeStruct((SUBLANES, N_MOD * d), F32),
        compiler_params=_params(),
        name="adaln",
    )(cond8, w, b.reshape(1, -1))


def _mod_row(ref, is_ctx):
    return jnp.where(is_ctx, ref[1:2, :], ref[0:1, :])


def _inproj0_kernel(nt, x_ref, ctx_ref, g_ref, sh_ref, sc_ref, w_ref, cw_ref, cb_ref,
                    gate_ref, u_ref, ext_ref):
    s = pl.program_id(0)
    is_ctx = s == 0

    @pl.when(s == 0)
    def _():
        ext_ref[...] = jnp.zeros_like(ext_ref)

    xin = jnp.where(is_ctx, ctx_ref[...], x_ref[...])
    h = _rms(xin, g_ref[...]) * (1.0 + _mod_row(sc_ref, is_ctx)) + _mod_row(sh_ref, is_ctx)
    p = jnp.dot(h.astype(BF16), w_ref[...], preferred_element_type=F32)
    gate_ref[...] = p[:, :D_RNN]
    rec = p[:, D_RNN:]

    right_valid = jnp.logical_and(s >= 2, s <= nt - 1)
    base = SUBLANES - CONV_PAD_L
    for j in range(RG_BLOCKS):
        ln = slice(j * RG_BLOCK_W, (j + 1) * RG_BLOCK_W)
        ext_ref[j, SUBLANES + TM:, :] = jnp.where(right_valid, rec[0:SUBLANES, ln], 0.0)
        u = cb_ref[:, ln] + ext_ref[j, pl.ds(base, TM), :] * cw_ref[0:1, ln]
        for k in range(1, CONV_W):
            u = u + ext_ref[j, pl.ds(base + k, TM), :] * cw_ref[k:k + 1, ln]
        u_ref[:, ln] = u
        ext_ref[j, 0:SUBLANES, :] = jnp.where(s >= 2, ext_ref[j, TM:TM + SUBLANES, :], 0.0)
        ext_ref[j, SUBLANES:SUBLANES + TM, :] = rec[:, ln]


def _inproj0(x, ctx, g1, mod, w_in_bf, conv_w8, conv_b):
    n, d = x.shape
    nx = n // TM
    nt = nx + 1
    t = nt * TM
    return pl.pallas_call(
        functools.partial(_inproj0_kernel, nt),
        grid=(nt + 1,),
        in_specs=[pl.BlockSpec((TM, d), lambda s: (jnp.clip(s - 1, 0, nx - 1), 0)),
                  _full((TM, d)),
                  _full((1, d)),
                  pl.BlockSpec((SUBLANES, d), lambda s: (0, 0)),
                  pl.BlockSpec((SUBLANES, d), lambda s: (0, 1)),
                  _full((d, 2 * D_RNN)),
                  _full((SUBLANES, D_RNN)), _full((1, D_RNN))],
        out_specs=[pl.BlockSpec((TM, D_RNN), lambda s: (jnp.minimum(s, nt - 1), 0)),
                   pl.BlockSpec((TM, D_RNN), lambda s: (jnp.maximum(s - 1, 0), 0))],
        out_shape=[jax.ShapeDtypeStruct((t, D_RNN), F32)] * 2,
        scratch_shapes=[pltpu.VMEM((RG_BLOCKS, TM + 2 * SUBLANES, LANES), F32)],
        compiler_params=_params(),
        name="inproj0",
    )(x, ctx, g1.reshape(1, -1), mod, mod, w_in_bf, conv_w8, conv_b.reshape(1, -1))


def _scan_tile(reverse, s, nt):
    if not reverse:
        return s
    return jnp.where(s == 0, 0, nt - s)


SCAN_CHUNK = TM // SUBLANES
SCAN_PITCH = SCAN_CHUNK + 4


def _rglru_kernel(reverse, u_ref, wcat_ref, ba_ref, bx_ref, lam_ref, h_ref,
                  a_ref, b_ref, o_ref, carry_ref):
    @pl.when(pl.program_id(0) == 0)
    def _():
        carry_ref[...] = jnp.zeros_like(carry_ref)

    steps = range(SCAN_CHUNK - 1, -1, -1) if reverse else range(SCAN_CHUNK)
    chunks = range(SUBLANES - 1, -1, -1) if reverse else range(SUBLANES)
    for j in range(RG_BLOCKS):
        ln = slice(j * RG_BLOCK_W, (j + 1) * RG_BLOCK_W)
        u = u_ref[:, ln]
        g = jnp.dot(u.astype(BF16), wcat_ref[j], preferred_element_type=F32)
        half_rate = (-0.5 * RG_C) * _softplus(-lam_ref[:, ln])
        log_a = jnp.tanh(0.5 * (g[:, :RG_BLOCK_W] + ba_ref[:, ln])) * half_rate + half_rate
        ig = 0.5 * jnp.tanh(0.5 * (g[:, RG_BLOCK_W:] + bx_ref[:, ln])) + 0.5
        a = jnp.exp(log_a)
        b = jnp.sqrt(-jnp.tanh(log_a) * (a * a + 1.0)) * ig * u
        for c in range(SUBLANES):
            a_ref[j, pl.ds(c * SCAN_PITCH, SCAN_CHUNK), :] = a[c * SCAN_CHUNK:(c + 1) * SCAN_CHUNK]
            b_ref[j, pl.ds(c * SCAN_PITCH, SCAN_CHUNK), :] = b[c * SCAN_CHUNK:(c + 1) * SCAN_CHUNK]

        row = lambda ref, i: ref[j, pl.ds(i, SUBLANES, stride=SCAN_PITCH), :]
        end = jnp.zeros((SUBLANES, LANES), F32)
        decay = jnp.ones((SUBLANES, LANES), F32)
        for i in steps:
            ai = row(a_ref, i)
            end = ai * end + row(b_ref, i)
            decay = decay * ai

        state = carry_ref[0:1, ln]
        entry = [None] * SUBLANES
        for c in chunks:
            entry[c] = state
            state = decay[c:c + 1] * state + end[c:c + 1]
        carry_ref[0:1, ln] = state

        hcur = jnp.concatenate(entry, axis=0)
        for i in steps:
            hcur = row(a_ref, i) * hcur + row(b_ref, i)
            o_ref[j, pl.ds(i, SUBLANES, stride=SCAN_PITCH), :] = hcur
        for c in range(SUBLANES):
            h_ref[c * SCAN_CHUNK:(c + 1) * SCAN_CHUNK, ln] = o_ref[j, pl.ds(c * SCAN_PITCH, SCAN_CHUNK), :]


def _rglru_scan(u, reverse, wcat, b_a, b_x, lam):
    t = u.shape[0]
    nt = t // TM
    tile_map = lambda s: (_scan_tile(reverse, s, nt), 0)
    row = lambda v: v.reshape(1, -1)
    slab = pltpu.VMEM((RG_BLOCKS, SUBLANES * SCAN_PITCH, LANES), F32)
    return pl.pallas_call(
        functools.partial(_rglru_kernel, reverse),
        grid=(nt,),
        in_specs=[pl.BlockSpec((TM, D_RNN), tile_map),
                  _full((RG_BLOCKS, RG_BLOCK_W, 2 * RG_BLOCK_W)),
                  _full((1, D_RNN)), _full((1, D_RNN)), _full((1, D_RNN))],
        out_specs=pl.BlockSpec((TM, D_RNN), tile_map),
        out_shape=jax.ShapeDtypeStruct((t, D_RNN), F32),
        scratch_shapes=[slab, slab, slab, pltpu.VMEM((SUBLANES, D_RNN), F32)],
        compiler_params=_params(),
        name="rglru_bwd" if reverse else "rglru_fwd",
    )(u, wcat, row(b_a), row(b_x), row(lam))


def _router_logits(h2, wr_ref, br_ref):
    hi = h2.astype(BF16)
    lo = (h2 - hi.astype(F32)).astype(BF16)
    acc = jnp.dot(hi, wr_ref[0], preferred_element_type=F32)
    acc = acc + jnp.dot(lo, wr_ref[0], preferred_element_type=F32)
    acc = acc + jnp.dot(hi, wr_ref[1], preferred_element_type=F32)
    return acc + br_ref[...]


def _route_kernel(lg_ref, meta_ref, cnt_ref, carry_ref):
    step = pl.program_id(0)
    tm = lg_ref.shape[0]

    @pl.when(step == 0)
    def _():
        carry_ref[...] = jnp.zeros_like(carry_ref)

    logits = jnp.concatenate([lg_ref[i * LANES:(i + 1) * LANES, :].T for i in range(tm // LANES)],
                             axis=1)
    row8 = lax.broadcasted_iota(jnp.int32, (SUBLANES, tm), 0)
    grp_logits = jnp.where(row8 < N_GROUPS, logits[0:SUBLANES], NEG_INF)
    gmax = jnp.max(grp_logits, axis=0, keepdims=True)
    p_sel = 1.0 / jnp.sum(jnp.exp(grp_logits - gmax), axis=0, keepdims=True)
    grp = jnp.min(jnp.where(grp_logits == gmax, row8, SUBLANES), axis=0, keepdims=True)

    in_grp = logits[SUBLANES + (N_GROUPS - 1) * EXPERTS_PER_GROUP:SUBLANES + N_GROUPS * EXPERTS_PER_GROUP]
    for gi in range(N_GROUPS - 2, -1, -1):
        lo = SUBLANES + gi * EXPERTS_PER_GROUP
        in_grp = jnp.where(grp == gi, logits[lo:lo + EXPERTS_PER_GROUP], in_grp)
    v1 = jnp.max(in_grp, axis=0, keepdims=True)
    i1 = jnp.min(jnp.where(in_grp == v1, row8, EXPERTS_PER_GROUP), axis=0, keepdims=True)
    rest = jnp.where(row8 == i1, NEG_INF, in_grp)
    v2 = jnp.max(rest, axis=0, keepdims=True)
    i2 = jnp.min(jnp.where(rest == v2, row8, EXPERTS_PER_GROUP), axis=0, keepdims=True)
    e2 = jnp.exp(v2 - v1)
    w1 = p_sel / (1.0 + e2)
    w2 = p_sel * e2 / (1.0 + e2)
    eid = (grp * EXPERTS_PER_GROUP + i1, grp * EXPERTS_PER_GROUP + i2)

    rr = lax.broadcasted_iota(jnp.int32, (LANES, LANES), 0)
    cc = lax.broadcasted_iota(jnp.int32, (LANES, LANES), 1)
    strict_upper = jnp.where(rr < cc, 1.0, 0.0).astype(BF16)
    erow = lax.broadcasted_iota(jnp.int32, (N_EXPERTS, LANES), 0)
    base = carry_ref[:, 0:1]
    for k in range(2):
        for i in range(tm // LANES):
            ln = slice(i * LANES, (i + 1) * LANES)
            onehot = jnp.where(erow == eid[k][:, ln], 1.0, 0.0)
            pre = jnp.dot(onehot.astype(BF16), strict_upper, preferred_element_type=F32)
            meta_ref[2 + k:3 + k, ln] = jnp.sum(onehot * (base + pre), axis=0, keepdims=True)
            base = base + jnp.sum(onehot, axis=1, keepdims=True)
    carry_ref[...] = jnp.broadcast_to(base, carry_ref.shape)
    cnt_ref[...] = jnp.broadcast_to(base, cnt_ref.shape)

    meta_ref[0:1, :] = eid[0].astype(F32)
    meta_ref[1:2, :] = eid[1].astype(F32)
    meta_ref[4:5, :] = w1
    meta_ref[5:6, :] = w2
    meta_ref[6:8, :] = jnp.zeros((2, tm), F32)


def _router_weights(w_grp, b_grp, w_exp, b_exp):
    d = w_grp.shape[0]
    wr = jnp.zeros((d, LANES), F32).at[:, :N_GROUPS].set(w_grp)
    wr = wr.at[:, SUBLANES:SUBLANES + N_EXPERTS].set(w_exp)
    br = jnp.zeros((1, LANES), F32).at[0, :N_GROUPS].set(b_grp)
    br = br.at[0, SUBLANES:SUBLANES + N_EXPERTS].set(b_exp)
    hi = wr.astype(BF16)
    lo = (wr - hi.astype(F32)).astype(BF16)
    return jnp.stack([hi, lo]), br


def _route(logits, tile):
    t = logits.shape[0]
    return pl.pallas_call(
        _route_kernel,
        grid=(t // tile,),
        in_specs=[pl.BlockSpec((tile, LANES), lambda i: (i, 0))],
        out_specs=[pl.BlockSpec((SUBLANES, tile), lambda i: (0, i)), _full((N_EXPERTS, LANES))],
        out_shape=[jax.ShapeDtypeStruct((SUBLANES, t), F32),
                   jax.ShapeDtypeStruct((N_EXPERTS, LANES), F32)],
        scratch_shapes=[pltpu.VMEM((N_EXPERTS, LANES), F32)],
        compiler_params=_params(),
        name="moe_route",
    )(logits)


def _outproj0_kernel(x_ref, ctx_ref, gate_ref, hf_ref, hb_ref, w_ref, gt_ref, g2_ref, sh_ref, sc_ref,
                     wr_ref, br_ref, x1_ref, h2_ref, lg_ref):
    is_ctx = pl.program_id(0) == 0
    xin = jnp.where(is_ctx, ctx_ref[...], x_ref[...])
    y = _gelu_tanh(gate_ref[...]) * (hf_ref[...] + hb_ref[...])
    y = jnp.dot(y.astype(BF16), w_ref[...], preferred_element_type=F32)
    x1 = xin + _mod_row(gt_ref, is_ctx) * y
    x1_ref[...] = x1
    h2 = _rms(x1, g2_ref[...]) * (1.0 + _mod_row(sc_ref, is_ctx)) + _mod_row(sh_ref, is_ctx)
    h2_ref[...] = h2
    lg_ref[...] = _router_logits(h2, wr_ref, br_ref)


_TOKEN_OUTS = lambda d: [pl.BlockSpec((TM, d), lambda i: (i, 0)), pl.BlockSpec((TM, d), lambda i: (i, 0)),
                         pl.BlockSpec((TM, LANES), lambda i: (i, 0))]


def _token_out_shapes(t, d):
    return [jax.ShapeDtypeStruct((t, d), F32), jax.ShapeDtypeStruct((t, d), F32),
            jax.ShapeDtypeStruct((t, LANES), F32)]


def _outproj0(x, ctx, gate, hf, hb, w_out_bf, mod, g2, wr, br):
    n, d = x.shape
    nt = n // TM + 1
    t = nt * TM
    tok = pl.BlockSpec((TM, D_RNN), lambda i: (i, 0))
    modspec = lambda c: pl.BlockSpec((SUBLANES, d), lambda i: (0, c))
    return pl.pallas_call(
        _outproj0_kernel,
        grid=(nt,),
        in_specs=[pl.BlockSpec((TM, d), lambda i: (jnp.maximum(i - 1, 0), 0)),
                  _full((TM, d)),
                  tok, tok, tok,
                  _full((D_RNN, d)),
                  modspec(2), _full((1, d)), modspec(3), modspec(4),
                  _full(wr.shape), _full(br.shape)],
        out_specs=_TOKEN_OUTS(d),
        out_shape=_token_out_shapes(t, d),
        compiler_params=_params(),
        name="outproj0",
    )(x, ctx, gate, hf, hb, w_out_bf, mod, g2.reshape(1, -1), mod, mod, wr, br)


def _plan_kernel(n_tiles, nbp, cnt_ref, meta_ref, dest_ref, blk_ref):
    c = cnt_ref[...]
    padded = jnp.floor((c + (TB - 1)) * (1.0 / TB)) * TB
    r = lax.broadcasted_iota(jnp.int32, (N_EXPERTS, N_EXPERTS), 0)
    q = lax.broadcasted_iota(jnp.int32, (N_EXPERTS, N_EXPERTS), 1)
    lower = jnp.where(q <= r, 1.0, 0.0)
    pad_end = jnp.dot(lower, padded, precision=HI, preferred_element_type=F32)
    pad_start = pad_end - padded

    first_row = lax.broadcasted_iota(jnp.int32, (N_EXPERTS, nbp), 1).astype(F32) * TB
    owner = jnp.sum(jnp.where(pad_end[:, 0:1] <= first_row, 1.0, 0.0), axis=0, keepdims=True)
    blk_ref[0:1, :] = jnp.minimum(owner, N_EXPERTS - 1).astype(jnp.int32)
    n_used = pad_end[N_EXPERTS - 1:N_EXPERTS, 0:1] * (1.0 / TB)
    blk_ref[1:2, :] = jnp.broadcast_to(n_used, (1, nbp)).astype(jnp.int32)
    ends = jnp.concatenate([pad_end, jnp.zeros((LANES - N_EXPERTS, LANES), F32)], axis=0).T[0:1, :]
    blk_ref[2:3, :] = jnp.concatenate([ends, jnp.zeros((1, nbp - LANES), F32)], axis=1).astype(jnp.int32)
    blk_ref[3:SUBLANES, :] = jnp.zeros((SUBLANES - 3, nbp), jnp.int32)

    erow = lax.broadcasted_iota(jnp.int32, (N_EXPERTS, TM), 0).astype(F32)

    def body(i, carry):
        ln = pl.ds(pl.multiple_of(i * TM, TM), TM)
        for k in range(2):
            onehot = jnp.where(erow == meta_ref[k:k + 1, ln], 1.0, 0.0)
            start = jnp.sum(onehot * pad_start[:, 0:1], axis=0, keepdims=True)
            dest_ref[k:k + 1, ln] = (start + meta_ref[2 + k:3 + k, ln]).astype(jnp.int32)
        dest_ref[2:SUBLANES, ln] = jnp.zeros((SUBLANES - 2, TM), jnp.int32)
        return carry

    lax.fori_loop(0, n_tiles, body, 0)


def _plan(cnt, meta):
    t = meta.shape[1]
    n_blocks = (2 * t + N_EXPERTS * TB) // TB
    nbp = -(-n_blocks // LANES) * LANES
    vm = pl.BlockSpec(memory_space=pltpu.VMEM)
    dest, blk = pl.pallas_call(
        functools.partial(_plan_kernel, t // TM, nbp),
        in_specs=[vm, vm],
        out_specs=[vm, vm],
        out_shape=[jax.ShapeDtypeStruct((SUBLANES, t), jnp.int32),
                   jax.ShapeDtypeStruct((SUBLANES, nbp), jnp.int32)],
        compiler_params=pltpu.CompilerParams(vmem_limit_bytes=VMEM_LIMIT),
        name="moe_plan",
    )(cnt, meta)
    return dest, blk, n_blocks


def _row_wait(src_ref, dst_ref, sem):
    pltpu.make_async_copy(src_ref.at[pl.ds(0, TM), :], dst_ref.at[pl.ds(0, TM), :], sem).wait()


def _dispatch_kernel(n_blocks, pe_ref, nu_ref, dest_ref, h2_ref, buf_ref, dsm_ref, zero_ref,
                     sem_idx, sem_zero, sem_rows):
    step = pl.program_id(0)

    def zero_block(first_row):
        return pltpu.make_async_copy(zero_ref, buf_ref.at[pl.ds(pl.multiple_of(first_row, TB), TB), :],
                                     sem_zero)

    @pl.when(step == 0)
    def _():
        zero_ref[...] = jnp.zeros_like(zero_ref)

        def for_segments(fn):
            for e in range(N_EXPERTS):
                seg_start = pe_ref[e - 1] if e else 0

                @pl.when(pe_ref[e] > seg_start)
                def _():
                    fn(zero_block(pe_ref[e] - TB))

        for_segments(lambda cp: cp.start())
        lax.fori_loop(nu_ref[0], n_blocks, lambda b, c: (zero_block(b * TB).start(), c)[1], 0)
        for_segments(lambda cp: cp.wait())
        lax.fori_loop(nu_ref[0], n_blocks, lambda b, c: (zero_block(b * TB).wait(), c)[1], 0)

    cp = pltpu.make_async_copy(dest_ref, dsm_ref, sem_idx)
    cp.start()
    cp.wait()

    def body(r, carry):
        for k in range(2):
            pltpu.make_async_copy(h2_ref.at[pl.ds(r, 1), :],
                                  buf_ref.at[pl.ds(dsm_ref[k, r], 1), :], sem_rows).start(priority=k)
        return carry

    lax.fori_loop(0, TM, body, 0, unroll=8)
    for k in range(2):
        _row_wait(h2_ref, buf_ref, sem_rows)


def _dispatch(h2, dest, pad_end, n_used, n_blocks):
    t, d = h2.shape
    n_rows = n_blocks * TB
    grid_spec = pltpu.PrefetchScalarGridSpec(
        num_scalar_prefetch=2,
        grid=(t // TM,),
        in_specs=[pl.BlockSpec((SUBLANES, TM), lambda i, pe, nu: (0, i)),
                  pl.BlockSpec((TM, d), lambda i, pe, nu: (i, 0))],
        out_specs=pl.BlockSpec(memory_space=pl.ANY),
        scratch_shapes=[pltpu.SMEM((SUBLANES, TM), jnp.int32),
                        pltpu.VMEM((TB, d), F32),
                        pltpu.SemaphoreType.DMA, pltpu.SemaphoreType.DMA,
                        pltpu.SemaphoreType.DMA],
    )
    return pl.pallas_call(
        functools.partial(_dispatch_kernel, n_blocks),
        grid_spec=grid_spec,
        out_shape=jax.ShapeDtypeStruct((n_rows, d), F32),
        compiler_params=_params(),
        name="moe_dispatch",
    )(pad_end, n_used, dest, h2)


def _experts_kernel(be_ref, nu_ref, x_ref, wg_ref, wu_ref, wd_ref, y_ref, wg_bf, wu_bf, wd_bf):
    step = pl.program_id(0)
    used = step < nu_ref[0]
    new_expert = jnp.logical_or(step == 0, be_ref[step] != be_ref[jnp.maximum(step - 1, 0)])

    @pl.when(jnp.logical_and(used, new_expert))
    def _():
        wg_bf[...] = wg_ref[0].astype(BF16)
        wu_bf[...] = wu_ref[0].astype(BF16)
        wd_bf[...] = wd_ref[0].astype(BF16)

    @pl.when(used)
    def _():
        xb = x_ref[...].astype(BF16)
        g = jnp.dot(xb, wg_bf[...], preferred_element_type=F32)
        u = jnp.dot(xb, wu_bf[...], preferred_element_type=F32)
        a = (g * _sigmoid(g)) * u
        y_ref[...] = jnp.dot(a.astype(BF16), wd_bf[...], preferred_element_type=F32)

    @pl.when(jnp.logical_not(used))
    def _():
        y_ref[...] = jnp.zeros_like(y_ref)


def _experts(buf, blk_e, n_used, wg, wu, wd, n_blocks):
    d = buf.shape[1]
    last = lambda i, nu: jnp.maximum(jnp.minimum(i, nu[0] - 1), 0)
    grid_spec = pltpu.PrefetchScalarGridSpec(
        num_scalar_prefetch=2,
        grid=(n_blocks,),
        in_specs=[pl.BlockSpec((TB, d), lambda i, be, nu: (last(i, nu), 0)),
                  pl.BlockSpec((1, d, D_EXPERT), lambda i, be, nu: (be[last(i, nu)], 0, 0)),
                  pl.BlockSpec((1, d, D_EXPERT), lambda i, be, nu: (be[last(i, nu)], 0, 0)),
                  pl.BlockSpec((1, D_EXPERT, d), lambda i, be, nu: (be[last(i, nu)], 0, 0))],
        out_specs=pl.BlockSpec((TB, d), lambda i, be, nu: (i, 0)),
        scratch_shapes=[pltpu.VMEM((d, D_EXPERT), BF16), pltpu.VMEM((d, D_EXPERT), BF16),
                        pltpu.VMEM((D_EXPERT, d), BF16)],
    )
    return pl.pallas_call(
        _experts_kernel,
        grid_spec=grid_spec,
        out_shape=jax.ShapeDtypeStruct(buf.shape, F32),
        compiler_params=_params(),
        name="moe_experts",
    )(blk_e, n_used, buf, wg, wu, wd)


def _gather_start(dest_ref, dsm_ref, ybuf_ref, rows_ref, slot, sem_idx, sem):
    cp = pltpu.make_async_copy(dest_ref, dsm_ref, sem_idx)
    cp.start()
    cp.wait()

    def body(r, carry):
        for k in range(2):
            pltpu.make_async_copy(ybuf_ref.at[pl.ds(dsm_ref[k, r], 1), :],
                                  rows_ref.at[slot, k, pl.ds(r, 1), :], sem).start(priority=k)
        return carry

    lax.fori_loop(0, TM, body, 0, unroll=8)


def _gathered_rows(dest_ref, next_ref, dsm_ref, ybuf_ref, rows_ref, sem_idx, sems):
    step = pl.program_id(0)
    nt = pl.num_programs(0)
    slot = step % 2

    @pl.when(step == 0)
    def _():
        _gather_start(dest_ref, dsm_ref, ybuf_ref, rows_ref, 0, sem_idx, sems.at[0])

    @pl.when(step + 1 < nt)
    def _():
        _gather_start(next_ref, dsm_ref, ybuf_ref, rows_ref, 1 - slot, sem_idx, sems.at[1 - slot])

    for k in range(2):
        _row_wait(ybuf_ref, rows_ref.at[slot, k], sems.at[slot])
    return rows_ref[slot, 0], rows_ref[slot, 1]


def _token_weights(meta_ref):
    meta = jnp.concatenate([meta_ref[...], jnp.zeros((LANES - SUBLANES, TM), F32)], axis=0)
    mt = meta.T
    return mt[:, 4:5], mt[:, 5:6]


_COMBINE_SCRATCH = lambda d: [pltpu.SMEM((SUBLANES, TM), jnp.int32),
                              pltpu.VMEM((2, 2, TM, d), F32),
                              pltpu.SemaphoreType.DMA, pltpu.SemaphoreType.DMA((2,))]


def _next_tile_spec(nt):
    return pl.BlockSpec((SUBLANES, TM), lambda i: (0, jnp.minimum(i + 1, nt - 1)))


def _combine0_kernel(x1_ref, dest_ref, next_ref, meta_ref, gt_ref, ybuf_ref, x2_ref, ctx2_ref,
                     dsm_ref, rows_ref, sem_idx, sems):
    is_ctx = pl.program_id(0) == 0
    y0, y1 = _gathered_rows(dest_ref, next_ref, dsm_ref, ybuf_ref, rows_ref, sem_idx, sems)
    w0, w1 = _token_weights(meta_ref)
    out = x1_ref[...] + _mod_row(gt_ref, is_ctx) * (w0 * y0 + w1 * y1)
    x2_ref[...] = out

    @pl.when(is_ctx)
    def _():
        ctx2_ref[...] = out


def _combine0(x1, dest, meta, mod, ybuf):
    t, d = x1.shape
    nt = t // TM
    return pl.pallas_call(
        _combine0_kernel,
        grid=(nt,),
        in_specs=[pl.BlockSpec((TM, d), lambda i: (i, 0)),
                  pl.BlockSpec((SUBLANES, TM), lambda i: (0, i)),
                  _next_tile_spec(nt),
                  pl.BlockSpec((SUBLANES, TM), lambda i: (0, i)),
                  pl.BlockSpec((SUBLANES, d), lambda i: (0, 5)),
                  pl.BlockSpec(memory_space=pl.ANY)],
        out_specs=[pl.BlockSpec((TM, d), lambda i: (jnp.maximum(i - 1, 0), 0)),
                   _full((TM, d))],
        out_shape=[jax.ShapeDtypeStruct((t - TM, d), F32), jax.ShapeDtypeStruct((TM, d), F32)],
        scratch_shapes=_COMBINE_SCRATCH(d),
        compiler_params=_params(),
        name="moe_combine0",
    )(x1, dest, dest, meta, mod, ybuf)


def _combine1_kernel(x1_ref, dest_ref, next_ref, meta_ref, gt_ref, gf_ref, ybuf_ref, o_ref,
                     dsm_ref, rows_ref, sem_idx, sems):
    y0, y1 = _gathered_rows(dest_ref, next_ref, dsm_ref, ybuf_ref, rows_ref, sem_idx, sems)
    w0, w1 = _token_weights(meta_ref)
    out = x1_ref[...] + gt_ref[0:1, :] * (w0 * y0 + w1 * y1)
    o_ref[...] = _rms(out, gf_ref[...])


def _combine1(x1, dest, meta, mod, gf, ybuf):
    t, d = x1.shape
    rows = t // GRID_W
    nt = t // TM
    out = pl.pallas_call(
        _combine1_kernel,
        grid=(nt,),
        in_specs=[pl.BlockSpec((TM, d), lambda i: (i, 0)),
                  pl.BlockSpec((SUBLANES, TM), lambda i: (0, i)),
                  _next_tile_spec(nt),
                  pl.BlockSpec((SUBLANES, TM), lambda i: (0, i)),
                  pl.BlockSpec((SUBLANES, d), lambda i: (0, 5)),
                  _full((1, d)),
                  pl.BlockSpec(memory_space=pl.ANY)],
        out_specs=pl.BlockSpec((TM, d), lambda i: (0, i)),
        out_shape=jax.ShapeDtypeStruct((rows, GRID_W * d), F32),
        scratch_shapes=_COMBINE_SCRATCH(d),
        compiler_params=_params(),
        name="moe_combine1",
    )(x1, dest, dest, meta, mod, gf.reshape(1, -1), ybuf)
    return out.reshape(t, d)


def _moe(h2, meta, cnt, w_gate, w_up, w_down):
    dest, blk, n_blocks = _plan(cnt, meta)
    n_used = blk[1, :1]
    buf = _dispatch(h2, dest, blk[2, :N_EXPERTS], n_used, n_blocks)
    ybuf = _experts(buf, blk[0], n_used, w_gate, w_up, w_down, n_blocks)
    return dest, ybuf


def _inproj1_kernel(x_ref, ctx_ref, g_ref, sh_ref, sc_ref, wq_ref, wkt_ref, wv_ref, wo_ref,
                    wg_ref, wgt_ref, bg_ref, bgt_ref,
                    q_ref, kt_ref, v_ref, o_ref, gcol_ref, grow_ref):
    is_ctx = pl.program_id(0) == 0
    xin = jnp.where(is_ctx, ctx_ref[...], x_ref[...])
    h = _rms(xin, g_ref[...]) * (1.0 + _mod_row(sc_ref, is_ctx)) + _mod_row(sh_ref, is_ctx)
    hb = h.astype(BF16)
    q = jnp.dot(hb, wq_ref[...], preferred_element_type=F32) * (ML_DQK ** -0.5)
    q_ref[...] = q.astype(BF16)
    kt = lax.dot_general(wkt_ref[...], hb, (((1,), (1,)), ((), ())), preferred_element_type=F32)
    kt_ref[...] = kt.astype(BF16)
    v_ref[...] = jnp.dot(hb, wv_ref[...], preferred_element_type=F32).astype(BF16)
    o_ref[...] = jnp.dot(hb, wo_ref[...], preferred_element_type=F32)
    gcol_ref[...] = jnp.dot(h, wg_ref[...], precision=HI, preferred_element_type=F32) + bg_ref[...]
    grow_ref[...] = lax.dot_general(wgt_ref[...], h, (((1,), (1,)), ((), ())), precision=HI,
                                    preferred_element_type=F32) + bgt_ref[:, 0:1]


def _inproj1(x, ctx, g1, mod, w_in, b_gates):
    n, d = x.shape
    rows = n // GRID_W
    nt = GRID_W + 1
    t = nt * TM
    ng = 4 * ML_HEADS
    wq = w_in[:, :ML_QK_W].astype(BF16)
    wkt = w_in[:, ML_QK_W:2 * ML_QK_W].T.astype(BF16)
    wv = w_in[:, 2 * ML_QK_W:2 * ML_QK_W + ML_V_W].astype(BF16)
    wo = w_in[:, 2 * ML_QK_W + ML_V_W:2 * ML_QK_W + ML_V_W + d].astype(BF16)
    wg = jnp.zeros((d, LANES), F32).at[:, :ng].set(w_in[:, -ng:])
    wgt = w_in[:, -ng:].T
    bg = jnp.zeros((1, LANES), F32).at[0, :ng].set(b_gates)
    bgt = jnp.broadcast_to(b_gates[:, None], (ng, LANES))
    tok = lambda w: pl.BlockSpec((TM, w), lambda i: (i, 0))
    return pl.pallas_call(
        _inproj1_kernel,
        grid=(nt,),
        in_specs=[pl.BlockSpec((rows, d), lambda i: (0, jnp.maximum(i - 1, 0))),
                  _full((TM, d)),
                  _full((1, d)),
                  pl.BlockSpec((SUBLANES, d), lambda i: (0, 0)),
                  pl.BlockSpec((SUBLANES, d), lambda i: (0, 1)),
                  _full(wq.shape), _full(wkt.shape), _full(wv.shape), _full(wo.shape),
                  _full(wg.shape), _full(wgt.shape), _full(bg.shape), _full(bgt.shape)],
        out_specs=[tok(ML_QK_W),
                   pl.BlockSpec((ML_QK_W, TM), lambda i: (0, i)),
                   tok(ML_V_W), tok(d), tok(LANES),
                   pl.BlockSpec((ng, TM), lambda i: (0, i))],
        out_shape=[jax.ShapeDtypeStruct((t, ML_QK_W), BF16),
                   jax.ShapeDtypeStruct((ML_QK_W, t), BF16),
                   jax.ShapeDtypeStruct((t, ML_V_W), BF16),
                   jax.ShapeDtypeStruct((t, d), F32),
                   jax.ShapeDtypeStruct((t, LANES), F32),
                   jax.ShapeDtypeStruct((ng, t), F32)],
        compiler_params=_params(),
        name="inproj1",
    )(x.reshape(rows, GRID_W * d), ctx, g1.reshape(1, -1), mod, mod, wq, wkt, wv, wo, wg, wgt, bg, bgt)


def _log_sigmoid(x):
    return jnp.minimum(x, 0.0) - jnp.log1p(jnp.exp(-jnp.abs(x)))


def _mlstm_kernel(reverse, q_ref, kt_ref, v_ref, gcol_ref, grow_ref, h_ref, c_ref, m_ref):
    @pl.when(pl.program_id(0) == 0)
    def _():
        c_ref[...] = jnp.zeros_like(c_ref)
        m_ref[...] = jnp.zeros_like(m_ref)

    L = TM
    gi = 2 * ML_HEADS if reverse else 0
    gf = gi + ML_HEADS
    end = 0 if reverse else L - 1
    rr = lax.broadcasted_iota(jnp.int32, (L, L), 0)
    cc = lax.broadcasted_iota(jnp.int32, (L, L), 1)
    allowed = (cc >= rr) if reverse else (cc <= rr)
    tri = jnp.where(allowed, 1.0, 0.0)
    lf_col = _log_sigmoid(gcol_ref[...])
    lf_row = _log_sigmoid(grow_ref[gf:gf + ML_HEADS, :])
    cum_col = jnp.dot(tri, lf_col, precision=HI, preferred_element_type=F32)
    cum_row = lax.dot_general(lf_row, tri, (((1,), (1,)), ((), ())), precision=HI,
                              preferred_element_type=F32)
    ones_col = jnp.where(lax.broadcasted_iota(jnp.int32, (L, ML_DV), 1) == 0, 1.0, 0.0).astype(BF16)

    for hd in range(ML_HEADS):
        m_prev = m_ref[hd:hd + 1, 0:1]
        g_col = cum_col[:, gf + hd:gf + hd + 1]
        g_row = cum_row[hd:hd + 1, :]
        li_row = grow_ref[gi + hd:gi + hd + 1, :]
        total = g_col[end:end + 1, :]

        log_w = jnp.where(allowed, g_col - g_row + li_row, NEG_INF)
        log_inter = g_col + m_prev
        m_t = jnp.maximum(log_inter, jnp.max(log_w, axis=-1, keepdims=True))
        w_intra = jnp.exp(log_w - m_t)
        w_inter = jnp.exp(log_inter - m_t)

        qh = q_ref[:, hd * ML_DQK:(hd + 1) * ML_DQK]
        kth = kt_ref[hd * ML_DQK:(hd + 1) * ML_DQK, :]
        vext = jnp.concatenate([v_ref[:, hd * ML_DV:(hd + 1) * ML_DV], ones_col], axis=1)
        s = jnp.dot(qh, kth, preferred_element_type=F32) * w_intra
        state = c_ref[hd]
        tot = (w_inter * jnp.dot(qh, state.astype(BF16), preferred_element_type=F32)
               + jnp.dot(s.astype(BF16), vext, preferred_element_type=F32))
        den = tot[:, ML_DV:ML_DV + 1]
        h_ref[:, hd * ML_DV:(hd + 1) * ML_DV] = tot[:, :ML_DV] / jnp.maximum(jnp.abs(den), jnp.exp(-m_t))

        m_new = m_t[end:end + 1, :]
        w_state = jnp.exp(total - g_row + li_row - m_new)
        decay = jnp.exp(total + m_prev - m_new)
        kw = (kth.astype(F32) * w_state).astype(BF16)
        c_ref[hd] = decay * state + jnp.dot(kw, vext, preferred_element_type=F32)
        m_ref[hd:hd + 1, :] = jnp.broadcast_to(m_new, (1, LANES))


def _mlstm(q, kt, v, gcol, grow, reverse):
    t = q.shape[0]
    nt = t // TM
    tile = lambda s: _scan_tile(reverse, s, nt)
    return pl.pallas_call(
        functools.partial(_mlstm_kernel, reverse),
        grid=(nt,),
        in_specs=[pl.BlockSpec((TM, ML_QK_W), lambda s: (tile(s), 0)),
                  pl.BlockSpec((ML_QK_W, TM), lambda s: (0, tile(s))),
                  pl.BlockSpec((TM, ML_V_W), lambda s: (tile(s), 0)),
                  pl.BlockSpec((TM, LANES), lambda s: (tile(s), 0)),
                  pl.BlockSpec((4 * ML_HEADS, TM), lambda s: (0, tile(s)))],
        out_specs=pl.BlockSpec((TM, ML_V_W), lambda s: (tile(s), 0)),
        out_shape=jax.ShapeDtypeStruct((t, ML_V_W), F32),
        scratch_shapes=[pltpu.VMEM((ML_HEADS, ML_DQK, 2 * ML_DV), F32),
                        pltpu.VMEM((ML_HEADS, LANES), F32)],
        compiler_params=_params(),
        name="mlstm_bwd" if reverse else "mlstm_fwd",
    )(q, kt, v, gcol, grow)


def _finish1_kernel(x_ref, hf_ref, hb_ref, o_ref, ng_ref, w_ref, gt_ref, g2_ref, sh_ref, sc_ref,
                    wr_ref, br_ref, x1_ref, h2_ref, lg_ref):
    hs = hf_ref[...] + hb_ref[...]
    parts = []
    for hd in range(ML_HEADS):
        blk = hs[:, hd * ML_DV:(hd + 1) * ML_DV]
        parts.append(blk * lax.rsqrt(jnp.mean(blk * blk, axis=-1, keepdims=True) + EPS))
    hn = jnp.concatenate(parts, axis=1) * ng_ref[...]
    y = jnp.dot((hn * _sigmoid(o_ref[...])).astype(BF16), w_ref[...], preferred_element_type=F32)
    x1 = x_ref[...] + gt_ref[0:1, :] * y
    x1_ref[...] = x1
    h2 = _rms(x1, g2_ref[...]) * (1.0 + sc_ref[0:1, :]) + sh_ref[0:1, :]
    h2_ref[...] = h2
    lg_ref[...] = _router_logits(h2, wr_ref, br_ref)


def _finish1(x, hf, hb, o, norm_g, w_out_bf, mod, g2, wr, br):
    n, d = x.shape
    rows = n // GRID_W
    lat = lambda w: pl.BlockSpec((TM, w), lambda i: (i + 1, 0))
    modspec = lambda c: pl.BlockSpec((SUBLANES, d), lambda i: (0, c))
    return pl.pallas_call(
        _finish1_kernel,
        grid=(GRID_W,),
        in_specs=[pl.BlockSpec((rows, d), lambda i: (0, i)),
                  lat(ML_V_W), lat(ML_V_W), lat(d),
                  _full((1, ML_V_W)), _full((ML_V_W, d)),
                  modspec(2), _full((1, d)), modspec(3), modspec(4),
                  _full(wr.shape), _full(br.shape)],
        out_specs=_TOKEN_OUTS(d),
        out_shape=_token_out_shapes(n, d),
        compiler_params=_params(),
        name="finish1",
    )(x.reshape(rows, GRID_W * d), hf, hb, o, norm_g.reshape(1, -1), w_out_bf, mod,
      g2.reshape(1, -1), mod, mod, wr, br)


def kernel(x, c, ctx, c_ctx,
           l0_ada_w, l0_ada_b, l0_norm1_g, l0_norm2_g,
           l0_rg_w_in, l0_rg_conv_w, l0_rg_conv_b, l0_rg_w_a, l0_rg_b_a, l0_rg_w_x, l0_rg_b_x,
           l0_rg_lambda, l0_rg_w_out,
           l0_moe_w_grp, l0_moe_b_grp, l0_moe_w_exp, l0_moe_b_exp, l0_moe_w_gate, l0_moe_w_up,
           l0_moe_w_down,
           l1_ada_w, l1_ada_b, l1_norm1_g, l1_norm2_g,
           l1_ml_w_in, l1_ml_b_gates, l1_ml_norm_g, l1_ml_w_out,
           l1_moe_w_grp, l1_moe_b_grp, l1_moe_w_exp, l1_moe_b_exp, l1_moe_w_gate, l1_moe_w_up,
           l1_moe_w_down,
           final_norm_g):
    assert x.shape[0] == 1 and ctx.shape[1] == TM and x.shape[1] == GRID_W * TM
    xs, cs = x[0], ctx[0]
    d = xs.shape[1]
    cond8 = jnp.zeros((SUBLANES, d), F32).at[0].set(c[0]).at[1].set(c_ctx)

    mod0 = _adaln(cond8, l0_ada_w, l0_ada_b)
    conv_w8 = jnp.zeros((SUBLANES, D_RNN), F32).at[:CONV_W].set(l0_rg_conv_w)
    gate, u = _inproj0(xs, cs, l0_norm1_g, mod0, l0_rg_w_in.astype(BF16), conv_w8, l0_rg_conv_b)
    h_dirs = []
    for dr in range(2):
        wcat = jnp.concatenate([l0_rg_w_a[dr], l0_rg_w_x[dr]], axis=-1).astype(BF16)
        h_dirs.append(_rglru_scan(u, dr == 1, wcat, l0_rg_b_a[dr], l0_rg_b_x[dr], l0_rg_lambda[dr]))
    wr0, br0 = _router_weights(l0_moe_w_grp, l0_moe_b_grp, l0_moe_w_exp, l0_moe_b_exp)
    x1, h2, logits = _outproj0(xs, cs, gate, h_dirs[0], h_dirs[1], l0_rg_w_out.astype(BF16), mod0,
                               l0_norm2_g, wr0, br0)
    meta, cnt = _route(logits, ROUTE_TILE_0)
    dest, ybuf = _moe(h2, meta, cnt, l0_moe_w_gate, l0_moe_w_up, l0_moe_w_down)
    x2, ctx2 = _combine0(x1, dest, meta, mod0, ybuf)

    mod1 = _adaln(cond8, l1_ada_w, l1_ada_b)
    q, kt, v, o, gcol, grow = _inproj1(x2, ctx2, l1_norm1_g, mod1, l1_ml_w_in, l1_ml_b_gates)
    hf = _mlstm(q, kt, v, gcol, grow, False)
    hb = _mlstm(q, kt, v, gcol, grow, True)
    wr1, br1 = _router_weights(l1_moe_w_grp, l1_moe_b_grp, l1_moe_w_exp, l1_moe_b_exp)
    x1, h2, logits = _finish1(x2, hf, hb, o, l1_ml_norm_g, l1_ml_w_out.astype(BF16), mod1,
                              l1_norm2_g, wr1, br1)
    meta, cnt = _route(logits, ROUTE_TILE_1)
    dest, ybuf = _moe(h2, meta, cnt, l1_moe_w_gate, l1_moe_w_up, l1_moe_w_down)
    out = _combine1(x1, dest, meta, mod1, final_norm_g, ybuf)
    return out[None]
```

```python
import functools

import jax
import jax.numpy as jnp
from jax import lax
from jax.experimental import pallas as pl
from jax.experimental.pallas import tpu as pltpu

D_MODEL = 1024
GRID_W = 64
N_MOD = 6
EPS = 1e-6

D_RNN = 1280
RG_BLOCKS = 10
RG_BLOCK_W = D_RNN // RG_BLOCKS
CONV_W = 4
CONV_PAD_L = 2
RG_C = 8.0

ML_HEADS = 8
ML_DQK = D_MODEL // (2 * ML_HEADS)
ML_DV = D_MODEL // ML_HEADS
ML_QK_W = ML_HEADS * ML_DQK
ML_V_W = ML_HEADS * ML_DV

N_GROUPS = 4
EXPERTS_PER_GROUP = 8
N_EXPERTS = N_GROUPS * EXPERTS_PER_GROUP
D_EXPERT = 512

TM = 256
TB = 256
ROUTE_TILE_0 = 1280
ROUTE_TILE_1 = 1024
SUBLANES = 8
LANES = 128
VMEM_LIMIT = 48 * 1024 * 1024

F32 = jnp.float32
BF16 = jnp.bfloat16
HI = lax.Precision.HIGHEST
NEG_INF = float("-inf")


def _params(n_axes=1):
    return pltpu.CompilerParams(dimension_semantics=("arbitrary",) * n_axes,
                                vmem_limit_bytes=VMEM_LIMIT)


def _rms(x, g):
    return x * lax.rsqrt(jnp.mean(x * x, axis=-1, keepdims=True) + EPS) * g


def _sigmoid(x):
    return 1.0 / (1.0 + jnp.exp(-x))


def _softplus(x):
    return jnp.maximum(x, 0.0) + jnp.log1p(jnp.exp(-jnp.abs(x)))


def _gelu_tanh(x):
    return 0.5 * x * (1.0 + jnp.tanh(0.7978845608028654 * (x + 0.044715 * (x * x * x))))


def _full(shape):
    return pl.BlockSpec(shape, lambda *_: (0,) * len(shape))


def _store_token_tiles(ref, x):
    per = x.shape[1] // LANES
    for c in range(per):
        ref[pl.ds(c, x.shape[0], stride=per), :] = x[:, c * LANES:(c + 1) * LANES]


def _load_token_tiles(ref, rows, d):
    per = d // LANES
    return jnp.concatenate([ref[pl.ds(c, rows, stride=per), :] for c in range(per)], axis=1)


def _token_tile(ref, row, d):
    per = d // LANES
    start = row * per if isinstance(row, int) else pl.multiple_of(row * per, per)
    return ref.at[pl.ds(start, per), :]


def _adaln_kernel(cond_ref, w_ref, b_ref, o_ref):
    c = cond_ref[...]
    s = c * _sigmoid(c)
    o_ref[...] = jnp.dot(s, w_ref[...], precision=HI, preferred_element_type=F32) + b_ref[...]


def _adaln(cond8, w, b):
    d = w.shape[0]
    return pl.pallas_call(
        _adaln_kernel,
        grid=(N_MOD,),
        in_specs=[_full((SUBLANES, d)),
                  pl.BlockSpec((d, d), lambda j: (0, j)),
                  pl.BlockSpec((1, d), lambda j: (0, j))],
        out_specs=pl.BlockSpec((SUBLANES, d), lambda j: (0, j)),
        out_shape=jax.ShapeDtypeStruct((SUBLANES, N_MOD * d), F32),
        compiler_params=_params(),
        name="adaln",
    )(cond8, w, b.reshape(1, -1))


def _mod_row(ref, is_ctx):
    return jnp.where(is_ctx, ref[1:2, :], ref[0:1, :])


def _inproj0_kernel(nt, x_ref, ctx_ref, g_ref, sh_ref, sc_ref, w_ref, cw_ref, cb_ref,
                    gate_ref, u_ref, ext_ref):
    s = pl.program_id(0)
    is_ctx = s == 0

    @pl.when(s == 0)
    def _():
        ext_ref[...] = jnp.zeros_like(ext_ref)

    xin = jnp.where(is_ctx, ctx_ref[...], x_ref[...])
    h = _rms(xin, g_ref[...]) * (1.0 + _mod_row(sc_ref, is_ctx)) + _mod_row(sh_ref, is_ctx)
    p = jnp.dot(h.astype(BF16), w_ref[...], preferred_element_type=F32)
    gate_ref[...] = p[:, :D_RNN]
    rec = p[:, D_RNN:]

    right_valid = jnp.logical_and(s >= 2, s <= nt - 1)
    base = SUBLANES - CONV_PAD_L
    for j in range(RG_BLOCKS):
        ln = slice(j * RG_BLOCK_W, (j + 1) * RG_BLOCK_W)
        ext_ref[j, SUBLANES + TM:, :] = jnp.where(right_valid, rec[0:SUBLANES, ln], 0.0)
        u = cb_ref[:, ln] + ext_ref[j, pl.ds(base, TM), :] * cw_ref[0:1, ln]
        for k in range(1, CONV_W):
            u = u + ext_ref[j, pl.ds(base + k, TM), :] * cw_ref[k:k + 1, ln]
        u_ref[:, ln] = u
        ext_ref[j, 0:SUBLANES, :] = jnp.where(s >= 2, ext_ref[j, TM:TM + SUBLANES, :], 0.0)
        ext_ref[j, SUBLANES:SUBLANES + TM, :] = rec[:, ln]


def _inproj0(x, ctx, g1, mod, w_in_bf, conv_w8, conv_b):
    n, d = x.shape
    nx = n // TM
    nt = nx + 1
    t = nt * TM
    return pl.pallas_call(
        functools.partial(_inproj0_kernel, nt),
        grid=(nt + 1,),
        in_specs=[pl.BlockSpec((TM, d), lambda s: (jnp.clip(s - 1, 0, nx - 1), 0)),
                  _full((TM, d)),
                  _full((1, d)),
                  pl.BlockSpec((SUBLANES, d), lambda s: (0, 0)),
                  pl.BlockSpec((SUBLANES, d), lambda s: (0, 1)),
                  _full((d, 2 * D_RNN)),
                  _full((SUBLANES, D_RNN)), _full((1, D_RNN))],
        out_specs=[pl.BlockSpec((TM, D_RNN), lambda s: (jnp.minimum(s, nt - 1), 0)),
                   pl.BlockSpec((TM, D_RNN), lambda s: (jnp.maximum(s - 1, 0), 0))],
        out_shape=[jax.ShapeDtypeStruct((t, D_RNN), F32)] * 2,
        scratch_shapes=[pltpu.VMEM((RG_BLOCKS, TM + 2 * SUBLANES, LANES), F32)],
        compiler_params=_params(),
        name="inproj0",
    )(x, ctx, g1.reshape(1, -1), mod, mod, w_in_bf, conv_w8, conv_b.reshape(1, -1))


def _scan_tile(reverse, s, nt):
    if not reverse:
        return s
    return jnp.where(s == 0, 0, nt - s)


SCAN_CHUNK = TM // SUBLANES
SCAN_PITCH = SCAN_CHUNK + 4


def _rglru_kernel(reverse, u_ref, wcat_ref, ba_ref, bx_ref, lam_ref, h_ref,
                  a_ref, b_ref, o_ref, carry_ref):
    @pl.when(pl.program_id(0) == 0)
    def _():
        carry_ref[...] = jnp.zeros_like(carry_ref)

    steps = range(SCAN_CHUNK - 1, -1, -1) if reverse else range(SCAN_CHUNK)
    chunks = range(SUBLANES - 1, -1, -1) if reverse else range(SUBLANES)
    for j in range(RG_BLOCKS):
        ln = slice(j * RG_BLOCK_W, (j + 1) * RG_BLOCK_W)
        u = u_ref[:, ln]
        g = jnp.dot(u.astype(BF16), wcat_ref[j], preferred_element_type=F32)
        half_rate = (-0.5 * RG_C) * _softplus(-lam_ref[:, ln])
        log_a = jnp.tanh(0.5 * (g[:, :RG_BLOCK_W] + ba_ref[:, ln])) * half_rate + half_rate
        ig = 0.5 * jnp.tanh(0.5 * (g[:, RG_BLOCK_W:] + bx_ref[:, ln])) + 0.5
        a = jnp.exp(log_a)
        b = jnp.sqrt(-jnp.tanh(log_a) * (a * a + 1.0)) * ig * u
        for c in range(SUBLANES):
            a_ref[j, pl.ds(c * SCAN_PITCH, SCAN_CHUNK), :] = a[c * SCAN_CHUNK:(c + 1) * SCAN_CHUNK]
            b_ref[j, pl.ds(c * SCAN_PITCH, SCAN_CHUNK), :] = b[c * SCAN_CHUNK:(c + 1) * SCAN_CHUNK]

        row = lambda ref, i: ref[j, pl.ds(i, SUBLANES, stride=SCAN_PITCH), :]
        end = jnp.zeros((SUBLANES, LANES), F32)
        decay = jnp.ones((SUBLANES, LANES), F32)
        for i in steps:
            ai = row(a_ref, i)
            end = ai * end + row(b_ref, i)
            decay = decay * ai

        state = carry_ref[0:1, ln]
        entry = [None] * SUBLANES
        for c in chunks:
            entry[c] = state
            state = decay[c:c + 1] * state + end[c:c + 1]
        carry_ref[0:1, ln] = state

        hcur = jnp.concatenate(entry, axis=0)
        for i in steps:
            hcur = row(a_ref, i) * hcur + row(b_ref, i)
            o_ref[j, pl.ds(i, SUBLANES, stride=SCAN_PITCH), :] = hcur
        for c in range(SUBLANES):
            h_ref[c * SCAN_CHUNK:(c + 1) * SCAN_CHUNK, ln] = o_ref[j, pl.ds(c * SCAN_PITCH, SCAN_CHUNK), :]


def _rglru_scan(u, reverse, wcat, b_a, b_x, lam):
    t = u.shape[0]
    nt = t // TM
    tile_map = lambda s: (_scan_tile(reverse, s, nt), 0)
    row = lambda v: v.reshape(1, -1)
    slab = pltpu.VMEM((RG_BLOCKS, SUBLANES * SCAN_PITCH, LANES), F32)
    return pl.pallas_call(
        functools.partial(_rglru_kernel, reverse),
        grid=(nt,),
        in_specs=[pl.BlockSpec((TM, D_RNN), tile_map),
                  _full((RG_BLOCKS, RG_BLOCK_W, 2 * RG_BLOCK_W)),
                  _full((1, D_RNN)), _full((1, D_RNN)), _full((1, D_RNN))],
        out_specs=pl.BlockSpec((TM, D_RNN), tile_map),
        out_shape=jax.ShapeDtypeStruct((t, D_RNN), F32),
        scratch_shapes=[slab, slab, slab, pltpu.VMEM((SUBLANES, D_RNN), F32)],
        compiler_params=_params(),
        name="rglru_bwd" if reverse else "rglru_fwd",
    )(u, wcat, row(b_a), row(b_x), row(lam))


def _split_bf16(w):
    hi = w.astype(BF16)
    return jnp.stack([hi, (w - hi.astype(F32)).astype(BF16)])


def _dot_bf16x3(x, w_ref):
    hi = x.astype(BF16)
    lo = (x - hi.astype(F32)).astype(BF16)
    acc = jnp.dot(hi, w_ref[0], preferred_element_type=F32)
    acc = acc + jnp.dot(lo, w_ref[0], preferred_element_type=F32)
    return acc + jnp.dot(hi, w_ref[1], preferred_element_type=F32)


def _router_logits(h2, wr_ref, br_ref):
    return _dot_bf16x3(h2, wr_ref) + br_ref[...]


def _route_kernel(lg_ref, meta_ref, cnt_ref, carry_ref):
    step = pl.program_id(0)
    tm = lg_ref.shape[0]

    @pl.when(step == 0)
    def _():
        carry_ref[...] = jnp.zeros_like(carry_ref)

    logits = jnp.concatenate([lg_ref[i * LANES:(i + 1) * LANES, :].T for i in range(tm // LANES)],
                             axis=1)
    row8 = lax.broadcasted_iota(jnp.int32, (SUBLANES, tm), 0)
    grp_logits = jnp.where(row8 < N_GROUPS, logits[0:SUBLANES], NEG_INF)
    gmax = jnp.max(grp_logits, axis=0, keepdims=True)
    p_sel = 1.0 / jnp.sum(jnp.exp(grp_logits - gmax), axis=0, keepdims=True)
    grp = jnp.min(jnp.where(grp_logits == gmax, row8, SUBLANES), axis=0, keepdims=True)

    in_grp = logits[SUBLANES + (N_GROUPS - 1) * EXPERTS_PER_GROUP:SUBLANES + N_GROUPS * EXPERTS_PER_GROUP]
    for gi in range(N_GROUPS - 2, -1, -1):
        lo = SUBLANES + gi * EXPERTS_PER_GROUP
        in_grp = jnp.where(grp == gi, logits[lo:lo + EXPERTS_PER_GROUP], in_grp)
    v1 = jnp.max(in_grp, axis=0, keepdims=True)
    i1 = jnp.min(jnp.where(in_grp == v1, row8, EXPERTS_PER_GROUP), axis=0, keepdims=True)
    rest = jnp.where(row8 == i1, NEG_INF, in_grp)
    v2 = jnp.max(rest, axis=0, keepdims=True)
    i2 = jnp.min(jnp.where(rest == v2, row8, EXPERTS_PER_GROUP), axis=0, keepdims=True)
    e2 = jnp.exp(v2 - v1)
    w1 = p_sel / (1.0 + e2)
    w2 = p_sel * e2 / (1.0 + e2)
    eid = (grp * EXPERTS_PER_GROUP + i1, grp * EXPERTS_PER_GROUP + i2)

    rr = lax.broadcasted_iota(jnp.int32, (LANES, LANES), 0)
    cc = lax.broadcasted_iota(jnp.int32, (LANES, LANES), 1)
    strict_upper = jnp.where(rr < cc, 1.0, 0.0).astype(BF16)
    erow = lax.broadcasted_iota(jnp.int32, (N_EXPERTS, LANES), 0)
    base = carry_ref[:, 0:1]
    for k in range(2):
        for i in range(tm // LANES):
            ln = slice(i * LANES, (i + 1) * LANES)
            onehot = jnp.where(erow == eid[k][:, ln], 1.0, 0.0)
            pre = jnp.dot(onehot.astype(BF16), strict_upper, preferred_element_type=F32)
            meta_ref[2 + k:3 + k, ln] = jnp.sum(onehot * (base + pre), axis=0, keepdims=True)
            base = base + jnp.sum(onehot, axis=1, keepdims=True)
    carry_ref[...] = jnp.broadcast_to(base, carry_ref.shape)
    cnt_ref[...] = jnp.broadcast_to(base, cnt_ref.shape)

    meta_ref[0:1, :] = eid[0].astype(F32)
    meta_ref[1:2, :] = eid[1].astype(F32)
    meta_ref[4:5, :] = w1
    meta_ref[5:6, :] = w2
    meta_ref[6:8, :] = jnp.zeros((2, tm), F32)


def _router_weights(w_grp, b_grp, w_exp, b_exp):
    d = w_grp.shape[0]
    wr = jnp.zeros((d, LANES), F32).at[:, :N_GROUPS].set(w_grp)
    wr = wr.at[:, SUBLANES:SUBLANES + N_EXPERTS].set(w_exp)
    br = jnp.zeros((1, LANES), F32).at[0, :N_GROUPS].set(b_grp)
    br = br.at[0, SUBLANES:SUBLANES + N_EXPERTS].set(b_exp)
    return _split_bf16(wr), br


def _route(logits, tile):
    t = logits.shape[0]
    return pl.pallas_call(
        _route_kernel,
        grid=(t // tile,),
        in_specs=[pl.BlockSpec((tile, LANES), lambda i: (i, 0))],
        out_specs=[pl.BlockSpec((SUBLANES, tile), lambda i: (0, i)), _full((N_EXPERTS, LANES))],
        out_shape=[jax.ShapeDtypeStruct((SUBLANES, t), F32),
                   jax.ShapeDtypeStruct((N_EXPERTS, LANES), F32)],
        scratch_shapes=[pltpu.VMEM((N_EXPERTS, LANES), F32)],
        compiler_params=_params(),
        name="moe_route",
    )(logits)


def _outproj0_kernel(x_ref, ctx_ref, gate_ref, hf_ref, hb_ref, w_ref, gt_ref, g2_ref, sh_ref, sc_ref,
                     wr_ref, br_ref, x1_ref, h2_ref, lg_ref):
    is_ctx = pl.program_id(0) == 0
    xin = jnp.where(is_ctx, ctx_ref[...], x_ref[...])
    y = _gelu_tanh(gate_ref[...]) * (hf_ref[...] + hb_ref[...])
    y = jnp.dot(y.astype(BF16), w_ref[...], preferred_element_type=F32)
    x1 = xin + _mod_row(gt_ref, is_ctx) * y
    x1_ref[...] = x1
    h2 = _rms(x1, g2_ref[...]) * (1.0 + _mod_row(sc_ref, is_ctx)) + _mod_row(sh_ref, is_ctx)
    _store_token_tiles(h2_ref, h2)
    lg_ref[...] = _router_logits(h2, wr_ref, br_ref)


_TOKEN_OUTS = lambda d: [pl.BlockSpec((TM, d), lambda i: (i, 0)),
                         pl.BlockSpec((TM * d // LANES, LANES), lambda i: (i, 0)),
                         pl.BlockSpec((TM, LANES), lambda i: (i, 0))]


def _token_out_shapes(t, d):
    return [jax.ShapeDtypeStruct((t, d), F32), jax.ShapeDtypeStruct((t * d // LANES, LANES), F32),
            jax.ShapeDtypeStruct((t, LANES), F32)]


def _outproj0(x, ctx, gate, hf, hb, w_out_bf, mod, g2, wr, br):
    n, d = x.shape
    nt = n // TM + 1
    t = nt * TM
    tok = pl.BlockSpec((TM, D_RNN), lambda i: (i, 0))
    modspec = lambda c: pl.BlockSpec((SUBLANES, d), lambda i: (0, c))
    return pl.pallas_call(
        _outproj0_kernel,
        grid=(nt,),
        in_specs=[pl.BlockSpec((TM, d), lambda i: (jnp.maximum(i - 1, 0), 0)),
                  _full((TM, d)),
                  tok, tok, tok,
                  _full((D_RNN, d)),
                  modspec(2), _full((1, d)), modspec(3), modspec(4),
                  _full(wr.shape), _full(br.shape)],
        out_specs=_TOKEN_OUTS(d),
        out_shape=_token_out_shapes(t, d),
        compiler_params=_params(),
        name="outproj0",
    )(x, ctx, gate, hf, hb, w_out_bf, mod, g2.reshape(1, -1), mod, mod, wr, br)


def _plan_kernel(n_tiles, nbp, cnt_ref, meta_ref, dest_ref, blk_ref):
    c = cnt_ref[...]
    padded = jnp.floor((c + (TB - 1)) * (1.0 / TB)) * TB
    r = lax.broadcasted_iota(jnp.int32, (N_EXPERTS, N_EXPERTS), 0)
    q = lax.broadcasted_iota(jnp.int32, (N_EXPERTS, N_EXPERTS), 1)
    lower = jnp.where(q <= r, 1.0, 0.0)
    pad_end = jnp.dot(lower, padded, precision=HI, preferred_element_type=F32)
    pad_start = pad_end - padded

    first_row = lax.broadcasted_iota(jnp.int32, (N_EXPERTS, nbp), 1).astype(F32) * TB
    owner = jnp.sum(jnp.where(pad_end[:, 0:1] <= first_row, 1.0, 0.0), axis=0, keepdims=True)
    blk_ref[0:1, :] = jnp.minimum(owner, N_EXPERTS - 1).astype(jnp.int32)
    n_used = pad_end[N_EXPERTS - 1:N_EXPERTS, 0:1] * (1.0 / TB)
    blk_ref[1:2, :] = jnp.broadcast_to(n_used, (1, nbp)).astype(jnp.int32)
    ends = jnp.concatenate([pad_end, jnp.zeros((LANES - N_EXPERTS, LANES), F32)], axis=0).T[0:1, :]
    blk_ref[2:3, :] = jnp.concatenate([ends, jnp.zeros((1, nbp - LANES), F32)], axis=1).astype(jnp.int32)
    blk_ref[3:SUBLANES, :] = jnp.zeros((SUBLANES - 3, nbp), jnp.int32)

    erow = lax.broadcasted_iota(jnp.int32, (N_EXPERTS, TM), 0).astype(F32)

    def body(i, carry):
        ln = pl.ds(pl.multiple_of(i * TM, TM), TM)
        for k in range(2):
            onehot = jnp.where(erow == meta_ref[k:k + 1, ln], 1.0, 0.0)
            start = jnp.sum(onehot * pad_start[:, 0:1], axis=0, keepdims=True)
            dest_ref[k:k + 1, ln] = (start + meta_ref[2 + k:3 + k, ln]).astype(jnp.int32)
        dest_ref[2:SUBLANES, ln] = jnp.zeros((SUBLANES - 2, TM), jnp.int32)
        return carry

    lax.fori_loop(0, n_tiles, body, 0)


def _plan(cnt, meta):
    t = meta.shape[1]
    n_blocks = (2 * t + N_EXPERTS * TB) // TB
    nbp = -(-n_blocks // LANES) * LANES
    vm = pl.BlockSpec(memory_space=pltpu.VMEM)
    dest, blk = pl.pallas_call(
        functools.partial(_plan_kernel, t // TM, nbp),
        in_specs=[vm, vm],
        out_specs=[vm, vm],
        out_shape=[jax.ShapeDtypeStruct((SUBLANES, t), jnp.int32),
                   jax.ShapeDtypeStruct((SUBLANES, nbp), jnp.int32)],
        compiler_params=pltpu.CompilerParams(vmem_limit_bytes=VMEM_LIMIT),
        name="moe_plan",
    )(cnt, meta)
    return dest, blk, n_blocks


def _row_wait(src_ref, dst_ref, d, sem):
    n = TM * d // LANES
    pltpu.make_async_copy(src_ref.at[pl.ds(0, n), :], dst_ref.at[pl.ds(0, n), :], sem).wait()


def _dispatch_kernel(n_blocks, d, pe_ref, nu_ref, dest_ref, h2_ref, buf_ref, dsm_ref, zero_ref,
                     sem_idx, sem_zero, sem_rows):
    step = pl.program_id(0)
    blk = TB * d // LANES

    def zero_block(first_row):
        start = pl.multiple_of(first_row * (d // LANES), blk)
        return pltpu.make_async_copy(zero_ref, buf_ref.at[pl.ds(start, blk), :], sem_zero)

    @pl.when(step == 0)
    def _():
        zero_ref[...] = jnp.zeros_like(zero_ref)

        def for_segments(fn):
            for e in range(N_EXPERTS):
                seg_start = pe_ref[e - 1] if e else 0

                @pl.when(pe_ref[e] > seg_start)
                def _():
                    fn(zero_block(pe_ref[e] - TB))

        for_segments(lambda cp: cp.start())
        lax.fori_loop(nu_ref[0], n_blocks, lambda b, c: (zero_block(b * TB).start(), c)[1], 0)
        for_segments(lambda cp: cp.wait())
        lax.fori_loop(nu_ref[0], n_blocks, lambda b, c: (zero_block(b * TB).wait(), c)[1], 0)

    cp = pltpu.make_async_copy(dest_ref, dsm_ref, sem_idx)
    cp.start()
    cp.wait()

    for r in range(TM):
        for k in range(2):
            pltpu.make_async_copy(_token_tile(h2_ref, r, d), _token_tile(buf_ref, dsm_ref[k, r], d),
                                  sem_rows).start(priority=k)
    for k in range(2):
        _row_wait(h2_ref, buf_ref, d, sem_rows)


def _dispatch(h2t, dest, pad_end, n_used, n_blocks, d):
    per = d // LANES
    t = h2t.shape[0] // per
    grid_spec = pltpu.PrefetchScalarGridSpec(
        num_scalar_prefetch=2,
        grid=(t // TM,),
        in_specs=[pl.BlockSpec((SUBLANES, TM), lambda i, pe, nu: (0, i)),
                  pl.BlockSpec((TM * per, LANES), lambda i, pe, nu: (i, 0))],
        out_specs=pl.BlockSpec(memory_space=pl.ANY),
        scratch_shapes=[pltpu.SMEM((SUBLANES, TM), jnp.int32),
                        pltpu.VMEM((TB * per, LANES), F32),
                        pltpu.SemaphoreType.DMA, pltpu.SemaphoreType.DMA,
                        pltpu.SemaphoreType.DMA],
    )
    return pl.pallas_call(
        functools.partial(_dispatch_kernel, n_blocks, d),
        grid_spec=grid_spec,
        out_shape=jax.ShapeDtypeStruct((n_blocks * TB * per, LANES), F32),
        compiler_params=_params(),
        name="moe_dispatch",
    )(pad_end, n_used, dest, h2t)


def _experts_kernel(be_ref, nu_ref, x_ref, wg_ref, wu_ref, wd_ref, y_ref, wg_bf, wu_bf, wd_bf):
    step = pl.program_id(0)
    used = step < nu_ref[0]
    new_expert = jnp.logical_or(step == 0, be_ref[step] != be_ref[jnp.maximum(step - 1, 0)])

    @pl.when(jnp.logical_and(used, new_expert))
    def _():
        wg_bf[...] = wg_ref[0].astype(BF16)
        wu_bf[...] = wu_ref[0].astype(BF16)
        wd_bf[...] = wd_ref[0].astype(BF16)

    @pl.when(used)
    def _():
        d = wg_bf.shape[0]
        xb = _load_token_tiles(x_ref, TB, d).astype(BF16)
        g = jnp.dot(xb, wg_bf[...], preferred_element_type=F32)
        u = jnp.dot(xb, wu_bf[...], preferred_element_type=F32)
        a = (g * _sigmoid(g)) * u
        _store_token_tiles(y_ref, jnp.dot(a.astype(BF16), wd_bf[...], preferred_element_type=F32))

    @pl.when(jnp.logical_not(used))
    def _():
        y_ref[...] = jnp.zeros_like(y_ref)


def _experts(buf, blk_e, n_used, wg, wu, wd, n_blocks):
    d = wg.shape[1]
    blk = TB * d // LANES
    last = lambda i, nu: jnp.maximum(jnp.minimum(i, nu[0] - 1), 0)
    grid_spec = pltpu.PrefetchScalarGridSpec(
        num_scalar_prefetch=2,
        grid=(n_blocks,),
        in_specs=[pl.BlockSpec((blk, LANES), lambda i, be, nu: (last(i, nu), 0)),
                  pl.BlockSpec((1, d, D_EXPERT), lambda i, be, nu: (be[last(i, nu)], 0, 0)),
                  pl.BlockSpec((1, d, D_EXPERT), lambda i, be, nu: (be[last(i, nu)], 0, 0)),
                  pl.BlockSpec((1, D_EXPERT, d), lambda i, be, nu: (be[last(i, nu)], 0, 0))],
        out_specs=pl.BlockSpec((blk, LANES), lambda i, be, nu: (i, 0)),
        scratch_shapes=[pltpu.VMEM((d, D_EXPERT), BF16), pltpu.VMEM((d, D_EXPERT), BF16),
                        pltpu.VMEM((D_EXPERT, d), BF16)],
    )
    return pl.pallas_call(
        _experts_kernel,
        grid_spec=grid_spec,
        out_shape=jax.ShapeDtypeStruct(buf.shape, F32),
        compiler_params=_params(),
        name="moe_experts",
    )(blk_e, n_used, buf, wg, wu, wd)


def _gather_start(dest_ref, dsm_ref, ybuf_ref, rows_ref, slot, d, sem_idx, sem):
    cp = pltpu.make_async_copy(dest_ref, dsm_ref, sem_idx)
    cp.start()
    cp.wait()
    for r in range(TM):
        for k in range(2):
            pltpu.make_async_copy(_token_tile(ybuf_ref, dsm_ref[k, r], d),
                                  _token_tile(rows_ref.at[slot, k], r, d), sem).start(priority=k)


def _gathered_rows(dest_ref, next_ref, dsm_ref, ybuf_ref, rows_ref, d, sem_idx, sems):
    step = pl.program_id(0)
    nt = pl.num_programs(0)
    slot = step % 2

    @pl.when(step == 0)
    def _():
        _gather_start(dest_ref, dsm_ref, ybuf_ref, rows_ref, 0, d, sem_idx, sems.at[0])

    @pl.when(step + 1 < nt)
    def _():
        _gather_start(next_ref, dsm_ref, ybuf_ref, rows_ref, 1 - slot, d, sem_idx, sems.at[1 - slot])

    for k in range(2):
        _row_wait(ybuf_ref, rows_ref.at[slot, k], d, sems.at[slot])
    return [_load_token_tiles(rows_ref.at[slot, k], TM, d) for k in range(2)]


def _token_weights(meta_ref):
    meta = jnp.concatenate([meta_ref[...], jnp.zeros((LANES - SUBLANES, TM), F32)], axis=0)
    mt = meta.T
    return mt[:, 4:5], mt[:, 5:6]


_COMBINE_SCRATCH = lambda d: [pltpu.SMEM((SUBLANES, TM), jnp.int32),
                              pltpu.VMEM((2, 2, TM * d // LANES, LANES), F32),
                              pltpu.SemaphoreType.DMA, pltpu.SemaphoreType.DMA((2,))]


def _next_tile_spec(nt):
    return pl.BlockSpec((SUBLANES, TM), lambda i: (0, jnp.minimum(i + 1, nt - 1)))


def _combine0_kernel(x1_ref, dest_ref, next_ref, meta_ref, gt_ref, ybuf_ref, x2_ref, ctx2_ref,
                     dsm_ref, rows_ref, sem_idx, sems):
    is_ctx = pl.program_id(0) == 0
    y0, y1 = _gathered_rows(dest_ref, next_ref, dsm_ref, ybuf_ref, rows_ref, x1_ref.shape[1],
                            sem_idx, sems)
    w0, w1 = _token_weights(meta_ref)
    out = x1_ref[...] + _mod_row(gt_ref, is_ctx) * (w0 * y0 + w1 * y1)
    x2_ref[...] = out

    @pl.when(is_ctx)
    def _():
        ctx2_ref[...] = out


def _combine0(x1, dest, meta, mod, ybuf):
    t, d = x1.shape
    nt = t // TM
    return pl.pallas_call(
        _combine0_kernel,
        grid=(nt,),
        in_specs=[pl.BlockSpec((TM, d), lambda i: (i, 0)),
                  pl.BlockSpec((SUBLANES, TM), lambda i: (0, i)),
                  _next_tile_spec(nt),
                  pl.BlockSpec((SUBLANES, TM), lambda i: (0, i)),
                  pl.BlockSpec((SUBLANES, d), lambda i: (0, 5)),
                  pl.BlockSpec(memory_space=pl.ANY)],
        out_specs=[pl.BlockSpec((TM, d), lambda i: (jnp.maximum(i - 1, 0), 0)),
                   _full((TM, d))],
        out_shape=[jax.ShapeDtypeStruct((t - TM, d), F32), jax.ShapeDtypeStruct((TM, d), F32)],
        scratch_shapes=_COMBINE_SCRATCH(d),
        compiler_params=_params(),
        name="moe_combine0",
    )(x1, dest, dest, meta, mod, ybuf)


def _combine1_kernel(x1_ref, dest_ref, next_ref, meta_ref, gt_ref, gf_ref, ybuf_ref, o_ref,
                     dsm_ref, rows_ref, sem_idx, sems):
    y0, y1 = _gathered_rows(dest_ref, next_ref, dsm_ref, ybuf_ref, rows_ref, x1_ref.shape[1],
                            sem_idx, sems)
    w0, w1 = _token_weights(meta_ref)
    out = x1_ref[...] + gt_ref[0:1, :] * (w0 * y0 + w1 * y1)
    o_ref[...] = _rms(out, gf_ref[...])


def _combine1(x1, dest, meta, mod, gf, ybuf):
    t, d = x1.shape
    rows = t // GRID_W
    nt = t // TM
    out = pl.pallas_call(
        _combine1_kernel,
        grid=(nt,),
        in_specs=[pl.BlockSpec((TM, d), lambda i: (i, 0)),
                  pl.BlockSpec((SUBLANES, TM), lambda i: (0, i)),
                  _next_tile_spec(nt),
                  pl.BlockSpec((SUBLANES, TM), lambda i: (0, i)),
                  pl.BlockSpec((SUBLANES, d), lambda i: (0, 5)),
                  _full((1, d)),
                  pl.BlockSpec(memory_space=pl.ANY)],
        out_specs=pl.BlockSpec((TM, d), lambda i: (0, i)),
        out_shape=jax.ShapeDtypeStruct((rows, GRID_W * d), F32),
        scratch_shapes=_COMBINE_SCRATCH(d),
        compiler_params=_params(),
        name="moe_combine1",
    )(x1, dest, dest, meta, mod, gf.reshape(1, -1), ybuf)
    return out.reshape(t, d)


def _moe(h2t, meta, cnt, w_gate, w_up, w_down):
    dest, blk, n_blocks = _plan(cnt, meta)
    n_used = blk[1, :1]
    buf = _dispatch(h2t, dest, blk[2, :N_EXPERTS], n_used, n_blocks, w_gate.shape[1])
    ybuf = _experts(buf, blk[0], n_used, w_gate, w_up, w_down, n_blocks)
    return dest, ybuf


def _inproj1_kernel(x_ref, ctx_ref, g_ref, sh_ref, sc_ref, wq_ref, wkt_ref, wv_ref, wo_ref,
                    wg_ref, bg_ref, q_ref, kt_ref, v_ref, o_ref, gcol_ref):
    is_ctx = pl.program_id(0) == 0
    xin = jnp.where(is_ctx, ctx_ref[...], x_ref[...])
    h = _rms(xin, g_ref[...]) * (1.0 + _mod_row(sc_ref, is_ctx)) + _mod_row(sh_ref, is_ctx)
    hb = h.astype(BF16)
    q = jnp.dot(hb, wq_ref[...], preferred_element_type=F32) * (ML_DQK ** -0.5)
    q_ref[...] = q.astype(BF16)
    kt = lax.dot_general(wkt_ref[...], hb, (((1,), (1,)), ((), ())), preferred_element_type=F32)
    kt_ref[...] = kt.astype(BF16)
    v_ref[...] = jnp.dot(hb, wv_ref[...], preferred_element_type=F32).astype(BF16)
    o_ref[...] = jnp.dot(hb, wo_ref[...], preferred_element_type=F32)
    gcol_ref[...] = _dot_bf16x3(h, wg_ref) + bg_ref[...]


def _inproj1(x, ctx, g1, mod, w_in, b_gates):
    n, d = x.shape
    rows = n // GRID_W
    nt = GRID_W + 1
    t = nt * TM
    ng = 4 * ML_HEADS
    wq = w_in[:, :ML_QK_W].astype(BF16)
    wkt = w_in[:, ML_QK_W:2 * ML_QK_W].T.astype(BF16)
    wv = w_in[:, 2 * ML_QK_W:2 * ML_QK_W + ML_V_W].astype(BF16)
    wo = w_in[:, 2 * ML_QK_W + ML_V_W:2 * ML_QK_W + ML_V_W + d].astype(BF16)
    wg = _split_bf16(jnp.zeros((d, LANES), F32).at[:, :ng].set(w_in[:, -ng:]))
    bg = jnp.zeros((1, LANES), F32).at[0, :ng].set(b_gates)
    tok = lambda w: pl.BlockSpec((TM, w), lambda i: (i, 0))
    return pl.pallas_call(
        _inproj1_kernel,
        grid=(nt,),
        in_specs=[pl.BlockSpec((rows, d), lambda i: (0, jnp.maximum(i - 1, 0))),
                  _full((TM, d)),
                  _full((1, d)),
                  pl.BlockSpec((SUBLANES, d), lambda i: (0, 0)),
                  pl.BlockSpec((SUBLANES, d), lambda i: (0, 1)),
                  _full(wq.shape), _full(wkt.shape), _full(wv.shape), _full(wo.shape),
                  _full(wg.shape), _full(bg.shape)],
        out_specs=[tok(ML_QK_W),
                   pl.BlockSpec((ML_QK_W, TM), lambda i: (0, i)),
                   tok(ML_V_W), tok(d), tok(LANES)],
        out_shape=[jax.ShapeDtypeStruct((t, ML_QK_W), BF16),
                   jax.ShapeDtypeStruct((ML_QK_W, t), BF16),
                   jax.ShapeDtypeStruct((t, ML_V_W), BF16),
                   jax.ShapeDtypeStruct((t, d), F32),
                   jax.ShapeDtypeStruct((t, LANES), F32)],
        compiler_params=_params(),
        name="inproj1",
    )(x.reshape(rows, GRID_W * d), ctx, g1.reshape(1, -1), mod, mod, wq, wkt, wv, wo, wg, bg)


def _log_sigmoid(x):
    return jnp.minimum(x, 0.0) - jnp.log1p(jnp.exp(-jnp.abs(x)))


LOG2E = 1.4426950408889634


def _mlstm_kernel(reverse, q_ref, kt_ref, v_ref, gcol_ref, h_ref, c_ref, m_ref):
    @pl.when(pl.program_id(0) == 0)
    def _():
        c_ref[...] = jnp.zeros_like(c_ref)
        m_ref[...] = jnp.zeros_like(m_ref)

    L = TM
    half = L // 2
    gi = 2 * ML_HEADS if reverse else 0
    gf = gi + ML_HEADS
    end = 0 if reverse else L - 1
    rr = lax.broadcasted_iota(jnp.int32, (L, L), 0)
    cc = lax.broadcasted_iota(jnp.int32, (L, L), 1)
    tri = jnp.where((cc >= rr) if reverse else (cc <= rr), 1.0, 0.0).astype(BF16)
    rh = lax.broadcasted_iota(jnp.int32, (half, half), 0)
    ch = lax.broadcasted_iota(jnp.int32, (half, half), 1)
    diag = (ch >= rh) if reverse else (ch <= rh)

    gates = gcol_ref[...]
    lane = lax.broadcasted_iota(jnp.int32, (L, LANES), 1)
    mine = jnp.logical_and(lane >= gf, lane < gf + ML_HEADS)
    lf = jnp.where(mine, _log_sigmoid(gates), 0.0)
    p0 = lf.astype(BF16)
    r1 = lf - p0.astype(F32)
    p1 = r1.astype(BF16)
    p2 = (r1 - p1.astype(F32)).astype(BF16)
    cum = (jnp.dot(tri, p0, preferred_element_type=F32) + jnp.dot(tri, p1, preferred_element_type=F32)
           + jnp.dot(tri, p2, preferred_element_type=F32))

    r = jnp.where(mine, pltpu.roll(gates, ML_HEADS, 1) - cum, 0.0)
    row = lax.broadcasted_iota(jnp.int32, (L, LANES), 0)
    cm = r
    sh = 1
    while sh < L:
        if sh < SUBLANES:
            if reverse:
                cm = jnp.where(row < L - sh, jnp.maximum(cm, pltpu.roll(cm, L - sh, 0)), cm)
            else:
                cm = jnp.where(row >= sh, jnp.maximum(cm, pltpu.roll(cm, sh, 0)), cm)
        else:
            pad = jnp.full((sh, LANES), NEG_INF, F32)
            moved = (jnp.concatenate([cm[sh:], pad], axis=0) if reverse
                     else jnp.concatenate([pad, cm[:L - sh]], axis=0))
            cm = jnp.maximum(cm, moved)
        sh *= 2
    r8 = r.T[gf:gf + ML_HEADS]

    m_prev = m_ref[0:1, :]
    mm = jnp.maximum(m_prev, cm)
    m_t = cum + mm
    w_inter = jnp.exp(m_prev - mm)
    floor = jnp.exp(-m_t)
    mm2 = mm * LOG2E
    m_new = m_t[end:end + 1, :]
    shift2 = (cum[end:end + 1, :] - m_new) * LOG2E
    decay = jnp.exp(cum[end:end + 1, :] + m_prev - m_new)
    m_ref[0:1, :] = m_new

    ones_col = jnp.where(lax.broadcasted_iota(jnp.int32, (L, ML_DV), 1) == 0, 1.0, 0.0).astype(BF16)
    top, bot = slice(0, half), slice(half, L)
    dot = functools.partial(jnp.dot, preferred_element_type=F32)

    for hd in range(ML_HEADS):
        ln = gf + hd
        r2_row = r8[hd:hd + 1, :] * LOG2E

        def weights(tq, ks, masked):
            w = jnp.exp2(r2_row[:, ks] - mm2[tq, ln:ln + 1])
            return jnp.where(diag, w, 0.0) if masked else w

        qh = q_ref[:, hd * ML_DQK:(hd + 1) * ML_DQK]
        kth = kt_ref[hd * ML_DQK:(hd + 1) * ML_DQK, :]
        vext = jnp.concatenate([v_ref[:, hd * ML_DV:(hd + 1) * ML_DV], ones_col], axis=1)
        if reverse:
            s_top = dot(qh[top], kth) * jnp.concatenate([weights(top, top, True),
                                                         weights(top, bot, False)], axis=1)
            s_bot = dot(qh[bot], kth[:, bot]) * weights(bot, bot, True)
            intra = jnp.concatenate([dot(s_top.astype(BF16), vext),
                                     dot(s_bot.astype(BF16), vext[bot])], axis=0)
        else:
            s_top = dot(qh[top], kth[:, top]) * weights(top, top, True)
            s_bot = dot(qh[bot], kth) * jnp.concatenate([weights(bot, top, False),
                                                         weights(bot, bot, True)], axis=1)
            intra = jnp.concatenate([dot(s_top.astype(BF16), vext[top]),
                                     dot(s_bot.astype(BF16), vext)], axis=0)
        state = c_ref[hd]
        tot = w_inter[:, ln:ln + 1] * dot(qh, state.astype(BF16)) + intra
        den = tot[:, ML_DV:ML_DV + 1]
        h_ref[:, hd * ML_DV:(hd + 1) * ML_DV] = (
            tot[:, :ML_DV] / jnp.maximum(jnp.abs(den), floor[:, ln:ln + 1]))

        w_state = jnp.exp2(r2_row + shift2[:, ln:ln + 1])
        kw = (kth.astype(F32) * w_state).astype(BF16)
        c_ref[hd] = decay[:, ln:ln + 1] * state + dot(kw, vext)


def _mlstm(q, kt, v, gcol, reverse):
    t = q.shape[0]
    nt = t // TM
    tile = lambda s: _scan_tile(reverse, s, nt)
    return pl.pallas_call(
        functools.partial(_mlstm_kernel, reverse),
        grid=(nt,),
        in_specs=[pl.BlockSpec((TM, ML_QK_W), lambda s: (tile(s), 0)),
                  pl.BlockSpec((ML_QK_W, TM), lambda s: (0, tile(s))),
                  pl.BlockSpec((TM, ML_V_W), lambda s: (tile(s), 0)),
                  pl.BlockSpec((TM, LANES), lambda s: (tile(s), 0))],
        out_specs=pl.BlockSpec((TM, ML_V_W), lambda s: (tile(s), 0)),
        out_shape=jax.ShapeDtypeStruct((t, ML_V_W), F32),
        scratch_shapes=[pltpu.VMEM((ML_HEADS, ML_DQK, 2 * ML_DV), F32),
                        pltpu.VMEM((SUBLANES, LANES), F32)],
        compiler_params=_params(),
        name="mlstm_bwd" if reverse else "mlstm_fwd",
    )(q, kt, v, gcol)


def _finish1_kernel(x_ref, hf_ref, hb_ref, o_ref, ng_ref, w_ref, gt_ref, g2_ref, sh_ref, sc_ref,
                    wr_ref, br_ref, x1_ref, h2_ref, lg_ref):
    hs = hf_ref[...] + hb_ref[...]
    parts = []
    for hd in range(ML_HEADS):
        blk = hs[:, hd * ML_DV:(hd + 1) * ML_DV]
        parts.append(blk * lax.rsqrt(jnp.mean(blk * blk, axis=-1, keepdims=True) + EPS))
    hn = jnp.concatenate(parts, axis=1) * ng_ref[...]
    y = jnp.dot((hn * _sigmoid(o_ref[...])).astype(BF16), w_ref[...], preferred_element_type=F32)
    x1 = x_ref[...] + gt_ref[0:1, :] * y
    x1_ref[...] = x1
    h2 = _rms(x1, g2_ref[...]) * (1.0 + sc_ref[0:1, :]) + sh_ref[0:1, :]
    _store_token_tiles(h2_ref, h2)
    lg_ref[...] = _router_logits(h2, wr_ref, br_ref)


def _finish1(x, hf, hb, o, norm_g, w_out_bf, mod, g2, wr, br):
    n, d = x.shape
    rows = n // GRID_W
    lat = lambda w: pl.BlockSpec((TM, w), lambda i: (i + 1, 0))
    modspec = lambda c: pl.BlockSpec((SUBLANES, d), lambda i: (0, c))
    return pl.pallas_call(
        _finish1_kernel,
        grid=(GRID_W,),
        in_specs=[pl.BlockSpec((rows, d), lambda i: (0, i)),
                  lat(ML_V_W), lat(ML_V_W), lat(d),
                  _full((1, ML_V_W)), _full((ML_V_W, d)),
                  modspec(2), _full((1, d)), modspec(3), modspec(4),
                  _full(wr.shape), _full(br.shape)],
        out_specs=_TOKEN_OUTS(d),
        out_shape=_token_out_shapes(n, d),
        compiler_params=_params(),
        name="finish1",
    )(x.reshape(rows, GRID_W * d), hf, hb, o, norm_g.reshape(1, -1), w_out_bf, mod,
      g2.reshape(1, -1), mod, mod, wr, br)


def kernel(x, c, ctx, c_ctx,
           l0_ada_w, l0_ada_b, l0_norm1_g, l0_norm2_g,
           l0_rg_w_in, l0_rg_conv_w, l0_rg_conv_b, l0_rg_w_a, l0_rg_b_a, l0_rg_w_x, l0_rg_b_x,
           l0_rg_lambda, l0_rg_w_out,
           l0_moe_w_grp, l0_moe_b_grp, l0_moe_w_exp, l0_moe_b_exp, l0_moe_w_gate, l0_moe_w_up,
           l0_moe_w_down,
           l1_ada_w, l1_ada_b, l1_norm1_g, l1_norm2_g,
           l1_ml_w_in, l1_ml_b_gates, l1_ml_norm_g, l1_ml_w_out,
           l1_moe_w_grp, l1_moe_b_grp, l1_moe_w_exp, l1_moe_b_exp, l1_moe_w_gate, l1_moe_w_up,
           l1_moe_w_down,
           final_norm_g):
    assert x.shape[0] == 1 and ctx.shape[1] == TM and x.shape[1] == GRID_W * TM
    xs, cs = x[0], ctx[0]
    d = xs.shape[1]
    cond8 = jnp.zeros((SUBLANES, d), F32).at[0].set(c[0]).at[1].set(c_ctx)

    mod0 = _adaln(cond8, l0_ada_w, l0_ada_b)
    conv_w8 = jnp.zeros((SUBLANES, D_RNN), F32).at[:CONV_W].set(l0_rg_conv_w)
    gate, u = _inproj0(xs, cs, l0_norm1_g, mod0, l0_rg_w_in.astype(BF16), conv_w8, l0_rg_conv_b)
    h_dirs = []
    for dr in range(2):
        wcat = jnp.concatenate([l0_rg_w_a[dr], l0_rg_w_x[dr]], axis=-1).astype(BF16)
        h_dirs.append(_rglru_scan(u, dr == 1, wcat, l0_rg_b_a[dr], l0_rg_b_x[dr], l0_rg_lambda[dr]))
    wr0, br0 = _router_weights(l0_moe_w_grp, l0_moe_b_grp, l0_moe_w_exp, l0_moe_b_exp)
    x1, h2, logits = _outproj0(xs, cs, gate, h_dirs[0], h_dirs[1], l0_rg_w_out.astype(BF16), mod0,
                               l0_norm2_g, wr0, br0)
    meta, cnt = _route(logits, ROUTE_TILE_0)
    dest, ybuf = _moe(h2, meta, cnt, l0_moe_w_gate, l0_moe_w_up, l0_moe_w_down)
    x2, ctx2 = _combine0(x1, dest, meta, mod0, ybuf)

    mod1 = _adaln(cond8, l1_ada_w, l1_ada_b)
    q, kt, v, o, gcol = _inproj1(x2, ctx2, l1_norm1_g, mod1, l1_ml_w_in, l1_ml_b_gates)
    hf = _mlstm(q, kt, v, gcol, False)
    hb = _mlstm(q, kt, v, gcol, True)
    wr1, br1 = _router_weights(l1_moe_w_grp, l1_moe_b_grp, l1_moe_w_exp, l1_moe_b_exp)
    x1, h2, logits = _finish1(x2, hf, hb, o, l1_ml_norm_g, l1_ml_w_out.astype(BF16), mod1,
                              l1_norm2_g, wr1, br1)
    meta, cnt = _route(logits, ROUTE_TILE_1)
    dest, ybuf = _moe(h2, meta, cnt, l1_moe_w_gate, l1_moe_w_up, l1_moe_w_down)
    out = _combine1(x1, dest, meta, mod1, final_norm_g, ybuf)
    return out[None]
```

```python
import functools

import jax
import jax.numpy as jnp
from jax import lax
from jax.experimental import pallas as pl
from jax.experimental.pallas import tpu as pltpu

D_MODEL = 1024
GRID_W = 64
N_MOD = 6
EPS = 1e-6

D_RNN = 1280
RG_BLOCKS = 10
RG_BLOCK_W = D_RNN // RG_BLOCKS
CONV_W = 4
CONV_PAD_L = 2
RG_C = 8.0

ML_HEADS = 8
ML_DQK = D_MODEL // (2 * ML_HEADS)
ML_DV = D_MODEL // ML_HEADS
ML_QK_W = ML_HEADS * ML_DQK
ML_V_W = ML_HEADS * ML_DV

N_GROUPS = 4
EXPERTS_PER_GROUP = 8
N_EXPERTS = N_GROUPS * EXPERTS_PER_GROUP
D_EXPERT = 512

TM = 256
TB = 256
ROUTE_TILE_0 = 1280
ROUTE_TILE_1 = 1024
SUBLANES = 8
LANES = 128
VMEM_LIMIT = 48 * 1024 * 1024

F32 = jnp.float32
BF16 = jnp.bfloat16
HI = lax.Precision.HIGHEST
NEG_INF = float("-inf")


def _params(n_axes=1):
    return pltpu.CompilerParams(dimension_semantics=("arbitrary",) * n_axes,
                                vmem_limit_bytes=VMEM_LIMIT)


def _rms(x, g):
    return x * lax.rsqrt(jnp.mean(x * x, axis=-1, keepdims=True) + EPS) * g


def _sigmoid(x):
    return 1.0 / (1.0 + jnp.exp(-x))


def _softplus(x):
    return jnp.maximum(x, 0.0) + jnp.log1p(jnp.exp(-jnp.abs(x)))


def _gelu_tanh(x):
    return 0.5 * x * (1.0 + jnp.tanh(0.7978845608028654 * (x + 0.044715 * (x * x * x))))


def _full(shape):
    return pl.BlockSpec(shape, lambda *_: (0,) * len(shape))


def _store_token_tiles(ref, x):
    per = x.shape[1] // LANES
    for c in range(per):
        ref[pl.ds(c, x.shape[0], stride=per), :] = x[:, c * LANES:(c + 1) * LANES]


def _load_token_tiles(ref, rows, d):
    per = d // LANES
    return jnp.concatenate([ref[pl.ds(c, rows, stride=per), :] for c in range(per)], axis=1)


def _token_tile(ref, row, d):
    per = d // LANES
    start = row * per if isinstance(row, int) else pl.multiple_of(row * per, per)
    return ref.at[pl.ds(start, per), :]


def _adaln_kernel(cond_ref, w_ref, b_ref, o_ref):
    c = cond_ref[...]
    s = c * _sigmoid(c)
    o_ref[...] = jnp.dot(s, w_ref[...], precision=HI, preferred_element_type=F32) + b_ref[...]


def _adaln(cond8, w, b):
    d = w.shape[0]
    return pl.pallas_call(
        _adaln_kernel,
        grid=(N_MOD,),
        in_specs=[_full((SUBLANES, d)),
                  pl.BlockSpec((d, d), lambda j: (0, j)),
                  pl.BlockSpec((1, d), lambda j: (0, j))],
        out_specs=pl.BlockSpec((SUBLANES, d), lambda j: (0, j)),
        out_shape=jax.ShapeDtypeStruct((SUBLANES, N_MOD * d), F32),
        compiler_params=_params(),
        name="adaln",
    )(cond8, w, b.reshape(1, -1))


def _mod_row(ref, is_ctx):
    return jnp.where(is_ctx, ref[1:2, :], ref[0:1, :])


def _front0_kernel(nt, x_ref, ctx_ref, g_ref, sh_ref, sc_ref, w_ref, cw_ref, cb_ref,
                   wcat_ref, ba_ref, bx_ref, lam_ref, gate_ref, u_ref, hf_ref,
                   ext_ref, ubuf_ref, a_ref, b_ref, o_ref, carry_ref):
    s = pl.program_id(0)
    is_ctx = s == 0

    @pl.when(s == 0)
    def _():
        ext_ref[...] = jnp.zeros_like(ext_ref)
        ubuf_ref[...] = jnp.zeros_like(ubuf_ref)

    @pl.when(s <= 2)
    def _():
        carry_ref[...] = jnp.zeros_like(carry_ref)

    xin = jnp.where(is_ctx, ctx_ref[...], x_ref[...])
    h = _rms(xin, g_ref[...]) * (1.0 + _mod_row(sc_ref, is_ctx)) + _mod_row(sh_ref, is_ctx)
    hb = h.astype(BF16)

    right_valid = jnp.logical_and(s >= 2, s <= nt - 1)
    base = SUBLANES - CONV_PAD_L
    for j in range(RG_BLOCKS):
        ln = slice(j * RG_BLOCK_W, (j + 1) * RG_BLOCK_W)
        _rglru_block(False, j, ubuf_ref.at[s % 2], wcat_ref, ba_ref, bx_ref, lam_ref, hf_ref,
                     a_ref, b_ref, o_ref, carry_ref)
        p = jnp.dot(hb, w_ref[j], preferred_element_type=F32)
        gate_ref[:, ln] = p[:, :RG_BLOCK_W]
        rec = p[:, RG_BLOCK_W:]
        ext_ref[j, SUBLANES + TM:, :] = jnp.where(right_valid, rec[0:SUBLANES], 0.0)
        u = cb_ref[:, ln] + ext_ref[j, pl.ds(base, TM), :] * cw_ref[0:1, ln]
        for k in range(1, CONV_W):
            u = u + ext_ref[j, pl.ds(base + k, TM), :] * cw_ref[k:k + 1, ln]
        u_ref[:, ln] = u
        ubuf_ref[(s + 1) % 2, :, ln] = u
        ext_ref[j, 0:SUBLANES, :] = jnp.where(s >= 2, ext_ref[j, TM:TM + SUBLANES, :], 0.0)
        ext_ref[j, SUBLANES:SUBLANES + TM, :] = rec


_SCAN_SLAB = lambda: pltpu.VMEM((RG_BLOCKS, SUBLANES * SCAN_PITCH, LANES), F32)
_SCAN_SCRATCH = lambda: [_SCAN_SLAB(), _SCAN_SLAB(), _SCAN_SLAB(), pltpu.VMEM((SUBLANES, D_RNN), F32)]


def _front0(x, ctx, g1, mod, w_in_bf, conv_w8, conv_b, wcat, b_a, b_x, lam):
    n, d = x.shape
    nx = n // TM
    nt = nx + 1
    t = nt * TM
    row = lambda v: v.reshape(1, -1)
    w_blocks = jnp.concatenate([w_in_bf[:, :D_RNN].reshape(d, RG_BLOCKS, RG_BLOCK_W),
                                w_in_bf[:, D_RNN:].reshape(d, RG_BLOCKS, RG_BLOCK_W)], axis=2)
    w_blocks = w_blocks.transpose(1, 0, 2)
    return pl.pallas_call(
        functools.partial(_front0_kernel, nt),
        grid=(nt + 2,),
        in_specs=[pl.BlockSpec((TM, d), lambda s: (jnp.clip(s - 1, 0, nx - 1), 0)),
                  _full((TM, d)),
                  _full((1, d)),
                  pl.BlockSpec((SUBLANES, d), lambda s: (0, 0)),
                  pl.BlockSpec((SUBLANES, d), lambda s: (0, 1)),
                  _full((RG_BLOCKS, d, 2 * RG_BLOCK_W)),
                  _full((SUBLANES, D_RNN)), _full((1, D_RNN)),
                  _full((RG_BLOCKS, RG_BLOCK_W, 2 * RG_BLOCK_W)),
                  _full((1, D_RNN)), _full((1, D_RNN)), _full((1, D_RNN))],
        out_specs=[pl.BlockSpec((TM, D_RNN), lambda s: (jnp.minimum(s, nt - 1), 0)),
                   pl.BlockSpec((TM, D_RNN), lambda s: (jnp.maximum(s - 1, 0), 0)),
                   pl.BlockSpec((TM, D_RNN), lambda s: (jnp.maximum(s - 2, 0), 0))],
        out_shape=[jax.ShapeDtypeStruct((t, D_RNN), F32),
                   jax.ShapeDtypeStruct((t + TM, D_RNN), F32),
                   jax.ShapeDtypeStruct((t, D_RNN), F32)],
        scratch_shapes=[pltpu.VMEM((RG_BLOCKS, TM + 2 * SUBLANES, LANES), F32),
                        pltpu.VMEM((2, TM, D_RNN), F32)] + _SCAN_SCRATCH(),
        compiler_params=_params(),
        name="front0",
    )(x, ctx, g1.reshape(1, -1), mod, mod, w_blocks, conv_w8, row(conv_b), wcat, row(b_a), row(b_x),
      row(lam))


def _scan_tile(reverse, s, nt):
    if not reverse:
        return s
    return jnp.where(s == 0, 0, nt - s)


SCAN_CHUNK = TM // SUBLANES
SCAN_PITCH = SCAN_CHUNK + 4


def _rglru_block(reverse, j, u_ref, wcat_ref, ba_ref, bx_ref, lam_ref, h_ref,
                 a_ref, b_ref, o_ref, carry_ref):
    steps = range(SCAN_CHUNK - 1, -1, -1) if reverse else range(SCAN_CHUNK)
    chunks = range(SUBLANES - 1, -1, -1) if reverse else range(SUBLANES)
    ln = slice(j * RG_BLOCK_W, (j + 1) * RG_BLOCK_W)
    u = u_ref[:, ln]
    g = jnp.dot(u.astype(BF16), wcat_ref[j], preferred_element_type=F32)
    half_rate = (-0.5 * RG_C) * _softplus(-lam_ref[:, ln])
    log_a = jnp.tanh(0.5 * (g[:, :RG_BLOCK_W] + ba_ref[:, ln])) * half_rate + half_rate
    ig = 0.5 * jnp.tanh(0.5 * (g[:, RG_BLOCK_W:] + bx_ref[:, ln])) + 0.5
    a = jnp.exp(log_a)
    b = jnp.sqrt(-jnp.tanh(log_a) * (a * a + 1.0)) * ig * u
    for c in range(SUBLANES):
        a_ref[j, pl.ds(c * SCAN_PITCH, SCAN_CHUNK), :] = a[c * SCAN_CHUNK:(c + 1) * SCAN_CHUNK]
        b_ref[j, pl.ds(c * SCAN_PITCH, SCAN_CHUNK), :] = b[c * SCAN_CHUNK:(c + 1) * SCAN_CHUNK]

    row = lambda ref, i: ref[j, pl.ds(i, SUBLANES, stride=SCAN_PITCH), :]
    end = jnp.zeros((SUBLANES, LANES), F32)
    decay = jnp.ones((SUBLANES, LANES), F32)
    for i in steps:
        ai = row(a_ref, i)
        end = ai * end + row(b_ref, i)
        decay = decay * ai

    state = carry_ref[0:1, ln]
    entry = [None] * SUBLANES
    for c in chunks:
        entry[c] = state
        state = decay[c:c + 1] * state + end[c:c + 1]
    carry_ref[0:1, ln] = state

    hcur = jnp.concatenate(entry, axis=0)
    for i in steps:
        hcur = row(a_ref, i) * hcur + row(b_ref, i)
        o_ref[j, pl.ds(i, SUBLANES, stride=SCAN_PITCH), :] = hcur
    for c in range(SUBLANES):
        h_ref[c * SCAN_CHUNK:(c + 1) * SCAN_CHUNK, ln] = o_ref[j, pl.ds(c * SCAN_PITCH, SCAN_CHUNK), :]


def _split_bf16(w):
    hi = w.astype(BF16)
    return jnp.stack([hi, (w - hi.astype(F32)).astype(BF16)])


def _dot_bf16x3(x, w_ref):
    hi = x.astype(BF16)
    lo = (x - hi.astype(F32)).astype(BF16)
    acc = jnp.dot(hi, w_ref[0], preferred_element_type=F32)
    acc = acc + jnp.dot(lo, w_ref[0], preferred_element_type=F32)
    return acc + jnp.dot(hi, w_ref[1], preferred_element_type=F32)


def _router_logits(h2, wr_ref, br_ref):
    return _dot_bf16x3(h2, wr_ref) + br_ref[...]


def _route_kernel(lg_ref, meta_ref, cnt_ref, carry_ref):
    step = pl.program_id(0)
    tm = lg_ref.shape[0]

    @pl.when(step == 0)
    def _():
        carry_ref[...] = jnp.zeros_like(carry_ref)

    logits = jnp.concatenate([lg_ref[i * LANES:(i + 1) * LANES, :].T for i in range(tm // LANES)],
                             axis=1)
    row8 = lax.broadcasted_iota(jnp.int32, (SUBLANES, tm), 0)
    grp_logits = jnp.where(row8 < N_GROUPS, logits[0:SUBLANES], NEG_INF)
    gmax = jnp.max(grp_logits, axis=0, keepdims=True)
    p_sel = 1.0 / jnp.sum(jnp.exp(grp_logits - gmax), axis=0, keepdims=True)
    grp = jnp.min(jnp.where(grp_logits == gmax, row8, SUBLANES), axis=0, keepdims=True)

    in_grp = logits[SUBLANES + (N_GROUPS - 1) * EXPERTS_PER_GROUP:SUBLANES + N_GROUPS * EXPERTS_PER_GROUP]
    for gi in range(N_GROUPS - 2, -1, -1):
        lo = SUBLANES + gi * EXPERTS_PER_GROUP
        in_grp = jnp.where(grp == gi, logits[lo:lo + EXPERTS_PER_GROUP], in_grp)
    v1 = jnp.max(in_grp, axis=0, keepdims=True)
    i1 = jnp.min(jnp.where(in_grp == v1, row8, EXPERTS_PER_GROUP), axis=0, keepdims=True)
    rest = jnp.where(row8 == i1, NEG_INF, in_grp)
    v2 = jnp.max(rest, axis=0, keepdims=True)
    i2 = jnp.min(jnp.where(rest == v2, row8, EXPERTS_PER_GROUP), axis=0, keepdims=True)
    e2 = jnp.exp(v2 - v1)
    w1 = p_sel / (1.0 + e2)
    w2 = p_sel * e2 / (1.0 + e2)
    eid = (grp * EXPERTS_PER_GROUP + i1, grp * EXPERTS_PER_GROUP + i2)

    rr = lax.broadcasted_iota(jnp.int32, (LANES, LANES), 0)
    cc = lax.broadcasted_iota(jnp.int32, (LANES, LANES), 1)
    strict_upper = jnp.where(rr < cc, 1.0, 0.0).astype(BF16)
    erow = lax.broadcasted_iota(jnp.int32, (N_EXPERTS, LANES), 0)
    base = carry_ref[:, 0:1]
    for k in range(2):
        for i in range(tm // LANES):
            ln = slice(i * LANES, (i + 1) * LANES)
            onehot = jnp.where(erow == eid[k][:, ln], 1.0, 0.0)
            pre = jnp.dot(onehot.astype(BF16), strict_upper, preferred_element_type=F32)
            meta_ref[2 + k:3 + k, ln] = jnp.sum(onehot * (base + pre), axis=0, keepdims=True)
            base = base + jnp.sum(onehot, axis=1, keepdims=True)
    carry_ref[...] = jnp.broadcast_to(base, carry_ref.shape)
    cnt_ref[...] = jnp.broadcast_to(base, cnt_ref.shape)

    meta_ref[0:1, :] = eid[0].astype(F32)
    meta_ref[1:2, :] = eid[1].astype(F32)
    meta_ref[4:5, :] = w1
    meta_ref[5:6, :] = w2
    meta_ref[6:8, :] = jnp.zeros((2, tm), F32)


def _router_weights(w_grp, b_grp, w_exp, b_exp):
    d = w_grp.shape[0]
    wr = jnp.zeros((d, LANES), F32).at[:, :N_GROUPS].set(w_grp)
    wr = wr.at[:, SUBLANES:SUBLANES + N_EXPERTS].set(w_exp)
    br = jnp.zeros((1, LANES), F32).at[0, :N_GROUPS].set(b_grp)
    br = br.at[0, SUBLANES:SUBLANES + N_EXPERTS].set(b_exp)
    return _split_bf16(wr), br


def _route(logits, tile):
    t = logits.shape[0]
    return pl.pallas_call(
        _route_kernel,
        grid=(t // tile,),
        in_specs=[pl.BlockSpec((tile, LANES), lambda i: (i, 0))],
        out_specs=[pl.BlockSpec((SUBLANES, tile), lambda i: (0, i)), _full((N_EXPERTS, LANES))],
        out_shape=[jax.ShapeDtypeStruct((SUBLANES, t), F32),
                   jax.ShapeDtypeStruct((N_EXPERTS, LANES), F32)],
        scratch_shapes=[pltpu.VMEM((N_EXPERTS, LANES), F32)],
        compiler_params=_params(),
        name="moe_route",
    )(logits)


def _back0_kernel(nt, x_ref, ctx_ref, u_ref, gate_ref, hf_ref, wcat_ref, ba_ref, bx_ref, lam_ref,
                  w_ref, gt_ref, g2_ref, sh_ref, sc_ref, wr_ref, br_ref, x1_ref, h2_ref, lg_ref,
                  hb_ref, a_ref, b_ref, o_ref, carry_ref):
    s = pl.program_id(0)
    is_ctx = s == 0

    @pl.when(s == 0)
    def _():
        carry_ref[...] = jnp.zeros_like(carry_ref)

    for j in range(RG_BLOCKS):
        _rglru_block(True, j, u_ref, wcat_ref, ba_ref, bx_ref, lam_ref, hb_ref,
                     a_ref, b_ref, o_ref, carry_ref)
    xin = jnp.where(is_ctx, ctx_ref[...], x_ref[...])
    y = _gelu_tanh(gate_ref[...]) * (hf_ref[...] + hb_ref[...])
    y = jnp.dot(y.astype(BF16), w_ref[...], preferred_element_type=F32)
    x1 = xin + _mod_row(gt_ref, is_ctx) * y
    x1_ref[...] = x1
    h2 = _rms(x1, g2_ref[...]) * (1.0 + _mod_row(sc_ref, is_ctx)) + _mod_row(sh_ref, is_ctx)
    _store_token_tiles(h2_ref, h2)
    lg_ref[...] = _router_logits(h2, wr_ref, br_ref)


_TOKEN_OUTS = lambda d: [pl.BlockSpec((TM, d), lambda i: (i, 0)),
                         pl.BlockSpec((TM * d // LANES, LANES), lambda i: (i, 0)),
                         pl.BlockSpec((TM, LANES), lambda i: (i, 0))]


def _token_out_shapes(t, d):
    return [jax.ShapeDtypeStruct((t, d), F32), jax.ShapeDtypeStruct((t * d // LANES, LANES), F32),
            jax.ShapeDtypeStruct((t, LANES), F32)]


def _back0(x, ctx, u, gate, hf, wcat, b_a, b_x, lam, w_out_bf, mod, g2, wr, br):
    n, d = x.shape
    nt = n // TM + 1
    t = nt * TM
    tile = lambda s: _scan_tile(True, s, nt)
    row = lambda v: v.reshape(1, -1)
    tok = pl.BlockSpec((TM, D_RNN), lambda s: (tile(s), 0))
    modspec = lambda c: pl.BlockSpec((SUBLANES, d), lambda s: (0, c))
    return pl.pallas_call(
        functools.partial(_back0_kernel, nt),
        grid=(nt,),
        in_specs=[pl.BlockSpec((TM, d), lambda s: (jnp.maximum(tile(s) - 1, 0), 0)),
                  _full((TM, d)),
                  tok, tok, tok,
                  _full((RG_BLOCKS, RG_BLOCK_W, 2 * RG_BLOCK_W)),
                  _full((1, D_RNN)), _full((1, D_RNN)), _full((1, D_RNN)),
                  _full((D_RNN, d)),
                  modspec(2), _full((1, d)), modspec(3), modspec(4),
                  _full(wr.shape), _full(br.shape)],
        out_specs=[pl.BlockSpec((TM, d), lambda s: (tile(s), 0)),
                   pl.BlockSpec((TM * d // LANES, LANES), lambda s: (tile(s), 0)),
                   pl.BlockSpec((TM, LANES), lambda s: (tile(s), 0))],
        out_shape=_token_out_shapes(t, d),
        scratch_shapes=[pltpu.VMEM((TM, D_RNN), F32)] + _SCAN_SCRATCH(),
        compiler_params=_params(),
        name="back0",
    )(x, ctx, u, gate, hf, wcat, row(b_a), row(b_x), row(lam), w_out_bf, mod, g2.reshape(1, -1),
      mod, mod, wr, br)


def _plan_kernel(n_tiles, nbp, cnt_ref, meta_ref, dest_ref, blk_ref):
    c = cnt_ref[...]
    padded = jnp.floor((c + (TB - 1)) * (1.0 / TB)) * TB
    r = lax.broadcasted_iota(jnp.int32, (N_EXPERTS, N_EXPERTS), 0)
    q = lax.broadcasted_iota(jnp.int32, (N_EXPERTS, N_EXPERTS), 1)
    lower = jnp.where(q <= r, 1.0, 0.0)
    pad_end = jnp.dot(lower, padded, precision=HI, preferred_element_type=F32)
    pad_start = pad_end - padded

    first_row = lax.broadcasted_iota(jnp.int32, (N_EXPERTS, nbp), 1).astype(F32) * TB
    owner = jnp.sum(jnp.where(pad_end[:, 0:1] <= first_row, 1.0, 0.0), axis=0, keepdims=True)
    blk_ref[0:1, :] = jnp.minimum(owner, N_EXPERTS - 1).astype(jnp.int32)
    n_used = pad_end[N_EXPERTS - 1:N_EXPERTS, 0:1] * (1.0 / TB)
    blk_ref[1:2, :] = jnp.broadcast_to(n_used, (1, nbp)).astype(jnp.int32)
    ends = jnp.concatenate([pad_end, jnp.zeros((LANES - N_EXPERTS, LANES), F32)], axis=0).T[0:1, :]
    blk_ref[2:3, :] = jnp.concatenate([ends, jnp.zeros((1, nbp - LANES), F32)], axis=1).astype(jnp.int32)
    blk_ref[3:SUBLANES, :] = jnp.zeros((SUBLANES - 3, nbp), jnp.int32)

    erow = lax.broadcasted_iota(jnp.int32, (N_EXPERTS, TM), 0).astype(F32)

    def body(i, carry):
        ln = pl.ds(pl.multiple_of(i * TM, TM), TM)
        for k in range(2):
            onehot = jnp.where(erow == meta_ref[k:k + 1, ln], 1.0, 0.0)
            start = jnp.sum(onehot * pad_start[:, 0:1], axis=0, keepdims=True)
            dest_ref[k:k + 1, ln] = (start + meta_ref[2 + k:3 + k, ln]).astype(jnp.int32)
        dest_ref[2:SUBLANES, ln] = jnp.zeros((SUBLANES - 2, TM), jnp.int32)
        return carry

    lax.fori_loop(0, n_tiles, body, 0)


def _plan(cnt, meta):
    t = meta.shape[1]
    n_blocks = (2 * t + N_EXPERTS * TB) // TB
    nbp = -(-n_blocks // LANES) * LANES
    vm = pl.BlockSpec(memory_space=pltpu.VMEM)
    dest, blk = pl.pallas_call(
        functools.partial(_plan_kernel, t // TM, nbp),
        in_specs=[vm, vm],
        out_specs=[vm, vm],
        out_shape=[jax.ShapeDtypeStruct((SUBLANES, t), jnp.int32),
                   jax.ShapeDtypeStruct((SUBLANES, nbp), jnp.int32)],
        compiler_params=pltpu.CompilerParams(vmem_limit_bytes=VMEM_LIMIT),
        name="moe_plan",
    )(cnt, meta)
    return dest, blk, n_blocks


def _row_wait(src_ref, dst_ref, d, sem):
    n = TM * d // LANES
    pltpu.make_async_copy(src_ref.at[pl.ds(0, n), :], dst_ref.at[pl.ds(0, n), :], sem).wait()


def _dispatch_kernel(n_blocks, d, pe_ref, nu_ref, dest_ref, h2_ref, buf_ref, dsm_ref, zero_ref,
                     sem_idx, sem_zero, sem_rows):
    step = pl.program_id(0)
    blk = TB * d // LANES

    def zero_block(first_row):
        start = pl.multiple_of(first_row * (d // LANES), blk)
        return pltpu.make_async_copy(zero_ref, buf_ref.at[pl.ds(start, blk), :], sem_zero)

    @pl.when(step == 0)
    def _():
        zero_ref[...] = jnp.zeros_like(zero_ref)

        def for_segments(fn):
            for e in range(N_EXPERTS):
                seg_start = pe_ref[e - 1] if e else 0

                @pl.when(pe_ref[e] > seg_start)
                def _():
                    fn(zero_block(pe_ref[e] - TB))

        for_segments(lambda cp: cp.start())
        lax.fori_loop(nu_ref[0], n_blocks, lambda b, c: (zero_block(b * TB).start(), c)[1], 0)
        for_segments(lambda cp: cp.wait())
        lax.fori_loop(nu_ref[0], n_blocks, lambda b, c: (zero_block(b * TB).wait(), c)[1], 0)

    cp = pltpu.make_async_copy(dest_ref, dsm_ref, sem_idx)
    cp.start()
    cp.wait()

    for r in range(TM):
        for k in range(2):
            pltpu.make_async_copy(_token_tile(h2_ref, r, d), _token_tile(buf_ref, dsm_ref[k, r], d),
                                  sem_rows).start(priority=k)
    for k in range(2):
        _row_wait(h2_ref, buf_ref, d, sem_rows)


def _dispatch(h2t, dest, pad_end, n_used, n_blocks, d):
    per = d // LANES
    t = h2t.shape[0] // per
    grid_spec = pltpu.PrefetchScalarGridSpec(
        num_scalar_prefetch=2,
        grid=(t // TM,),
        in_specs=[pl.BlockSpec((SUBLANES, TM), lambda i, pe, nu: (0, i)),
                  pl.BlockSpec((TM * per, LANES), lambda i, pe, nu: (i, 0))],
        out_specs=pl.BlockSpec(memory_space=pl.ANY),
        scratch_shapes=[pltpu.SMEM((SUBLANES, TM), jnp.int32),
                        pltpu.VMEM((TB * per, LANES), F32),
                        pltpu.SemaphoreType.DMA, pltpu.SemaphoreType.DMA,
                        pltpu.SemaphoreType.DMA],
    )
    return pl.pallas_call(
        functools.partial(_dispatch_kernel, n_blocks, d),
        grid_spec=grid_spec,
        out_shape=jax.ShapeDtypeStruct((n_blocks * TB * per, LANES), F32),
        compiler_params=_params(),
        name="moe_dispatch",
    )(pad_end, n_used, dest, h2t)


def _experts_kernel(be_ref, nu_ref, x_ref, wg_ref, wu_ref, wd_ref, y_ref, wg_bf, wu_bf, wd_bf):
    step = pl.program_id(0)
    used = step < nu_ref[0]
    new_expert = jnp.logical_or(step == 0, be_ref[step] != be_ref[jnp.maximum(step - 1, 0)])

    @pl.when(jnp.logical_and(used, new_expert))
    def _():
        wg_bf[...] = wg_ref[0].astype(BF16)
        wu_bf[...] = wu_ref[0].astype(BF16)
        wd_bf[...] = wd_ref[0].astype(BF16)

    @pl.when(used)
    def _():
        d = wg_bf.shape[0]
        xb = _load_token_tiles(x_ref, TB, d).astype(BF16)
        g = jnp.dot(xb, wg_bf[...], preferred_element_type=F32)
        u = jnp.dot(xb, wu_bf[...], preferred_element_type=F32)
        a = (g * _sigmoid(g)) * u
        _store_token_tiles(y_ref, jnp.dot(a.astype(BF16), wd_bf[...], preferred_element_type=F32))

    @pl.when(jnp.logical_not(used))
    def _():
        y_ref[...] = jnp.zeros_like(y_ref)


def _experts(buf, blk_e, n_used, wg, wu, wd, n_blocks):
    d = wg.shape[1]
    blk = TB * d // LANES
    last = lambda i, nu: jnp.maximum(jnp.minimum(i, nu[0] - 1), 0)
    grid_spec = pltpu.PrefetchScalarGridSpec(
        num_scalar_prefetch=2,
        grid=(n_blocks,),
        in_specs=[pl.BlockSpec((blk, LANES), lambda i, be, nu: (last(i, nu), 0)),
                  pl.BlockSpec((1, d, D_EXPERT), lambda i, be, nu: (be[last(i, nu)], 0, 0)),
                  pl.BlockSpec((1, d, D_EXPERT), lambda i, be, nu: (be[last(i, nu)], 0, 0)),
                  pl.BlockSpec((1, D_EXPERT, d), lambda i, be, nu: (be[last(i, nu)], 0, 0))],
        out_specs=pl.BlockSpec((blk, LANES), lambda i, be, nu: (i, 0)),
        scratch_shapes=[pltpu.VMEM((d, D_EXPERT), BF16), pltpu.VMEM((d, D_EXPERT), BF16),
                        pltpu.VMEM((D_EXPERT, d), BF16)],
    )
    return pl.pallas_call(
        _experts_kernel,
        grid_spec=grid_spec,
        out_shape=jax.ShapeDtypeStruct(buf.shape, F32),
        compiler_params=_params(),
        name="moe_experts",
    )(blk_e, n_used, buf, wg, wu, wd)


def _gather_start(dest_ref, dsm_ref, ybuf_ref, rows_ref, slot, d, sem_idx, sem):
    cp = pltpu.make_async_copy(dest_ref, dsm_ref, sem_idx)
    cp.start()
    cp.wait()
    for r in range(TM):
        for k in range(2):
            pltpu.make_async_copy(_token_tile(ybuf_ref, dsm_ref[k, r], d),
                                  _token_tile(rows_ref.at[slot, k], r, d), sem).start(priority=k)


def _gathered_rows(dest_ref, next_ref, dsm_ref, ybuf_ref, rows_ref, d, sem_idx, sems):
    step = pl.program_id(0)
    nt = pl.num_programs(0)
    slot = step % 2

    @pl.when(step == 0)
    def _():
        _gather_start(dest_ref, dsm_ref, ybuf_ref, rows_ref, 0, d, sem_idx, sems.at[0])

    @pl.when(step + 1 < nt)
    def _():
        _gather_start(next_ref, dsm_ref, ybuf_ref, rows_ref, 1 - slot, d, sem_idx, sems.at[1 - slot])

    for k in range(2):
        _row_wait(ybuf_ref, rows_ref.at[slot, k], d, sems.at[slot])
    return [_load_token_tiles(rows_ref.at[slot, k], TM, d) for k in range(2)]


def _token_weights(meta_ref):
    meta = jnp.concatenate([meta_ref[...], jnp.zeros((LANES - SUBLANES, TM), F32)], axis=0)
    mt = meta.T
    return mt[:, 4:5], mt[:, 5:6]


_COMBINE_SCRATCH = lambda d: [pltpu.SMEM((SUBLANES, TM), jnp.int32),
                              pltpu.VMEM((2, 2, TM * d // LANES, LANES), F32),
                              pltpu.SemaphoreType.DMA, pltpu.SemaphoreType.DMA((2,))]


def _next_tile_spec(nt):
    return pl.BlockSpec((SUBLANES, TM), lambda i: (0, jnp.minimum(i + 1, nt - 1)))


def _combine0_kernel(x1_ref, dest_ref, next_ref, meta_ref, gt_ref, ybuf_ref, x2_ref, ctx2_ref,
                     dsm_ref, rows_ref, sem_idx, sems):
    is_ctx = pl.program_id(0) == 0
    y0, y1 = _gathered_rows(dest_ref, next_ref, dsm_ref, ybuf_ref, rows_ref, x1_ref.shape[1],
                            sem_idx, sems)
    w0, w1 = _token_weights(meta_ref)
    out = x1_ref[...] + _mod_row(gt_ref, is_ctx) * (w0 * y0 + w1 * y1)
    x2_ref[...] = out

    @pl.when(is_ctx)
    def _():
        ctx2_ref[...] = out


def _combine0(x1, dest, meta, mod, ybuf):
    t, d = x1.shape
    nt = t // TM
    return pl.pallas_call(
        _combine0_kernel,
        grid=(nt,),
        in_specs=[pl.BlockSpec((TM, d), lambda i: (i, 0)),
                  pl.BlockSpec((SUBLANES, TM), lambda i: (0, i)),
                  _next_tile_spec(nt),
                  pl.BlockSpec((SUBLANES, TM), lambda i: (0, i)),
                  pl.BlockSpec((SUBLANES, d), lambda i: (0, 5)),
                  pl.BlockSpec(memory_space=pl.ANY)],
        out_specs=[pl.BlockSpec((TM, d), lambda i: (jnp.maximum(i - 1, 0), 0)),
                   _full((TM, d))],
        out_shape=[jax.ShapeDtypeStruct((t - TM, d), F32), jax.ShapeDtypeStruct((TM, d), F32)],
        scratch_shapes=_COMBINE_SCRATCH(d),
        compiler_params=_params(),
        name="moe_combine0",
    )(x1, dest, dest, meta, mod, ybuf)


def _combine1_kernel(x1_ref, dest_ref, next_ref, meta_ref, gt_ref, gf_ref, ybuf_ref, o_ref,
                     dsm_ref, rows_ref, sem_idx, sems):
    y0, y1 = _gathered_rows(dest_ref, next_ref, dsm_ref, ybuf_ref, rows_ref, x1_ref.shape[1],
                            sem_idx, sems)
    w0, w1 = _token_weights(meta_ref)
    out = x1_ref[...] + gt_ref[0:1, :] * (w0 * y0 + w1 * y1)
    o_ref[...] = _rms(out, gf_ref[...])


def _combine1(x1, dest, meta, mod, gf, ybuf):
    t, d = x1.shape
    rows = t // GRID_W
    nt = t // TM
    out = pl.pallas_call(
        _combine1_kernel,
        grid=(nt,),
        in_specs=[pl.BlockSpec((TM, d), lambda i: (i, 0)),
                  pl.BlockSpec((SUBLANES, TM), lambda i: (0, i)),
                  _next_tile_spec(nt),
                  pl.BlockSpec((SUBLANES, TM), lambda i: (0, i)),
                  pl.BlockSpec((SUBLANES, d), lambda i: (0, 5)),
                  _full((1, d)),
                  pl.BlockSpec(memory_space=pl.ANY)],
        out_specs=pl.BlockSpec((TM, d), lambda i: (0, i)),
        out_shape=jax.ShapeDtypeStruct((rows, GRID_W * d), F32),
        scratch_shapes=_COMBINE_SCRATCH(d),
        compiler_params=_params(),
        name="moe_combine1",
    )(x1, dest, dest, meta, mod, gf.reshape(1, -1), ybuf)
    return out.reshape(t, d)


def _moe(h2t, meta, cnt, w_gate, w_up, w_down):
    dest, blk, n_blocks = _plan(cnt, meta)
    n_used = blk[1, :1]
    buf = _dispatch(h2t, dest, blk[2, :N_EXPERTS], n_used, n_blocks, w_gate.shape[1])
    ybuf = _experts(buf, blk[0], n_used, w_gate, w_up, w_down, n_blocks)
    return dest, ybuf


def _inproj1_kernel(x_ref, ctx_ref, g_ref, sh_ref, sc_ref, wq_ref, wkt_ref, wv_ref, wo_ref,
                    wg_ref, bg_ref, q_ref, kt_ref, v_ref, o_ref, gcol_ref):
    is_ctx = pl.program_id(0) == 0
    xin = jnp.where(is_ctx, ctx_ref[...], x_ref[...])
    h = _rms(xin, g_ref[...]) * (1.0 + _mod_row(sc_ref, is_ctx)) + _mod_row(sh_ref, is_ctx)
    hb = h.astype(BF16)
    q = jnp.dot(hb, wq_ref[...], preferred_element_type=F32) * (ML_DQK ** -0.5)
    q_ref[...] = q.astype(BF16)
    kt = lax.dot_general(wkt_ref[...], hb, (((1,), (1,)), ((), ())), preferred_element_type=F32)
    kt_ref[...] = kt.astype(BF16)
    v_ref[...] = jnp.dot(hb, wv_ref[...], preferred_element_type=F32).astype(BF16)
    o_ref[...] = jnp.dot(hb, wo_ref[...], preferred_element_type=F32)
    gcol_ref[...] = _dot_bf16x3(h, wg_ref) + bg_ref[...]


def _inproj1(x, ctx, g1, mod, w_in, b_gates):
    n, d = x.shape
    rows = n // GRID_W
    nt = GRID_W + 1
    t = nt * TM
    ng = 4 * ML_HEADS
    wq = w_in[:, :ML_QK_W].astype(BF16)
    wkt = w_in[:, ML_QK_W:2 * ML_QK_W].T.astype(BF16)
    wv = w_in[:, 2 * ML_QK_W:2 * ML_QK_W + ML_V_W].astype(BF16)
    wo = w_in[:, 2 * ML_QK_W + ML_V_W:2 * ML_QK_W + ML_V_W + d].astype(BF16)
    wg = _split_bf16(jnp.zeros((d, LANES), F32).at[:, :ng].set(w_in[:, -ng:]))
    bg = jnp.zeros((1, LANES), F32).at[0, :ng].set(b_gates)
    tok = lambda w: pl.BlockSpec((TM, w), lambda i: (i, 0))
    return pl.pallas_call(
        _inproj1_kernel,
        grid=(nt,),
        in_specs=[pl.BlockSpec((rows, d), lambda i: (0, jnp.maximum(i - 1, 0))),
                  _full((TM, d)),
                  _full((1, d)),
                  pl.BlockSpec((SUBLANES, d), lambda i: (0, 0)),
                  pl.BlockSpec((SUBLANES, d), lambda i: (0, 1)),
                  _full(wq.shape), _full(wkt.shape), _full(wv.shape), _full(wo.shape),
                  _full(wg.shape), _full(bg.shape)],
        out_specs=[tok(ML_QK_W),
                   pl.BlockSpec((ML_QK_W, TM), lambda i: (0, i)),
                   tok(ML_V_W), tok(d), tok(LANES)],
        out_shape=[jax.ShapeDtypeStruct((t, ML_QK_W), BF16),
                   jax.ShapeDtypeStruct((ML_QK_W, t), BF16),
                   jax.ShapeDtypeStruct((t, ML_V_W), BF16),
                   jax.ShapeDtypeStruct((t, d), F32),
                   jax.ShapeDtypeStruct((t, LANES), F32)],
        compiler_params=_params(),
        name="inproj1",
    )(x.reshape(rows, GRID_W * d), ctx, g1.reshape(1, -1), mod, mod, wq, wkt, wv, wo, wg, bg)


def _log_sigmoid(x):
    return jnp.minimum(x, 0.0) - jnp.log1p(jnp.exp(-jnp.abs(x)))


LOG2E = 1.4426950408889634


def _mlstm_kernel(reverse, q_ref, kt_ref, v_ref, gcol_ref, h_ref, c_ref, m_ref):
    @pl.when(pl.program_id(0) == 0)
    def _():
        c_ref[...] = jnp.zeros_like(c_ref)
        m_ref[...] = jnp.zeros_like(m_ref)

    L = TM
    half = L // 2
    gi = 2 * ML_HEADS if reverse else 0
    gf = gi + ML_HEADS
    end = 0 if reverse else L - 1
    rr = lax.broadcasted_iota(jnp.int32, (L, L), 0)
    cc = lax.broadcasted_iota(jnp.int32, (L, L), 1)
    tri = jnp.where((cc >= rr) if reverse else (cc <= rr), 1.0, 0.0).astype(BF16)
    rh = lax.broadcasted_iota(jnp.int32, (half, half), 0)
    ch = lax.broadcasted_iota(jnp.int32, (half, half), 1)
    diag = (ch >= rh) if reverse else (ch <= rh)

    gates = gcol_ref[...]
    lane = lax.broadcasted_iota(jnp.int32, (L, LANES), 1)
    mine = jnp.logical_and(lane >= gf, lane < gf + ML_HEADS)
    lf = jnp.where(mine, _log_sigmoid(gates), 0.0)
    p0 = lf.astype(BF16)
    r1 = lf - p0.astype(F32)
    p1 = r1.astype(BF16)
    p2 = (r1 - p1.astype(F32)).astype(BF16)
    cum = (jnp.dot(tri, p0, preferred_element_type=F32) + jnp.dot(tri, p1, preferred_element_type=F32)
           + jnp.dot(tri, p2, preferred_element_type=F32))

    r = jnp.where(mine, pltpu.roll(gates, ML_HEADS, 1) - cum, 0.0)
    row = lax.broadcasted_iota(jnp.int32, (L, LANES), 0)
    cm = r
    sh = 1
    while sh < L:
        if sh < SUBLANES:
            if reverse:
                cm = jnp.where(row < L - sh, jnp.maximum(cm, pltpu.roll(cm, L - sh, 0)), cm)
            else:
                cm = jnp.where(row >= sh, jnp.maximum(cm, pltpu.roll(cm, sh, 0)), cm)
        else:
            pad = jnp.full((sh, LANES), NEG_INF, F32)
            moved = (jnp.concatenate([cm[sh:], pad], axis=0) if reverse
                     else jnp.concatenate([pad, cm[:L - sh]], axis=0))
            cm = jnp.maximum(cm, moved)
        sh *= 2
    r8 = r.T[gf:gf + ML_HEADS]

    m_prev = m_ref[0:1, :]
    mm = jnp.maximum(m_prev, cm)
    m_t = cum + mm
    w_inter = jnp.exp(m_prev - mm)
    floor = jnp.exp(-m_t)
    mm2 = mm * LOG2E
    m_new = m_t[end:end + 1, :]
    shift2 = (cum[end:end + 1, :] - m_new) * LOG2E
    decay = jnp.exp(cum[end:end + 1, :] + m_prev - m_new)
    m_ref[0:1, :] = m_new

    ones_col = jnp.where(lax.broadcasted_iota(jnp.int32, (L, ML_DV), 1) == 0, 1.0, 0.0).astype(BF16)
    top, bot = slice(0, half), slice(half, L)
    dot = functools.partial(jnp.dot, preferred_element_type=F32)

    for hd in range(ML_HEADS):
        ln = gf + hd
        r2_row = r8[hd:hd + 1, :] * LOG2E

        def weights(tq, ks, masked):
            w = jnp.exp2(r2_row[:, ks] - mm2[tq, ln:ln + 1])
            return jnp.where(diag, w, 0.0) if masked else w

        qh = q_ref[:, hd * ML_DQK:(hd + 1) * ML_DQK]
        kth = kt_ref[hd * ML_DQK:(hd + 1) * ML_DQK, :]
        vext = jnp.concatenate([v_ref[:, hd * ML_DV:(hd + 1) * ML_DV], ones_col], axis=1)
        if reverse:
            s_top = dot(qh[top], kth) * jnp.concatenate([weights(top, top, True),
                                                         weights(top, bot, False)], axis=1)
            s_bot = dot(qh[bot], kth[:, bot]) * weights(bot, bot, True)
            intra = jnp.concatenate([dot(s_top.astype(BF16), vext),
                                     dot(s_bot.astype(BF16), vext[bot])], axis=0)
        else:
            s_top = dot(qh[top], kth[:, top]) * weights(top, top, True)
            s_bot = dot(qh[bot], kth) * jnp.concatenate([weights(bot, top, False),
                                                         weights(bot, bot, True)], axis=1)
            intra = jnp.concatenate([dot(s_top.astype(BF16), vext[top]),
                                     dot(s_bot.astype(BF16), vext)], axis=0)
        state = c_ref[hd]
        tot = w_inter[:, ln:ln + 1] * dot(qh, state.astype(BF16)) + intra
        den = tot[:, ML_DV:ML_DV + 1]
        h_ref[:, hd * ML_DV:(hd + 1) * ML_DV] = (
            tot[:, :ML_DV] / jnp.maximum(jnp.abs(den), floor[:, ln:ln + 1]))

        w_state = jnp.exp2(r2_row + shift2[:, ln:ln + 1])
        kw = (kth.astype(F32) * w_state).astype(BF16)
        c_ref[hd] = decay[:, ln:ln + 1] * state + dot(kw, vext)


def _mlstm(q, kt, v, gcol, reverse):
    t = q.shape[0]
    nt = t // TM
    tile = lambda s: _scan_tile(reverse, s, nt)
    return pl.pallas_call(
        functools.partial(_mlstm_kernel, reverse),
        grid=(nt,),
        in_specs=[pl.BlockSpec((TM, ML_QK_W), lambda s: (tile(s), 0)),
                  pl.BlockSpec((ML_QK_W, TM), lambda s: (0, tile(s))),
                  pl.BlockSpec((TM, ML_V_W), lambda s: (tile(s), 0)),
                  pl.BlockSpec((TM, LANES), lambda s: (tile(s), 0))],
        out_specs=pl.BlockSpec((TM, ML_V_W), lambda s: (tile(s), 0)),
        out_shape=jax.ShapeDtypeStruct((t, ML_V_W), F32),
        scratch_shapes=[pltpu.VMEM((ML_HEADS, ML_DQK, 2 * ML_DV), F32),
                        pltpu.VMEM((SUBLANES, LANES), F32)],
        compiler_params=_params(),
        name="mlstm_bwd" if reverse else "mlstm_fwd",
    )(q, kt, v, gcol)


def _finish1_kernel(x_ref, hf_ref, hb_ref, o_ref, ng_ref, w_ref, gt_ref, g2_ref, sh_ref, sc_ref,
                    wr_ref, br_ref, x1_ref, h2_ref, lg_ref):
    hs = hf_ref[...] + hb_ref[...]
    parts = []
    for hd in range(ML_HEADS):
        blk = hs[:, hd * ML_DV:(hd + 1) * ML_DV]
        parts.append(blk * lax.rsqrt(jnp.mean(blk * blk, axis=-1, keepdims=True) + EPS))
    hn = jnp.concatenate(parts, axis=1) * ng_ref[...]
    y = jnp.dot((hn * _sigmoid(o_ref[...])).astype(BF16), w_ref[...], preferred_element_type=F32)
    x1 = x_ref[...] + gt_ref[0:1, :] * y
    x1_ref[...] = x1
    h2 = _rms(x1, g2_ref[...]) * (1.0 + sc_ref[0:1, :]) + sh_ref[0:1, :]
    _store_token_tiles(h2_ref, h2)
    lg_ref[...] = _router_logits(h2, wr_ref, br_ref)


def _finish1(x, hf, hb, o, norm_g, w_out_bf, mod, g2, wr, br):
    n, d = x.shape
    rows = n // GRID_W
    lat = lambda w: pl.BlockSpec((TM, w), lambda i: (i + 1, 0))
    modspec = lambda c: pl.BlockSpec((SUBLANES, d), lambda i: (0, c))
    return pl.pallas_call(
        _finish1_kernel,
        grid=(GRID_W,),
        in_specs=[pl.BlockSpec((rows, d), lambda i: (0, i)),
                  lat(ML_V_W), lat(ML_V_W), lat(d),
                  _full((1, ML_V_W)), _full((ML_V_W, d)),
                  modspec(2), _full((1, d)), modspec(3), modspec(4),
                  _full(wr.shape), _full(br.shape)],
        out_specs=_TOKEN_OUTS(d),
        out_shape=_token_out_shapes(n, d),
        compiler_params=_params(),
        name="finish1",
    )(x.reshape(rows, GRID_W * d), hf, hb, o, norm_g.reshape(1, -1), w_out_bf, mod,
      g2.reshape(1, -1), mod, mod, wr, br)


def kernel(x, c, ctx, c_ctx,
           l0_ada_w, l0_ada_b, l0_norm1_g, l0_norm2_g,
           l0_rg_w_in, l0_rg_conv_w, l0_rg_conv_b, l0_rg_w_a, l0_rg_b_a, l0_rg_w_x, l0_rg_b_x,
           l0_rg_lambda, l0_rg_w_out,
           l0_moe_w_grp, l0_moe_b_grp, l0_moe_w_exp, l0_moe_b_exp, l0_moe_w_gate, l0_moe_w_up,
           l0_moe_w_down,
           l1_ada_w, l1_ada_b, l1_norm1_g, l1_norm2_g,
           l1_ml_w_in, l1_ml_b_gates, l1_ml_norm_g, l1_ml_w_out,
           l1_moe_w_grp, l1_moe_b_grp, l1_moe_w_exp, l1_moe_b_exp, l1_moe_w_gate, l1_moe_w_up,
           l1_moe_w_down,
           final_norm_g):
    assert x.shape[0] == 1 and ctx.shape[1] == TM and x.shape[1] == GRID_W * TM
    xs, cs = x[0], ctx[0]
    d = xs.shape[1]
    cond8 = jnp.zeros((SUBLANES, d), F32).at[0].set(c[0]).at[1].set(c_ctx)

    mod0 = _adaln(cond8, l0_ada_w, l0_ada_b)
    conv_w8 = jnp.zeros((SUBLANES, D_RNN), F32).at[:CONV_W].set(l0_rg_conv_w)
    wcat = [jnp.concatenate([l0_rg_w_a[dr], l0_rg_w_x[dr]], axis=-1).astype(BF16) for dr in range(2)]
    gate, u, hf0 = _front0(xs, cs, l0_norm1_g, mod0, l0_rg_w_in.astype(BF16), conv_w8, l0_rg_conv_b,
                           wcat[0], l0_rg_b_a[0], l0_rg_b_x[0], l0_rg_lambda[0])
    wr0, br0 = _router_weights(l0_moe_w_grp, l0_moe_b_grp, l0_moe_w_exp, l0_moe_b_exp)
    x1, h2, logits = _back0(xs, cs, u, gate, hf0, wcat[1], l0_rg_b_a[1], l0_rg_b_x[1], l0_rg_lambda[1],
                            l0_rg_w_out.astype(BF16), mod0, l0_norm2_g, wr0, br0)
    meta, cnt = _route(logits, ROUTE_TILE_0)
    dest, ybuf = _moe(h2, meta, cnt, l0_moe_w_gate, l0_moe_w_up, l0_moe_w_down)
    x2, ctx2 = _combine0(x1, dest, meta, mod0, ybuf)

    mod1 = _adaln(cond8, l1_ada_w, l1_ada_b)
    q, kt, v, o, gcol = _inproj1(x2, ctx2, l1_norm1_g, mod1, l1_ml_w_in, l1_ml_b_gates)
    hf = _mlstm(q, kt, v, gcol, False)
    hb = _mlstm(q, kt, v, gcol, True)
    wr1, br1 = _router_weights(l1_moe_w_grp, l1_moe_b_grp, l1_moe_w_exp, l1_moe_b_exp)
    x1, h2, logits = _finish1(x2, hf, hb, o, l1_ml_norm_g, l1_ml_w_out.astype(BF16), mod1,
                              l1_norm2_g, wr1, br1)
    meta, cnt = _route(logits, ROUTE_TILE_1)
    dest, ybuf = _moe(h2, meta, cnt, l1_moe_w_gate, l1_moe_w_up, l1_moe_w_down)
    out = _combine1(x1, dest, meta, mod1, final_norm_g, ybuf)
    return out[None]
```

```python
import functools

import jax
import jax.numpy as jnp
from jax import lax
from jax.experimental import pallas as pl
from jax.experimental.pallas import tpu as pltpu

D_MODEL = 1024
GRID_W = 64
N_MOD = 6
EPS = 1e-6

D_RNN = 1280
RG_BLOCKS = 10
RG_BLOCK_W = D_RNN // RG_BLOCKS
CONV_W = 4
CONV_PAD_L = 2
RG_C = 8.0

ML_HEADS = 8
ML_DQK = D_MODEL // (2 * ML_HEADS)
ML_DV = D_MODEL // ML_HEADS
ML_QK_W = ML_HEADS * ML_DQK
ML_V_W = ML_HEADS * ML_DV

N_GROUPS = 4
EXPERTS_PER_GROUP = 8
N_EXPERTS = N_GROUPS * EXPERTS_PER_GROUP
D_EXPERT = 512

TM = 256
TB = 256
ROUTE_TILE_0 = 1280
ROUTE_TILE_1 = 1024
SUBLANES = 8
LANES = 128
VMEM_LIMIT = 48 * 1024 * 1024

F32 = jnp.float32
BF16 = jnp.bfloat16
HI = lax.Precision.HIGHEST
NEG_INF = float("-inf")


def _params(n_axes=1):
    return pltpu.CompilerParams(dimension_semantics=("arbitrary",) * n_axes,
                                vmem_limit_bytes=VMEM_LIMIT)


def _rms(x, g):
    return x * lax.rsqrt(jnp.mean(x * x, axis=-1, keepdims=True) + EPS) * g


def _sigmoid(x):
    return 1.0 / (1.0 + jnp.exp(-x))


def _softplus(x):
    return jnp.maximum(x, 0.0) + jnp.log1p(jnp.exp(-jnp.abs(x)))


def _gelu_tanh(x):
    return 0.5 * x * (1.0 + jnp.tanh(0.7978845608028654 * (x + 0.044715 * (x * x * x))))


def _full(shape):
    return pl.BlockSpec(shape, lambda *_: (0,) * len(shape))


def _store_token_tiles(ref, x):
    per = x.shape[1] // LANES
    for c in range(per):
        ref[pl.ds(c, x.shape[0], stride=per), :] = x[:, c * LANES:(c + 1) * LANES]


def _load_token_tiles(ref, rows, d):
    per = d // LANES
    return jnp.concatenate([ref[pl.ds(c, rows, stride=per), :] for c in range(per)], axis=1)


def _token_tile(ref, row, d):
    per = d // LANES
    start = row * per if isinstance(row, int) else pl.multiple_of(row * per, per)
    return ref.at[pl.ds(start, per), :]


def _adaln_kernel(cond_ref, w_ref, b_ref, o_ref):
    c = cond_ref[...]
    s = c * _sigmoid(c)
    o_ref[...] = jnp.dot(s, w_ref[...], precision=HI, preferred_element_type=F32) + b_ref[...]


def _adaln(cond8, w, b):
    d = w.shape[0]
    return pl.pallas_call(
        _adaln_kernel,
        grid=(N_MOD,),
        in_specs=[_full((SUBLANES, d)),
                  pl.BlockSpec((d, d), lambda j: (0, j)),
                  pl.BlockSpec((1, d), lambda j: (0, j))],
        out_specs=pl.BlockSpec((SUBLANES, d), lambda j: (0, j)),
        out_shape=jax.ShapeDtypeStruct((SUBLANES, N_MOD * d), F32),
        compiler_params=_params(),
        name="adaln",
    )(cond8, w, b.reshape(1, -1))


def _mod_row(ref, is_ctx):
    return jnp.where(is_ctx, ref[1:2, :], ref[0:1, :])


def _front0_kernel(nt, x_ref, ctx_ref, g_ref, sh_ref, sc_ref, w_ref, cw_ref, cb_ref,
                   wcat_ref, ba_ref, bx_ref, lam_ref, gate_ref, u_ref, hf_ref,
                   ext_ref, ubuf_ref, a_ref, b_ref, o_ref, carry_ref):
    s = pl.program_id(0)
    is_ctx = s == 0

    @pl.when(s == 0)
    def _():
        ext_ref[...] = jnp.zeros_like(ext_ref)
        ubuf_ref[...] = jnp.zeros_like(ubuf_ref)

    @pl.when(s <= 2)
    def _():
        carry_ref[...] = jnp.zeros_like(carry_ref)

    xin = jnp.where(is_ctx, ctx_ref[...], x_ref[...])
    h = _rms(xin, g_ref[...]) * (1.0 + _mod_row(sc_ref, is_ctx)) + _mod_row(sh_ref, is_ctx)
    hb = h.astype(BF16)

    right_valid = jnp.logical_and(s >= 2, s <= nt - 1)
    base = SUBLANES - CONV_PAD_L
    for j in range(RG_BLOCKS):
        ln = slice(j * RG_BLOCK_W, (j + 1) * RG_BLOCK_W)
        _rglru_block(False, j, ubuf_ref.at[s % 2], wcat_ref, ba_ref, bx_ref, lam_ref, hf_ref,
                     a_ref, b_ref, o_ref, carry_ref)
        p = jnp.dot(hb, w_ref[j], preferred_element_type=F32)
        gate_ref[:, ln] = p[:, :RG_BLOCK_W]
        rec = p[:, RG_BLOCK_W:]
        ext_ref[j, SUBLANES + TM:, :] = jnp.where(right_valid, rec[0:SUBLANES], 0.0)
        u = cb_ref[:, ln] + ext_ref[j, pl.ds(base, TM), :] * cw_ref[0:1, ln]
        for k in range(1, CONV_W):
            u = u + ext_ref[j, pl.ds(base + k, TM), :] * cw_ref[k:k + 1, ln]
        u_ref[:, ln] = u
        ubuf_ref[(s + 1) % 2, :, ln] = u
        ext_ref[j, 0:SUBLANES, :] = jnp.where(s >= 2, ext_ref[j, TM:TM + SUBLANES, :], 0.0)
        ext_ref[j, SUBLANES:SUBLANES + TM, :] = rec


_SCAN_SLAB = lambda: pltpu.VMEM((RG_BLOCKS, SUBLANES * SCAN_PITCH, LANES), F32)
_SCAN_SCRATCH = lambda: [_SCAN_SLAB(), _SCAN_SLAB(), _SCAN_SLAB(), pltpu.VMEM((SUBLANES, D_RNN), F32)]


def _front0(x, ctx, g1, mod, w_in_bf, conv_w8, conv_b, wcat, b_a, b_x, lam):
    n, d = x.shape
    nx = n // TM
    nt = nx + 1
    t = nt * TM
    row = lambda v: v.reshape(1, -1)
    w_blocks = jnp.concatenate([w_in_bf[:, :D_RNN].reshape(d, RG_BLOCKS, RG_BLOCK_W),
                                w_in_bf[:, D_RNN:].reshape(d, RG_BLOCKS, RG_BLOCK_W)], axis=2)
    w_blocks = w_blocks.transpose(1, 0, 2)
    return pl.pallas_call(
        functools.partial(_front0_kernel, nt),
        grid=(nt + 2,),
        in_specs=[pl.BlockSpec((TM, d), lambda s: (jnp.clip(s - 1, 0, nx - 1), 0)),
                  _full((TM, d)),
                  _full((1, d)),
                  pl.BlockSpec((SUBLANES, d), lambda s: (0, 0)),
                  pl.BlockSpec((SUBLANES, d), lambda s: (0, 1)),
                  _full((RG_BLOCKS, d, 2 * RG_BLOCK_W)),
                  _full((SUBLANES, D_RNN)), _full((1, D_RNN)),
                  _full((RG_BLOCKS, RG_BLOCK_W, 2 * RG_BLOCK_W)),
                  _full((1, D_RNN)), _full((1, D_RNN)), _full((1, D_RNN))],
        out_specs=[pl.BlockSpec((TM, D_RNN), lambda s: (jnp.minimum(s, nt - 1), 0)),
                   pl.BlockSpec((TM, D_RNN), lambda s: (jnp.maximum(s - 1, 0), 0)),
                   pl.BlockSpec((TM, D_RNN), lambda s: (jnp.maximum(s - 2, 0), 0))],
        out_shape=[jax.ShapeDtypeStruct((t, D_RNN), F32),
                   jax.ShapeDtypeStruct((t + TM, D_RNN), F32),
                   jax.ShapeDtypeStruct((t, D_RNN), F32)],
        scratch_shapes=[pltpu.VMEM((RG_BLOCKS, TM + 2 * SUBLANES, LANES), F32),
                        pltpu.VMEM((2, TM, D_RNN), F32)] + _SCAN_SCRATCH(),
        compiler_params=_params(),
        name="front0",
    )(x, ctx, g1.reshape(1, -1), mod, mod, w_blocks, conv_w8, row(conv_b), wcat, row(b_a), row(b_x),
      row(lam))


def _scan_tile(reverse, s, nt):
    if not reverse:
        return s
    return jnp.where(s == 0, 0, nt - s)


SCAN_CHUNK = TM // SUBLANES
SCAN_PITCH = SCAN_CHUNK + 4


def _rglru_block(reverse, j, u_ref, wcat_ref, ba_ref, bx_ref, lam_ref, h_ref,
                 a_ref, b_ref, o_ref, carry_ref):
    steps = range(SCAN_CHUNK - 1, -1, -1) if reverse else range(SCAN_CHUNK)
    chunks = range(SUBLANES - 1, -1, -1) if reverse else range(SUBLANES)
    ln = slice(j * RG_BLOCK_W, (j + 1) * RG_BLOCK_W)
    u = u_ref[:, ln]
    g = jnp.dot(u.astype(BF16), wcat_ref[j], preferred_element_type=F32)
    half_rate = (-0.5 * RG_C) * _softplus(-lam_ref[:, ln])
    log_a = jnp.tanh(0.5 * (g[:, :RG_BLOCK_W] + ba_ref[:, ln])) * half_rate + half_rate
    ig = 0.5 * jnp.tanh(0.5 * (g[:, RG_BLOCK_W:] + bx_ref[:, ln])) + 0.5
    a = jnp.exp(log_a)
    b = jnp.sqrt(-jnp.tanh(log_a) * (a * a + 1.0)) * ig * u
    for c in range(SUBLANES):
        a_ref[j, pl.ds(c * SCAN_PITCH, SCAN_CHUNK), :] = a[c * SCAN_CHUNK:(c + 1) * SCAN_CHUNK]
        b_ref[j, pl.ds(c * SCAN_PITCH, SCAN_CHUNK), :] = b[c * SCAN_CHUNK:(c + 1) * SCAN_CHUNK]

    row = lambda ref, i: ref[j, pl.ds(i, SUBLANES, stride=SCAN_PITCH), :]
    end = jnp.zeros((SUBLANES, LANES), F32)
    decay = jnp.ones((SUBLANES, LANES), F32)
    for i in steps:
        ai = row(a_ref, i)
        end = ai * end + row(b_ref, i)
        decay = decay * ai

    state = carry_ref[0:1, ln]
    entry = [None] * SUBLANES
    for c in chunks:
        entry[c] = state
        state = decay[c:c + 1] * state + end[c:c + 1]
    carry_ref[0:1, ln] = state

    hcur = jnp.concatenate(entry, axis=0)
    for i in steps:
        hcur = row(a_ref, i) * hcur + row(b_ref, i)
        o_ref[j, pl.ds(i, SUBLANES, stride=SCAN_PITCH), :] = hcur
    for c in range(SUBLANES):
        h_ref[c * SCAN_CHUNK:(c + 1) * SCAN_CHUNK, ln] = o_ref[j, pl.ds(c * SCAN_PITCH, SCAN_CHUNK), :]


def _split_bf16(w):
    hi = w.astype(BF16)
    return jnp.stack([hi, (w - hi.astype(F32)).astype(BF16)])


def _dot_bf16x3(x, w_ref):
    hi = x.astype(BF16)
    lo = (x - hi.astype(F32)).astype(BF16)
    acc = jnp.dot(hi, w_ref[0], preferred_element_type=F32)
    acc = acc + jnp.dot(lo, w_ref[0], preferred_element_type=F32)
    return acc + jnp.dot(hi, w_ref[1], preferred_element_type=F32)


def _router_logits(h2, wr_ref, br_ref):
    return _dot_bf16x3(h2, wr_ref) + br_ref[...]


def _route_kernel(lg_ref, meta_ref, cnt_ref, carry_ref):
    step = pl.program_id(0)
    tm = lg_ref.shape[0]

    @pl.when(step == 0)
    def _():
        carry_ref[...] = jnp.zeros_like(carry_ref)

    logits = jnp.concatenate([lg_ref[i * LANES:(i + 1) * LANES, :].T for i in range(tm // LANES)],
                             axis=1)
    row8 = lax.broadcasted_iota(jnp.int32, (SUBLANES, tm), 0)
    grp_logits = jnp.where(row8 < N_GROUPS, logits[0:SUBLANES], NEG_INF)
    gmax = jnp.max(grp_logits, axis=0, keepdims=True)
    p_sel = 1.0 / jnp.sum(jnp.exp(grp_logits - gmax), axis=0, keepdims=True)
    grp = jnp.min(jnp.where(grp_logits == gmax, row8, SUBLANES), axis=0, keepdims=True)

    in_grp = logits[SUBLANES + (N_GROUPS - 1) * EXPERTS_PER_GROUP:SUBLANES + N_GROUPS * EXPERTS_PER_GROUP]
    for gi in range(N_GROUPS - 2, -1, -1):
        lo = SUBLANES + gi * EXPERTS_PER_GROUP
        in_grp = jnp.where(grp == gi, logits[lo:lo + EXPERTS_PER_GROUP], in_grp)
    v1 = jnp.max(in_grp, axis=0, keepdims=True)
    i1 = jnp.min(jnp.where(in_grp == v1, row8, EXPERTS_PER_GROUP), axis=0, keepdims=True)
    rest = jnp.where(row8 == i1, NEG_INF, in_grp)
    v2 = jnp.max(rest, axis=0, keepdims=True)
    i2 = jnp.min(jnp.where(rest == v2, row8, EXPERTS_PER_GROUP), axis=0, keepdims=True)
    e2 = jnp.exp(v2 - v1)
    w1 = p_sel / (1.0 + e2)
    w2 = p_sel * e2 / (1.0 + e2)
    eid = (grp * EXPERTS_PER_GROUP + i1, grp * EXPERTS_PER_GROUP + i2)

    rr = lax.broadcasted_iota(jnp.int32, (LANES, LANES), 0)
    cc = lax.broadcasted_iota(jnp.int32, (LANES, LANES), 1)
    strict_upper = jnp.where(rr < cc, 1.0, 0.0).astype(BF16)
    erow = lax.broadcasted_iota(jnp.int32, (N_EXPERTS, LANES), 0)
    base = carry_ref[:, 0:1]
    for k in range(2):
        for i in range(tm // LANES):
            ln = slice(i * LANES, (i + 1) * LANES)
            onehot = jnp.where(erow == eid[k][:, ln], 1.0, 0.0)
            pre = jnp.dot(onehot.astype(BF16), strict_upper, preferred_element_type=F32)
            meta_ref[2 + k:3 + k, ln] = jnp.sum(onehot * (base + pre), axis=0, keepdims=True)
            base = base + jnp.sum(onehot, axis=1, keepdims=True)
    carry_ref[...] = jnp.broadcast_to(base, carry_ref.shape)
    cnt_ref[...] = jnp.broadcast_to(base, cnt_ref.shape)

    meta_ref[0:1, :] = eid[0].astype(F32)
    meta_ref[1:2, :] = eid[1].astype(F32)
    meta_ref[4:5, :] = w1
    meta_ref[5:6, :] = w2
    meta_ref[6:8, :] = jnp.zeros((2, tm), F32)


def _router_weights(w_grp, b_grp, w_exp, b_exp):
    d = w_grp.shape[0]
    wr = jnp.zeros((d, LANES), F32).at[:, :N_GROUPS].set(w_grp)
    wr = wr.at[:, SUBLANES:SUBLANES + N_EXPERTS].set(w_exp)
    br = jnp.zeros((1, LANES), F32).at[0, :N_GROUPS].set(b_grp)
    br = br.at[0, SUBLANES:SUBLANES + N_EXPERTS].set(b_exp)
    return _split_bf16(wr), br


def _route(logits, tile):
    t = logits.shape[0]
    return pl.pallas_call(
        _route_kernel,
        grid=(t // tile,),
        in_specs=[pl.BlockSpec((tile, LANES), lambda i: (i, 0))],
        out_specs=[pl.BlockSpec((SUBLANES, tile), lambda i: (0, i)), _full((N_EXPERTS, LANES))],
        out_shape=[jax.ShapeDtypeStruct((SUBLANES, t), F32),
                   jax.ShapeDtypeStruct((N_EXPERTS, LANES), F32)],
        scratch_shapes=[pltpu.VMEM((N_EXPERTS, LANES), F32)],
        compiler_params=_params(),
        name="moe_route",
    )(logits)


def _back0_kernel(nt, x_ref, ctx_ref, u_ref, gate_ref, hf_ref, wcat_ref, ba_ref, bx_ref, lam_ref,
                  w_ref, gt_ref, g2_ref, sh_ref, sc_ref, wr_ref, br_ref, x1_ref, h2_ref, lg_ref,
                  hb_ref, a_ref, b_ref, o_ref, carry_ref):
    s = pl.program_id(0)
    is_ctx = s == 0

    @pl.when(s == 0)
    def _():
        carry_ref[...] = jnp.zeros_like(carry_ref)

    for j in range(RG_BLOCKS):
        _rglru_block(True, j, u_ref, wcat_ref, ba_ref, bx_ref, lam_ref, hb_ref,
                     a_ref, b_ref, o_ref, carry_ref)
    xin = jnp.where(is_ctx, ctx_ref[...], x_ref[...])
    y = _gelu_tanh(gate_ref[...]) * (hf_ref[...] + hb_ref[...])
    y = jnp.dot(y.astype(BF16), w_ref[...], preferred_element_type=F32)
    x1 = xin + _mod_row(gt_ref, is_ctx) * y
    x1_ref[...] = x1
    h2 = _rms(x1, g2_ref[...]) * (1.0 + _mod_row(sc_ref, is_ctx)) + _mod_row(sh_ref, is_ctx)
    _store_token_tiles(h2_ref, h2)
    lg_ref[...] = _router_logits(h2, wr_ref, br_ref)


_TOKEN_OUTS = lambda d: [pl.BlockSpec((TM, d), lambda i: (i, 0)),
                         pl.BlockSpec((TM * d // LANES, LANES), lambda i: (i, 0)),
                         pl.BlockSpec((TM, LANES), lambda i: (i, 0))]


def _token_out_shapes(t, d):
    return [jax.ShapeDtypeStruct((t, d), F32), jax.ShapeDtypeStruct((t * d // LANES, LANES), F32),
            jax.ShapeDtypeStruct((t, LANES), F32)]


def _back0(x, ctx, u, gate, hf, wcat, b_a, b_x, lam, w_out_bf, mod, g2, wr, br):
    n, d = x.shape
    nt = n // TM + 1
    t = nt * TM
    tile = lambda s: _scan_tile(True, s, nt)
    row = lambda v: v.reshape(1, -1)
    tok = pl.BlockSpec((TM, D_RNN), lambda s: (tile(s), 0))
    modspec = lambda c: pl.BlockSpec((SUBLANES, d), lambda s: (0, c))
    return pl.pallas_call(
        functools.partial(_back0_kernel, nt),
        grid=(nt,),
        in_specs=[pl.BlockSpec((TM, d), lambda s: (jnp.maximum(tile(s) - 1, 0), 0)),
                  _full((TM, d)),
                  tok, tok, tok,
                  _full((RG_BLOCKS, RG_BLOCK_W, 2 * RG_BLOCK_W)),
                  _full((1, D_RNN)), _full((1, D_RNN)), _full((1, D_RNN)),
                  _full((D_RNN, d)),
                  modspec(2), _full((1, d)), modspec(3), modspec(4),
                  _full(wr.shape), _full(br.shape)],
        out_specs=[pl.BlockSpec((TM, d), lambda s: (tile(s), 0)),
                   pl.BlockSpec((TM * d // LANES, LANES), lambda s: (tile(s), 0)),
                   pl.BlockSpec((TM, LANES), lambda s: (tile(s), 0))],
        out_shape=_token_out_shapes(t, d),
        scratch_shapes=[pltpu.VMEM((TM, D_RNN), F32)] + _SCAN_SCRATCH(),
        compiler_params=_params(),
        name="back0",
    )(x, ctx, u, gate, hf, wcat, row(b_a), row(b_x), row(lam), w_out_bf, mod, g2.reshape(1, -1),
      mod, mod, wr, br)


def _plan_kernel(n_tiles, nbp, cnt_ref, meta_ref, dest_ref, blk_ref):
    c = cnt_ref[...]
    padded = jnp.floor((c + (TB - 1)) * (1.0 / TB)) * TB
    r = lax.broadcasted_iota(jnp.int32, (N_EXPERTS, N_EXPERTS), 0)
    q = lax.broadcasted_iota(jnp.int32, (N_EXPERTS, N_EXPERTS), 1)
    lower = jnp.where(q <= r, 1.0, 0.0)
    pad_end = jnp.dot(lower, padded, precision=HI, preferred_element_type=F32)
    pad_start = pad_end - padded

    first_row = lax.broadcasted_iota(jnp.int32, (N_EXPERTS, nbp), 1).astype(F32) * TB
    owner = jnp.sum(jnp.where(pad_end[:, 0:1] <= first_row, 1.0, 0.0), axis=0, keepdims=True)
    blk_ref[0:1, :] = jnp.minimum(owner, N_EXPERTS - 1).astype(jnp.int32)
    n_used = pad_end[N_EXPERTS - 1:N_EXPERTS, 0:1] * (1.0 / TB)
    blk_ref[1:2, :] = jnp.broadcast_to(n_used, (1, nbp)).astype(jnp.int32)
    ends = jnp.concatenate([pad_end, jnp.zeros((LANES - N_EXPERTS, LANES), F32)], axis=0).T[0:1, :]
    blk_ref[2:3, :] = jnp.concatenate([ends, jnp.zeros((1, nbp - LANES), F32)], axis=1).astype(jnp.int32)
    blk_ref[3:SUBLANES, :] = jnp.zeros((SUBLANES - 3, nbp), jnp.int32)

    erow = lax.broadcasted_iota(jnp.int32, (N_EXPERTS, TM), 0).astype(F32)

    def body(i, carry):
        ln = pl.ds(pl.multiple_of(i * TM, TM), TM)
        for k in range(2):
            onehot = jnp.where(erow == meta_ref[k:k + 1, ln], 1.0, 0.0)
            start = jnp.sum(onehot * pad_start[:, 0:1], axis=0, keepdims=True)
            dest_ref[k:k + 1, ln] = (start + meta_ref[2 + k:3 + k, ln]).astype(jnp.int32)
        dest_ref[2:SUBLANES, ln] = jnp.zeros((SUBLANES - 2, TM), jnp.int32)
        return carry

    lax.fori_loop(0, n_tiles, body, 0)


def _plan(cnt, meta):
    t = meta.shape[1]
    n_blocks = (2 * t + N_EXPERTS * TB) // TB
    nbp = -(-n_blocks // LANES) * LANES
    vm = pl.BlockSpec(memory_space=pltpu.VMEM)
    dest, blk = pl.pallas_call(
        functools.partial(_plan_kernel, t // TM, nbp),
        in_specs=[vm, vm],
        out_specs=[vm, vm],
        out_shape=[jax.ShapeDtypeStruct((SUBLANES, t), jnp.int32),
                   jax.ShapeDtypeStruct((SUBLANES, nbp), jnp.int32)],
        compiler_params=pltpu.CompilerParams(vmem_limit_bytes=VMEM_LIMIT),
        name="moe_plan",
    )(cnt, meta)
    return dest, blk, n_blocks


def _row_wait(src_ref, dst_ref, d, sem):
    n = TM * d // LANES
    pltpu.make_async_copy(src_ref.at[pl.ds(0, n), :], dst_ref.at[pl.ds(0, n), :], sem).wait()


def _dispatch_kernel(n_blocks, d, pe_ref, nu_ref, dest_ref, h2_ref, buf_ref, dsm_ref, zero_ref,
                     sem_idx, sem_zero, sem_rows):
    step = pl.program_id(0)
    blk = TB * d // LANES

    def zero_block(first_row):
        start = pl.multiple_of(first_row * (d // LANES), blk)
        return pltpu.make_async_copy(zero_ref, buf_ref.at[pl.ds(start, blk), :], sem_zero)

    @pl.when(step == 0)
    def _():
        zero_ref[...] = jnp.zeros_like(zero_ref)

        def for_segments(fn):
            for e in range(N_EXPERTS):
                seg_start = pe_ref[e - 1] if e else 0

                @pl.when(pe_ref[e] > seg_start)
                def _():
                    fn(zero_block(pe_ref[e] - TB))

        for_segments(lambda cp: cp.start())
        lax.fori_loop(nu_ref[0], n_blocks, lambda b, c: (zero_block(b * TB).start(), c)[1], 0)
        for_segments(lambda cp: cp.wait())
        lax.fori_loop(nu_ref[0], n_blocks, lambda b, c: (zero_block(b * TB).wait(), c)[1], 0)

    cp = pltpu.make_async_copy(dest_ref, dsm_ref, sem_idx)
    cp.start()
    cp.wait()

    for r in range(TM):
        for k in range(2):
            pltpu.make_async_copy(_token_tile(h2_ref, r, d), _token_tile(buf_ref, dsm_ref[k, r], d),
                                  sem_rows).start(priority=k)
    for k in range(2):
        _row_wait(h2_ref, buf_ref, d, sem_rows)


def _dispatch(h2t, dest, pad_end, n_used, n_blocks, d):
    per = d // LANES
    t = h2t.shape[0] // per
    grid_spec = pltpu.PrefetchScalarGridSpec(
        num_scalar_prefetch=2,
        grid=(t // TM,),
        in_specs=[pl.BlockSpec((SUBLANES, TM), lambda i, pe, nu: (0, i)),
                  pl.BlockSpec((TM * per, LANES), lambda i, pe, nu: (i, 0))],
        out_specs=pl.BlockSpec(memory_space=pl.ANY),
        scratch_shapes=[pltpu.SMEM((SUBLANES, TM), jnp.int32),
                        pltpu.VMEM((TB * per, LANES), F32),
                        pltpu.SemaphoreType.DMA, pltpu.SemaphoreType.DMA,
                        pltpu.SemaphoreType.DMA],
    )
    return pl.pallas_call(
        functools.partial(_dispatch_kernel, n_blocks, d),
        grid_spec=grid_spec,
        out_shape=jax.ShapeDtypeStruct((n_blocks * TB * per, LANES), F32),
        compiler_params=_params(),
        name="moe_dispatch",
    )(pad_end, n_used, dest, h2t)


EXPERT_RING = 3


def _experts_kernel(be_ref, nu_ref, buf_ref, wg_ref, wu_ref, wd_ref, y_ref, wg_bf, wu_bf, wd_bf,
                    x_ref, sems):
    step = pl.program_id(0)
    n_used = nu_ref[0]
    used = step < n_used
    new_expert = jnp.logical_or(step == 0, be_ref[step] != be_ref[jnp.maximum(step - 1, 0)])
    d = wg_bf.shape[0]
    blk = TB * d // LANES

    def block_copy(b):
        slot = b % EXPERT_RING
        return pltpu.make_async_copy(buf_ref.at[pl.ds(pl.multiple_of(b * blk, blk), blk), :],
                                     x_ref.at[slot], sems.at[slot])

    for b in range(EXPERT_RING - 1):
        @pl.when(jnp.logical_and(step == 0, b < n_used))
        def _():
            block_copy(b).start()

    @pl.when(step + EXPERT_RING - 1 < n_used)
    def _():
        block_copy(step + EXPERT_RING - 1).start()

    @pl.when(jnp.logical_and(used, new_expert))
    def _():
        wg_bf[...] = wg_ref[0].astype(BF16)
        wu_bf[...] = wu_ref[0].astype(BF16)
        wd_bf[...] = wd_ref[0].astype(BF16)

    @pl.when(used)
    def _():
        block_copy(step).wait()
        xb = _load_token_tiles(x_ref.at[step % EXPERT_RING], TB, d).astype(BF16)
        g = jnp.dot(xb, wg_bf[...], preferred_element_type=F32)
        u = jnp.dot(xb, wu_bf[...], preferred_element_type=F32)
        a = (g * _sigmoid(g)) * u
        _store_token_tiles(y_ref, jnp.dot(a.astype(BF16), wd_bf[...], preferred_element_type=F32))

    @pl.when(jnp.logical_not(used))
    def _():
        y_ref[...] = jnp.zeros_like(y_ref)


def _experts(buf, blk_e, n_used, wg, wu, wd, n_blocks):
    d = wg.shape[1]
    blk = TB * d // LANES
    last = lambda i, nu: jnp.maximum(jnp.minimum(i, nu[0] - 1), 0)
    grid_spec = pltpu.PrefetchScalarGridSpec(
        num_scalar_prefetch=2,
        grid=(n_blocks,),
        in_specs=[pl.BlockSpec(memory_space=pl.ANY),
                  pl.BlockSpec((1, d, D_EXPERT), lambda i, be, nu: (be[last(i, nu)], 0, 0)),
                  pl.BlockSpec((1, d, D_EXPERT), lambda i, be, nu: (be[last(i, nu)], 0, 0)),
                  pl.BlockSpec((1, D_EXPERT, d), lambda i, be, nu: (be[last(i, nu)], 0, 0))],
        out_specs=pl.BlockSpec((blk, LANES), lambda i, be, nu: (i, 0)),
        scratch_shapes=[pltpu.VMEM((d, D_EXPERT), BF16), pltpu.VMEM((d, D_EXPERT), BF16),
                        pltpu.VMEM((D_EXPERT, d), BF16),
                        pltpu.VMEM((EXPERT_RING, blk, LANES), F32),
                        pltpu.SemaphoreType.DMA((EXPERT_RING,))],
    )
    return pl.pallas_call(
        _experts_kernel,
        grid_spec=grid_spec,
        out_shape=jax.ShapeDtypeStruct(buf.shape, F32),
        compiler_params=_params(),
        name="moe_experts",
    )(blk_e, n_used, buf, wg, wu, wd)


def _gather_start(dest_ref, dsm_ref, ybuf_ref, rows_ref, slot, d, sem_idx, sem):
    cp = pltpu.make_async_copy(dest_ref, dsm_ref, sem_idx)
    cp.start()
    cp.wait()
    for r in range(TM):
        for k in range(2):
            pltpu.make_async_copy(_token_tile(ybuf_ref, dsm_ref[k, r], d),
                                  _token_tile(rows_ref.at[slot, k], r, d), sem).start(priority=k)


def _gathered_rows(dest_ref, next_ref, dsm_ref, ybuf_ref, rows_ref, d, sem_idx, sems):
    step = pl.program_id(0)
    nt = pl.num_programs(0)
    slot = step % 2

    @pl.when(step == 0)
    def _():
        _gather_start(dest_ref, dsm_ref, ybuf_ref, rows_ref, 0, d, sem_idx, sems.at[0])

    @pl.when(step + 1 < nt)
    def _():
        _gather_start(next_ref, dsm_ref, ybuf_ref, rows_ref, 1 - slot, d, sem_idx, sems.at[1 - slot])

    for k in range(2):
        _row_wait(ybuf_ref, rows_ref.at[slot, k], d, sems.at[slot])
    return [_load_token_tiles(rows_ref.at[slot, k], TM, d) for k in range(2)]


def _token_weights(meta_ref):
    meta = jnp.concatenate([meta_ref[...], jnp.zeros((LANES - SUBLANES, TM), F32)], axis=0)
    mt = meta.T
    return mt[:, 4:5], mt[:, 5:6]


_COMBINE_SCRATCH = lambda d: [pltpu.SMEM((SUBLANES, TM), jnp.int32),
                              pltpu.VMEM((2, 2, TM * d // LANES, LANES), F32),
                              pltpu.SemaphoreType.DMA, pltpu.SemaphoreType.DMA((2,))]


def _next_tile_spec(nt):
    return pl.BlockSpec((SUBLANES, TM), lambda i: (0, jnp.minimum(i + 1, nt - 1)))


def _combine0_kernel(x1_ref, dest_ref, next_ref, meta_ref, gt_ref, ybuf_ref, x2_ref, ctx2_ref,
                     dsm_ref, rows_ref, sem_idx, sems):
    is_ctx = pl.program_id(0) == 0
    y0, y1 = _gathered_rows(dest_ref, next_ref, dsm_ref, ybuf_ref, rows_ref, x1_ref.shape[1],
                            sem_idx, sems)
    w0, w1 = _token_weights(meta_ref)
    out = x1_ref[...] + _mod_row(gt_ref, is_ctx) * (w0 * y0 + w1 * y1)
    x2_ref[...] = out

    @pl.when(is_ctx)
    def _():
        ctx2_ref[...] = out


def _combine0(x1, dest, meta, mod, ybuf):
    t, d = x1.shape
    nt = t // TM
    return pl.pallas_call(
        _combine0_kernel,
        grid=(nt,),
        in_specs=[pl.BlockSpec((TM, d), lambda i: (i, 0)),
                  pl.BlockSpec((SUBLANES, TM), lambda i: (0, i)),
                  _next_tile_spec(nt),
                  pl.BlockSpec((SUBLANES, TM), lambda i: (0, i)),
                  pl.BlockSpec((SUBLANES, d), lambda i: (0, 5)),
                  pl.BlockSpec(memory_space=pl.ANY)],
        out_specs=[pl.BlockSpec((TM, d), lambda i: (jnp.maximum(i - 1, 0), 0)),
                   _full((TM, d))],
        out_shape=[jax.ShapeDtypeStruct((t - TM, d), F32), jax.ShapeDtypeStruct((TM, d), F32)],
        scratch_shapes=_COMBINE_SCRATCH(d),
        compiler_params=_params(),
        name="moe_combine0",
    )(x1, dest, dest, meta, mod, ybuf)


def _combine1_kernel(x1_ref, dest_ref, next_ref, meta_ref, gt_ref, gf_ref, ybuf_ref, o_ref,
                     dsm_ref, rows_ref, sem_idx, sems):
    y0, y1 = _gathered_rows(dest_ref, next_ref, dsm_ref, ybuf_ref, rows_ref, x1_ref.shape[1],
                            sem_idx, sems)
    w0, w1 = _token_weights(meta_ref)
    out = x1_ref[...] + gt_ref[0:1, :] * (w0 * y0 + w1 * y1)
    o_ref[...] = _rms(out, gf_ref[...])


def _combine1(x1, dest, meta, mod, gf, ybuf):
    t, d = x1.shape
    rows = t // GRID_W
    nt = t // TM
    out = pl.pallas_call(
        _combine1_kernel,
        grid=(nt,),
        in_specs=[pl.BlockSpec((TM, d), lambda i: (i, 0)),
                  pl.BlockSpec((SUBLANES, TM), lambda i: (0, i)),
                  _next_tile_spec(nt),
                  pl.BlockSpec((SUBLANES, TM), lambda i: (0, i)),
                  pl.BlockSpec((SUBLANES, d), lambda i: (0, 5)),
                  _full((1, d)),
                  pl.BlockSpec(memory_space=pl.ANY)],
        out_specs=pl.BlockSpec((TM, d), lambda i: (0, i)),
        out_shape=jax.ShapeDtypeStruct((rows, GRID_W * d), F32),
        scratch_shapes=_COMBINE_SCRATCH(d),
        compiler_params=_params(),
        name="moe_combine1",
    )(x1, dest, dest, meta, mod, gf.reshape(1, -1), ybuf)
    return out.reshape(t, d)


def _moe(h2t, meta, cnt, w_gate, w_up, w_down):
    dest, blk, n_blocks = _plan(cnt, meta)
    n_used = blk[1, :1]
    buf = _dispatch(h2t, dest, blk[2, :N_EXPERTS], n_used, n_blocks, w_gate.shape[1])
    ybuf = _experts(buf, blk[0], n_used, w_gate, w_up, w_down, n_blocks)
    return dest, ybuf


def _inproj1_kernel(x_ref, ctx_ref, g_ref, sh_ref, sc_ref, wq_ref, wkt_ref, wv_ref, wo_ref,
                    wg_ref, bg_ref, q_ref, kt_ref, v_ref, o_ref, gcol_ref):
    is_ctx = pl.program_id(0) == 0
    xin = jnp.where(is_ctx, ctx_ref[...], x_ref[...])
    h = _rms(xin, g_ref[...]) * (1.0 + _mod_row(sc_ref, is_ctx)) + _mod_row(sh_ref, is_ctx)
    hb = h.astype(BF16)
    q = jnp.dot(hb, wq_ref[...], preferred_element_type=F32) * (ML_DQK ** -0.5)
    q_ref[...] = q.astype(BF16)
    kt = lax.dot_general(wkt_ref[...], hb, (((1,), (1,)), ((), ())), preferred_element_type=F32)
    kt_ref[...] = kt.astype(BF16)
    v_ref[...] = jnp.dot(hb, wv_ref[...], preferred_element_type=F32).astype(BF16)
    o_ref[...] = jnp.dot(hb, wo_ref[...], preferred_element_type=F32).astype(BF16)
    gcol_ref[...] = _dot_bf16x3(h, wg_ref) + bg_ref[...]


def _inproj1(x, ctx, g1, mod, w_in, b_gates):
    n, d = x.shape
    rows = n // GRID_W
    nt = GRID_W + 1
    t = nt * TM
    ng = 4 * ML_HEADS
    wq = w_in[:, :ML_QK_W].astype(BF16)
    wkt = w_in[:, ML_QK_W:2 * ML_QK_W].T.astype(BF16)
    wv = w_in[:, 2 * ML_QK_W:2 * ML_QK_W + ML_V_W].astype(BF16)
    wo = w_in[:, 2 * ML_QK_W + ML_V_W:2 * ML_QK_W + ML_V_W + d].astype(BF16)
    wg = _split_bf16(jnp.zeros((d, LANES), F32).at[:, :ng].set(w_in[:, -ng:]))
    bg = jnp.zeros((1, LANES), F32).at[0, :ng].set(b_gates)
    tok = lambda w: pl.BlockSpec((TM, w), lambda i: (i, 0))
    return pl.pallas_call(
        _inproj1_kernel,
        grid=(nt,),
        in_specs=[pl.BlockSpec((rows, d), lambda i: (0, jnp.maximum(i - 1, 0))),
                  _full((TM, d)),
                  _full((1, d)),
                  pl.BlockSpec((SUBLANES, d), lambda i: (0, 0)),
                  pl.BlockSpec((SUBLANES, d), lambda i: (0, 1)),
                  _full(wq.shape), _full(wkt.shape), _full(wv.shape), _full(wo.shape),
                  _full(wg.shape), _full(bg.shape)],
        out_specs=[tok(ML_QK_W),
                   pl.BlockSpec((ML_QK_W, TM), lambda i: (0, i)),
                   tok(ML_V_W), tok(d), tok(LANES)],
        out_shape=[jax.ShapeDtypeStruct((t, ML_QK_W), BF16),
                   jax.ShapeDtypeStruct((ML_QK_W, t), BF16),
                   jax.ShapeDtypeStruct((t, ML_V_W), BF16),
                   jax.ShapeDtypeStruct((t, d), BF16),
                   jax.ShapeDtypeStruct((t, LANES), F32)],
        compiler_params=_params(),
        name="inproj1",
    )(x.reshape(rows, GRID_W * d), ctx, g1.reshape(1, -1), mod, mod, wq, wkt, wv, wo, wg, bg)


def _log_sigmoid(x):
    return jnp.minimum(x, 0.0) - jnp.log1p(jnp.exp(-jnp.abs(x)))


LOG2E = 1.4426950408889634


def _mlstm_kernel(reverse, q_ref, kt_ref, v_ref, gcol_ref, h_ref, c_ref, m_ref):
    @pl.when(pl.program_id(0) == 0)
    def _():
        c_ref[...] = jnp.zeros_like(c_ref)
        m_ref[...] = jnp.zeros_like(m_ref)

    L = TM
    half = L // 2
    gi = 2 * ML_HEADS if reverse else 0
    gf = gi + ML_HEADS
    end = 0 if reverse else L - 1
    rr = lax.broadcasted_iota(jnp.int32, (L, L), 0)
    cc = lax.broadcasted_iota(jnp.int32, (L, L), 1)
    tri = jnp.where((cc >= rr) if reverse else (cc <= rr), 1.0, 0.0).astype(BF16)
    rh = lax.broadcasted_iota(jnp.int32, (half, half), 0)
    ch = lax.broadcasted_iota(jnp.int32, (half, half), 1)
    diag = (ch >= rh) if reverse else (ch <= rh)

    gates = gcol_ref[...]
    lane = lax.broadcasted_iota(jnp.int32, (L, LANES), 1)
    mine = jnp.logical_and(lane >= gf, lane < gf + ML_HEADS)
    lf = jnp.where(mine, _log_sigmoid(gates), 0.0)
    p0 = lf.astype(BF16)
    r1 = lf - p0.astype(F32)
    p1 = r1.astype(BF16)
    p2 = (r1 - p1.astype(F32)).astype(BF16)
    cum = (jnp.dot(tri, p0, preferred_element_type=F32) + jnp.dot(tri, p1, preferred_element_type=F32)
           + jnp.dot(tri, p2, preferred_element_type=F32))

    r = jnp.where(mine, pltpu.roll(gates, ML_HEADS, 1) - cum, 0.0)
    row = lax.broadcasted_iota(jnp.int32, (L, LANES), 0)
    cm = r
    sh = 1
    while sh < L:
        if sh < SUBLANES:
            if reverse:
                cm = jnp.where(row < L - sh, jnp.maximum(cm, pltpu.roll(cm, L - sh, 0)), cm)
            else:
                cm = jnp.where(row >= sh, jnp.maximum(cm, pltpu.roll(cm, sh, 0)), cm)
        else:
            pad = jnp.full((sh, LANES), NEG_INF, F32)
            moved = (jnp.concatenate([cm[sh:], pad], axis=0) if reverse
                     else jnp.concatenate([pad, cm[:L - sh]], axis=0))
            cm = jnp.maximum(cm, moved)
        sh *= 2
    r8 = r.T[gf:gf + ML_HEADS]

    m_prev = m_ref[0:1, :]
    mm = jnp.maximum(m_prev, cm)
    m_t = cum + mm
    w_inter = jnp.exp(m_prev - mm)
    floor = jnp.exp(-m_t)
    mm2 = mm * LOG2E
    m_new = m_t[end:end + 1, :]
    shift2 = (cum[end:end + 1, :] - m_new) * LOG2E
    decay = jnp.exp(cum[end:end + 1, :] + m_prev - m_new)
    m_ref[0:1, :] = m_new

    ones_col = jnp.where(lax.broadcasted_iota(jnp.int32, (L, ML_DV), 1) == 0, 1.0, 0.0).astype(BF16)
    top, bot = slice(0, half), slice(half, L)
    dot = functools.partial(jnp.dot, preferred_element_type=F32)

    for hd in range(ML_HEADS):
        ln = gf + hd
        r2_row = r8[hd:hd + 1, :] * LOG2E

        def weights(tq, ks, masked):
            w = jnp.exp2(r2_row[:, ks] - mm2[tq, ln:ln + 1])
            return jnp.where(diag, w, 0.0) if masked else w

        qh = q_ref[:, hd * ML_DQK:(hd + 1) * ML_DQK]
        kth = kt_ref[hd * ML_DQK:(hd + 1) * ML_DQK, :]
        vext = jnp.concatenate([v_ref[:, hd * ML_DV:(hd + 1) * ML_DV], ones_col], axis=1)
        if reverse:
            s_top = dot(qh[top], kth) * jnp.concatenate([weights(top, top, True),
                                                         weights(top, bot, False)], axis=1)
            s_bot = dot(qh[bot], kth[:, bot]) * weights(bot, bot, True)
            intra = jnp.concatenate([dot(s_top.astype(BF16), vext),
                                     dot(s_bot.astype(BF16), vext[bot])], axis=0)
        else:
            s_top = dot(qh[top], kth[:, top]) * weights(top, top, True)
            s_bot = dot(qh[bot], kth) * jnp.concatenate([weights(bot, top, False),
                                                         weights(bot, bot, True)], axis=1)
            intra = jnp.concatenate([dot(s_top.astype(BF16), vext[top]),
                                     dot(s_bot.astype(BF16), vext)], axis=0)
        state = c_ref[hd]
        tot = w_inter[:, ln:ln + 1] * dot(qh, state.astype(BF16)) + intra
        den = tot[:, ML_DV:ML_DV + 1]
        h_ref[:, hd * ML_DV:(hd + 1) * ML_DV] = (
            tot[:, :ML_DV] / jnp.maximum(jnp.abs(den), floor[:, ln:ln + 1])).astype(h_ref.dtype)

        w_state = jnp.exp2(r2_row + shift2[:, ln:ln + 1])
        kw = (kth.astype(F32) * w_state).astype(BF16)
        c_ref[hd] = decay[:, ln:ln + 1] * state + dot(kw, vext)


def _mlstm(q, kt, v, gcol, reverse):
    t = q.shape[0]
    nt = t // TM
    tile = lambda s: _scan_tile(reverse, s, nt)
    return pl.pallas_call(
        functools.partial(_mlstm_kernel, reverse),
        grid=(nt,),
        in_specs=[pl.BlockSpec((TM, ML_QK_W), lambda s: (tile(s), 0)),
                  pl.BlockSpec((ML_QK_W, TM), lambda s: (0, tile(s))),
                  pl.BlockSpec((TM, ML_V_W), lambda s: (tile(s), 0)),
                  pl.BlockSpec((TM, LANES), lambda s: (tile(s), 0))],
        out_specs=pl.BlockSpec((TM, ML_V_W), lambda s: (tile(s), 0)),
        out_shape=jax.ShapeDtypeStruct((t, ML_V_W), BF16),
        scratch_shapes=[pltpu.VMEM((ML_HEADS, ML_DQK, 2 * ML_DV), F32),
                        pltpu.VMEM((SUBLANES, LANES), F32)],
        compiler_params=_params(),
        name="mlstm_bwd" if reverse else "mlstm_fwd",
    )(q, kt, v, gcol)


def _finish1_kernel(x_ref, hf_ref, hb_ref, o_ref, ng_ref, w_ref, gt_ref, g2_ref, sh_ref, sc_ref,
                    wr_ref, br_ref, x1_ref, h2_ref, lg_ref):
    hs = hf_ref[...].astype(F32) + hb_ref[...].astype(F32)
    parts = []
    for hd in range(ML_HEADS):
        blk = hs[:, hd * ML_DV:(hd + 1) * ML_DV]
        parts.append(blk * lax.rsqrt(jnp.mean(blk * blk, axis=-1, keepdims=True) + EPS))
    hn = jnp.concatenate(parts, axis=1) * ng_ref[...]
    y = jnp.dot((hn * _sigmoid(o_ref[...].astype(F32))).astype(BF16), w_ref[...],
                preferred_element_type=F32)
    x1 = x_ref[...] + gt_ref[0:1, :] * y
    x1_ref[...] = x1
    h2 = _rms(x1, g2_ref[...]) * (1.0 + sc_ref[0:1, :]) + sh_ref[0:1, :]
    _store_token_tiles(h2_ref, h2)
    lg_ref[...] = _router_logits(h2, wr_ref, br_ref)


def _finish1(x, hf, hb, o, norm_g, w_out_bf, mod, g2, wr, br):
    n, d = x.shape
    rows = n // GRID_W
    lat = lambda w: pl.BlockSpec((TM, w), lambda i: (i + 1, 0))
    modspec = lambda c: pl.BlockSpec((SUBLANES, d), lambda i: (0, c))
    return pl.pallas_call(
        _finish1_kernel,
        grid=(GRID_W,),
        in_specs=[pl.BlockSpec((rows, d), lambda i: (0, i)),
                  lat(ML_V_W), lat(ML_V_W), lat(d),
                  _full((1, ML_V_W)), _full((ML_V_W, d)),
                  modspec(2), _full((1, d)), modspec(3), modspec(4),
                  _full(wr.shape), _full(br.shape)],
        out_specs=_TOKEN_OUTS(d),
        out_shape=_token_out_shapes(n, d),
        compiler_params=_params(),
        name="finish1",
    )(x.reshape(rows, GRID_W * d), hf, hb, o, norm_g.reshape(1, -1), w_out_bf, mod,
      g2.reshape(1, -1), mod, mod, wr, br)


def kernel(x, c, ctx, c_ctx,
           l0_ada_w, l0_ada_b, l0_norm1_g, l0_norm2_g,
           l0_rg_w_in, l0_rg_conv_w, l0_rg_conv_b, l0_rg_w_a, l0_rg_b_a, l0_rg_w_x, l0_rg_b_x,
           l0_rg_lambda, l0_rg_w_out,
           l0_moe_w_grp, l0_moe_b_grp, l0_moe_w_exp, l0_moe_b_exp, l0_moe_w_gate, l0_moe_w_up,
           l0_moe_w_down,
           l1_ada_w, l1_ada_b, l1_norm1_g, l1_norm2_g,
           l1_ml_w_in, l1_ml_b_gates, l1_ml_norm_g, l1_ml_w_out,
           l1_moe_w_grp, l1_moe_b_grp, l1_moe_w_exp, l1_moe_b_exp, l1_moe_w_gate, l1_moe_w_up,
           l1_moe_w_down,
           final_norm_g):
    assert x.shape[0] == 1 and ctx.shape[1] == TM and x.shape[1] == GRID_W * TM
    xs, cs = x[0], ctx[0]
    d = xs.shape[1]
    cond8 = jnp.zeros((SUBLANES, d), F32).at[0].set(c[0]).at[1].set(c_ctx)

    mod0 = _adaln(cond8, l0_ada_w, l0_ada_b)
    conv_w8 = jnp.zeros((SUBLANES, D_RNN), F32).at[:CONV_W].set(l0_rg_conv_w)
    wcat = [jnp.concatenate([l0_rg_w_a[dr], l0_rg_w_x[dr]], axis=-1).astype(BF16) for dr in range(2)]
    gate, u, hf0 = _front0(xs, cs, l0_norm1_g, mod0, l0_rg_w_in.astype(BF16), conv_w8, l0_rg_conv_b,
                           wcat[0], l0_rg_b_a[0], l0_rg_b_x[0], l0_rg_lambda[0])
    wr0, br0 = _router_weights(l0_moe_w_grp, l0_moe_b_grp, l0_moe_w_exp, l0_moe_b_exp)
    x1, h2, logits = _back0(xs, cs, u, gate, hf0, wcat[1], l0_rg_b_a[1], l0_rg_b_x[1], l0_rg_lambda[1],
                            l0_rg_w_out.astype(BF16), mod0, l0_norm2_g, wr0, br0)
    meta, cnt = _route(logits, ROUTE_TILE_0)
    dest, ybuf = _moe(h2, meta, cnt, l0_moe_w_gate, l0_moe_w_up, l0_moe_w_down)
    x2, ctx2 = _combine0(x1, dest, meta, mod0, ybuf)

    mod1 = _adaln(cond8, l1_ada_w, l1_ada_b)
    q, kt, v, o, gcol = _inproj1(x2, ctx2, l1_norm1_g, mod1, l1_ml_w_in, l1_ml_b_gates)
    hf = _mlstm(q, kt, v, gcol, False)
    hb = _mlstm(q, kt, v, gcol, True)
    wr1, br1 = _router_weights(l1_moe_w_grp, l1_moe_b_grp, l1_moe_w_exp, l1_moe_b_exp)
    x1, h2, logits = _finish1(x2, hf, hb, o, l1_ml_norm_g, l1_ml_w_out.astype(BF16), mod1,
                              l1_norm2_g, wr1, br1)
    meta, cnt = _route(logits, ROUTE_TILE_1)
    dest, ybuf = _moe(h2, meta, cnt, l1_moe_w_gate, l1_moe_w_up, l1_moe_w_down)
    out = _combine1(x1, dest, meta, mod1, final_norm_g, ybuf)
    return out[None]
```

```python
import functools

import jax
import jax.numpy as jnp
from jax import lax
from jax.experimental import pallas as pl
from jax.experimental.pallas import tpu as pltpu

D_MODEL = 1024
GRID_W = 64
N_MOD = 6
EPS = 1e-6

D_RNN = 1280
RG_BLOCKS = 10
RG_BLOCK_W = D_RNN // RG_BLOCKS
CONV_W = 4
CONV_PAD_L = 2
RG_C = 8.0

ML_HEADS = 8
ML_DQK = D_MODEL // (2 * ML_HEADS)
ML_DV = D_MODEL // ML_HEADS
ML_QK_W = ML_HEADS * ML_DQK
ML_V_W = ML_HEADS * ML_DV

N_GROUPS = 4
EXPERTS_PER_GROUP = 8
N_EXPERTS = N_GROUPS * EXPERTS_PER_GROUP
D_EXPERT = 512

TM = 256
TB = 256
ROUTE_TILE_0 = 1280
ROUTE_TILE_1 = 1024
SUBLANES = 8
LANES = 128
VMEM_LIMIT = 48 * 1024 * 1024

F32 = jnp.float32
BF16 = jnp.bfloat16
HI = lax.Precision.HIGHEST
NEG_INF = float("-inf")


def _params(n_axes=1):
    return pltpu.CompilerParams(dimension_semantics=("arbitrary",) * n_axes,
                                vmem_limit_bytes=VMEM_LIMIT)


def _rms(x, g):
    return x * lax.rsqrt(jnp.mean(x * x, axis=-1, keepdims=True) + EPS) * g


def _sigmoid(x):
    return 1.0 / (1.0 + jnp.exp(-x))


def _softplus(x):
    return jnp.maximum(x, 0.0) + jnp.log1p(jnp.exp(-jnp.abs(x)))


def _gelu_tanh(x):
    return 0.5 * x * (1.0 + jnp.tanh(0.7978845608028654 * (x + 0.044715 * (x * x * x))))


def _full(shape):
    return pl.BlockSpec(shape, lambda *_: (0,) * len(shape))


def _store_token_tiles(ref, x, first_row=0):
    per = x.shape[1] // LANES
    for c in range(per):
        ref[pl.ds(first_row * per + c, x.shape[0], stride=per), :] = x[:, c * LANES:(c + 1) * LANES]


def _load_token_tiles(ref, rows, d):
    per = d // LANES
    return jnp.concatenate([ref[pl.ds(c, rows, stride=per), :] for c in range(per)], axis=1)


def _token_tile(ref, row, d):
    per = d // LANES
    start = row * per if isinstance(row, int) else pl.multiple_of(row * per, per)
    return ref.at[pl.ds(start, per), :]


def _adaln_kernel(cond_ref, w_ref, b_ref, o_ref):
    c = cond_ref[...]
    s = c * _sigmoid(c)
    o_ref[...] = jnp.dot(s, w_ref[...], precision=HI, preferred_element_type=F32) + b_ref[...]


def _adaln(cond8, w, b):
    d = w.shape[0]
    return pl.pallas_call(
        _adaln_kernel,
        grid=(N_MOD,),
        in_specs=[_full((SUBLANES, d)),
                  pl.BlockSpec((d, d), lambda j: (0, j)),
                  pl.BlockSpec((1, d), lambda j: (0, j))],
        out_specs=pl.BlockSpec((SUBLANES, d), lambda j: (0, j)),
        out_shape=jax.ShapeDtypeStruct((SUBLANES, N_MOD * d), F32),
        compiler_params=_params(),
        name="adaln",
    )(cond8, w, b.reshape(1, -1))


def _mod_row(ref, is_ctx):
    return jnp.where(is_ctx, ref[1:2, :], ref[0:1, :])


def _front0_kernel(nt, x_ref, ctx_ref, g_ref, sh_ref, sc_ref, w_ref, cw_ref, cb_ref,
                   wcat_ref, ba_ref, bx_ref, lam_ref, gate_ref, u_ref, hf_ref,
                   ext_ref, ubuf_ref, a_ref, b_ref, o_ref, carry_ref):
    s = pl.program_id(0)
    is_ctx = s == 0

    @pl.when(s == 0)
    def _():
        ext_ref[...] = jnp.zeros_like(ext_ref)
        ubuf_ref[...] = jnp.zeros_like(ubuf_ref)

    @pl.when(s <= 2)
    def _():
        carry_ref[...] = jnp.zeros_like(carry_ref)

    xin = jnp.where(is_ctx, ctx_ref[...], x_ref[...])
    h = _rms(xin, g_ref[...]) * (1.0 + _mod_row(sc_ref, is_ctx)) + _mod_row(sh_ref, is_ctx)
    hb = h.astype(BF16)

    right_valid = jnp.logical_and(s >= 2, s <= nt - 1)
    base = SUBLANES - CONV_PAD_L
    for j in range(RG_BLOCKS):
        ln = slice(j * RG_BLOCK_W, (j + 1) * RG_BLOCK_W)
        _rglru_block(False, j, ubuf_ref.at[s % 2], wcat_ref, ba_ref, bx_ref, lam_ref, hf_ref,
                     a_ref, b_ref, o_ref, carry_ref)
        p = jnp.dot(hb, w_ref[j], preferred_element_type=F32)
        gate_ref[:, ln] = p[:, :RG_BLOCK_W]
        rec = p[:, RG_BLOCK_W:]
        ext_ref[j, SUBLANES + TM:, :] = jnp.where(right_valid, rec[0:SUBLANES], 0.0)
        u = cb_ref[:, ln] + ext_ref[j, pl.ds(base, TM), :] * cw_ref[0:1, ln]
        for k in range(1, CONV_W):
            u = u + ext_ref[j, pl.ds(base + k, TM), :] * cw_ref[k:k + 1, ln]
        u_ref[:, ln] = u
        ubuf_ref[(s + 1) % 2, :, ln] = u
        ext_ref[j, 0:SUBLANES, :] = jnp.where(s >= 2, ext_ref[j, TM:TM + SUBLANES, :], 0.0)
        ext_ref[j, SUBLANES:SUBLANES + TM, :] = rec


_SCAN_SLAB = lambda: pltpu.VMEM((RG_BLOCKS, SUBLANES * SCAN_PITCH, LANES), F32)
_SCAN_SCRATCH = lambda: [_SCAN_SLAB(), _SCAN_SLAB(), _SCAN_SLAB(), pltpu.VMEM((SUBLANES, D_RNN), F32)]


def _front0(x, ctx, g1, mod, w_in_bf, conv_w8, conv_b, wcat, b_a, b_x, lam):
    n, d = x.shape
    nx = n // TM
    nt = nx + 1
    t = nt * TM
    row = lambda v: v.reshape(1, -1)
    w_blocks = jnp.concatenate([w_in_bf[:, :D_RNN].reshape(d, RG_BLOCKS, RG_BLOCK_W),
                                w_in_bf[:, D_RNN:].reshape(d, RG_BLOCKS, RG_BLOCK_W)], axis=2)
    w_blocks = w_blocks.transpose(1, 0, 2)
    return pl.pallas_call(
        functools.partial(_front0_kernel, nt),
        grid=(nt + 2,),
        in_specs=[pl.BlockSpec((TM, d), lambda s: (jnp.clip(s - 1, 0, nx - 1), 0)),
                  _full((TM, d)),
                  _full((1, d)),
                  pl.BlockSpec((SUBLANES, d), lambda s: (0, 0)),
                  pl.BlockSpec((SUBLANES, d), lambda s: (0, 1)),
                  _full((RG_BLOCKS, d, 2 * RG_BLOCK_W)),
                  _full((SUBLANES, D_RNN)), _full((1, D_RNN)),
                  _full((RG_BLOCKS, RG_BLOCK_W, 2 * RG_BLOCK_W)),
                  _full((1, D_RNN)), _full((1, D_RNN)), _full((1, D_RNN))],
        out_specs=[pl.BlockSpec((TM, D_RNN), lambda s: (jnp.minimum(s, nt - 1), 0)),
                   pl.BlockSpec((TM, D_RNN), lambda s: (jnp.maximum(s - 1, 0), 0)),
                   pl.BlockSpec((TM, D_RNN), lambda s: (jnp.maximum(s - 2, 0), 0))],
        out_shape=[jax.ShapeDtypeStruct((t, D_RNN), F32),
                   jax.ShapeDtypeStruct((t + TM, D_RNN), F32),
                   jax.ShapeDtypeStruct((t, D_RNN), F32)],
        scratch_shapes=[pltpu.VMEM((RG_BLOCKS, TM + 2 * SUBLANES, LANES), F32),
                        pltpu.VMEM((2, TM, D_RNN), F32)] + _SCAN_SCRATCH(),
        compiler_params=_params(),
        name="front0",
    )(x, ctx, g1.reshape(1, -1), mod, mod, w_blocks, conv_w8, row(conv_b), wcat, row(b_a), row(b_x),
      row(lam))


def _scan_tile(reverse, s, nt):
    if not reverse:
        return s
    return jnp.where(s == 0, 0, nt - s)


SCAN_CHUNK = TM // SUBLANES
SCAN_PITCH = SCAN_CHUNK + 4


def _rglru_block(reverse, j, u_ref, wcat_ref, ba_ref, bx_ref, lam_ref, h_ref,
                 a_ref, b_ref, o_ref, carry_ref):
    steps = range(SCAN_CHUNK - 1, -1, -1) if reverse else range(SCAN_CHUNK)
    chunks = range(SUBLANES - 1, -1, -1) if reverse else range(SUBLANES)
    ln = slice(j * RG_BLOCK_W, (j + 1) * RG_BLOCK_W)
    u = u_ref[:, ln]
    g = jnp.dot(u.astype(BF16), wcat_ref[j], preferred_element_type=F32)
    half_rate = (-0.5 * RG_C) * _softplus(-lam_ref[:, ln])
    log_a = jnp.tanh(0.5 * (g[:, :RG_BLOCK_W] + ba_ref[:, ln])) * half_rate + half_rate
    ig = 0.5 * jnp.tanh(0.5 * (g[:, RG_BLOCK_W:] + bx_ref[:, ln])) + 0.5
    a = jnp.exp(log_a)
    b = jnp.sqrt(-jnp.tanh(log_a) * (a * a + 1.0)) * ig * u
    for c in range(SUBLANES):
        a_ref[j, pl.ds(c * SCAN_PITCH, SCAN_CHUNK), :] = a[c * SCAN_CHUNK:(c + 1) * SCAN_CHUNK]
        b_ref[j, pl.ds(c * SCAN_PITCH, SCAN_CHUNK), :] = b[c * SCAN_CHUNK:(c + 1) * SCAN_CHUNK]

    row = lambda ref, i: ref[j, pl.ds(i, SUBLANES, stride=SCAN_PITCH), :]
    end = jnp.zeros((SUBLANES, LANES), F32)
    decay = jnp.ones((SUBLANES, LANES), F32)
    for i in steps:
        ai = row(a_ref, i)
        end = ai * end + row(b_ref, i)
        decay = decay * ai

    state = carry_ref[0:1, ln]
    entry = [None] * SUBLANES
    for c in chunks:
        entry[c] = state
        state = decay[c:c + 1] * state + end[c:c + 1]
    carry_ref[0:1, ln] = state

    hcur = jnp.concatenate(entry, axis=0)
    for i in steps:
        hcur = row(a_ref, i) * hcur + row(b_ref, i)
        o_ref[j, pl.ds(i, SUBLANES, stride=SCAN_PITCH), :] = hcur
    for c in range(SUBLANES):
        h_ref[c * SCAN_CHUNK:(c + 1) * SCAN_CHUNK, ln] = o_ref[j, pl.ds(c * SCAN_PITCH, SCAN_CHUNK), :]


def _split_bf16(w):
    hi = w.astype(BF16)
    return jnp.stack([hi, (w - hi.astype(F32)).astype(BF16)])


def _dot_bf16x3(x, w_ref):
    hi = x.astype(BF16)
    lo = (x - hi.astype(F32)).astype(BF16)
    acc = jnp.dot(hi, w_ref[0], preferred_element_type=F32)
    acc = acc + jnp.dot(lo, w_ref[0], preferred_element_type=F32)
    return acc + jnp.dot(hi, w_ref[1], preferred_element_type=F32)


def _router_logits(h2, wr_ref, br_ref):
    return _dot_bf16x3(h2, wr_ref) + br_ref[...]


def _route_kernel(lg_ref, meta_ref, cnt_ref, carry_ref):
    step = pl.program_id(0)
    tm = lg_ref.shape[0]

    @pl.when(step == 0)
    def _():
        carry_ref[...] = jnp.zeros_like(carry_ref)

    logits = jnp.concatenate([lg_ref[i * LANES:(i + 1) * LANES, :].T for i in range(tm // LANES)],
                             axis=1)
    row8 = lax.broadcasted_iota(jnp.int32, (SUBLANES, tm), 0)
    grp_logits = jnp.where(row8 < N_GROUPS, logits[0:SUBLANES], NEG_INF)
    gmax = jnp.max(grp_logits, axis=0, keepdims=True)
    p_sel = 1.0 / jnp.sum(jnp.exp(grp_logits - gmax), axis=0, keepdims=True)
    grp = jnp.min(jnp.where(grp_logits == gmax, row8, SUBLANES), axis=0, keepdims=True)

    in_grp = logits[SUBLANES + (N_GROUPS - 1) * EXPERTS_PER_GROUP:SUBLANES + N_GROUPS * EXPERTS_PER_GROUP]
    for gi in range(N_GROUPS - 2, -1, -1):
        lo = SUBLANES + gi * EXPERTS_PER_GROUP
        in_grp = jnp.where(grp == gi, logits[lo:lo + EXPERTS_PER_GROUP], in_grp)
    v1 = jnp.max(in_grp, axis=0, keepdims=True)
    i1 = jnp.min(jnp.where(in_grp == v1, row8, EXPERTS_PER_GROUP), axis=0, keepdims=True)
    rest = jnp.where(row8 == i1, NEG_INF, in_grp)
    v2 = jnp.max(rest, axis=0, keepdims=True)
    i2 = jnp.min(jnp.where(rest == v2, row8, EXPERTS_PER_GROUP), axis=0, keepdims=True)
    e2 = jnp.exp(v2 - v1)
    w1 = p_sel / (1.0 + e2)
    w2 = p_sel * e2 / (1.0 + e2)
    eid = (grp * EXPERTS_PER_GROUP + i1, grp * EXPERTS_PER_GROUP + i2)

    rr = lax.broadcasted_iota(jnp.int32, (LANES, LANES), 0)
    cc = lax.broadcasted_iota(jnp.int32, (LANES, LANES), 1)
    strict_upper = jnp.where(rr < cc, 1.0, 0.0).astype(BF16)
    erow = lax.broadcasted_iota(jnp.int32, (N_EXPERTS, LANES), 0)
    base = carry_ref[:, 0:1]
    for k in range(2):
        for i in range(tm // LANES):
            ln = slice(i * LANES, (i + 1) * LANES)
            onehot = jnp.where(erow == eid[k][:, ln], 1.0, 0.0)
            pre = jnp.dot(onehot.astype(BF16), strict_upper, preferred_element_type=F32)
            meta_ref[2 + k:3 + k, ln] = jnp.sum(onehot * (base + pre), axis=0, keepdims=True)
            base = base + jnp.sum(onehot, axis=1, keepdims=True)
    carry_ref[...] = jnp.broadcast_to(base, carry_ref.shape)
    cnt_ref[...] = jnp.broadcast_to(base, cnt_ref.shape)

    meta_ref[0:1, :] = eid[0].astype(F32)
    meta_ref[1:2, :] = eid[1].astype(F32)
    meta_ref[4:5, :] = w1
    meta_ref[5:6, :] = w2
    meta_ref[6:8, :] = jnp.zeros((2, tm), F32)


def _router_weights(w_grp, b_grp, w_exp, b_exp):
    d = w_grp.shape[0]
    wr = jnp.zeros((d, LANES), F32).at[:, :N_GROUPS].set(w_grp)
    wr = wr.at[:, SUBLANES:SUBLANES + N_EXPERTS].set(w_exp)
    br = jnp.zeros((1, LANES), F32).at[0, :N_GROUPS].set(b_grp)
    br = br.at[0, SUBLANES:SUBLANES + N_EXPERTS].set(b_exp)
    return _split_bf16(wr), br


def _route(logits, tile):
    t = logits.shape[0]
    return pl.pallas_call(
        _route_kernel,
        grid=(t // tile,),
        in_specs=[pl.BlockSpec((tile, LANES), lambda i: (i, 0))],
        out_specs=[pl.BlockSpec((SUBLANES, tile), lambda i: (0, i)), _full((N_EXPERTS, LANES))],
        out_shape=[jax.ShapeDtypeStruct((SUBLANES, t), F32),
                   jax.ShapeDtypeStruct((N_EXPERTS, LANES), F32)],
        scratch_shapes=[pltpu.VMEM((N_EXPERTS, LANES), F32)],
        compiler_params=_params(),
        name="moe_route",
    )(logits)


def _back0_kernel(nt, x_ref, ctx_ref, u_ref, gate_ref, hf_ref, wcat_ref, ba_ref, bx_ref, lam_ref,
                  w_ref, gt_ref, g2_ref, sh_ref, sc_ref, wr_ref, br_ref, x1_ref, h2_ref, lg_ref,
                  hb_ref, a_ref, b_ref, o_ref, carry_ref):
    s = pl.program_id(0)
    is_ctx = s == 0

    @pl.when(s == 0)
    def _():
        carry_ref[...] = jnp.zeros_like(carry_ref)

    for j in range(RG_BLOCKS):
        _rglru_block(True, j, u_ref, wcat_ref, ba_ref, bx_ref, lam_ref, hb_ref,
                     a_ref, b_ref, o_ref, carry_ref)
    xin = jnp.where(is_ctx, ctx_ref[...], x_ref[...])
    y = _gelu_tanh(gate_ref[...]) * (hf_ref[...] + hb_ref[...])
    y = jnp.dot(y.astype(BF16), w_ref[...], preferred_element_type=F32)
    x1 = xin + _mod_row(gt_ref, is_ctx) * y
    x1_ref[...] = x1
    h2 = _rms(x1, g2_ref[...]) * (1.0 + _mod_row(sc_ref, is_ctx)) + _mod_row(sh_ref, is_ctx)
    _store_token_tiles(h2_ref, h2)
    lg_ref[...] = _router_logits(h2, wr_ref, br_ref)


_TOKEN_OUTS = lambda d: [pl.BlockSpec((TM, d), lambda i: (i, 0)),
                         pl.BlockSpec((TM * d // LANES, LANES), lambda i: (i, 0)),
                         pl.BlockSpec((TM, LANES), lambda i: (i, 0))]


def _token_out_shapes(t, d):
    return [jax.ShapeDtypeStruct((t, d), F32), jax.ShapeDtypeStruct((t * d // LANES, LANES), F32),
            jax.ShapeDtypeStruct((t, LANES), F32)]


def _back0(x, ctx, u, gate, hf, wcat, b_a, b_x, lam, w_out_bf, mod, g2, wr, br):
    n, d = x.shape
    nt = n // TM + 1
    t = nt * TM
    tile = lambda s: _scan_tile(True, s, nt)
    row = lambda v: v.reshape(1, -1)
    tok = pl.BlockSpec((TM, D_RNN), lambda s: (tile(s), 0))
    modspec = lambda c: pl.BlockSpec((SUBLANES, d), lambda s: (0, c))
    return pl.pallas_call(
        functools.partial(_back0_kernel, nt),
        grid=(nt,),
        in_specs=[pl.BlockSpec((TM, d), lambda s: (jnp.maximum(tile(s) - 1, 0), 0)),
                  _full((TM, d)),
                  tok, tok, tok,
                  _full((RG_BLOCKS, RG_BLOCK_W, 2 * RG_BLOCK_W)),
                  _full((1, D_RNN)), _full((1, D_RNN)), _full((1, D_RNN)),
                  _full((D_RNN, d)),
                  modspec(2), _full((1, d)), modspec(3), modspec(4),
                  _full(wr.shape), _full(br.shape)],
        out_specs=[pl.BlockSpec((TM, d), lambda s: (tile(s), 0)),
                   pl.BlockSpec((TM * d // LANES, LANES), lambda s: (tile(s), 0)),
                   pl.BlockSpec((TM, LANES), lambda s: (tile(s), 0))],
        out_shape=_token_out_shapes(t, d),
        scratch_shapes=[pltpu.VMEM((TM, D_RNN), F32)] + _SCAN_SCRATCH(),
        compiler_params=_params(),
        name="back0",
    )(x, ctx, u, gate, hf, wcat, row(b_a), row(b_x), row(lam), w_out_bf, mod, g2.reshape(1, -1),
      mod, mod, wr, br)


def _plan_kernel(n_tiles, nbp, cnt_ref, meta_ref, dest_ref, blk_ref):
    c = cnt_ref[...]
    padded = jnp.floor((c + (TB - 1)) * (1.0 / TB)) * TB
    r = lax.broadcasted_iota(jnp.int32, (N_EXPERTS, N_EXPERTS), 0)
    q = lax.broadcasted_iota(jnp.int32, (N_EXPERTS, N_EXPERTS), 1)
    lower = jnp.where(q <= r, 1.0, 0.0)
    pad_end = jnp.dot(lower, padded, precision=HI, preferred_element_type=F32)
    pad_start = pad_end - padded

    first_row = lax.broadcasted_iota(jnp.int32, (N_EXPERTS, nbp), 1).astype(F32) * TB
    owner = jnp.sum(jnp.where(pad_end[:, 0:1] <= first_row, 1.0, 0.0), axis=0, keepdims=True)
    blk_ref[0:1, :] = jnp.minimum(owner, N_EXPERTS - 1).astype(jnp.int32)
    n_used = pad_end[N_EXPERTS - 1:N_EXPERTS, 0:1] * (1.0 / TB)
    blk_ref[1:2, :] = jnp.broadcast_to(n_used, (1, nbp)).astype(jnp.int32)
    ends = jnp.concatenate([pad_end, jnp.zeros((LANES - N_EXPERTS, LANES), F32)], axis=0).T[0:1, :]
    blk_ref[2:3, :] = jnp.concatenate([ends, jnp.zeros((1, nbp - LANES), F32)], axis=1).astype(jnp.int32)
    blk_ref[3:SUBLANES, :] = jnp.zeros((SUBLANES - 3, nbp), jnp.int32)

    erow = lax.broadcasted_iota(jnp.int32, (N_EXPERTS, TM), 0).astype(F32)

    def body(i, carry):
        ln = pl.ds(pl.multiple_of(i * TM, TM), TM)
        for k in range(2):
            onehot = jnp.where(erow == meta_ref[k:k + 1, ln], 1.0, 0.0)
            start = jnp.sum(onehot * pad_start[:, 0:1], axis=0, keepdims=True)
            dest_ref[k:k + 1, ln] = (start + meta_ref[2 + k:3 + k, ln]).astype(jnp.int32)
        dest_ref[2:SUBLANES, ln] = jnp.zeros((SUBLANES - 2, TM), jnp.int32)
        return carry

    lax.fori_loop(0, n_tiles, body, 0)


def _plan(cnt, meta):
    t = meta.shape[1]
    n_blocks = (2 * t + N_EXPERTS * TB) // TB
    nbp = -(-n_blocks // LANES) * LANES
    vm = pl.BlockSpec(memory_space=pltpu.VMEM)
    dest, blk = pl.pallas_call(
        functools.partial(_plan_kernel, t // TM, nbp),
        in_specs=[vm, vm],
        out_specs=[vm, vm],
        out_shape=[jax.ShapeDtypeStruct((SUBLANES, t), jnp.int32),
                   jax.ShapeDtypeStruct((SUBLANES, nbp), jnp.int32)],
        compiler_params=pltpu.CompilerParams(vmem_limit_bytes=VMEM_LIMIT),
        name="moe_plan",
    )(cnt, meta)
    return dest, blk, n_blocks


def _row_wait(src_ref, dst_ref, d, sem):
    n = TM * d // LANES
    pltpu.make_async_copy(src_ref.at[pl.ds(0, n), :], dst_ref.at[pl.ds(0, n), :], sem).wait()


def _dispatch_kernel(n_blocks, d, pe_ref, nu_ref, dest_ref, h2_ref, buf_ref, dsm_ref, zero_ref,
                     sem_idx, sem_zero, sem_rows):
    step = pl.program_id(0)
    blk = TB * d // LANES

    def zero_block(first_row):
        start = pl.multiple_of(first_row * (d // LANES), blk)
        return pltpu.make_async_copy(zero_ref, buf_ref.at[pl.ds(start, blk), :], sem_zero)

    @pl.when(step == 0)
    def _():
        zero_ref[...] = jnp.zeros_like(zero_ref)

        def for_segments(fn):
            for e in range(N_EXPERTS):
                seg_start = pe_ref[e - 1] if e else 0

                @pl.when(pe_ref[e] > seg_start)
                def _():
                    fn(zero_block(pe_ref[e] - TB))

        for_segments(lambda cp: cp.start())
        lax.fori_loop(nu_ref[0], n_blocks, lambda b, c: (zero_block(b * TB).start(), c)[1], 0)
        for_segments(lambda cp: cp.wait())
        lax.fori_loop(nu_ref[0], n_blocks, lambda b, c: (zero_block(b * TB).wait(), c)[1], 0)

    cp = pltpu.make_async_copy(dest_ref, dsm_ref, sem_idx)
    cp.start()
    cp.wait()

    for r in range(TM):
        for k in range(2):
            pltpu.make_async_copy(_token_tile(h2_ref, r, d), _token_tile(buf_ref, dsm_ref[k, r], d),
                                  sem_rows).start(priority=k)
    for k in range(2):
        _row_wait(h2_ref, buf_ref, d, sem_rows)


def _dispatch(h2t, dest, pad_end, n_used, n_blocks, d):
    per = d // LANES
    t = h2t.shape[0] // per
    grid_spec = pltpu.PrefetchScalarGridSpec(
        num_scalar_prefetch=2,
        grid=(t // TM,),
        in_specs=[pl.BlockSpec((SUBLANES, TM), lambda i, pe, nu: (0, i)),
                  pl.BlockSpec((TM * per, LANES), lambda i, pe, nu: (i, 0))],
        out_specs=pl.BlockSpec(memory_space=pl.ANY),
        scratch_shapes=[pltpu.SMEM((SUBLANES, TM), jnp.int32),
                        pltpu.VMEM((TB * per, LANES), F32),
                        pltpu.SemaphoreType.DMA, pltpu.SemaphoreType.DMA,
                        pltpu.SemaphoreType.DMA],
    )
    return pl.pallas_call(
        functools.partial(_dispatch_kernel, n_blocks, d),
        grid_spec=grid_spec,
        out_shape=jax.ShapeDtypeStruct((n_blocks * TB * per, LANES), F32),
        compiler_params=_params(),
        name="moe_dispatch",
    )(pad_end, n_used, dest, h2t)


EXPERT_PAIR = 2
EXPERT_RING = 2 * EXPERT_PAIR


def _experts_kernel(be_ref, nu_ref, buf_ref, *refs):
    w_refs = refs[:3 * EXPERT_PAIR]
    y_ref, wg_bf, wu_bf, wd_bf, x_ref, cur_ref, sems = refs[3 * EXPERT_PAIR:]
    step = pl.program_id(0)
    n_used = nu_ref[0]
    d = wg_bf.shape[0]
    blk = TB * d // LANES
    first = step * EXPERT_PAIR

    @pl.when(step == 0)
    def _():
        cur_ref[0] = -1

    def block_copy(b):
        slot = b % EXPERT_RING
        return pltpu.make_async_copy(buf_ref.at[pl.ds(pl.multiple_of(b * blk, blk), blk), :],
                                     x_ref.at[slot], sems.at[slot])

    for h in range(EXPERT_PAIR):
        @pl.when(jnp.logical_and(step == 0, h < n_used))
        def _():
            block_copy(h).start()

        @pl.when(first + EXPERT_PAIR + h < n_used)
        def _():
            block_copy(first + EXPERT_PAIR + h).start()

    for h in range(EXPERT_PAIR):
        b = first + h
        used = b < n_used
        expert = be_ref[jnp.clip(b, 0, n_used - 1)]
        wg_ref, wu_ref, wd_ref = w_refs[3 * h:3 * h + 3]

        @pl.when(jnp.logical_and(used, expert != cur_ref[0]))
        def _():
            wg_bf[...] = wg_ref[0].astype(BF16)
            wu_bf[...] = wu_ref[0].astype(BF16)
            wd_bf[...] = wd_ref[0].astype(BF16)
            cur_ref[0] = expert

        @pl.when(used)
        def _():
            block_copy(b).wait()
            xb = _load_token_tiles(x_ref.at[b % EXPERT_RING], TB, d).astype(BF16)
            g = jnp.dot(xb, wg_bf[...], preferred_element_type=F32)
            u = jnp.dot(xb, wu_bf[...], preferred_element_type=F32)
            a = (g * _sigmoid(g)) * u
            _store_token_tiles(y_ref, jnp.dot(a.astype(BF16), wd_bf[...], preferred_element_type=F32),
                               first_row=h * TB)

        @pl.when(jnp.logical_not(used))
        def _():
            y_ref[pl.ds(h * blk, blk), :] = jnp.zeros((blk, LANES), F32)


def _experts(buf, blk_e, n_used, wg, wu, wd, n_blocks):
    d = wg.shape[1]
    blk = TB * d // LANES
    assert n_blocks % EXPERT_PAIR == 0
    last = lambda b, nu: jnp.clip(b, 0, nu[0] - 1)

    def weight_specs(h):
        pick = lambda i, be, nu: (be[last(i * EXPERT_PAIR + h, nu)], 0, 0)
        return [pl.BlockSpec((1, d, D_EXPERT), pick), pl.BlockSpec((1, d, D_EXPERT), pick),
                pl.BlockSpec((1, D_EXPERT, d), pick)]

    grid_spec = pltpu.PrefetchScalarGridSpec(
        num_scalar_prefetch=2,
        grid=(n_blocks // EXPERT_PAIR,),
        in_specs=[pl.BlockSpec(memory_space=pl.ANY)] + [s for h in range(EXPERT_PAIR)
                                                        for s in weight_specs(h)],
        out_specs=pl.BlockSpec((EXPERT_PAIR * blk, LANES), lambda i, be, nu: (i, 0)),
        scratch_shapes=[pltpu.VMEM((d, D_EXPERT), BF16), pltpu.VMEM((d, D_EXPERT), BF16),
                        pltpu.VMEM((D_EXPERT, d), BF16),
                        pltpu.VMEM((EXPERT_RING, blk, LANES), F32),
                        pltpu.SMEM((1,), jnp.int32),
                        pltpu.SemaphoreType.DMA((EXPERT_RING,))],
    )
    return pl.pallas_call(
        _experts_kernel,
        grid_spec=grid_spec,
        out_shape=jax.ShapeDtypeStruct(buf.shape, F32),
        compiler_params=_params(),
        name="moe_experts",
    )(blk_e, n_used, buf, *([wg, wu, wd] * EXPERT_PAIR))


def _gather_start(dest_ref, dsm_ref, ybuf_ref, rows_ref, slot, d, sem_idx, sem):
    cp = pltpu.make_async_copy(dest_ref, dsm_ref, sem_idx)
    cp.start()
    cp.wait()
    for r in range(TM):
        for k in range(2):
            pltpu.make_async_copy(_token_tile(ybuf_ref, dsm_ref[k, r], d),
                                  _token_tile(rows_ref.at[slot, k], r, d), sem).start(priority=k)


def _gathered_rows(dest_ref, next_ref, dsm_ref, ybuf_ref, rows_ref, d, sem_idx, sems):
    step = pl.program_id(0)
    nt = pl.num_programs(0)
    slot = step % 2

    @pl.when(step == 0)
    def _():
        _gather_start(dest_ref, dsm_ref, ybuf_ref, rows_ref, 0, d, sem_idx, sems.at[0])

    @pl.when(step + 1 < nt)
    def _():
        _gather_start(next_ref, dsm_ref, ybuf_ref, rows_ref, 1 - slot, d, sem_idx, sems.at[1 - slot])

    for k in range(2):
        _row_wait(ybuf_ref, rows_ref.at[slot, k], d, sems.at[slot])
    return [_load_token_tiles(rows_ref.at[slot, k], TM, d) for k in range(2)]


def _token_weights(meta_ref):
    meta = jnp.concatenate([meta_ref[...], jnp.zeros((LANES - SUBLANES, TM), F32)], axis=0)
    mt = meta.T
    return mt[:, 4:5], mt[:, 5:6]


_COMBINE_SCRATCH = lambda d: [pltpu.SMEM((SUBLANES, TM), jnp.int32),
                              pltpu.VMEM((2, 2, TM * d // LANES, LANES), F32),
                              pltpu.SemaphoreType.DMA, pltpu.SemaphoreType.DMA((2,))]


def _next_tile_spec(nt):
    return pl.BlockSpec((SUBLANES, TM), lambda i: (0, jnp.minimum(i + 1, nt - 1)))


def _combine0_kernel(x1_ref, dest_ref, next_ref, meta_ref, gt_ref, ybuf_ref, x2_ref, ctx2_ref,
                     dsm_ref, rows_ref, sem_idx, sems):
    is_ctx = pl.program_id(0) == 0
    y0, y1 = _gathered_rows(dest_ref, next_ref, dsm_ref, ybuf_ref, rows_ref, x1_ref.shape[1],
                            sem_idx, sems)
    w0, w1 = _token_weights(meta_ref)
    out = x1_ref[...] + _mod_row(gt_ref, is_ctx) * (w0 * y0 + w1 * y1)
    x2_ref[...] = out

    @pl.when(is_ctx)
    def _():
        ctx2_ref[...] = out


def _combine0(x1, dest, meta, mod, ybuf):
    t, d = x1.shape
    nt = t // TM
    return pl.pallas_call(
        _combine0_kernel,
        grid=(nt,),
        in_specs=[pl.BlockSpec((TM, d), lambda i: (i, 0)),
                  pl.BlockSpec((SUBLANES, TM), lambda i: (0, i)),
                  _next_tile_spec(nt),
                  pl.BlockSpec((SUBLANES, TM), lambda i: (0, i)),
                  pl.BlockSpec((SUBLANES, d), lambda i: (0, 5)),
                  pl.BlockSpec(memory_space=pl.ANY)],
        out_specs=[pl.BlockSpec((TM, d), lambda i: (jnp.maximum(i - 1, 0), 0)),
                   _full((TM, d))],
        out_shape=[jax.ShapeDtypeStruct((t - TM, d), F32), jax.ShapeDtypeStruct((TM, d), F32)],
        scratch_shapes=_COMBINE_SCRATCH(d),
        compiler_params=_params(),
        name="moe_combine0",
    )(x1, dest, dest, meta, mod, ybuf)


def _combine1_kernel(x1_ref, dest_ref, next_ref, meta_ref, gt_ref, gf_ref, ybuf_ref, o_ref,
                     dsm_ref, rows_ref, sem_idx, sems):
    y0, y1 = _gathered_rows(dest_ref, next_ref, dsm_ref, ybuf_ref, rows_ref, x1_ref.shape[1],
                            sem_idx, sems)
    w0, w1 = _token_weights(meta_ref)
    out = x1_ref[...] + gt_ref[0:1, :] * (w0 * y0 + w1 * y1)
    o_ref[...] = _rms(out, gf_ref[...])


def _combine1(x1, dest, meta, mod, gf, ybuf):
    t, d = x1.shape
    rows = t // GRID_W
    nt = t // TM
    out = pl.pallas_call(
        _combine1_kernel,
        grid=(nt,),
        in_specs=[pl.BlockSpec((TM, d), lambda i: (i, 0)),
                  pl.BlockSpec((SUBLANES, TM), lambda i: (0, i)),
                  _next_tile_spec(nt),
                  pl.BlockSpec((SUBLANES, TM), lambda i: (0, i)),
                  pl.BlockSpec((SUBLANES, d), lambda i: (0, 5)),
                  _full((1, d)),
                  pl.BlockSpec(memory_space=pl.ANY)],
        out_specs=pl.BlockSpec((TM, d), lambda i: (0, i)),
        out_shape=jax.ShapeDtypeStruct((rows, GRID_W * d), F32),
        scratch_shapes=_COMBINE_SCRATCH(d),
        compiler_params=_params(),
        name="moe_combine1",
    )(x1, dest, dest, meta, mod, gf.reshape(1, -1), ybuf)
    return out.reshape(t, d)


def _moe(h2t, meta, cnt, w_gate, w_up, w_down):
    dest, blk, n_blocks = _plan(cnt, meta)
    n_used = blk[1, :1]
    buf = _dispatch(h2t, dest, blk[2, :N_EXPERTS], n_used, n_blocks, w_gate.shape[1])
    ybuf = _experts(buf, blk[0], n_used, w_gate, w_up, w_down, n_blocks)
    return dest, ybuf


def _inproj1_kernel(x_ref, ctx_ref, g_ref, sh_ref, sc_ref, wq_ref, wkt_ref, wv_ref, wo_ref,
                    wg_ref, bg_ref, q_ref, kt_ref, v_ref, o_ref, gcol_ref):
    is_ctx = pl.program_id(0) == 0
    xin = jnp.where(is_ctx, ctx_ref[...], x_ref[...])
    h = _rms(xin, g_ref[...]) * (1.0 + _mod_row(sc_ref, is_ctx)) + _mod_row(sh_ref, is_ctx)
    hb = h.astype(BF16)
    q = jnp.dot(hb, wq_ref[...], preferred_element_type=F32) * (ML_DQK ** -0.5)
    q_ref[...] = q.astype(BF16)
    kt = lax.dot_general(wkt_ref[...], hb, (((1,), (1,)), ((), ())), preferred_element_type=F32)
    kt_ref[...] = kt.astype(BF16)
    v_ref[...] = jnp.dot(hb, wv_ref[...], preferred_element_type=F32).astype(BF16)
    o_ref[...] = jnp.dot(hb, wo_ref[...], preferred_element_type=F32).astype(BF16)
    gcol_ref[...] = _dot_bf16x3(h, wg_ref) + bg_ref[...]


def _inproj1(x, ctx, g1, mod, w_in, b_gates):
    n, d = x.shape
    rows = n // GRID_W
    nt = GRID_W + 1
    t = nt * TM
    ng = 4 * ML_HEADS
    wq = w_in[:, :ML_QK_W].astype(BF16)
    wkt = w_in[:, ML_QK_W:2 * ML_QK_W].T.astype(BF16)
    wv = w_in[:, 2 * ML_QK_W:2 * ML_QK_W + ML_V_W].astype(BF16)
    wo = w_in[:, 2 * ML_QK_W + ML_V_W:2 * ML_QK_W + ML_V_W + d].astype(BF16)
    wg = _split_bf16(jnp.zeros((d, LANES), F32).at[:, :ng].set(w_in[:, -ng:]))
    bg = jnp.zeros((1, LANES), F32).at[0, :ng].set(b_gates)
    tok = lambda w: pl.BlockSpec((TM, w), lambda i: (i, 0))
    return pl.pallas_call(
        _inproj1_kernel,
        grid=(nt,),
        in_specs=[pl.BlockSpec((rows, d), lambda i: (0, jnp.maximum(i - 1, 0))),
                  _full((TM, d)),
                  _full((1, d)),
                  pl.BlockSpec((SUBLANES, d), lambda i: (0, 0)),
                  pl.BlockSpec((SUBLANES, d), lambda i: (0, 1)),
                  _full(wq.shape), _full(wkt.shape), _full(wv.shape), _full(wo.shape),
                  _full(wg.shape), _full(bg.shape)],
        out_specs=[tok(ML_QK_W),
                   pl.BlockSpec((ML_QK_W, TM), lambda i: (0, i)),
                   tok(ML_V_W), tok(d), tok(LANES)],
        out_shape=[jax.ShapeDtypeStruct((t, ML_QK_W), BF16),
                   jax.ShapeDtypeStruct((ML_QK_W, t), BF16),
                   jax.ShapeDtypeStruct((t, ML_V_W), BF16),
                   jax.ShapeDtypeStruct((t, d), BF16),
                   jax.ShapeDtypeStruct((t, LANES), F32)],
        compiler_params=_params(),
        name="inproj1",
    )(x.reshape(rows, GRID_W * d), ctx, g1.reshape(1, -1), mod, mod, wq, wkt, wv, wo, wg, bg)


def _log_sigmoid(x):
    return jnp.minimum(x, 0.0) - jnp.log1p(jnp.exp(-jnp.abs(x)))


LOG2E = 1.4426950408889634


def _mlstm_kernel(reverse, q_ref, kt_ref, v_ref, gcol_ref, h_ref, c_ref, m_ref):
    @pl.when(pl.program_id(0) == 0)
    def _():
        c_ref[...] = jnp.zeros_like(c_ref)
        m_ref[...] = jnp.zeros_like(m_ref)

    L = TM
    half = L // 2
    gi = 2 * ML_HEADS if reverse else 0
    gf = gi + ML_HEADS
    end = 0 if reverse else L - 1
    rr = lax.broadcasted_iota(jnp.int32, (L, L), 0)
    cc = lax.broadcasted_iota(jnp.int32, (L, L), 1)
    tri = jnp.where((cc >= rr) if reverse else (cc <= rr), 1.0, 0.0).astype(BF16)
    rh = lax.broadcasted_iota(jnp.int32, (half, half), 0)
    ch = lax.broadcasted_iota(jnp.int32, (half, half), 1)
    diag = (ch >= rh) if reverse else (ch <= rh)

    gates = gcol_ref[...]
    lane = lax.broadcasted_iota(jnp.int32, (L, LANES), 1)
    mine = jnp.logical_and(lane >= gf, lane < gf + ML_HEADS)
    lf = jnp.where(mine, _log_sigmoid(gates), 0.0)
    p0 = lf.astype(BF16)
    r1 = lf - p0.astype(F32)
    p1 = r1.astype(BF16)
    p2 = (r1 - p1.astype(F32)).astype(BF16)
    cum = (jnp.dot(tri, p0, preferred_element_type=F32) + jnp.dot(tri, p1, preferred_element_type=F32)
           + jnp.dot(tri, p2, preferred_element_type=F32))

    r = jnp.where(mine, pltpu.roll(gates, ML_HEADS, 1) - cum, 0.0)
    row = lax.broadcasted_iota(jnp.int32, (L, LANES), 0)
    cm = r
    sh = 1
    while sh < L:
        if sh < SUBLANES:
            if reverse:
                cm = jnp.where(row < L - sh, jnp.maximum(cm, pltpu.roll(cm, L - sh, 0)), cm)
            else:
                cm = jnp.where(row >= sh, jnp.maximum(cm, pltpu.roll(cm, sh, 0)), cm)
        else:
            pad = jnp.full((sh, LANES), NEG_INF, F32)
            moved = (jnp.concatenate([cm[sh:], pad], axis=0) if reverse
                     else jnp.concatenate([pad, cm[:L - sh]], axis=0))
            cm = jnp.maximum(cm, moved)
        sh *= 2
    r8 = r.T[gf:gf + ML_HEADS]

    m_prev = m_ref[0:1, :]
    mm = jnp.maximum(m_prev, cm)
    m_t = cum + mm
    w_inter = jnp.exp(m_prev - mm)
    floor = jnp.exp(-m_t)
    mm2 = mm * LOG2E
    m_new = m_t[end:end + 1, :]
    shift2 = (cum[end:end + 1, :] - m_new) * LOG2E
    decay = jnp.exp(cum[end:end + 1, :] + m_prev - m_new)
    m_ref[0:1, :] = m_new

    ones_col = jnp.where(lax.broadcasted_iota(jnp.int32, (L, ML_DV), 1) == 0, 1.0, 0.0).astype(BF16)
    top, bot = slice(0, half), slice(half, L)
    dot = functools.partial(jnp.dot, preferred_element_type=F32)

    for hd in range(ML_HEADS):
        ln = gf + hd
        r2_row = r8[hd:hd + 1, :] * LOG2E

        def weights(tq, ks, masked):
            w = jnp.exp2(r2_row[:, ks] - mm2[tq, ln:ln + 1])
            return jnp.where(diag, w, 0.0) if masked else w

        qh = q_ref[:, hd * ML_DQK:(hd + 1) * ML_DQK]
        kth = kt_ref[hd * ML_DQK:(hd + 1) * ML_DQK, :]
        vext = jnp.concatenate([v_ref[:, hd * ML_DV:(hd + 1) * ML_DV], ones_col], axis=1)
        if reverse:
            s_top = dot(qh[top], kth) * jnp.concatenate([weights(top, top, True),
                                                         weights(top, bot, False)], axis=1)
            s_bot = dot(qh[bot], kth[:, bot]) * weights(bot, bot, True)
            intra = jnp.concatenate([dot(s_top.astype(BF16), vext),
                                     dot(s_bot.astype(BF16), vext[bot])], axis=0)
        else:
            s_top = dot(qh[top], kth[:, top]) * weights(top, top, True)
            s_bot = dot(qh[bot], kth) * jnp.concatenate([weights(bot, top, False),
                                                         weights(bot, bot, True)], axis=1)
            intra = jnp.concatenate([dot(s_top.astype(BF16), vext[top]),
                                     dot(s_bot.astype(BF16), vext)], axis=0)
        state = c_ref[hd]
        tot = w_inter[:, ln:ln + 1] * dot(qh, state.astype(BF16)) + intra
        den = tot[:, ML_DV:ML_DV + 1]
        h_ref[:, hd * ML_DV:(hd + 1) * ML_DV] = (
            tot[:, :ML_DV] / jnp.maximum(jnp.abs(den), floor[:, ln:ln + 1])).astype(h_ref.dtype)

        w_state = jnp.exp2(r2_row + shift2[:, ln:ln + 1])
        kw = (kth.astype(F32) * w_state).astype(BF16)
        c_ref[hd] = decay[:, ln:ln + 1] * state + dot(kw, vext)


def _mlstm(q, kt, v, gcol, reverse):
    t = q.shape[0]
    nt = t // TM
    tile = lambda s: _scan_tile(reverse, s, nt)
    return pl.pallas_call(
        functools.partial(_mlstm_kernel, reverse),
        grid=(nt,),
        in_specs=[pl.BlockSpec((TM, ML_QK_W), lambda s: (tile(s), 0)),
                  pl.BlockSpec((ML_QK_W, TM), lambda s: (0, tile(s))),
                  pl.BlockSpec((TM, ML_V_W), lambda s: (tile(s), 0)),
                  pl.BlockSpec((TM, LANES), lambda s: (tile(s), 0))],
        out_specs=pl.BlockSpec((TM, ML_V_W), lambda s: (tile(s), 0)),
        out_shape=jax.ShapeDtypeStruct((t, ML_V_W), BF16),
        scratch_shapes=[pltpu.VMEM((ML_HEADS, ML_DQK, 2 * ML_DV), F32),
                        pltpu.VMEM((SUBLANES, LANES), F32)],
        compiler_params=_params(),
        name="mlstm_bwd" if reverse else "mlstm_fwd",
    )(q, kt, v, gcol)


def _finish1_kernel(x_ref, hf_ref, hb_ref, o_ref, ng_ref, w_ref, gt_ref, g2_ref, sh_ref, sc_ref,
                    wr_ref, br_ref, x1_ref, h2_ref, lg_ref):
    hs = hf_ref[...].astype(F32) + hb_ref[...].astype(F32)
    parts = []
    for hd in range(ML_HEADS):
        blk = hs[:, hd * ML_DV:(hd + 1) * ML_DV]
        parts.append(blk * lax.rsqrt(jnp.mean(blk * blk, axis=-1, keepdims=True) + EPS))
    hn = jnp.concatenate(parts, axis=1) * ng_ref[...]
    y = jnp.dot((hn * _sigmoid(o_ref[...].astype(F32))).astype(BF16), w_ref[...],
                preferred_element_type=F32)
    x1 = x_ref[...] + gt_ref[0:1, :] * y
    x1_ref[...] = x1
    h2 = _rms(x1, g2_ref[...]) * (1.0 + sc_ref[0:1, :]) + sh_ref[0:1, :]
    _store_token_tiles(h2_ref, h2)
    lg_ref[...] = _router_logits(h2, wr_ref, br_ref)


def _finish1(x, hf, hb, o, norm_g, w_out_bf, mod, g2, wr, br):
    n, d = x.shape
    rows = n // GRID_W
    lat = lambda w: pl.BlockSpec((TM, w), lambda i: (i + 1, 0))
    modspec = lambda c: pl.BlockSpec((SUBLANES, d), lambda i: (0, c))
    return pl.pallas_call(
        _finish1_kernel,
        grid=(GRID_W,),
        in_specs=[pl.BlockSpec((rows, d), lambda i: (0, i)),
                  lat(ML_V_W), lat(ML_V_W), lat(d),
                  _full((1, ML_V_W)), _full((ML_V_W, d)),
                  modspec(2), _full((1, d)), modspec(3), modspec(4),
                  _full(wr.shape), _full(br.shape)],
        out_specs=_TOKEN_OUTS(d),
        out_shape=_token_out_shapes(n, d),
        compiler_params=_params(),
        name="finish1",
    )(x.reshape(rows, GRID_W * d), hf, hb, o, norm_g.reshape(1, -1), w_out_bf, mod,
      g2.reshape(1, -1), mod, mod, wr, br)


def kernel(x, c, ctx, c_ctx,
           l0_ada_w, l0_ada_b, l0_norm1_g, l0_norm2_g,
           l0_rg_w_in, l0_rg_conv_w, l0_rg_conv_b, l0_rg_w_a, l0_rg_b_a, l0_rg_w_x, l0_rg_b_x,
           l0_rg_lambda, l0_rg_w_out,
           l0_moe_w_grp, l0_moe_b_grp, l0_moe_w_exp, l0_moe_b_exp, l0_moe_w_gate, l0_moe_w_up,
           l0_moe_w_down,
           l1_ada_w, l1_ada_b, l1_norm1_g, l1_norm2_g,
           l1_ml_w_in, l1_ml_b_gates, l1_ml_norm_g, l1_ml_w_out,
           l1_moe_w_grp, l1_moe_b_grp, l1_moe_w_exp, l1_moe_b_exp, l1_moe_w_gate, l1_moe_w_up,
           l1_moe_w_down,
           final_norm_g):
    assert x.shape[0] == 1 and ctx.shape[1] == TM and x.shape[1] == GRID_W * TM
    xs, cs = x[0], ctx[0]
    d = xs.shape[1]
    cond8 = jnp.zeros((SUBLANES, d), F32).at[0].set(c[0]).at[1].set(c_ctx)

    mod0 = _adaln(cond8, l0_ada_w, l0_ada_b)
    conv_w8 = jnp.zeros((SUBLANES, D_RNN), F32).at[:CONV_W].set(l0_rg_conv_w)
    wcat = [jnp.concatenate([l0_rg_w_a[dr], l0_rg_w_x[dr]], axis=-1).astype(BF16) for dr in range(2)]
    gate, u, hf0 = _front0(xs, cs, l0_norm1_g, mod0, l0_rg_w_in.astype(BF16), conv_w8, l0_rg_conv_b,
                           wcat[0], l0_rg_b_a[0], l0_rg_b_x[0], l0_rg_lambda[0])
    wr0, br0 = _router_weights(l0_moe_w_grp, l0_moe_b_grp, l0_moe_w_exp, l0_moe_b_exp)
    x1, h2, logits = _back0(xs, cs, u, gate, hf0, wcat[1], l0_rg_b_a[1], l0_rg_b_x[1], l0_rg_lambda[1],
                            l0_rg_w_out.astype(BF16), mod0, l0_norm2_g, wr0, br0)
    meta, cnt = _route(logits, ROUTE_TILE_0)
    dest, ybuf = _moe(h2, meta, cnt, l0_moe_w_gate, l0_moe_w_up, l0_moe_w_down)
    x2, ctx2 = _combine0(x1, dest, meta, mod0, ybuf)

    mod1 = _adaln(cond8, l1_ada_w, l1_ada_b)
    q, kt, v, o, gcol = _inproj1(x2, ctx2, l1_norm1_g, mod1, l1_ml_w_in, l1_ml_b_gates)
    hf = _mlstm(q, kt, v, gcol, False)
    hb = _mlstm(q, kt, v, gcol, True)
    wr1, br1 = _router_weights(l1_moe_w_grp, l1_moe_b_grp, l1_moe_w_exp, l1_moe_b_exp)
    x1, h2, logits = _finish1(x2, hf, hb, o, l1_ml_norm_g, l1_ml_w_out.astype(BF16), mod1,
                              l1_norm2_g, wr1, br1)
    meta, cnt = _route(logits, ROUTE_TILE_1)
    dest, ybuf = _moe(h2, meta, cnt, l1_moe_w_gate, l1_moe_w_up, l1_moe_w_down)
    out = _combine1(x1, dest, meta, mod1, final_norm_g, ybuf)
    return out[None]
```

```python
import functools

import jax
import jax.numpy as jnp
from jax import lax
from jax.experimental import pallas as pl
from jax.experimental.pallas import tpu as pltpu

D_MODEL = 1024
GRID_W = 64
N_MOD = 6
EPS = 1e-6

D_RNN = 1280
RG_BLOCKS = 10
RG_BLOCK_W = D_RNN // RG_BLOCKS
CONV_W = 4
CONV_PAD_L = 2
RG_C = 8.0

ML_HEADS = 8
ML_DQK = D_MODEL // (2 * ML_HEADS)
ML_DV = D_MODEL // ML_HEADS
ML_QK_W = ML_HEADS * ML_DQK
ML_V_W = ML_HEADS * ML_DV

N_GROUPS = 4
EXPERTS_PER_GROUP = 8
N_EXPERTS = N_GROUPS * EXPERTS_PER_GROUP
D_EXPERT = 512

TM = 256
TB = 256
ROUTE_TILE_0 = 1280
ROUTE_TILE_1 = 1024
SUBLANES = 8
LANES = 128
VMEM_LIMIT = 48 * 1024 * 1024

F32 = jnp.float32
BF16 = jnp.bfloat16
HI = lax.Precision.HIGHEST
NEG_INF = float("-inf")


def _params(n_axes=1):
    return pltpu.CompilerParams(dimension_semantics=("arbitrary",) * n_axes,
                                vmem_limit_bytes=VMEM_LIMIT)


def _rms(x, g):
    return x * lax.rsqrt(jnp.mean(x * x, axis=-1, keepdims=True) + EPS) * g


def _sigmoid(x):
    return 1.0 / (1.0 + jnp.exp(-x))


def _softplus(x):
    return jnp.maximum(x, 0.0) + jnp.log1p(jnp.exp(-jnp.abs(x)))


def _gelu_tanh(x):
    return 0.5 * x * (1.0 + jnp.tanh(0.7978845608028654 * (x + 0.044715 * (x * x * x))))


def _full(shape):
    return pl.BlockSpec(shape, lambda *_: (0,) * len(shape))


def _store_token_tiles(ref, x, first_row=0):
    per = x.shape[1] // LANES
    for c in range(per):
        ref[pl.ds(first_row * per + c, x.shape[0], stride=per), :] = x[:, c * LANES:(c + 1) * LANES]


def _load_token_tiles(ref, rows, d):
    per = d // LANES
    return jnp.concatenate([ref[pl.ds(c, rows, stride=per), :] for c in range(per)], axis=1)


def _token_tile(ref, row, d):
    per = d // LANES
    start = row * per if isinstance(row, int) else pl.multiple_of(row * per, per)
    return ref.at[pl.ds(start, per), :]


def _adaln_kernel(cond_ref, w_ref, b_ref, o_ref):
    c = cond_ref[...]
    s = c * _sigmoid(c)
    o_ref[...] = jnp.dot(s, w_ref[...], precision=HI, preferred_element_type=F32) + b_ref[...]


def _adaln(cond8, w, b):
    d = w.shape[0]
    return pl.pallas_call(
        _adaln_kernel,
        grid=(N_MOD,),
        in_specs=[_full((SUBLANES, d)),
                  pl.BlockSpec((d, d), lambda j: (0, j)),
                  pl.BlockSpec((1, d), lambda j: (0, j))],
        out_specs=pl.BlockSpec((SUBLANES, d), lambda j: (0, j)),
        out_shape=jax.ShapeDtypeStruct((SUBLANES, N_MOD * d), F32),
        compiler_params=_params(),
        name="adaln",
    )(cond8, w, b.reshape(1, -1))


def _mod_row(ref, is_ctx):
    return jnp.where(is_ctx, ref[1:2, :], ref[0:1, :])


def _front0_kernel(nt, x_ref, ctx_ref, g_ref, sh_ref, sc_ref, w_ref, cw_ref, cb_ref,
                   wcat_ref, ba_ref, bx_ref, lam_ref, gate_ref, u_ref, hf_ref,
                   ext_ref, ubuf_ref, a_ref, b_ref, o_ref, carry_ref):
    s = pl.program_id(0)
    is_ctx = s == 0

    @pl.when(s == 0)
    def _():
        ext_ref[...] = jnp.zeros_like(ext_ref)
        ubuf_ref[...] = jnp.zeros_like(ubuf_ref)

    @pl.when(s <= 2)
    def _():
        carry_ref[...] = jnp.zeros_like(carry_ref)

    xin = jnp.where(is_ctx, ctx_ref[...], x_ref[...])
    h = _rms(xin, g_ref[...]) * (1.0 + _mod_row(sc_ref, is_ctx)) + _mod_row(sh_ref, is_ctx)
    hb = h.astype(BF16)

    right_valid = jnp.logical_and(s >= 2, s <= nt - 1)
    base = SUBLANES - CONV_PAD_L
    for j in range(RG_BLOCKS):
        ln = slice(j * RG_BLOCK_W, (j + 1) * RG_BLOCK_W)
        _rglru_block(False, j, ubuf_ref.at[s % 2], wcat_ref, ba_ref, bx_ref, lam_ref, hf_ref,
                     a_ref, b_ref, o_ref, carry_ref)
        p = jnp.dot(hb, w_ref[j], preferred_element_type=F32)
        gate_ref[:, ln] = p[:, :RG_BLOCK_W]
        rec = p[:, RG_BLOCK_W:]
        ext_ref[j, SUBLANES + TM:, :] = jnp.where(right_valid, rec[0:SUBLANES], 0.0)
        u = cb_ref[:, ln] + ext_ref[j, pl.ds(base, TM), :] * cw_ref[0:1, ln]
        for k in range(1, CONV_W):
            u = u + ext_ref[j, pl.ds(base + k, TM), :] * cw_ref[k:k + 1, ln]
        u_ref[:, ln] = u
        ubuf_ref[(s + 1) % 2, :, ln] = u
        ext_ref[j, 0:SUBLANES, :] = jnp.where(s >= 2, ext_ref[j, TM:TM + SUBLANES, :], 0.0)
        ext_ref[j, SUBLANES:SUBLANES + TM, :] = rec


_SCAN_SLAB = lambda: pltpu.VMEM((RG_BLOCKS, SUBLANES * SCAN_PITCH, LANES), F32)
_SCAN_SCRATCH = lambda: [_SCAN_SLAB(), _SCAN_SLAB(), _SCAN_SLAB(), pltpu.VMEM((SUBLANES, D_RNN), F32)]


def _front0(x, ctx, g1, mod, w_in_bf, conv_w8, conv_b, wcat, b_a, b_x, lam):
    n, d = x.shape
    nx = n // TM
    nt = nx + 1
    t = nt * TM
    row = lambda v: v.reshape(1, -1)
    w_blocks = jnp.concatenate([w_in_bf[:, :D_RNN].reshape(d, RG_BLOCKS, RG_BLOCK_W),
                                w_in_bf[:, D_RNN:].reshape(d, RG_BLOCKS, RG_BLOCK_W)], axis=2)
    w_blocks = w_blocks.transpose(1, 0, 2)
    return pl.pallas_call(
        functools.partial(_front0_kernel, nt),
        grid=(nt + 2,),
        in_specs=[pl.BlockSpec((TM, d), lambda s: (jnp.clip(s - 1, 0, nx - 1), 0)),
                  _full((TM, d)),
                  _full((1, d)),
                  pl.BlockSpec((SUBLANES, d), lambda s: (0, 0)),
                  pl.BlockSpec((SUBLANES, d), lambda s: (0, 1)),
                  _full((RG_BLOCKS, d, 2 * RG_BLOCK_W)),
                  _full((SUBLANES, D_RNN)), _full((1, D_RNN)),
                  _full((RG_BLOCKS, RG_BLOCK_W, 2 * RG_BLOCK_W)),
                  _full((1, D_RNN)), _full((1, D_RNN)), _full((1, D_RNN))],
        out_specs=[pl.BlockSpec((TM, D_RNN), lambda s: (jnp.minimum(s, nt - 1), 0)),
                   pl.BlockSpec((TM, D_RNN), lambda s: (jnp.maximum(s - 1, 0), 0)),
                   pl.BlockSpec((TM, D_RNN), lambda s: (jnp.maximum(s - 2, 0), 0))],
        out_shape=[jax.ShapeDtypeStruct((t, D_RNN), F32),
                   jax.ShapeDtypeStruct((t + TM, D_RNN), F32),
                   jax.ShapeDtypeStruct((t, D_RNN), F32)],
        scratch_shapes=[pltpu.VMEM((RG_BLOCKS, TM + 2 * SUBLANES, LANES), F32),
                        pltpu.VMEM((2, TM, D_RNN), F32)] + _SCAN_SCRATCH(),
        compiler_params=_params(),
        name="front0",
    )(x, ctx, g1.reshape(1, -1), mod, mod, w_blocks, conv_w8, row(conv_b), wcat, row(b_a), row(b_x),
      row(lam))


def _scan_tile(reverse, s, nt):
    if not reverse:
        return s
    return jnp.where(s == 0, 0, nt - s)


SCAN_CHUNK = TM // SUBLANES
SCAN_PITCH = SCAN_CHUNK + 4


def _rglru_block(reverse, j, u_ref, wcat_ref, ba_ref, bx_ref, lam_ref, h_ref,
                 a_ref, b_ref, o_ref, carry_ref):
    steps = range(SCAN_CHUNK - 1, -1, -1) if reverse else range(SCAN_CHUNK)
    chunks = range(SUBLANES - 1, -1, -1) if reverse else range(SUBLANES)
    ln = slice(j * RG_BLOCK_W, (j + 1) * RG_BLOCK_W)
    u = u_ref[:, ln]
    g = jnp.dot(u.astype(BF16), wcat_ref[j], preferred_element_type=F32)
    half_rate = (-0.5 * RG_C) * _softplus(-lam_ref[:, ln])
    log_a = jnp.tanh(0.5 * (g[:, :RG_BLOCK_W] + ba_ref[:, ln])) * half_rate + half_rate
    ig = 0.5 * jnp.tanh(0.5 * (g[:, RG_BLOCK_W:] + bx_ref[:, ln])) + 0.5
    a = jnp.exp(log_a)
    b = jnp.sqrt(-jnp.tanh(log_a) * (a * a + 1.0)) * ig * u
    for c in range(SUBLANES):
        a_ref[j, pl.ds(c * SCAN_PITCH, SCAN_CHUNK), :] = a[c * SCAN_CHUNK:(c + 1) * SCAN_CHUNK]
        b_ref[j, pl.ds(c * SCAN_PITCH, SCAN_CHUNK), :] = b[c * SCAN_CHUNK:(c + 1) * SCAN_CHUNK]

    row = lambda ref, i: ref[j, pl.ds(i, SUBLANES, stride=SCAN_PITCH), :]
    end = jnp.zeros((SUBLANES, LANES), F32)
    decay = jnp.ones((SUBLANES, LANES), F32)
    for i in steps:
        ai = row(a_ref, i)
        end = ai * end + row(b_ref, i)
        decay = decay * ai

    state = carry_ref[0:1, ln]
    entry = [None] * SUBLANES
    for c in chunks:
        entry[c] = state
        state = decay[c:c + 1] * state + end[c:c + 1]
    carry_ref[0:1, ln] = state

    hcur = jnp.concatenate(entry, axis=0)
    for i in steps:
        hcur = row(a_ref, i) * hcur + row(b_ref, i)
        o_ref[j, pl.ds(i, SUBLANES, stride=SCAN_PITCH), :] = hcur
    for c in range(SUBLANES):
        h_ref[c * SCAN_CHUNK:(c + 1) * SCAN_CHUNK, ln] = o_ref[j, pl.ds(c * SCAN_PITCH, SCAN_CHUNK), :]


def _split_bf16(w):
    hi = w.astype(BF16)
    return jnp.stack([hi, (w - hi.astype(F32)).astype(BF16)])


def _dot_bf16x3(x, w_ref):
    hi = x.astype(BF16)
    lo = (x - hi.astype(F32)).astype(BF16)
    acc = jnp.dot(hi, w_ref[0], preferred_element_type=F32)
    acc = acc + jnp.dot(lo, w_ref[0], preferred_element_type=F32)
    return acc + jnp.dot(hi, w_ref[1], preferred_element_type=F32)


def _router_logits(h2, wr_ref, br_ref):
    return _dot_bf16x3(h2, wr_ref) + br_ref[...]


def _route_kernel(lg_ref, meta_ref, cnt_ref, carry_ref):
    step = pl.program_id(0)
    tm = lg_ref.shape[0]

    @pl.when(step == 0)
    def _():
        carry_ref[...] = jnp.zeros_like(carry_ref)

    logits = jnp.concatenate([lg_ref[i * LANES:(i + 1) * LANES, :].T for i in range(tm // LANES)],
                             axis=1)
    row8 = lax.broadcasted_iota(jnp.int32, (SUBLANES, tm), 0)
    grp_logits = jnp.where(row8 < N_GROUPS, logits[0:SUBLANES], NEG_INF)
    gmax = jnp.max(grp_logits, axis=0, keepdims=True)
    p_sel = 1.0 / jnp.sum(jnp.exp(grp_logits - gmax), axis=0, keepdims=True)
    grp = jnp.min(jnp.where(grp_logits == gmax, row8, SUBLANES), axis=0, keepdims=True)

    in_grp = logits[SUBLANES + (N_GROUPS - 1) * EXPERTS_PER_GROUP:SUBLANES + N_GROUPS * EXPERTS_PER_GROUP]
    for gi in range(N_GROUPS - 2, -1, -1):
        lo = SUBLANES + gi * EXPERTS_PER_GROUP
        in_grp = jnp.where(grp == gi, logits[lo:lo + EXPERTS_PER_GROUP], in_grp)
    v1 = jnp.max(in_grp, axis=0, keepdims=True)
    i1 = jnp.min(jnp.where(in_grp == v1, row8, EXPERTS_PER_GROUP), axis=0, keepdims=True)
    rest = jnp.where(row8 == i1, NEG_INF, in_grp)
    v2 = jnp.max(rest, axis=0, keepdims=True)
    i2 = jnp.min(jnp.where(rest == v2, row8, EXPERTS_PER_GROUP), axis=0, keepdims=True)
    e2 = jnp.exp(v2 - v1)
    w1 = p_sel / (1.0 + e2)
    w2 = p_sel * e2 / (1.0 + e2)
    eid = (grp * EXPERTS_PER_GROUP + i1, grp * EXPERTS_PER_GROUP + i2)

    rr = lax.broadcasted_iota(jnp.int32, (LANES, LANES), 0)
    cc = lax.broadcasted_iota(jnp.int32, (LANES, LANES), 1)
    strict_upper = jnp.where(rr < cc, 1.0, 0.0).astype(BF16)
    erow = lax.broadcasted_iota(jnp.int32, (N_EXPERTS, LANES), 0)
    base = carry_ref[:, 0:1]
    for k in range(2):
        for i in range(tm // LANES):
            ln = slice(i * LANES, (i + 1) * LANES)
            onehot = jnp.where(erow == eid[k][:, ln], 1.0, 0.0)
            pre = jnp.dot(onehot.astype(BF16), strict_upper, preferred_element_type=F32)
            meta_ref[2 + k:3 + k, ln] = jnp.sum(onehot * (base + pre), axis=0, keepdims=True)
            base = base + jnp.sum(onehot, axis=1, keepdims=True)
    carry_ref[...] = jnp.broadcast_to(base, carry_ref.shape)
    cnt_ref[...] = jnp.broadcast_to(base, cnt_ref.shape)

    meta_ref[0:1, :] = eid[0].astype(F32)
    meta_ref[1:2, :] = eid[1].astype(F32)
    meta_ref[4:5, :] = w1
    meta_ref[5:6, :] = w2
    meta_ref[6:8, :] = jnp.zeros((2, tm), F32)


def _router_weights(w_grp, b_grp, w_exp, b_exp):
    d = w_grp.shape[0]
    wr = jnp.zeros((d, LANES), F32).at[:, :N_GROUPS].set(w_grp)
    wr = wr.at[:, SUBLANES:SUBLANES + N_EXPERTS].set(w_exp)
    br = jnp.zeros((1, LANES), F32).at[0, :N_GROUPS].set(b_grp)
    br = br.at[0, SUBLANES:SUBLANES + N_EXPERTS].set(b_exp)
    return _split_bf16(wr), br


def _route(logits, tile):
    t = logits.shape[0]
    return pl.pallas_call(
        _route_kernel,
        grid=(t // tile,),
        in_specs=[pl.BlockSpec((tile, LANES), lambda i: (i, 0))],
        out_specs=[pl.BlockSpec((SUBLANES, tile), lambda i: (0, i)), _full((N_EXPERTS, LANES))],
        out_shape=[jax.ShapeDtypeStruct((SUBLANES, t), F32),
                   jax.ShapeDtypeStruct((N_EXPERTS, LANES), F32)],
        scratch_shapes=[pltpu.VMEM((N_EXPERTS, LANES), F32)],
        compiler_params=_params(),
        name="moe_route",
    )(logits)


def _back0_kernel(nt, x_ref, ctx_ref, u_ref, gate_ref, hf_ref, wcat_ref, ba_ref, bx_ref, lam_ref,
                  w_ref, gt_ref, g2_ref, sh_ref, sc_ref, wr_ref, br_ref, x1_ref, h2_ref, lg_ref,
                  hb_ref, a_ref, b_ref, o_ref, carry_ref):
    s = pl.program_id(0)
    is_ctx = s == 0

    @pl.when(s == 0)
    def _():
        carry_ref[...] = jnp.zeros_like(carry_ref)

    for j in range(RG_BLOCKS):
        _rglru_block(True, j, u_ref, wcat_ref, ba_ref, bx_ref, lam_ref, hb_ref,
                     a_ref, b_ref, o_ref, carry_ref)
    xin = jnp.where(is_ctx, ctx_ref[...], x_ref[...])
    y = _gelu_tanh(gate_ref[...]) * (hf_ref[...] + hb_ref[...])
    y = jnp.dot(y.astype(BF16), w_ref[...], preferred_element_type=F32)
    x1 = xin + _mod_row(gt_ref, is_ctx) * y
    x1_ref[...] = x1
    h2 = _rms(x1, g2_ref[...]) * (1.0 + _mod_row(sc_ref, is_ctx)) + _mod_row(sh_ref, is_ctx)
    _store_token_tiles(h2_ref, h2)
    lg_ref[...] = _router_logits(h2, wr_ref, br_ref)


_TOKEN_OUTS = lambda d: [pl.BlockSpec((TM, d), lambda i: (i, 0)),
                         pl.BlockSpec((TM * d // LANES, LANES), lambda i: (i, 0)),
                         pl.BlockSpec((TM, LANES), lambda i: (i, 0))]


def _token_out_shapes(t, d):
    return [jax.ShapeDtypeStruct((t, d), F32), jax.ShapeDtypeStruct((t * d // LANES, LANES), F32),
            jax.ShapeDtypeStruct((t, LANES), F32)]


def _back0(x, ctx, u, gate, hf, wcat, b_a, b_x, lam, w_out_bf, mod, g2, wr, br):
    n, d = x.shape
    nt = n // TM + 1
    t = nt * TM
    tile = lambda s: _scan_tile(True, s, nt)
    row = lambda v: v.reshape(1, -1)
    tok = pl.BlockSpec((TM, D_RNN), lambda s: (tile(s), 0))
    modspec = lambda c: pl.BlockSpec((SUBLANES, d), lambda s: (0, c))
    return pl.pallas_call(
        functools.partial(_back0_kernel, nt),
        grid=(nt,),
        in_specs=[pl.BlockSpec((TM, d), lambda s: (jnp.maximum(tile(s) - 1, 0), 0)),
                  _full((TM, d)),
                  tok, tok, tok,
                  _full((RG_BLOCKS, RG_BLOCK_W, 2 * RG_BLOCK_W)),
                  _full((1, D_RNN)), _full((1, D_RNN)), _full((1, D_RNN)),
                  _full((D_RNN, d)),
                  modspec(2), _full((1, d)), modspec(3), modspec(4),
                  _full(wr.shape), _full(br.shape)],
        out_specs=[pl.BlockSpec((TM, d), lambda s: (tile(s), 0)),
                   pl.BlockSpec((TM * d // LANES, LANES), lambda s: (tile(s), 0)),
                   pl.BlockSpec((TM, LANES), lambda s: (tile(s), 0))],
        out_shape=_token_out_shapes(t, d),
        scratch_shapes=[pltpu.VMEM((TM, D_RNN), F32)] + _SCAN_SCRATCH(),
        compiler_params=_params(),
        name="back0",
    )(x, ctx, u, gate, hf, wcat, row(b_a), row(b_x), row(lam), w_out_bf, mod, g2.reshape(1, -1),
      mod, mod, wr, br)


def _plan_kernel(n_tiles, nbp, cnt_ref, meta_ref, dest_ref, blk_ref):
    c = cnt_ref[...]
    padded = jnp.floor((c + (TB - 1)) * (1.0 / TB)) * TB
    r = lax.broadcasted_iota(jnp.int32, (N_EXPERTS, N_EXPERTS), 0)
    q = lax.broadcasted_iota(jnp.int32, (N_EXPERTS, N_EXPERTS), 1)
    lower = jnp.where(q <= r, 1.0, 0.0)
    pad_end = jnp.dot(lower, padded, precision=HI, preferred_element_type=F32)
    pad_start = pad_end - padded

    first_row = lax.broadcasted_iota(jnp.int32, (N_EXPERTS, nbp), 1).astype(F32) * TB
    owner = jnp.sum(jnp.where(pad_end[:, 0:1] <= first_row, 1.0, 0.0), axis=0, keepdims=True)
    blk_ref[0:1, :] = jnp.minimum(owner, N_EXPERTS - 1).astype(jnp.int32)
    n_used = pad_end[N_EXPERTS - 1:N_EXPERTS, 0:1] * (1.0 / TB)
    blk_ref[1:2, :] = jnp.broadcast_to(n_used, (1, nbp)).astype(jnp.int32)
    ends = jnp.concatenate([pad_end, jnp.zeros((LANES - N_EXPERTS, LANES), F32)], axis=0).T[0:1, :]
    blk_ref[2:3, :] = jnp.concatenate([ends, jnp.zeros((1, nbp - LANES), F32)], axis=1).astype(jnp.int32)
    blk_ref[3:SUBLANES, :] = jnp.zeros((SUBLANES - 3, nbp), jnp.int32)

    erow = lax.broadcasted_iota(jnp.int32, (N_EXPERTS, TM), 0).astype(F32)

    def body(i, carry):
        ln = pl.ds(pl.multiple_of(i * TM, TM), TM)
        for k in range(2):
            onehot = jnp.where(erow == meta_ref[k:k + 1, ln], 1.0, 0.0)
            start = jnp.sum(onehot * pad_start[:, 0:1], axis=0, keepdims=True)
            dest_ref[k:k + 1, ln] = (start + meta_ref[2 + k:3 + k, ln]).astype(jnp.int32)
        dest_ref[2:SUBLANES, ln] = jnp.zeros((SUBLANES - 2, TM), jnp.int32)
        return carry

    lax.fori_loop(0, n_tiles, body, 0)


def _plan(cnt, meta):
    t = meta.shape[1]
    n_blocks = (2 * t + N_EXPERTS * TB) // TB
    nbp = -(-n_blocks // LANES) * LANES
    vm = pl.BlockSpec(memory_space=pltpu.VMEM)
    dest, blk = pl.pallas_call(
        functools.partial(_plan_kernel, t // TM, nbp),
        in_specs=[vm, vm],
        out_specs=[vm, vm],
        out_shape=[jax.ShapeDtypeStruct((SUBLANES, t), jnp.int32),
                   jax.ShapeDtypeStruct((SUBLANES, nbp), jnp.int32)],
        compiler_params=pltpu.CompilerParams(vmem_limit_bytes=VMEM_LIMIT),
        name="moe_plan",
    )(cnt, meta)
    return dest, blk, n_blocks


def _row_wait(src_ref, dst_ref, d, sem):
    n = TM * d // LANES
    pltpu.make_async_copy(src_ref.at[pl.ds(0, n), :], dst_ref.at[pl.ds(0, n), :], sem).wait()


def _dispatch_kernel(n_blocks, d, pe_ref, nu_ref, dest_ref, h2_ref, buf_ref, dsm_ref, zero_ref,
                     sem_idx, sem_zero, sem_rows):
    step = pl.program_id(0)
    blk = TB * d // LANES

    def zero_block(first_row):
        start = pl.multiple_of(first_row * (d // LANES), blk)
        return pltpu.make_async_copy(zero_ref, buf_ref.at[pl.ds(start, blk), :], sem_zero)

    @pl.when(step == 0)
    def _():
        zero_ref[...] = jnp.zeros_like(zero_ref)

        def for_segments(fn):
            for e in range(N_EXPERTS):
                seg_start = pe_ref[e - 1] if e else 0

                @pl.when(pe_ref[e] > seg_start)
                def _():
                    fn(zero_block(pe_ref[e] - TB))

        for_segments(lambda cp: cp.start())
        lax.fori_loop(nu_ref[0], n_blocks, lambda b, c: (zero_block(b * TB).start(), c)[1], 0)
        for_segments(lambda cp: cp.wait())
        lax.fori_loop(nu_ref[0], n_blocks, lambda b, c: (zero_block(b * TB).wait(), c)[1], 0)

    cp = pltpu.make_async_copy(dest_ref, dsm_ref, sem_idx)
    cp.start()
    cp.wait()

    for r in range(TM):
        for k in range(2):
            pltpu.make_async_copy(_token_tile(h2_ref, r, d), _token_tile(buf_ref, dsm_ref[k, r], d),
                                  sem_rows).start(priority=k)
    for k in range(2):
        _row_wait(h2_ref, buf_ref, d, sem_rows)


def _dispatch(h2t, dest, pad_end, n_used, n_blocks, d):
    per = d // LANES
    t = h2t.shape[0] // per
    grid_spec = pltpu.PrefetchScalarGridSpec(
        num_scalar_prefetch=2,
        grid=(t // TM,),
        in_specs=[pl.BlockSpec((SUBLANES, TM), lambda i, pe, nu: (0, i)),
                  pl.BlockSpec((TM * per, LANES), lambda i, pe, nu: (i, 0))],
        out_specs=pl.BlockSpec(memory_space=pl.ANY),
        scratch_shapes=[pltpu.SMEM((SUBLANES, TM), jnp.int32),
                        pltpu.VMEM((TB * per, LANES), F32),
                        pltpu.SemaphoreType.DMA, pltpu.SemaphoreType.DMA,
                        pltpu.SemaphoreType.DMA],
    )
    return pl.pallas_call(
        functools.partial(_dispatch_kernel, n_blocks, d),
        grid_spec=grid_spec,
        out_shape=jax.ShapeDtypeStruct((n_blocks * TB * per, LANES), F32),
        compiler_params=_params(),
        name="moe_dispatch",
    )(pad_end, n_used, dest, h2t)


EXPERT_PAIR = 2
EXPERT_RING = 2 * EXPERT_PAIR


def _experts_kernel(be_ref, nu_ref, buf_ref, wg_ref, wu_ref, wd_ref, y_ref, wg_bf, wu_bf, wd_bf,
                    wg_f32, wu_f32, wd_f32, x_ref, st_ref, sems, wsems):
    step = pl.program_id(0)
    n_used = nu_ref[0]
    d = wg_bf.shape[0]
    blk = TB * d // LANES
    first = step * EXPERT_PAIR

    def weight_copies(expert, slot):
        return [pltpu.make_async_copy(src.at[expert], dst.at[slot], wsems.at[slot])
                for src, dst in ((wg_ref, wg_f32), (wu_ref, wu_f32), (wd_ref, wd_f32))]

    @pl.when(step == 0)
    def _():
        st_ref[0] = -1
        st_ref[1] = 0
        for cp in weight_copies(be_ref[0], 0):
            cp.start()

    def block_copy(b):
        slot = b % EXPERT_RING
        return pltpu.make_async_copy(buf_ref.at[pl.ds(pl.multiple_of(b * blk, blk), blk), :],
                                     x_ref.at[slot], sems.at[slot])

    for h in range(EXPERT_PAIR):
        @pl.when(jnp.logical_and(step == 0, h < n_used))
        def _():
            block_copy(h).start()

        @pl.when(first + EXPERT_PAIR + h < n_used)
        def _():
            block_copy(first + EXPERT_PAIR + h).start()

    for h in range(EXPERT_PAIR):
        b = first + h
        used = b < n_used
        expert = be_ref[jnp.clip(b, 0, n_used - 1)]

        @pl.when(jnp.logical_and(used, expert != st_ref[0]))
        def _():
            slot = st_ref[1]
            for cp in weight_copies(expert, slot):
                cp.wait()
            wg_bf[...] = wg_f32[slot].astype(BF16)
            wu_bf[...] = wu_f32[slot].astype(BF16)
            wd_bf[...] = wd_f32[slot].astype(BF16)
            st_ref[0] = expert
            seg_end = lax.while_loop(
                lambda j: jnp.logical_and(j < n_used, be_ref[jnp.minimum(j, n_used - 1)] == expert),
                lambda j: j + 1, b + 1)

            @pl.when(seg_end < n_used)
            def _():
                nxt = be_ref[jnp.minimum(seg_end, n_used - 1)]
                st_ref[1] = 1 - slot
                for cp in weight_copies(nxt, 1 - slot):
                    cp.start()

        @pl.when(used)
        def _():
            block_copy(b).wait()
            xb = _load_token_tiles(x_ref.at[b % EXPERT_RING], TB, d).astype(BF16)
            g = jnp.dot(xb, wg_bf[...], preferred_element_type=F32)
            u = jnp.dot(xb, wu_bf[...], preferred_element_type=F32)
            a = (g * _sigmoid(g)) * u
            _store_token_tiles(y_ref, jnp.dot(a.astype(BF16), wd_bf[...], preferred_element_type=F32),
                               first_row=h * TB)

        @pl.when(jnp.logical_not(used))
        def _():
            y_ref[pl.ds(h * blk, blk), :] = jnp.zeros((blk, LANES), F32)


def _experts(buf, blk_e, n_used, wg, wu, wd, n_blocks):
    d = wg.shape[1]
    blk = TB * d // LANES
    assert n_blocks % EXPERT_PAIR == 0
    hbm = pl.BlockSpec(memory_space=pl.ANY)
    grid_spec = pltpu.PrefetchScalarGridSpec(
        num_scalar_prefetch=2,
        grid=(n_blocks // EXPERT_PAIR,),
        in_specs=[hbm, hbm, hbm, hbm],
        out_specs=pl.BlockSpec((EXPERT_PAIR * blk, LANES), lambda i, be, nu: (i, 0)),
        scratch_shapes=[pltpu.VMEM((d, D_EXPERT), BF16), pltpu.VMEM((d, D_EXPERT), BF16),
                        pltpu.VMEM((D_EXPERT, d), BF16),
                        pltpu.VMEM((2, d, D_EXPERT), F32), pltpu.VMEM((2, d, D_EXPERT), F32),
                        pltpu.VMEM((2, D_EXPERT, d), F32),
                        pltpu.VMEM((EXPERT_RING, blk, LANES), F32),
                        pltpu.SMEM((2,), jnp.int32),
                        pltpu.SemaphoreType.DMA((EXPERT_RING,)),
                        pltpu.SemaphoreType.DMA((2,))],
    )
    return pl.pallas_call(
        _experts_kernel,
        grid_spec=grid_spec,
        out_shape=jax.ShapeDtypeStruct(buf.shape, F32),
        compiler_params=_params(),
        name="moe_experts",
    )(blk_e, n_used, buf, wg, wu, wd)


def _gather_start(dest_ref, dsm_ref, ybuf_ref, rows_ref, slot, d, sem_idx, sem):
    cp = pltpu.make_async_copy(dest_ref, dsm_ref, sem_idx)
    cp.start()
    cp.wait()
    for r in range(TM):
        for k in range(2):
            pltpu.make_async_copy(_token_tile(ybuf_ref, dsm_ref[k, r], d),
                                  _token_tile(rows_ref.at[slot, k], r, d), sem).start(priority=k)


def _gathered_rows(dest_ref, next_ref, dsm_ref, ybuf_ref, rows_ref, d, sem_idx, sems):
    step = pl.program_id(0)
    nt = pl.num_programs(0)
    slot = step % 2

    @pl.when(step == 0)
    def _():
        _gather_start(dest_ref, dsm_ref, ybuf_ref, rows_ref, 0, d, sem_idx, sems.at[0])

    @pl.when(step + 1 < nt)
    def _():
        _gather_start(next_ref, dsm_ref, ybuf_ref, rows_ref, 1 - slot, d, sem_idx, sems.at[1 - slot])

    for k in range(2):
        _row_wait(ybuf_ref, rows_ref.at[slot, k], d, sems.at[slot])
    return [_load_token_tiles(rows_ref.at[slot, k], TM, d) for k in range(2)]


def _token_weights(meta_ref):
    meta = jnp.concatenate([meta_ref[...], jnp.zeros((LANES - SUBLANES, TM), F32)], axis=0)
    mt = meta.T
    return mt[:, 4:5], mt[:, 5:6]


_COMBINE_SCRATCH = lambda d: [pltpu.SMEM((SUBLANES, TM), jnp.int32),
                              pltpu.VMEM((2, 2, TM * d // LANES, LANES), F32),
                              pltpu.SemaphoreType.DMA, pltpu.SemaphoreType.DMA((2,))]


def _next_tile_spec(nt):
    return pl.BlockSpec((SUBLANES, TM), lambda i: (0, jnp.minimum(i + 1, nt - 1)))


def _combine0_kernel(x1_ref, dest_ref, next_ref, meta_ref, gt_ref, ybuf_ref, x2_ref, ctx2_ref,
                     dsm_ref, rows_ref, sem_idx, sems):
    is_ctx = pl.program_id(0) == 0
    y0, y1 = _gathered_rows(dest_ref, next_ref, dsm_ref, ybuf_ref, rows_ref, x1_ref.shape[1],
                            sem_idx, sems)
    w0, w1 = _token_weights(meta_ref)
    out = x1_ref[...] + _mod_row(gt_ref, is_ctx) * (w0 * y0 + w1 * y1)
    x2_ref[...] = out

    @pl.when(is_ctx)
    def _():
        ctx2_ref[...] = out


def _combine0(x1, dest, meta, mod, ybuf):
    t, d = x1.shape
    nt = t // TM
    return pl.pallas_call(
        _combine0_kernel,
        grid=(nt,),
        in_specs=[pl.BlockSpec((TM, d), lambda i: (i, 0)),
                  pl.BlockSpec((SUBLANES, TM), lambda i: (0, i)),
                  _next_tile_spec(nt),
                  pl.BlockSpec((SUBLANES, TM), lambda i: (0, i)),
                  pl.BlockSpec((SUBLANES, d), lambda i: (0, 5)),
                  pl.BlockSpec(memory_space=pl.ANY)],
        out_specs=[pl.BlockSpec((TM, d), lambda i: (jnp.maximum(i - 1, 0), 0)),
                   _full((TM, d))],
        out_shape=[jax.ShapeDtypeStruct((t - TM, d), F32), jax.ShapeDtypeStruct((TM, d), F32)],
        scratch_shapes=_COMBINE_SCRATCH(d),
        compiler_params=_params(),
        name="moe_combine0",
    )(x1, dest, dest, meta, mod, ybuf)


def _combine1_kernel(x1_ref, dest_ref, next_ref, meta_ref, gt_ref, gf_ref, ybuf_ref, o_ref,
                     dsm_ref, rows_ref, sem_idx, sems):
    y0, y1 = _gathered_rows(dest_ref, next_ref, dsm_ref, ybuf_ref, rows_ref, x1_ref.shape[1],
                            sem_idx, sems)
    w0, w1 = _token_weights(meta_ref)
    out = x1_ref[...] + gt_ref[0:1, :] * (w0 * y0 + w1 * y1)
    o_ref[...] = _rms(out, gf_ref[...])


def _combine1(x1, dest, meta, mod, gf, ybuf):
    t, d = x1.shape
    rows = t // GRID_W
    nt = t // TM
    out = pl.pallas_call(
        _combine1_kernel,
        grid=(nt,),
        in_specs=[pl.BlockSpec((TM, d), lambda i: (i, 0)),
                  pl.BlockSpec((SUBLANES, TM), lambda i: (0, i)),
                  _next_tile_spec(nt),
                  pl.BlockSpec((SUBLANES, TM), lambda i: (0, i)),
                  pl.BlockSpec((SUBLANES, d), lambda i: (0, 5)),
                  _full((1, d)),
                  pl.BlockSpec(memory_space=pl.ANY)],
        out_specs=pl.BlockSpec((TM, d), lambda i: (0, i)),
        out_shape=jax.ShapeDtypeStruct((rows, GRID_W * d), F32),
        scratch_shapes=_COMBINE_SCRATCH(d),
        compiler_params=_params(),
        name="moe_combine1",
    )(x1, dest, dest, meta, mod, gf.reshape(1, -1), ybuf)
    return out.reshape(t, d)


def _moe(h2t, meta, cnt, w_gate, w_up, w_down):
    dest, blk, n_blocks = _plan(cnt, meta)
    n_used = blk[1, :1]
    buf = _dispatch(h2t, dest, blk[2, :N_EXPERTS], n_used, n_blocks, w_gate.shape[1])
    ybuf = _experts(buf, blk[0], n_used, w_gate, w_up, w_down, n_blocks)
    return dest, ybuf


def _inproj1_kernel(x_ref, ctx_ref, g_ref, sh_ref, sc_ref, wq_ref, wkt_ref, wv_ref, wo_ref,
                    wg_ref, bg_ref, q_ref, kt_ref, v_ref, o_ref, gcol_ref):
    is_ctx = pl.program_id(0) == 0
    xin = jnp.where(is_ctx, ctx_ref[...], x_ref[...])
    h = _rms(xin, g_ref[...]) * (1.0 + _mod_row(sc_ref, is_ctx)) + _mod_row(sh_ref, is_ctx)
    hb = h.astype(BF16)
    q = jnp.dot(hb, wq_ref[...], preferred_element_type=F32) * (ML_DQK ** -0.5)
    q_ref[...] = q.astype(BF16)
    kt = lax.dot_general(wkt_ref[...], hb, (((1,), (1,)), ((), ())), preferred_element_type=F32)
    kt_ref[...] = kt.astype(BF16)
    v_ref[...] = jnp.dot(hb, wv_ref[...], preferred_element_type=F32).astype(BF16)
    o_ref[...] = jnp.dot(hb, wo_ref[...], preferred_element_type=F32).astype(BF16)
    gcol_ref[...] = _dot_bf16x3(h, wg_ref) + bg_ref[...]


def _inproj1(x, ctx, g1, mod, w_in, b_gates):
    n, d = x.shape
    rows = n // GRID_W
    nt = GRID_W + 1
    t = nt * TM
    ng = 4 * ML_HEADS
    wq = w_in[:, :ML_QK_W].astype(BF16)
    wkt = w_in[:, ML_QK_W:2 * ML_QK_W].T.astype(BF16)
    wv = w_in[:, 2 * ML_QK_W:2 * ML_QK_W + ML_V_W].astype(BF16)
    wo = w_in[:, 2 * ML_QK_W + ML_V_W:2 * ML_QK_W + ML_V_W + d].astype(BF16)
    wg = _split_bf16(jnp.zeros((d, LANES), F32).at[:, :ng].set(w_in[:, -ng:]))
    bg = jnp.zeros((1, LANES), F32).at[0, :ng].set(b_gates)
    tok = lambda w: pl.BlockSpec((TM, w), lambda i: (i, 0))
    return pl.pallas_call(
        _inproj1_kernel,
        grid=(nt,),
        in_specs=[pl.BlockSpec((rows, d), lambda i: (0, jnp.maximum(i - 1, 0))),
                  _full((TM, d)),
                  _full((1, d)),
                  pl.BlockSpec((SUBLANES, d), lambda i: (0, 0)),
                  pl.BlockSpec((SUBLANES, d), lambda i: (0, 1)),
                  _full(wq.shape), _full(wkt.shape), _full(wv.shape), _full(wo.shape),
                  _full(wg.shape), _full(bg.shape)],
        out_specs=[tok(ML_QK_W),
                   pl.BlockSpec((ML_QK_W, TM), lambda i: (0, i)),
                   tok(ML_V_W), tok(d), tok(LANES)],
        out_shape=[jax.ShapeDtypeStruct((t, ML_QK_W), BF16),
                   jax.ShapeDtypeStruct((ML_QK_W, t), BF16),
                   jax.ShapeDtypeStruct((t, ML_V_W), BF16),
                   jax.ShapeDtypeStruct((t, d), BF16),
                   jax.ShapeDtypeStruct((t, LANES), F32)],
        compiler_params=_params(),
        name="inproj1",
    )(x.reshape(rows, GRID_W * d), ctx, g1.reshape(1, -1), mod, mod, wq, wkt, wv, wo, wg, bg)


def _log_sigmoid(x):
    return jnp.minimum(x, 0.0) - jnp.log1p(jnp.exp(-jnp.abs(x)))


LOG2E = 1.4426950408889634


def _mlstm_kernel(reverse, q_ref, kt_ref, v_ref, gcol_ref, h_ref, c_ref, m_ref):
    @pl.when(pl.program_id(0) == 0)
    def _():
        c_ref[...] = jnp.zeros_like(c_ref)
        m_ref[...] = jnp.zeros_like(m_ref)

    L = TM
    half = L // 2
    gi = 2 * ML_HEADS if reverse else 0
    gf = gi + ML_HEADS
    end = 0 if reverse else L - 1
    rr = lax.broadcasted_iota(jnp.int32, (L, L), 0)
    cc = lax.broadcasted_iota(jnp.int32, (L, L), 1)
    tri = jnp.where((cc >= rr) if reverse else (cc <= rr), 1.0, 0.0).astype(BF16)
    rh = lax.broadcasted_iota(jnp.int32, (half, half), 0)
    ch = lax.broadcasted_iota(jnp.int32, (half, half), 1)
    diag = (ch >= rh) if reverse else (ch <= rh)

    gates = gcol_ref[...]
    lane = lax.broadcasted_iota(jnp.int32, (L, LANES), 1)
    mine = jnp.logical_and(lane >= gf, lane < gf + ML_HEADS)
    lf = jnp.where(mine, _log_sigmoid(gates), 0.0)
    p0 = lf.astype(BF16)
    r1 = lf - p0.astype(F32)
    p1 = r1.astype(BF16)
    p2 = (r1 - p1.astype(F32)).astype(BF16)
    cum = (jnp.dot(tri, p0, preferred_element_type=F32) + jnp.dot(tri, p1, preferred_element_type=F32)
           + jnp.dot(tri, p2, preferred_element_type=F32))

    r = jnp.where(mine, pltpu.roll(gates, ML_HEADS, 1) - cum, 0.0)
    row = lax.broadcasted_iota(jnp.int32, (L, LANES), 0)
    cm = r
    sh = 1
    while sh < L:
        if sh < SUBLANES:
            if reverse:
                cm = jnp.where(row < L - sh, jnp.maximum(cm, pltpu.roll(cm, L - sh, 0)), cm)
            else:
                cm = jnp.where(row >= sh, jnp.maximum(cm, pltpu.roll(cm, sh, 0)), cm)
        else:
            pad = jnp.full((sh, LANES), NEG_INF, F32)
            moved = (jnp.concatenate([cm[sh:], pad], axis=0) if reverse
                     else jnp.concatenate([pad, cm[:L - sh]], axis=0))
            cm = jnp.maximum(cm, moved)
        sh *= 2
    r8 = r.T[gf:gf + ML_HEADS]

    m_prev = m_ref[0:1, :]
    mm = jnp.maximum(m_prev, cm)
    m_t = cum + mm
    w_inter = jnp.exp(m_prev - mm)
    floor = jnp.exp(-m_t)
    mm2 = mm * LOG2E
    m_new = m_t[end:end + 1, :]
    shift2 = (cum[end:end + 1, :] - m_new) * LOG2E
    decay = jnp.exp(cum[end:end + 1, :] + m_prev - m_new)
    m_ref[0:1, :] = m_new

    ones_col = jnp.where(lax.broadcasted_iota(jnp.int32, (L, ML_DV), 1) == 0, 1.0, 0.0).astype(BF16)
    top, bot = slice(0, half), slice(half, L)
    dot = functools.partial(jnp.dot, preferred_element_type=F32)

    for hd in range(ML_HEADS):
        ln = gf + hd
        r2_row = r8[hd:hd + 1, :] * LOG2E

        def weights(tq, ks, masked):
            w = jnp.exp2(r2_row[:, ks] - mm2[tq, ln:ln + 1])
            return jnp.where(diag, w, 0.0) if masked else w

        qh = q_ref[:, hd * ML_DQK:(hd + 1) * ML_DQK]
        kth = kt_ref[hd * ML_DQK:(hd + 1) * ML_DQK, :]
        vext = jnp.concatenate([v_ref[:, hd * ML_DV:(hd + 1) * ML_DV], ones_col], axis=1)
        if reverse:
            s_top = dot(qh[top], kth) * jnp.concatenate([weights(top, top, True),
                                                         weights(top, bot, False)], axis=1)
            s_bot = dot(qh[bot], kth[:, bot]) * weights(bot, bot, True)
            intra = jnp.concatenate([dot(s_top.astype(BF16), vext),
                                     dot(s_bot.astype(BF16), vext[bot])], axis=0)
        else:
            s_top = dot(qh[top], kth[:, top]) * weights(top, top, True)
            s_bot = dot(qh[bot], kth) * jnp.concatenate([weights(bot, top, False),
                                                         weights(bot, bot, True)], axis=1)
            intra = jnp.concatenate([dot(s_top.astype(BF16), vext[top]),
                                     dot(s_bot.astype(BF16), vext)], axis=0)
        state = c_ref[hd]
        tot = w_inter[:, ln:ln + 1] * dot(qh, state.astype(BF16)) + intra
        den = tot[:, ML_DV:ML_DV + 1]
        h_ref[:, hd * ML_DV:(hd + 1) * ML_DV] = (
            tot[:, :ML_DV] / jnp.maximum(jnp.abs(den), floor[:, ln:ln + 1])).astype(h_ref.dtype)

        w_state = jnp.exp2(r2_row + shift2[:, ln:ln + 1])
        kw = (kth.astype(F32) * w_state).astype(BF16)
        c_ref[hd] = decay[:, ln:ln + 1] * state + dot(kw, vext)


def _mlstm(q, kt, v, gcol, reverse):
    t = q.shape[0]
    nt = t // TM
    tile = lambda s: _scan_tile(reverse, s, nt)
    return pl.pallas_call(
        functools.partial(_mlstm_kernel, reverse),
        grid=(nt,),
        in_specs=[pl.BlockSpec((TM, ML_QK_W), lambda s: (tile(s), 0)),
                  pl.BlockSpec((ML_QK_W, TM), lambda s: (0, tile(s))),
                  pl.BlockSpec((TM, ML_V_W), lambda s: (tile(s), 0)),
                  pl.BlockSpec((TM, LANES), lambda s: (tile(s), 0))],
        out_specs=pl.BlockSpec((TM, ML_V_W), lambda s: (tile(s), 0)),
        out_shape=jax.ShapeDtypeStruct((t, ML_V_W), BF16),
        scratch_shapes=[pltpu.VMEM((ML_HEADS, ML_DQK, 2 * ML_DV), F32),
                        pltpu.VMEM((SUBLANES, LANES), F32)],
        compiler_params=_params(),
        name="mlstm_bwd" if reverse else "mlstm_fwd",
    )(q, kt, v, gcol)


def _finish1_kernel(x_ref, hf_ref, hb_ref, o_ref, ng_ref, w_ref, gt_ref, g2_ref, sh_ref, sc_ref,
                    wr_ref, br_ref, x1_ref, h2_ref, lg_ref):
    hs = hf_ref[...].astype(F32) + hb_ref[...].astype(F32)
    parts = []
    for hd in range(ML_HEADS):
        blk = hs[:, hd * ML_DV:(hd + 1) * ML_DV]
        parts.append(blk * lax.rsqrt(jnp.mean(blk * blk, axis=-1, keepdims=True) + EPS))
    hn = jnp.concatenate(parts, axis=1) * ng_ref[...]
    y = jnp.dot((hn * _sigmoid(o_ref[...].astype(F32))).astype(BF16), w_ref[...],
                preferred_element_type=F32)
    x1 = x_ref[...] + gt_ref[0:1, :] * y
    x1_ref[...] = x1
    h2 = _rms(x1, g2_ref[...]) * (1.0 + sc_ref[0:1, :]) + sh_ref[0:1, :]
    _store_token_tiles(h2_ref, h2)
    lg_ref[...] = _router_logits(h2, wr_ref, br_ref)


def _finish1(x, hf, hb, o, norm_g, w_out_bf, mod, g2, wr, br):
    n, d = x.shape
    rows = n // GRID_W
    lat = lambda w: pl.BlockSpec((TM, w), lambda i: (i + 1, 0))
    modspec = lambda c: pl.BlockSpec((SUBLANES, d), lambda i: (0, c))
    return pl.pallas_call(
        _finish1_kernel,
        grid=(GRID_W,),
        in_specs=[pl.BlockSpec((rows, d), lambda i: (0, i)),
                  lat(ML_V_W), lat(ML_V_W), lat(d),
                  _full((1, ML_V_W)), _full((ML_V_W, d)),
                  modspec(2), _full((1, d)), modspec(3), modspec(4),
                  _full(wr.shape), _full(br.shape)],
        out_specs=_TOKEN_OUTS(d),
        out_shape=_token_out_shapes(n, d),
        compiler_params=_params(),
        name="finish1",
    )(x.reshape(rows, GRID_W * d), hf, hb, o, norm_g.reshape(1, -1), w_out_bf, mod,
      g2.reshape(1, -1), mod, mod, wr, br)


def kernel(x, c, ctx, c_ctx,
           l0_ada_w, l0_ada_b, l0_norm1_g, l0_norm2_g,
           l0_rg_w_in, l0_rg_conv_w, l0_rg_conv_b, l0_rg_w_a, l0_rg_b_a, l0_rg_w_x, l0_rg_b_x,
           l0_rg_lambda, l0_rg_w_out,
           l0_moe_w_grp, l0_moe_b_grp, l0_moe_w_exp, l0_moe_b_exp, l0_moe_w_gate, l0_moe_w_up,
           l0_moe_w_down,
           l1_ada_w, l1_ada_b, l1_norm1_g, l1_norm2_g,
           l1_ml_w_in, l1_ml_b_gates, l1_ml_norm_g, l1_ml_w_out,
           l1_moe_w_grp, l1_moe_b_grp, l1_moe_w_exp, l1_moe_b_exp, l1_moe_w_gate, l1_moe_w_up,
           l1_moe_w_down,
           final_norm_g):
    assert x.shape[0] == 1 and ctx.shape[1] == TM and x.shape[1] == GRID_W * TM
    xs, cs = x[0], ctx[0]
    d = xs.shape[1]
    cond8 = jnp.zeros((SUBLANES, d), F32).at[0].set(c[0]).at[1].set(c_ctx)

    mod0 = _adaln(cond8, l0_ada_w, l0_ada_b)
    conv_w8 = jnp.zeros((SUBLANES, D_RNN), F32).at[:CONV_W].set(l0_rg_conv_w)
    wcat = [jnp.concatenate([l0_rg_w_a[dr], l0_rg_w_x[dr]], axis=-1).astype(BF16) for dr in range(2)]
    gate, u, hf0 = _front0(xs, cs, l0_norm1_g, mod0, l0_rg_w_in.astype(BF16), conv_w8, l0_rg_conv_b,
                           wcat[0], l0_rg_b_a[0], l0_rg_b_x[0], l0_rg_lambda[0])
    wr0, br0 = _router_weights(l0_moe_w_grp, l0_moe_b_grp, l0_moe_w_exp, l0_moe_b_exp)
    x1, h2, logits = _back0(xs, cs, u, gate, hf0, wcat[1], l0_rg_b_a[1], l0_rg_b_x[1], l0_rg_lambda[1],
                            l0_rg_w_out.astype(BF16), mod0, l0_norm2_g, wr0, br0)
    meta, cnt = _route(logits, ROUTE_TILE_0)
    dest, ybuf = _moe(h2, meta, cnt, l0_moe_w_gate, l0_moe_w_up, l0_moe_w_down)
    x2, ctx2 = _combine0(x1, dest, meta, mod0, ybuf)

    mod1 = _adaln(cond8, l1_ada_w, l1_ada_b)
    q, kt, v, o, gcol = _inproj1(x2, ctx2, l1_norm1_g, mod1, l1_ml_w_in, l1_ml_b_gates)
    hf = _mlstm(q, kt, v, gcol, False)
    hb = _mlstm(q, kt, v, gcol, True)
    wr1, br1 = _router_weights(l1_moe_w_grp, l1_moe_b_grp, l1_moe_w_exp, l1_moe_b_exp)
    x1, h2, logits = _finish1(x2, hf, hb, o, l1_ml_norm_g, l1_ml_w_out.astype(BF16), mod1,
                              l1_norm2_g, wr1, br1)
    meta, cnt = _route(logits, ROUTE_TILE_1)
    dest, ybuf = _moe(h2, meta, cnt, l1_moe_w_gate, l1_moe_w_up, l1_moe_w_down)
    out = _combine1(x1, dest, meta, mod1, final_norm_g, ybuf)
    return out[None]
```

```python
import functools

import jax
import jax.numpy as jnp
from jax import lax
from jax.experimental import pallas as pl
from jax.experimental.pallas import tpu as pltpu

D_MODEL = 1024
GRID_W = 64
N_MOD = 6
EPS = 1e-6

D_RNN = 1280
RG_BLOCKS = 10
RG_BLOCK_W = D_RNN // RG_BLOCKS
CONV_W = 4
CONV_PAD_L = 2
RG_C = 8.0

ML_HEADS = 8
ML_DQK = D_MODEL // (2 * ML_HEADS)
ML_DV = D_MODEL // ML_HEADS
ML_QK_W = ML_HEADS * ML_DQK
ML_V_W = ML_HEADS * ML_DV

N_GROUPS = 4
EXPERTS_PER_GROUP = 8
N_EXPERTS = N_GROUPS * EXPERTS_PER_GROUP
D_EXPERT = 512

TM = 256
TB = 256
ROUTE_TILE_0 = 1280
ROUTE_TILE_1 = 1024
SUBLANES = 8
LANES = 128
VMEM_LIMIT = 48 * 1024 * 1024

F32 = jnp.float32
BF16 = jnp.bfloat16
HI = lax.Precision.HIGHEST
NEG_INF = float("-inf")


def _params(n_axes=1):
    return pltpu.CompilerParams(dimension_semantics=("arbitrary",) * n_axes,
                                vmem_limit_bytes=VMEM_LIMIT)


def _rms(x, g):
    return x * lax.rsqrt(jnp.mean(x * x, axis=-1, keepdims=True) + EPS) * g


def _sigmoid(x):
    return 1.0 / (1.0 + jnp.exp(-x))


def _softplus(x):
    return jnp.maximum(x, 0.0) + jnp.log1p(jnp.exp(-jnp.abs(x)))


def _gelu_tanh(x):
    return 0.5 * x * (1.0 + jnp.tanh(0.7978845608028654 * (x + 0.044715 * (x * x * x))))


def _full(shape):
    return pl.BlockSpec(shape, lambda *_: (0,) * len(shape))


def _store_token_tiles(ref, x, first_row=0):
    per = x.shape[1] // LANES
    for c in range(per):
        ref[pl.ds(first_row * per + c, x.shape[0], stride=per), :] = x[:, c * LANES:(c + 1) * LANES]


def _load_token_tiles(ref, rows, d):
    per = d // LANES
    return jnp.concatenate([ref[pl.ds(c, rows, stride=per), :] for c in range(per)], axis=1)


def _token_tile(ref, row, d):
    per = d // LANES
    start = row * per if isinstance(row, int) else pl.multiple_of(row * per, per)
    return ref.at[pl.ds(start, per), :]


def _adaln_kernel(cond_ref, w_ref, b_ref, o_ref):
    c = cond_ref[...]
    s = c * _sigmoid(c)
    o_ref[...] = jnp.dot(s, w_ref[...], precision=HI, preferred_element_type=F32) + b_ref[...]


def _adaln(cond8, w, b):
    d = w.shape[0]
    return pl.pallas_call(
        _adaln_kernel,
        grid=(N_MOD,),
        in_specs=[_full((SUBLANES, d)),
                  pl.BlockSpec((d, d), lambda j: (0, j)),
                  pl.BlockSpec((1, d), lambda j: (0, j))],
        out_specs=pl.BlockSpec((SUBLANES, d), lambda j: (0, j)),
        out_shape=jax.ShapeDtypeStruct((SUBLANES, N_MOD * d), F32),
        compiler_params=_params(),
        name="adaln",
    )(cond8, w, b.reshape(1, -1))


def _mod_row(ref, is_ctx):
    return jnp.where(is_ctx, ref[1:2, :], ref[0:1, :])


def _front0_kernel(nt, x_ref, ctx_ref, g_ref, sh_ref, sc_ref, w_ref, cw_ref, cb_ref,
                   wcat_ref, ba_ref, bx_ref, lam_ref, gate_ref, u_ref, hf_ref,
                   ext_ref, ubuf_ref, a_ref, b_ref, o_ref, carry_ref):
    s = pl.program_id(0)
    is_ctx = s == 0

    @pl.when(s == 0)
    def _():
        ext_ref[...] = jnp.zeros_like(ext_ref)
        ubuf_ref[...] = jnp.zeros_like(ubuf_ref)

    @pl.when(s <= 2)
    def _():
        carry_ref[...] = jnp.zeros_like(carry_ref)

    xin = jnp.where(is_ctx, ctx_ref[...], x_ref[...])
    h = _rms(xin, g_ref[...]) * (1.0 + _mod_row(sc_ref, is_ctx)) + _mod_row(sh_ref, is_ctx)
    hb = h.astype(BF16)

    right_valid = jnp.logical_and(s >= 2, s <= nt - 1)
    base = SUBLANES - CONV_PAD_L
    for j in range(RG_BLOCKS):
        ln = slice(j * RG_BLOCK_W, (j + 1) * RG_BLOCK_W)
        _rglru_block(False, j, ubuf_ref.at[s % 2], wcat_ref, ba_ref, bx_ref, lam_ref, hf_ref,
                     a_ref, b_ref, o_ref, carry_ref)
        rc = slice(D_RNN + j * RG_BLOCK_W, D_RNN + (j + 1) * RG_BLOCK_W)
        w_j = jnp.concatenate([w_ref[:, ln], w_ref[:, rc]], axis=1)
        p = jnp.dot(hb, w_j, preferred_element_type=F32)
        gate_ref[:, ln] = p[:, :RG_BLOCK_W]
        rec = p[:, RG_BLOCK_W:]
        ext_ref[j, SUBLANES + TM:, :] = jnp.where(right_valid, rec[0:SUBLANES], 0.0)
        u = cb_ref[:, ln] + ext_ref[j, pl.ds(base, TM), :] * cw_ref[0:1, ln]
        for k in range(1, CONV_W):
            u = u + ext_ref[j, pl.ds(base + k, TM), :] * cw_ref[k:k + 1, ln]
        u_ref[:, ln] = u
        ubuf_ref[(s + 1) % 2, :, ln] = u
        ext_ref[j, 0:SUBLANES, :] = jnp.where(s >= 2, ext_ref[j, TM:TM + SUBLANES, :], 0.0)
        ext_ref[j, SUBLANES:SUBLANES + TM, :] = rec


_SCAN_SLAB = lambda: pltpu.VMEM((RG_BLOCKS, SUBLANES * SCAN_PITCH, LANES), F32)
_SCAN_SCRATCH = lambda: [_SCAN_SLAB(), _SCAN_SLAB(), _SCAN_SLAB(), pltpu.VMEM((SUBLANES, D_RNN), F32)]


def _front0(x, ctx, g1, mod, w_in_bf, conv_w8, conv_b, wcat, b_a, b_x, lam):
    n, d = x.shape
    nx = n // TM
    nt = nx + 1
    t = nt * TM
    row = lambda v: v.reshape(1, -1)
    return pl.pallas_call(
        functools.partial(_front0_kernel, nt),
        grid=(nt + 2,),
        in_specs=[pl.BlockSpec((TM, d), lambda s: (jnp.clip(s - 1, 0, nx - 1), 0)),
                  _full((TM, d)),
                  _full((1, d)),
                  pl.BlockSpec((SUBLANES, d), lambda s: (0, 0)),
                  pl.BlockSpec((SUBLANES, d), lambda s: (0, 1)),
                  _full((d, 2 * D_RNN)),
                  _full((SUBLANES, D_RNN)), _full((1, D_RNN)),
                  _full((RG_BLOCKS, RG_BLOCK_W, 2 * RG_BLOCK_W)),
                  _full((1, D_RNN)), _full((1, D_RNN)), _full((1, D_RNN))],
        out_specs=[pl.BlockSpec((TM, D_RNN), lambda s: (jnp.minimum(s, nt - 1), 0)),
                   pl.BlockSpec((TM, D_RNN), lambda s: (jnp.maximum(s - 1, 0), 0)),
                   pl.BlockSpec((TM, D_RNN), lambda s: (jnp.maximum(s - 2, 0), 0))],
        out_shape=[jax.ShapeDtypeStruct((t, D_RNN), F32),
                   jax.ShapeDtypeStruct((t + TM, D_RNN), F32),
                   jax.ShapeDtypeStruct((t, D_RNN), F32)],
        scratch_shapes=[pltpu.VMEM((RG_BLOCKS, TM + 2 * SUBLANES, LANES), F32),
                        pltpu.VMEM((2, TM, D_RNN), F32)] + _SCAN_SCRATCH(),
        compiler_params=_params(),
        name="front0",
    )(x, ctx, g1.reshape(1, -1), mod, mod, w_in_bf, conv_w8, row(conv_b), wcat, row(b_a), row(b_x),
      row(lam))


def _scan_tile(reverse, s, nt):
    if not reverse:
        return s
    return jnp.where(s == 0, 0, nt - s)


SCAN_CHUNK = TM // SUBLANES
SCAN_PITCH = SCAN_CHUNK + 4


def _rglru_block(reverse, j, u_ref, wcat_ref, ba_ref, bx_ref, lam_ref, h_ref,
                 a_ref, b_ref, o_ref, carry_ref):
    steps = range(SCAN_CHUNK - 1, -1, -1) if reverse else range(SCAN_CHUNK)
    chunks = range(SUBLANES - 1, -1, -1) if reverse else range(SUBLANES)
    ln = slice(j * RG_BLOCK_W, (j + 1) * RG_BLOCK_W)
    u = u_ref[:, ln]
    g = jnp.dot(u.astype(BF16), wcat_ref[j], preferred_element_type=F32)
    half_rate = (-0.5 * RG_C) * _softplus(-lam_ref[:, ln])
    log_a = jnp.tanh(0.5 * (g[:, :RG_BLOCK_W] + ba_ref[:, ln])) * half_rate + half_rate
    ig = 0.5 * jnp.tanh(0.5 * (g[:, RG_BLOCK_W:] + bx_ref[:, ln])) + 0.5
    a = jnp.exp(log_a)
    b = jnp.sqrt(-jnp.tanh(log_a) * (a * a + 1.0)) * ig * u
    for c in range(SUBLANES):
        a_ref[j, pl.ds(c * SCAN_PITCH, SCAN_CHUNK), :] = a[c * SCAN_CHUNK:(c + 1) * SCAN_CHUNK]
        b_ref[j, pl.ds(c * SCAN_PITCH, SCAN_CHUNK), :] = b[c * SCAN_CHUNK:(c + 1) * SCAN_CHUNK]

    row = lambda ref, i: ref[j, pl.ds(i, SUBLANES, stride=SCAN_PITCH), :]
    end = jnp.zeros((SUBLANES, LANES), F32)
    decay = jnp.ones((SUBLANES, LANES), F32)
    for i in steps:
        ai = row(a_ref, i)
        end = ai * end + row(b_ref, i)
        decay = decay * ai

    state = carry_ref[0:1, ln]
    entry = [None] * SUBLANES
    for c in chunks:
        entry[c] = state
        state = decay[c:c + 1] * state + end[c:c + 1]
    carry_ref[0:1, ln] = state

    hcur = jnp.concatenate(entry, axis=0)
    for i in steps:
        hcur = row(a_ref, i) * hcur + row(b_ref, i)
        o_ref[j, pl.ds(i, SUBLANES, stride=SCAN_PITCH), :] = hcur
    for c in range(SUBLANES):
        h_ref[c * SCAN_CHUNK:(c + 1) * SCAN_CHUNK, ln] = o_ref[j, pl.ds(c * SCAN_PITCH, SCAN_CHUNK), :]


def _split_bf16(w):
    hi = w.astype(BF16)
    return jnp.stack([hi, (w - hi.astype(F32)).astype(BF16)])


def _dot_bf16x3(x, w_ref):
    hi = x.astype(BF16)
    lo = (x - hi.astype(F32)).astype(BF16)
    acc = jnp.dot(hi, w_ref[0], preferred_element_type=F32)
    acc = acc + jnp.dot(lo, w_ref[0], preferred_element_type=F32)
    return acc + jnp.dot(hi, w_ref[1], preferred_element_type=F32)


def _router_logits(h2, wr_ref, br_ref):
    return _dot_bf16x3(h2, wr_ref) + br_ref[...]


def _route_kernel(lg_ref, meta_ref, cnt_ref, carry_ref):
    step = pl.program_id(0)
    tm = lg_ref.shape[0]

    @pl.when(step == 0)
    def _():
        carry_ref[...] = jnp.zeros_like(carry_ref)

    logits = jnp.concatenate([lg_ref[i * LANES:(i + 1) * LANES, :].T for i in range(tm // LANES)],
                             axis=1)
    row8 = lax.broadcasted_iota(jnp.int32, (SUBLANES, tm), 0)
    grp_logits = jnp.where(row8 < N_GROUPS, logits[0:SUBLANES], NEG_INF)
    gmax = jnp.max(grp_logits, axis=0, keepdims=True)
    p_sel = 1.0 / jnp.sum(jnp.exp(grp_logits - gmax), axis=0, keepdims=True)
    grp = jnp.min(jnp.where(grp_logits == gmax, row8, SUBLANES), axis=0, keepdims=True)

    in_grp = logits[SUBLANES + (N_GROUPS - 1) * EXPERTS_PER_GROUP:SUBLANES + N_GROUPS * EXPERTS_PER_GROUP]
    for gi in range(N_GROUPS - 2, -1, -1):
        lo = SUBLANES + gi * EXPERTS_PER_GROUP
        in_grp = jnp.where(grp == gi, logits[lo:lo + EXPERTS_PER_GROUP], in_grp)
    v1 = jnp.max(in_grp, axis=0, keepdims=True)
    i1 = jnp.min(jnp.where(in_grp == v1, row8, EXPERTS_PER_GROUP), axis=0, keepdims=True)
    rest = jnp.where(row8 == i1, NEG_INF, in_grp)
    v2 = jnp.max(rest, axis=0, keepdims=True)
    i2 = jnp.min(jnp.where(rest == v2, row8, EXPERTS_PER_GROUP), axis=0, keepdims=True)
    e2 = jnp.exp(v2 - v1)
    w1 = p_sel / (1.0 + e2)
    w2 = p_sel * e2 / (1.0 + e2)
    eid = (grp * EXPERTS_PER_GROUP + i1, grp * EXPERTS_PER_GROUP + i2)

    rr = lax.broadcasted_iota(jnp.int32, (LANES, LANES), 0)
    cc = lax.broadcasted_iota(jnp.int32, (LANES, LANES), 1)
    strict_upper = jnp.where(rr < cc, 1.0, 0.0).astype(BF16)
    erow = lax.broadcasted_iota(jnp.int32, (N_EXPERTS, LANES), 0)
    base = carry_ref[:, 0:1]
    for k in range(2):
        for i in range(tm // LANES):
            ln = slice(i * LANES, (i + 1) * LANES)
            onehot = jnp.where(erow == eid[k][:, ln], 1.0, 0.0)
            pre = jnp.dot(onehot.astype(BF16), strict_upper, preferred_element_type=F32)
            meta_ref[2 + k:3 + k, ln] = jnp.sum(onehot * (base + pre), axis=0, keepdims=True)
            base = base + jnp.sum(onehot, axis=1, keepdims=True)
    carry_ref[...] = jnp.broadcast_to(base, carry_ref.shape)
    cnt_ref[...] = jnp.broadcast_to(base, cnt_ref.shape)

    meta_ref[0:1, :] = eid[0].astype(F32)
    meta_ref[1:2, :] = eid[1].astype(F32)
    meta_ref[4:5, :] = w1
    meta_ref[5:6, :] = w2
    meta_ref[6:8, :] = jnp.zeros((2, tm), F32)


def _router_weights(w_grp, b_grp, w_exp, b_exp):
    d = w_grp.shape[0]
    wr = jnp.zeros((d, LANES), F32).at[:, :N_GROUPS].set(w_grp)
    wr = wr.at[:, SUBLANES:SUBLANES + N_EXPERTS].set(w_exp)
    br = jnp.zeros((1, LANES), F32).at[0, :N_GROUPS].set(b_grp)
    br = br.at[0, SUBLANES:SUBLANES + N_EXPERTS].set(b_exp)
    return _split_bf16(wr), br


def _route(logits, tile):
    t = logits.shape[0]
    return pl.pallas_call(
        _route_kernel,
        grid=(t // tile,),
        in_specs=[pl.BlockSpec((tile, LANES), lambda i: (i, 0))],
        out_specs=[pl.BlockSpec((SUBLANES, tile), lambda i: (0, i)), _full((N_EXPERTS, LANES))],
        out_shape=[jax.ShapeDtypeStruct((SUBLANES, t), F32),
                   jax.ShapeDtypeStruct((N_EXPERTS, LANES), F32)],
        scratch_shapes=[pltpu.VMEM((N_EXPERTS, LANES), F32)],
        compiler_params=_params(),
        name="moe_route",
    )(logits)


def _back0_kernel(nt, x_ref, ctx_ref, u_ref, gate_ref, hf_ref, wcat_ref, ba_ref, bx_ref, lam_ref,
                  w_ref, gt_ref, g2_ref, sh_ref, sc_ref, wr_ref, br_ref, x1_ref, h2_ref, lg_ref,
                  hb_ref, a_ref, b_ref, o_ref, carry_ref):
    s = pl.program_id(0)
    is_ctx = s == 0

    @pl.when(s == 0)
    def _():
        carry_ref[...] = jnp.zeros_like(carry_ref)

    for j in range(RG_BLOCKS):
        _rglru_block(True, j, u_ref, wcat_ref, ba_ref, bx_ref, lam_ref, hb_ref,
                     a_ref, b_ref, o_ref, carry_ref)
    xin = jnp.where(is_ctx, ctx_ref[...], x_ref[...])
    y = _gelu_tanh(gate_ref[...]) * (hf_ref[...] + hb_ref[...])
    y = jnp.dot(y.astype(BF16), w_ref[...], preferred_element_type=F32)
    x1 = xin + _mod_row(gt_ref, is_ctx) * y
    x1_ref[...] = x1
    h2 = _rms(x1, g2_ref[...]) * (1.0 + _mod_row(sc_ref, is_ctx)) + _mod_row(sh_ref, is_ctx)
    _store_token_tiles(h2_ref, h2)
    lg_ref[...] = _router_logits(h2, wr_ref, br_ref)


_TOKEN_OUTS = lambda d: [pl.BlockSpec((TM, d), lambda i: (i, 0)),
                         pl.BlockSpec((TM * d // LANES, LANES), lambda i: (i, 0)),
                         pl.BlockSpec((TM, LANES), lambda i: (i, 0))]


def _token_out_shapes(t, d):
    return [jax.ShapeDtypeStruct((t, d), F32), jax.ShapeDtypeStruct((t * d // LANES, LANES), F32),
            jax.ShapeDtypeStruct((t, LANES), F32)]


def _back0(x, ctx, u, gate, hf, wcat, b_a, b_x, lam, w_out_bf, mod, g2, wr, br):
    n, d = x.shape
    nt = n // TM + 1
    t = nt * TM
    tile = lambda s: _scan_tile(True, s, nt)
    row = lambda v: v.reshape(1, -1)
    tok = pl.BlockSpec((TM, D_RNN), lambda s: (tile(s), 0))
    modspec = lambda c: pl.BlockSpec((SUBLANES, d), lambda s: (0, c))
    return pl.pallas_call(
        functools.partial(_back0_kernel, nt),
        grid=(nt,),
        in_specs=[pl.BlockSpec((TM, d), lambda s: (jnp.maximum(tile(s) - 1, 0), 0)),
                  _full((TM, d)),
                  tok, tok, tok,
                  _full((RG_BLOCKS, RG_BLOCK_W, 2 * RG_BLOCK_W)),
                  _full((1, D_RNN)), _full((1, D_RNN)), _full((1, D_RNN)),
                  _full((D_RNN, d)),
                  modspec(2), _full((1, d)), modspec(3), modspec(4),
                  _full(wr.shape), _full(br.shape)],
        out_specs=[pl.BlockSpec((TM, d), lambda s: (tile(s), 0)),
                   pl.BlockSpec((TM * d // LANES, LANES), lambda s: (tile(s), 0)),
                   pl.BlockSpec((TM, LANES), lambda s: (tile(s), 0))],
        out_shape=_token_out_shapes(t, d),
        scratch_shapes=[pltpu.VMEM((TM, D_RNN), F32)] + _SCAN_SCRATCH(),
        compiler_params=_params(),
        name="back0",
    )(x, ctx, u, gate, hf, wcat, row(b_a), row(b_x), row(lam), w_out_bf, mod, g2.reshape(1, -1),
      mod, mod, wr, br)


def _plan_kernel(n_tiles, nbp, cnt_ref, meta_ref, dest_ref, blk_ref):
    c = cnt_ref[...]
    padded = jnp.floor((c + (TB - 1)) * (1.0 / TB)) * TB
    r = lax.broadcasted_iota(jnp.int32, (N_EXPERTS, N_EXPERTS), 0)
    q = lax.broadcasted_iota(jnp.int32, (N_EXPERTS, N_EXPERTS), 1)
    lower = jnp.where(q <= r, 1.0, 0.0)
    pad_end = jnp.dot(lower, padded, precision=HI, preferred_element_type=F32)
    pad_start = pad_end - padded

    first_row = lax.broadcasted_iota(jnp.int32, (N_EXPERTS, nbp), 1).astype(F32) * TB
    owner = jnp.sum(jnp.where(pad_end[:, 0:1] <= first_row, 1.0, 0.0), axis=0, keepdims=True)
    blk_ref[0:1, :] = jnp.minimum(owner, N_EXPERTS - 1).astype(jnp.int32)
    n_used = pad_end[N_EXPERTS - 1:N_EXPERTS, 0:1] * (1.0 / TB)
    blk_ref[1:2, :] = jnp.broadcast_to(n_used, (1, nbp)).astype(jnp.int32)
    ends = jnp.concatenate([pad_end, jnp.zeros((LANES - N_EXPERTS, LANES), F32)], axis=0).T[0:1, :]
    blk_ref[2:3, :] = jnp.concatenate([ends, jnp.zeros((1, nbp - LANES), F32)], axis=1).astype(jnp.int32)
    blk_ref[3:SUBLANES, :] = jnp.zeros((SUBLANES - 3, nbp), jnp.int32)

    erow = lax.broadcasted_iota(jnp.int32, (N_EXPERTS, TM), 0).astype(F32)

    def body(i, carry):
        ln = pl.ds(pl.multiple_of(i * TM, TM), TM)
        for k in range(2):
            onehot = jnp.where(erow == meta_ref[k:k + 1, ln], 1.0, 0.0)
            start = jnp.sum(onehot * pad_start[:, 0:1], axis=0, keepdims=True)
            dest_ref[k:k + 1, ln] = (start + meta_ref[2 + k:3 + k, ln]).astype(jnp.int32)
        dest_ref[2:SUBLANES, ln] = jnp.zeros((SUBLANES - 2, TM), jnp.int32)
        return carry

    lax.fori_loop(0, n_tiles, body, 0)


def _plan(cnt, meta):
    t = meta.shape[1]
    n_blocks = (2 * t + N_EXPERTS * TB) // TB
    nbp = -(-n_blocks // LANES) * LANES
    vm = pl.BlockSpec(memory_space=pltpu.VMEM)
    dest, blk = pl.pallas_call(
        functools.partial(_plan_kernel, t // TM, nbp),
        in_specs=[vm, vm],
        out_specs=[vm, vm],
        out_shape=[jax.ShapeDtypeStruct((SUBLANES, t), jnp.int32),
                   jax.ShapeDtypeStruct((SUBLANES, nbp), jnp.int32)],
        compiler_params=pltpu.CompilerParams(vmem_limit_bytes=VMEM_LIMIT),
        name="moe_plan",
    )(cnt, meta)
    return dest, blk, n_blocks


def _row_wait(src_ref, dst_ref, d, sem):
    n = TM * d // LANES
    pltpu.make_async_copy(src_ref.at[pl.ds(0, n), :], dst_ref.at[pl.ds(0, n), :], sem).wait()


def _dispatch_kernel(n_blocks, d, pe_ref, nu_ref, dest_ref, h2_ref, buf_ref, dsm_ref, zero_ref,
                     sem_idx, sem_zero, sem_rows):
    step = pl.program_id(0)
    blk = TB * d // LANES

    def zero_block(first_row):
        start = pl.multiple_of(first_row * (d // LANES), blk)
        return pltpu.make_async_copy(zero_ref, buf_ref.at[pl.ds(start, blk), :], sem_zero)

    @pl.when(step == 0)
    def _():
        zero_ref[...] = jnp.zeros_like(zero_ref)

        def for_segments(fn):
            for e in range(N_EXPERTS):
                seg_start = pe_ref[e - 1] if e else 0

                @pl.when(pe_ref[e] > seg_start)
                def _():
                    fn(zero_block(pe_ref[e] - TB))

        for_segments(lambda cp: cp.start())
        lax.fori_loop(nu_ref[0], n_blocks, lambda b, c: (zero_block(b * TB).start(), c)[1], 0)
        for_segments(lambda cp: cp.wait())
        lax.fori_loop(nu_ref[0], n_blocks, lambda b, c: (zero_block(b * TB).wait(), c)[1], 0)

    cp = pltpu.make_async_copy(dest_ref, dsm_ref, sem_idx)
    cp.start()
    cp.wait()

    for r in range(TM):
        for k in range(2):
            pltpu.make_async_copy(_token_tile(h2_ref, r, d), _token_tile(buf_ref, dsm_ref[k, r], d),
                                  sem_rows).start(priority=k)
    for k in range(2):
        _row_wait(h2_ref, buf_ref, d, sem_rows)


def _dispatch(h2t, dest, pad_end, n_used, n_blocks, d):
    per = d // LANES
    t = h2t.shape[0] // per
    grid_spec = pltpu.PrefetchScalarGridSpec(
        num_scalar_prefetch=2,
        grid=(t // TM,),
        in_specs=[pl.BlockSpec((SUBLANES, TM), lambda i, pe, nu: (0, i)),
                  pl.BlockSpec((TM * per, LANES), lambda i, pe, nu: (i, 0))],
        out_specs=pl.BlockSpec(memory_space=pl.ANY),
        scratch_shapes=[pltpu.SMEM((SUBLANES, TM), jnp.int32),
                        pltpu.VMEM((TB * per, LANES), F32),
                        pltpu.SemaphoreType.DMA, pltpu.SemaphoreType.DMA,
                        pltpu.SemaphoreType.DMA],
    )
    return pl.pallas_call(
        functools.partial(_dispatch_kernel, n_blocks, d),
        grid_spec=grid_spec,
        out_shape=jax.ShapeDtypeStruct((n_blocks * TB * per, LANES), F32),
        compiler_params=_params(),
        name="moe_dispatch",
    )(pad_end, n_used, dest, h2t)


EXPERT_PAIR = 2
EXPERT_RING = 2 * EXPERT_PAIR


def _experts_kernel(be_ref, nu_ref, buf_ref, wg_ref, wu_ref, wd_ref, y_ref, wg_bf, wu_bf, wd_bf,
                    wg_f32, wu_f32, wd_f32, x_ref, st_ref, sems, wsems):
    step = pl.program_id(0)
    n_used = nu_ref[0]
    d = wg_bf.shape[0]
    blk = TB * d // LANES
    first = step * EXPERT_PAIR

    def weight_copies(expert, slot):
        return [pltpu.make_async_copy(src.at[expert], dst.at[slot], wsems.at[slot])
                for src, dst in ((wg_ref, wg_f32), (wu_ref, wu_f32), (wd_ref, wd_f32))]

    @pl.when(step == 0)
    def _():
        st_ref[0] = -1
        st_ref[1] = 0
        for cp in weight_copies(be_ref[0], 0):
            cp.start()

    def block_copy(b):
        slot = b % EXPERT_RING
        return pltpu.make_async_copy(buf_ref.at[pl.ds(pl.multiple_of(b * blk, blk), blk), :],
                                     x_ref.at[slot], sems.at[slot])

    for h in range(EXPERT_PAIR):
        @pl.when(jnp.logical_and(step == 0, h < n_used))
        def _():
            block_copy(h).start()

        @pl.when(first + EXPERT_PAIR + h < n_used)
        def _():
            block_copy(first + EXPERT_PAIR + h).start()

    for h in range(EXPERT_PAIR):
        b = first + h
        used = b < n_used
        expert = be_ref[jnp.clip(b, 0, n_used - 1)]

        @pl.when(jnp.logical_and(used, expert != st_ref[0]))
        def _():
            slot = st_ref[1]
            for cp in weight_copies(expert, slot):
                cp.wait()
            wg_bf[...] = wg_f32[slot].astype(BF16)
            wu_bf[...] = wu_f32[slot].astype(BF16)
            wd_bf[...] = wd_f32[slot].astype(BF16)
            st_ref[0] = expert
            seg_end = lax.while_loop(
                lambda j: jnp.logical_and(j < n_used, be_ref[jnp.minimum(j, n_used - 1)] == expert),
                lambda j: j + 1, b + 1)

            @pl.when(seg_end < n_used)
            def _():
                nxt = be_ref[jnp.minimum(seg_end, n_used - 1)]
                st_ref[1] = 1 - slot
                for cp in weight_copies(nxt, 1 - slot):
                    cp.start()

        @pl.when(used)
        def _():
            block_copy(b).wait()
            xb = _load_token_tiles(x_ref.at[b % EXPERT_RING], TB, d).astype(BF16)
            g = jnp.dot(xb, wg_bf[...], preferred_element_type=F32)
            u = jnp.dot(xb, wu_bf[...], preferred_element_type=F32)
            a = (g * _sigmoid(g)) * u
            _store_token_tiles(y_ref, jnp.dot(a.astype(BF16), wd_bf[...], preferred_element_type=F32),
                               first_row=h * TB)

        @pl.when(jnp.logical_not(used))
        def _():
            y_ref[pl.ds(h * blk, blk), :] = jnp.zeros((blk, LANES), F32)


def _experts(buf, blk_e, n_used, wg, wu, wd, n_blocks):
    d = wg.shape[1]
    blk = TB * d // LANES
    assert n_blocks % EXPERT_PAIR == 0
    hbm = pl.BlockSpec(memory_space=pl.ANY)
    grid_spec = pltpu.PrefetchScalarGridSpec(
        num_scalar_prefetch=2,
        grid=(n_blocks // EXPERT_PAIR,),
        in_specs=[hbm, hbm, hbm, hbm],
        out_specs=pl.BlockSpec((EXPERT_PAIR * blk, LANES), lambda i, be, nu: (i, 0)),
        scratch_shapes=[pltpu.VMEM((d, D_EXPERT), BF16), pltpu.VMEM((d, D_EXPERT), BF16),
                        pltpu.VMEM((D_EXPERT, d), BF16),
                        pltpu.VMEM((2, d, D_EXPERT), F32), pltpu.VMEM((2, d, D_EXPERT), F32),
                        pltpu.VMEM((2, D_EXPERT, d), F32),
                        pltpu.VMEM((EXPERT_RING, blk, LANES), F32),
                        pltpu.SMEM((2,), jnp.int32),
                        pltpu.SemaphoreType.DMA((EXPERT_RING,)),
                        pltpu.SemaphoreType.DMA((2,))],
    )
    return pl.pallas_call(
        _experts_kernel,
        grid_spec=grid_spec,
        out_shape=jax.ShapeDtypeStruct(buf.shape, F32),
        compiler_params=_params(),
        name="moe_experts",
    )(blk_e, n_used, buf, wg, wu, wd)


def _gather_start(dest_ref, dsm_ref, ybuf_ref, rows_ref, slot, d, sem_idx, sem):
    cp = pltpu.make_async_copy(dest_ref, dsm_ref, sem_idx)
    cp.start()
    cp.wait()
    for r in range(TM):
        for k in range(2):
            pltpu.make_async_copy(_token_tile(ybuf_ref, dsm_ref[k, r], d),
                                  _token_tile(rows_ref.at[slot, k], r, d), sem).start(priority=k)


def _gathered_rows(dest_ref, next_ref, dsm_ref, ybuf_ref, rows_ref, d, sem_idx, sems):
    step = pl.program_id(0)
    nt = pl.num_programs(0)
    slot = step % 2

    @pl.when(step == 0)
    def _():
        _gather_start(dest_ref, dsm_ref, ybuf_ref, rows_ref, 0, d, sem_idx, sems.at[0])

    @pl.when(step + 1 < nt)
    def _():
        _gather_start(next_ref, dsm_ref, ybuf_ref, rows_ref, 1 - slot, d, sem_idx, sems.at[1 - slot])

    for k in range(2):
        _row_wait(ybuf_ref, rows_ref.at[slot, k], d, sems.at[slot])
    return [_load_token_tiles(rows_ref.at[slot, k], TM, d) for k in range(2)]


def _token_weights(meta_ref):
    meta = jnp.concatenate([meta_ref[...], jnp.zeros((LANES - SUBLANES, TM), F32)], axis=0)
    mt = meta.T
    return mt[:, 4:5], mt[:, 5:6]


_COMBINE_SCRATCH = lambda d: [pltpu.SMEM((SUBLANES, TM), jnp.int32),
                              pltpu.VMEM((2, 2, TM * d // LANES, LANES), F32),
                              pltpu.SemaphoreType.DMA, pltpu.SemaphoreType.DMA((2,))]


def _next_tile_spec(nt):
    return pl.BlockSpec((SUBLANES, TM), lambda i: (0, jnp.minimum(i + 1, nt - 1)))


def _combine0_kernel(x1_ref, dest_ref, next_ref, meta_ref, gt_ref, ybuf_ref, x2_ref, ctx2_ref,
                     dsm_ref, rows_ref, sem_idx, sems):
    is_ctx = pl.program_id(0) == 0
    y0, y1 = _gathered_rows(dest_ref, next_ref, dsm_ref, ybuf_ref, rows_ref, x1_ref.shape[1],
                            sem_idx, sems)
    w0, w1 = _token_weights(meta_ref)
    out = x1_ref[...] + _mod_row(gt_ref, is_ctx) * (w0 * y0 + w1 * y1)
    x2_ref[...] = out

    @pl.when(is_ctx)
    def _():
        ctx2_ref[...] = out


def _combine0(x1, dest, meta, mod, ybuf):
    t, d = x1.shape
    nt = t // TM
    return pl.pallas_call(
        _combine0_kernel,
        grid=(nt,),
        in_specs=[pl.BlockSpec((TM, d), lambda i: (i, 0)),
                  pl.BlockSpec((SUBLANES, TM), lambda i: (0, i)),
                  _next_tile_spec(nt),
                  pl.BlockSpec((SUBLANES, TM), lambda i: (0, i)),
                  pl.BlockSpec((SUBLANES, d), lambda i: (0, 5)),
                  pl.BlockSpec(memory_space=pl.ANY)],
        out_specs=[pl.BlockSpec((TM, d), lambda i: (jnp.maximum(i - 1, 0), 0)),
                   _full((TM, d))],
        out_shape=[jax.ShapeDtypeStruct((t - TM, d), F32), jax.ShapeDtypeStruct((TM, d), F32)],
        scratch_shapes=_COMBINE_SCRATCH(d),
        compiler_params=_params(),
        name="moe_combine0",
    )(x1, dest, dest, meta, mod, ybuf)


def _combine1_kernel(x1_ref, dest_ref, next_ref, meta_ref, gt_ref, gf_ref, ybuf_ref, o_ref,
                     dsm_ref, rows_ref, sem_idx, sems):
    y0, y1 = _gathered_rows(dest_ref, next_ref, dsm_ref, ybuf_ref, rows_ref, x1_ref.shape[1],
                            sem_idx, sems)
    w0, w1 = _token_weights(meta_ref)
    out = x1_ref[...] + gt_ref[0:1, :] * (w0 * y0 + w1 * y1)
    o_ref[...] = _rms(out, gf_ref[...])


def _combine1(x1, dest, meta, mod, gf, ybuf):
    t, d = x1.shape
    rows = t // GRID_W
    nt = t // TM
    out = pl.pallas_call(
        _combine1_kernel,
        grid=(nt,),
        in_specs=[pl.BlockSpec((TM, d), lambda i: (i, 0)),
                  pl.BlockSpec((SUBLANES, TM), lambda i: (0, i)),
                  _next_tile_spec(nt),
                  pl.BlockSpec((SUBLANES, TM), lambda i: (0, i)),
                  pl.BlockSpec((SUBLANES, d), lambda i: (0, 5)),
                  _full((1, d)),
                  pl.BlockSpec(memory_space=pl.ANY)],
        out_specs=pl.BlockSpec((TM, d), lambda i: (0, i)),
        out_shape=jax.ShapeDtypeStruct((rows, GRID_W * d), F32),
        scratch_shapes=_COMBINE_SCRATCH(d),
        compiler_params=_params(),
        name="moe_combine1",
    )(x1, dest, dest, meta, mod, gf.reshape(1, -1), ybuf)
    return out.reshape(t, d)


def _moe(h2t, meta, cnt, w_gate, w_up, w_down):
    dest, blk, n_blocks = _plan(cnt, meta)
    n_used = blk[1, :1]
    buf = _dispatch(h2t, dest, blk[2, :N_EXPERTS], n_used, n_blocks, w_gate.shape[1])
    ybuf = _experts(buf, blk[0], n_used, w_gate, w_up, w_down, n_blocks)
    return dest, ybuf


def _inproj1_kernel(x_ref, ctx_ref, g_ref, sh_ref, sc_ref, wq_ref, wkt_ref, wv_ref, wo_ref,
                    wg_ref, bg_ref, q_ref, kt_ref, v_ref, o_ref, gcol_ref):
    is_ctx = pl.program_id(0) == 0
    xin = jnp.where(is_ctx, ctx_ref[...], x_ref[...])
    h = _rms(xin, g_ref[...]) * (1.0 + _mod_row(sc_ref, is_ctx)) + _mod_row(sh_ref, is_ctx)
    hb = h.astype(BF16)
    q = jnp.dot(hb, wq_ref[...], preferred_element_type=F32) * (ML_DQK ** -0.5)
    q_ref[...] = q.astype(BF16)
    kt = lax.dot_general(wkt_ref[...], hb, (((1,), (1,)), ((), ())), preferred_element_type=F32)
    kt_ref[...] = kt.astype(BF16)
    v_ref[...] = jnp.dot(hb, wv_ref[...], preferred_element_type=F32).astype(BF16)
    o_ref[...] = jnp.dot(hb, wo_ref[...], preferred_element_type=F32).astype(BF16)
    gcol_ref[...] = _dot_bf16x3(h, wg_ref) + bg_ref[...]


def _inproj1(x, ctx, g1, mod, w_in, b_gates):
    n, d = x.shape
    rows = n // GRID_W
    nt = GRID_W + 1
    t = nt * TM
    ng = 4 * ML_HEADS
    wq = w_in[:, :ML_QK_W].astype(BF16)
    wkt = w_in[:, ML_QK_W:2 * ML_QK_W].T.astype(BF16)
    wv = w_in[:, 2 * ML_QK_W:2 * ML_QK_W + ML_V_W].astype(BF16)
    wo = w_in[:, 2 * ML_QK_W + ML_V_W:2 * ML_QK_W + ML_V_W + d].astype(BF16)
    wg = _split_bf16(jnp.zeros((d, LANES), F32).at[:, :ng].set(w_in[:, -ng:]))
    bg = jnp.zeros((1, LANES), F32).at[0, :ng].set(b_gates)
    tok = lambda w: pl.BlockSpec((TM, w), lambda i: (i, 0))
    return pl.pallas_call(
        _inproj1_kernel,
        grid=(nt,),
        in_specs=[pl.BlockSpec((rows, d), lambda i: (0, jnp.maximum(i - 1, 0))),
                  _full((TM, d)),
                  _full((1, d)),
                  pl.BlockSpec((SUBLANES, d), lambda i: (0, 0)),
                  pl.BlockSpec((SUBLANES, d), lambda i: (0, 1)),
                  _full(wq.shape), _full(wkt.shape), _full(wv.shape), _full(wo.shape),
                  _full(wg.shape), _full(bg.shape)],
        out_specs=[tok(ML_QK_W),
                   pl.BlockSpec((ML_QK_W, TM), lambda i: (0, i)),
                   tok(ML_V_W), tok(d), tok(LANES)],
        out_shape=[jax.ShapeDtypeStruct((t, ML_QK_W), BF16),
                   jax.ShapeDtypeStruct((ML_QK_W, t), BF16),
                   jax.ShapeDtypeStruct((t, ML_V_W), BF16),
                   jax.ShapeDtypeStruct((t, d), BF16),
                   jax.ShapeDtypeStruct((t, LANES), F32)],
        compiler_params=_params(),
        name="inproj1",
    )(x.reshape(rows, GRID_W * d), ctx, g1.reshape(1, -1), mod, mod, wq, wkt, wv, wo, wg, bg)


def _log_sigmoid(x):
    return jnp.minimum(x, 0.0) - jnp.log1p(jnp.exp(-jnp.abs(x)))


LOG2E = 1.4426950408889634


def _mlstm_chunk(reverse, q_ref, kt_ref, v_ref, gcol_ref, h_ref, c_ref, m_ref):
    L = TM
    half = L // 2
    gi = 2 * ML_HEADS if reverse else 0
    gf = gi + ML_HEADS
    end = 0 if reverse else L - 1
    rr = lax.broadcasted_iota(jnp.int32, (L, L), 0)
    cc = lax.broadcasted_iota(jnp.int32, (L, L), 1)
    tri = jnp.where((cc >= rr) if reverse else (cc <= rr), 1.0, 0.0).astype(BF16)
    rh = lax.broadcasted_iota(jnp.int32, (half, half), 0)
    ch = lax.broadcasted_iota(jnp.int32, (half, half), 1)
    diag = (ch >= rh) if reverse else (ch <= rh)

    gates = gcol_ref[...]
    lane = lax.broadcasted_iota(jnp.int32, (L, LANES), 1)
    mine = jnp.logical_and(lane >= gf, lane < gf + ML_HEADS)
    lf = jnp.where(mine, _log_sigmoid(gates), 0.0)
    p0 = lf.astype(BF16)
    r1 = lf - p0.astype(F32)
    p1 = r1.astype(BF16)
    p2 = (r1 - p1.astype(F32)).astype(BF16)
    cum = (jnp.dot(tri, p0, preferred_element_type=F32) + jnp.dot(tri, p1, preferred_element_type=F32)
           + jnp.dot(tri, p2, preferred_element_type=F32))

    r = jnp.where(mine, pltpu.roll(gates, ML_HEADS, 1) - cum, 0.0)
    row = lax.broadcasted_iota(jnp.int32, (L, LANES), 0)
    cm = r
    sh = 1
    while sh < L:
        if sh < SUBLANES:
            if reverse:
                cm = jnp.where(row < L - sh, jnp.maximum(cm, pltpu.roll(cm, L - sh, 0)), cm)
            else:
                cm = jnp.where(row >= sh, jnp.maximum(cm, pltpu.roll(cm, sh, 0)), cm)
        else:
            pad = jnp.full((sh, LANES), NEG_INF, F32)
            moved = (jnp.concatenate([cm[sh:], pad], axis=0) if reverse
                     else jnp.concatenate([pad, cm[:L - sh]], axis=0))
            cm = jnp.maximum(cm, moved)
        sh *= 2
    r8 = r.T[gf:gf + ML_HEADS]

    m_prev = m_ref[0:1, :]
    mm = jnp.maximum(m_prev, cm)
    m_t = cum + mm
    w_inter = jnp.exp(m_prev - mm)
    floor = jnp.exp(-m_t)
    mm2 = mm * LOG2E
    m_new = m_t[end:end + 1, :]
    shift2 = (cum[end:end + 1, :] - m_new) * LOG2E
    decay = jnp.exp(cum[end:end + 1, :] + m_prev - m_new)
    m_ref[0:1, :] = m_new

    ones_col = jnp.where(lax.broadcasted_iota(jnp.int32, (L, ML_DV), 1) == 0, 1.0, 0.0).astype(BF16)
    top, bot = slice(0, half), slice(half, L)
    dot = functools.partial(jnp.dot, preferred_element_type=F32)
    yield

    for hd in range(ML_HEADS):
        ln = gf + hd
        r2_row = r8[hd:hd + 1, :] * LOG2E

        def weights(tq, ks, masked):
            w = jnp.exp2(r2_row[:, ks] - mm2[tq, ln:ln + 1])
            return jnp.where(diag, w, 0.0) if masked else w

        qh = q_ref[:, hd * ML_DQK:(hd + 1) * ML_DQK]
        kth = kt_ref[hd * ML_DQK:(hd + 1) * ML_DQK, :]
        vext = jnp.concatenate([v_ref[:, hd * ML_DV:(hd + 1) * ML_DV], ones_col], axis=1)
        if reverse:
            s_top = dot(qh[top], kth) * jnp.concatenate([weights(top, top, True),
                                                         weights(top, bot, False)], axis=1)
            s_bot = dot(qh[bot], kth[:, bot]) * weights(bot, bot, True)
            intra = jnp.concatenate([dot(s_top.astype(BF16), vext),
                                     dot(s_bot.astype(BF16), vext[bot])], axis=0)
        else:
            s_top = dot(qh[top], kth[:, top]) * weights(top, top, True)
            s_bot = dot(qh[bot], kth) * jnp.concatenate([weights(bot, top, False),
                                                         weights(bot, bot, True)], axis=1)
            intra = jnp.concatenate([dot(s_top.astype(BF16), vext[top]),
                                     dot(s_bot.astype(BF16), vext)], axis=0)
        state = c_ref[hd]
        tot = w_inter[:, ln:ln + 1] * dot(qh, state.astype(BF16)) + intra
        den = tot[:, ML_DV:ML_DV + 1]
        h_ref[:, hd * ML_DV:(hd + 1) * ML_DV] = (
            tot[:, :ML_DV] / jnp.maximum(jnp.abs(den), floor[:, ln:ln + 1])).astype(h_ref.dtype)

        w_state = jnp.exp2(r2_row + shift2[:, ln:ln + 1])
        kw = (kth.astype(F32) * w_state).astype(BF16)
        c_ref[hd] = decay[:, ln:ln + 1] * state + dot(kw, vext)
        yield


def _mlstm_kernel(qf_ref, ktf_ref, vf_ref, gf_ref, qb_ref, ktb_ref, vb_ref, gb_ref, hf_ref, hb_ref,
                  cf_ref, mf_ref, cb_ref, mb_ref):
    @pl.when(pl.program_id(0) == 0)
    def _():
        for ref in (cf_ref, mf_ref, cb_ref, mb_ref):
            ref[...] = jnp.zeros_like(ref)

    fwd = _mlstm_chunk(False, qf_ref, ktf_ref, vf_ref, gf_ref, hf_ref, cf_ref, mf_ref)
    bwd = _mlstm_chunk(True, qb_ref, ktb_ref, vb_ref, gb_ref, hb_ref, cb_ref, mb_ref)
    for _ in zip(fwd, bwd):
        pass


def _mlstm(q, kt, v, gcol):
    t = q.shape[0]
    nt = t // TM
    back = lambda s: _scan_tile(True, s, nt)
    state = [pltpu.VMEM((ML_HEADS, ML_DQK, 2 * ML_DV), F32), pltpu.VMEM((SUBLANES, LANES), F32)]

    def specs(tile):
        return [pl.BlockSpec((TM, ML_QK_W), lambda s: (tile(s), 0)),
                pl.BlockSpec((ML_QK_W, TM), lambda s: (0, tile(s))),
                pl.BlockSpec((TM, ML_V_W), lambda s: (tile(s), 0)),
                pl.BlockSpec((TM, LANES), lambda s: (tile(s), 0))]

    return pl.pallas_call(
        _mlstm_kernel,
        grid=(nt,),
        in_specs=specs(lambda s: s) + specs(back),
        out_specs=[pl.BlockSpec((TM, ML_V_W), lambda s: (s, 0)),
                   pl.BlockSpec((TM, ML_V_W), lambda s: (back(s), 0))],
        out_shape=[jax.ShapeDtypeStruct((t, ML_V_W), BF16)] * 2,
        scratch_shapes=state + state,
        compiler_params=_params(),
        name="mlstm",
    )(q, kt, v, gcol, q, kt, v, gcol)


def _finish1_kernel(x_ref, hf_ref, hb_ref, o_ref, ng_ref, w_ref, gt_ref, g2_ref, sh_ref, sc_ref,
                    wr_ref, br_ref, x1_ref, h2_ref, lg_ref):
    hs = hf_ref[...].astype(F32) + hb_ref[...].astype(F32)
    parts = []
    for hd in range(ML_HEADS):
        blk = hs[:, hd * ML_DV:(hd + 1) * ML_DV]
        parts.append(blk * lax.rsqrt(jnp.mean(blk * blk, axis=-1, keepdims=True) + EPS))
    hn = jnp.concatenate(parts, axis=1) * ng_ref[...]
    y = jnp.dot((hn * _sigmoid(o_ref[...].astype(F32))).astype(BF16), w_ref[...],
                preferred_element_type=F32)
    x1 = x_ref[...] + gt_ref[0:1, :] * y
    x1_ref[...] = x1
    h2 = _rms(x1, g2_ref[...]) * (1.0 + sc_ref[0:1, :]) + sh_ref[0:1, :]
    _store_token_tiles(h2_ref, h2)
    lg_ref[...] = _router_logits(h2, wr_ref, br_ref)


def _finish1(x, hf, hb, o, norm_g, w_out_bf, mod, g2, wr, br):
    n, d = x.shape
    rows = n // GRID_W
    lat = lambda w: pl.BlockSpec((TM, w), lambda i: (i + 1, 0))
    modspec = lambda c: pl.BlockSpec((SUBLANES, d), lambda i: (0, c))
    return pl.pallas_call(
        _finish1_kernel,
        grid=(GRID_W,),
        in_specs=[pl.BlockSpec((rows, d), lambda i: (0, i)),
                  lat(ML_V_W), lat(ML_V_W), lat(d),
                  _full((1, ML_V_W)), _full((ML_V_W, d)),
                  modspec(2), _full((1, d)), modspec(3), modspec(4),
                  _full(wr.shape), _full(br.shape)],
        out_specs=_TOKEN_OUTS(d),
        out_shape=_token_out_shapes(n, d),
        compiler_params=_params(),
        name="finish1",
    )(x.reshape(rows, GRID_W * d), hf, hb, o, norm_g.reshape(1, -1), w_out_bf, mod,
      g2.reshape(1, -1), mod, mod, wr, br)


def kernel(x, c, ctx, c_ctx,
           l0_ada_w, l0_ada_b, l0_norm1_g, l0_norm2_g,
           l0_rg_w_in, l0_rg_conv_w, l0_rg_conv_b, l0_rg_w_a, l0_rg_b_a, l0_rg_w_x, l0_rg_b_x,
           l0_rg_lambda, l0_rg_w_out,
           l0_moe_w_grp, l0_moe_b_grp, l0_moe_w_exp, l0_moe_b_exp, l0_moe_w_gate, l0_moe_w_up,
           l0_moe_w_down,
           l1_ada_w, l1_ada_b, l1_norm1_g, l1_norm2_g,
           l1_ml_w_in, l1_ml_b_gates, l1_ml_norm_g, l1_ml_w_out,
           l1_moe_w_grp, l1_moe_b_grp, l1_moe_w_exp, l1_moe_b_exp, l1_moe_w_gate, l1_moe_w_up,
           l1_moe_w_down,
           final_norm_g):
    assert x.shape[0] == 1 and ctx.shape[1] == TM and x.shape[1] == GRID_W * TM
    xs, cs = x[0], ctx[0]
    d = xs.shape[1]
    cond8 = jnp.zeros((SUBLANES, d), F32).at[0].set(c[0]).at[1].set(c_ctx)

    mod0 = _adaln(cond8, l0_ada_w, l0_ada_b)
    conv_w8 = jnp.zeros((SUBLANES, D_RNN), F32).at[:CONV_W].set(l0_rg_conv_w)
    wcat = [jnp.concatenate([l0_rg_w_a[dr], l0_rg_w_x[dr]], axis=-1).astype(BF16) for dr in range(2)]
    gate, u, hf0 = _front0(xs, cs, l0_norm1_g, mod0, l0_rg_w_in.astype(BF16), conv_w8, l0_rg_conv_b,
                           wcat[0], l0_rg_b_a[0], l0_rg_b_x[0], l0_rg_lambda[0])
    wr0, br0 = _router_weights(l0_moe_w_grp, l0_moe_b_grp, l0_moe_w_exp, l0_moe_b_exp)
    x1, h2, logits = _back0(xs, cs, u, gate, hf0, wcat[1], l0_rg_b_a[1], l0_rg_b_x[1], l0_rg_lambda[1],
                            l0_rg_w_out.astype(BF16), mod0, l0_norm2_g, wr0, br0)
    meta, cnt = _route(logits, ROUTE_TILE_0)
    dest, ybuf = _moe(h2, meta, cnt, l0_moe_w_gate, l0_moe_w_up, l0_moe_w_down)
    x2, ctx2 = _combine0(x1, dest, meta, mod0, ybuf)

    mod1 = _adaln(cond8, l1_ada_w, l1_ada_b)
    q, kt, v, o, gcol = _inproj1(x2, ctx2, l1_norm1_g, mod1, l1_ml_w_in, l1_ml_b_gates)
    hf, hb = _mlstm(q, kt, v, gcol)
    wr1, br1 = _router_weights(l1_moe_w_grp, l1_moe_b_grp, l1_moe_w_exp, l1_moe_b_exp)
    x1, h2, logits = _finish1(x2, hf, hb, o, l1_ml_norm_g, l1_ml_w_out.astype(BF16), mod1,
                              l1_norm2_g, wr1, br1)
    meta, cnt = _route(logits, ROUTE_TILE_1)
    dest, ybuf = _moe(h2, meta, cnt, l1_moe_w_gate, l1_moe_w_up, l1_moe_w_down)
    out = _combine1(x1, dest, meta, mod1, final_norm_g, ybuf)
    return out[None]
```

```python
import functools

import jax
import jax.numpy as jnp
from jax import lax
from jax.experimental import pallas as pl
from jax.experimental.pallas import tpu as pltpu

D_MODEL = 1024
GRID_W = 64
N_MOD = 6
EPS = 1e-6

D_RNN = 1280
RG_BLOCKS = 10
RG_BLOCK_W = D_RNN // RG_BLOCKS
CONV_W = 4
CONV_PAD_L = 2
RG_C = 8.0

ML_HEADS = 8
ML_DQK = D_MODEL // (2 * ML_HEADS)
ML_DV = D_MODEL // ML_HEADS
ML_QK_W = ML_HEADS * ML_DQK
ML_V_W = ML_HEADS * ML_DV

N_GROUPS = 4
EXPERTS_PER_GROUP = 8
N_EXPERTS = N_GROUPS * EXPERTS_PER_GROUP
D_EXPERT = 512

TM = 256
TB = 256
ROUTE_TILE_0 = 1280
ROUTE_TILE_1 = 1024
SUBLANES = 8
LANES = 128
VMEM_LIMIT = 48 * 1024 * 1024

F32 = jnp.float32
BF16 = jnp.bfloat16
HI = lax.Precision.HIGHEST
NEG_INF = float("-inf")


def _params(n_axes=1):
    return pltpu.CompilerParams(dimension_semantics=("arbitrary",) * n_axes,
                                vmem_limit_bytes=VMEM_LIMIT)


def _rms(x, g):
    return x * lax.rsqrt(jnp.mean(x * x, axis=-1, keepdims=True) + EPS) * g


def _sigmoid(x):
    return 1.0 / (1.0 + jnp.exp(-x))


def _softplus(x):
    return jnp.maximum(x, 0.0) + jnp.log1p(jnp.exp(-jnp.abs(x)))


def _gelu_tanh(x):
    return 0.5 * x * (1.0 + jnp.tanh(0.7978845608028654 * (x + 0.044715 * (x * x * x))))


def _full(shape):
    return pl.BlockSpec(shape, lambda *_: (0,) * len(shape))


def _store_token_tiles(ref, x, first_row=0):
    per = x.shape[1] // LANES
    for c in range(per):
        ref[pl.ds(first_row * per + c, x.shape[0], stride=per), :] = x[:, c * LANES:(c + 1) * LANES]


def _load_token_tiles(ref, rows, d):
    per = d // LANES
    return jnp.concatenate([ref[pl.ds(c, rows, stride=per), :] for c in range(per)], axis=1)


def _token_tile(ref, row, d):
    per = d // LANES
    start = row * per if isinstance(row, int) else pl.multiple_of(row * per, per)
    return ref.at[pl.ds(start, per), :]


def _adaln_kernel(cond_ref, w_ref, b_ref, o_ref):
    c = cond_ref[...]
    s = c * _sigmoid(c)
    o_ref[...] = jnp.dot(s, w_ref[...], precision=HI, preferred_element_type=F32) + b_ref[...]


def _adaln(cond8, w, b):
    d = w.shape[0]
    return pl.pallas_call(
        _adaln_kernel,
        grid=(N_MOD,),
        in_specs=[_full((SUBLANES, d)),
                  pl.BlockSpec((d, d), lambda j: (0, j)),
                  pl.BlockSpec((1, d), lambda j: (0, j))],
        out_specs=pl.BlockSpec((SUBLANES, d), lambda j: (0, j)),
        out_shape=jax.ShapeDtypeStruct((SUBLANES, N_MOD * d), F32),
        compiler_params=_params(),
        name="adaln",
    )(cond8, w, b.reshape(1, -1))


def _mod_row(ref, is_ctx):
    return jnp.where(is_ctx, ref[1:2, :], ref[0:1, :])


def _front0_kernel(nt, x_ref, ctx_ref, g_ref, sh_ref, sc_ref, w_ref, cw_ref, cb_ref,
                   wcat_ref, ba_ref, bx_ref, lam_ref, gate_ref, u_ref, hf_ref,
                   ext_ref, ubuf_ref, a_ref, b_ref, o_ref, carry_ref):
    s = pl.program_id(0)
    is_ctx = s == 0

    @pl.when(s == 0)
    def _():
        ext_ref[...] = jnp.zeros_like(ext_ref)
        ubuf_ref[...] = jnp.zeros_like(ubuf_ref)

    @pl.when(s <= 2)
    def _():
        carry_ref[...] = jnp.zeros_like(carry_ref)

    xin = jnp.where(is_ctx, ctx_ref[...], x_ref[...])
    h = _rms(xin, g_ref[...]) * (1.0 + _mod_row(sc_ref, is_ctx)) + _mod_row(sh_ref, is_ctx)
    hb = h.astype(BF16)

    right_valid = jnp.logical_and(s >= 2, s <= nt - 1)
    base = SUBLANES - CONV_PAD_L
    for j in range(RG_BLOCKS):
        ln = slice(j * RG_BLOCK_W, (j + 1) * RG_BLOCK_W)
        _rglru_block(False, j, ubuf_ref.at[s % 2], wcat_ref, ba_ref, bx_ref, lam_ref, hf_ref,
                     a_ref, b_ref, o_ref, carry_ref)
        rc = slice(D_RNN + j * RG_BLOCK_W, D_RNN + (j + 1) * RG_BLOCK_W)
        w_j = jnp.concatenate([w_ref[:, ln], w_ref[:, rc]], axis=1)
        p = jnp.dot(hb, w_j, preferred_element_type=F32)
        gate_ref[:, ln] = p[:, :RG_BLOCK_W]
        rec = p[:, RG_BLOCK_W:]
        ext_ref[j, SUBLANES + TM:, :] = jnp.where(right_valid, rec[0:SUBLANES], 0.0)
        u = cb_ref[:, ln] + ext_ref[j, pl.ds(base, TM), :] * cw_ref[0:1, ln]
        for k in range(1, CONV_W):
            u = u + ext_ref[j, pl.ds(base + k, TM), :] * cw_ref[k:k + 1, ln]
        u_ref[:, ln] = u
        ubuf_ref[(s + 1) % 2, :, ln] = u
        ext_ref[j, 0:SUBLANES, :] = jnp.where(s >= 2, ext_ref[j, TM:TM + SUBLANES, :], 0.0)
        ext_ref[j, SUBLANES:SUBLANES + TM, :] = rec


_SCAN_SLAB = lambda: pltpu.VMEM((RG_BLOCKS, SUBLANES * SCAN_PITCH, LANES), F32)
_SCAN_SCRATCH = lambda: [_SCAN_SLAB(), _SCAN_SLAB(), _SCAN_SLAB(), pltpu.VMEM((SUBLANES, D_RNN), F32)]


def _front0(x, ctx, g1, mod, w_in_bf, conv_w8, conv_b, wcat, b_a, b_x, lam):
    n, d = x.shape
    nx = n // TM
    nt = nx + 1
    t = nt * TM
    row = lambda v: v.reshape(1, -1)
    return pl.pallas_call(
        functools.partial(_front0_kernel, nt),
        grid=(nt + 2,),
        in_specs=[pl.BlockSpec((TM, d), lambda s: (jnp.clip(s - 1, 0, nx - 1), 0)),
                  _full((TM, d)),
                  _full((1, d)),
                  pl.BlockSpec((SUBLANES, d), lambda s: (0, 0)),
                  pl.BlockSpec((SUBLANES, d), lambda s: (0, 1)),
                  _full((d, 2 * D_RNN)),
                  _full((SUBLANES, D_RNN)), _full((1, D_RNN)),
                  _full((RG_BLOCKS, RG_BLOCK_W, 2 * RG_BLOCK_W)),
                  _full((1, D_RNN)), _full((1, D_RNN)), _full((1, D_RNN))],
        out_specs=[pl.BlockSpec((TM, D_RNN), lambda s: (jnp.minimum(s, nt - 1), 0)),
                   pl.BlockSpec((TM, D_RNN), lambda s: (jnp.maximum(s - 1, 0), 0)),
                   pl.BlockSpec((TM, D_RNN), lambda s: (jnp.maximum(s - 2, 0), 0))],
        out_shape=[jax.ShapeDtypeStruct((t, D_RNN), F32),
                   jax.ShapeDtypeStruct((t + TM, D_RNN), F32),
                   jax.ShapeDtypeStruct((t, D_RNN), F32)],
        scratch_shapes=[pltpu.VMEM((RG_BLOCKS, TM + 2 * SUBLANES, LANES), F32),
                        pltpu.VMEM((2, TM, D_RNN), F32)] + _SCAN_SCRATCH(),
        compiler_params=_params(),
        name="front0",
    )(x, ctx, g1.reshape(1, -1), mod, mod, w_in_bf, conv_w8, row(conv_b), wcat, row(b_a), row(b_x),
      row(lam))


def _scan_tile(reverse, s, nt):
    if not reverse:
        return s
    return jnp.where(s == 0, 0, nt - s)


SCAN_CHUNK = TM // SUBLANES
SCAN_PITCH = SCAN_CHUNK + 4


def _rglru_block(reverse, j, u_ref, wcat_ref, ba_ref, bx_ref, lam_ref, h_ref,
                 a_ref, b_ref, o_ref, carry_ref):
    steps = range(SCAN_CHUNK - 1, -1, -1) if reverse else range(SCAN_CHUNK)
    chunks = range(SUBLANES - 1, -1, -1) if reverse else range(SUBLANES)
    ln = slice(j * RG_BLOCK_W, (j + 1) * RG_BLOCK_W)
    u = u_ref[:, ln]
    g = jnp.dot(u.astype(BF16), wcat_ref[j], preferred_element_type=F32)
    half_rate = (-0.5 * RG_C) * _softplus(-lam_ref[:, ln])
    log_a = jnp.tanh(0.5 * (g[:, :RG_BLOCK_W] + ba_ref[:, ln])) * half_rate + half_rate
    ig = 0.5 * jnp.tanh(0.5 * (g[:, RG_BLOCK_W:] + bx_ref[:, ln])) + 0.5
    a = jnp.exp(log_a)
    b = jnp.sqrt(-jnp.tanh(log_a) * (a * a + 1.0)) * ig * u
    for c in range(SUBLANES):
        a_ref[j, pl.ds(c * SCAN_PITCH, SCAN_CHUNK), :] = a[c * SCAN_CHUNK:(c + 1) * SCAN_CHUNK]
        b_ref[j, pl.ds(c * SCAN_PITCH, SCAN_CHUNK), :] = b[c * SCAN_CHUNK:(c + 1) * SCAN_CHUNK]

    row = lambda ref, i: ref[j, pl.ds(i, SUBLANES, stride=SCAN_PITCH), :]
    end = jnp.zeros((SUBLANES, LANES), F32)
    decay = jnp.ones((SUBLANES, LANES), F32)
    for i in steps:
        ai = row(a_ref, i)
        end = ai * end + row(b_ref, i)
        decay = decay * ai

    state = carry_ref[0:1, ln]
    entry = [None] * SUBLANES
    for c in chunks:
        entry[c] = state
        state = decay[c:c + 1] * state + end[c:c + 1]
    carry_ref[0:1, ln] = state

    hcur = jnp.concatenate(entry, axis=0)
    for i in steps:
        hcur = row(a_ref, i) * hcur + row(b_ref, i)
        o_ref[j, pl.ds(i, SUBLANES, stride=SCAN_PITCH), :] = hcur
    for c in range(SUBLANES):
        h_ref[c * SCAN_CHUNK:(c + 1) * SCAN_CHUNK, ln] = o_ref[j, pl.ds(c * SCAN_PITCH, SCAN_CHUNK), :]


def _split_bf16(w):
    hi = w.astype(BF16)
    return jnp.stack([hi, (w - hi.astype(F32)).astype(BF16)])


def _dot_bf16x3(x, w_ref):
    hi = x.astype(BF16)
    lo = (x - hi.astype(F32)).astype(BF16)
    acc = jnp.dot(hi, w_ref[0], preferred_element_type=F32)
    acc = acc + jnp.dot(lo, w_ref[0], preferred_element_type=F32)
    return acc + jnp.dot(hi, w_ref[1], preferred_element_type=F32)


def _router_logits(h2, wr_ref, br_ref):
    return _dot_bf16x3(h2, wr_ref) + br_ref[...]


def _route_kernel(lg_ref, meta_ref, cnt_ref, carry_ref):
    step = pl.program_id(0)
    tm = lg_ref.shape[0]

    @pl.when(step == 0)
    def _():
        carry_ref[...] = jnp.zeros_like(carry_ref)

    logits = jnp.concatenate([lg_ref[i * LANES:(i + 1) * LANES, :].T for i in range(tm // LANES)],
                             axis=1)
    row8 = lax.broadcasted_iota(jnp.int32, (SUBLANES, tm), 0)
    grp_logits = jnp.where(row8 < N_GROUPS, logits[0:SUBLANES], NEG_INF)
    gmax = jnp.max(grp_logits, axis=0, keepdims=True)
    p_sel = 1.0 / jnp.sum(jnp.exp(grp_logits - gmax), axis=0, keepdims=True)
    grp = jnp.min(jnp.where(grp_logits == gmax, row8, SUBLANES), axis=0, keepdims=True)

    in_grp = logits[SUBLANES + (N_GROUPS - 1) * EXPERTS_PER_GROUP:SUBLANES + N_GROUPS * EXPERTS_PER_GROUP]
    for gi in range(N_GROUPS - 2, -1, -1):
        lo = SUBLANES + gi * EXPERTS_PER_GROUP
        in_grp = jnp.where(grp == gi, logits[lo:lo + EXPERTS_PER_GROUP], in_grp)
    v1 = jnp.max(in_grp, axis=0, keepdims=True)
    i1 = jnp.min(jnp.where(in_grp == v1, row8, EXPERTS_PER_GROUP), axis=0, keepdims=True)
    rest = jnp.where(row8 == i1, NEG_INF, in_grp)
    v2 = jnp.max(rest, axis=0, keepdims=True)
    i2 = jnp.min(jnp.where(rest == v2, row8, EXPERTS_PER_GROUP), axis=0, keepdims=True)
    e2 = jnp.exp(v2 - v1)
    w1 = p_sel / (1.0 + e2)
    w2 = p_sel * e2 / (1.0 + e2)
    eid = (grp * EXPERTS_PER_GROUP + i1, grp * EXPERTS_PER_GROUP + i2)

    rr = lax.broadcasted_iota(jnp.int32, (LANES, LANES), 0)
    cc = lax.broadcasted_iota(jnp.int32, (LANES, LANES), 1)
    strict_upper = jnp.where(rr < cc, 1.0, 0.0).astype(BF16)
    erow = lax.broadcasted_iota(jnp.int32, (N_EXPERTS, LANES), 0)
    base = carry_ref[:, 0:1]
    for k in range(2):
        for i in range(tm // LANES):
            ln = slice(i * LANES, (i + 1) * LANES)
            onehot = jnp.where(erow == eid[k][:, ln], 1.0, 0.0)
            pre = jnp.dot(onehot.astype(BF16), strict_upper, preferred_element_type=F32)
            meta_ref[2 + k:3 + k, ln] = jnp.sum(onehot * (base + pre), axis=0, keepdims=True)
            base = base + jnp.sum(onehot, axis=1, keepdims=True)
    carry_ref[...] = jnp.broadcast_to(base, carry_ref.shape)
    cnt_ref[...] = jnp.broadcast_to(base, cnt_ref.shape)

    meta_ref[0:1, :] = eid[0].astype(F32)
    meta_ref[1:2, :] = eid[1].astype(F32)
    meta_ref[4:5, :] = w1
    meta_ref[5:6, :] = w2
    meta_ref[6:8, :] = jnp.zeros((2, tm), F32)


def _router_weights(w_grp, b_grp, w_exp, b_exp):
    d = w_grp.shape[0]
    wr = jnp.zeros((d, LANES), F32).at[:, :N_GROUPS].set(w_grp)
    wr = wr.at[:, SUBLANES:SUBLANES + N_EXPERTS].set(w_exp)
    br = jnp.zeros((1, LANES), F32).at[0, :N_GROUPS].set(b_grp)
    br = br.at[0, SUBLANES:SUBLANES + N_EXPERTS].set(b_exp)
    return _split_bf16(wr), br


def _route(logits, tile):
    t = logits.shape[0]
    return pl.pallas_call(
        _route_kernel,
        grid=(t // tile,),
        in_specs=[pl.BlockSpec((tile, LANES), lambda i: (i, 0))],
        out_specs=[pl.BlockSpec((SUBLANES, tile), lambda i: (0, i)), _full((N_EXPERTS, LANES))],
        out_shape=[jax.ShapeDtypeStruct((SUBLANES, t), F32),
                   jax.ShapeDtypeStruct((N_EXPERTS, LANES), F32)],
        scratch_shapes=[pltpu.VMEM((N_EXPERTS, LANES), F32)],
        compiler_params=_params(),
        name="moe_route",
    )(logits)


def _back0_kernel(nt, x_ref, ctx_ref, u_ref, gate_ref, hf_ref, wcat_ref, ba_ref, bx_ref, lam_ref,
                  w_ref, gt_ref, g2_ref, sh_ref, sc_ref, wr_ref, br_ref, x1_ref, h2_ref, lg_ref,
                  hb_ref, a_ref, b_ref, o_ref, carry_ref):
    s = pl.program_id(0)
    is_ctx = s == 0

    @pl.when(s == 0)
    def _():
        carry_ref[...] = jnp.zeros_like(carry_ref)

    for j in range(RG_BLOCKS):
        _rglru_block(True, j, u_ref, wcat_ref, ba_ref, bx_ref, lam_ref, hb_ref,
                     a_ref, b_ref, o_ref, carry_ref)
    xin = jnp.where(is_ctx, ctx_ref[...], x_ref[...])
    y = _gelu_tanh(gate_ref[...]) * (hf_ref[...] + hb_ref[...])
    y = jnp.dot(y.astype(BF16), w_ref[...], preferred_element_type=F32)
    x1 = xin + _mod_row(gt_ref, is_ctx) * y
    x1_ref[...] = x1
    h2 = _rms(x1, g2_ref[...]) * (1.0 + _mod_row(sc_ref, is_ctx)) + _mod_row(sh_ref, is_ctx)
    _store_token_tiles(h2_ref, h2)
    lg_ref[...] = _router_logits(h2, wr_ref, br_ref)


_TOKEN_OUTS = lambda d: [pl.BlockSpec((TM, d), lambda i: (i, 0)),
                         pl.BlockSpec((TM * d // LANES, LANES), lambda i: (i, 0)),
                         pl.BlockSpec((TM, LANES), lambda i: (i, 0))]


def _token_out_shapes(t, d):
    return [jax.ShapeDtypeStruct((t, d), F32), jax.ShapeDtypeStruct((t * d // LANES, LANES), F32),
            jax.ShapeDtypeStruct((t, LANES), F32)]


def _back0(x, ctx, u, gate, hf, wcat, b_a, b_x, lam, w_out_bf, mod, g2, wr, br):
    n, d = x.shape
    nt = n // TM + 1
    t = nt * TM
    tile = lambda s: _scan_tile(True, s, nt)
    row = lambda v: v.reshape(1, -1)
    tok = pl.BlockSpec((TM, D_RNN), lambda s: (tile(s), 0))
    modspec = lambda c: pl.BlockSpec((SUBLANES, d), lambda s: (0, c))
    return pl.pallas_call(
        functools.partial(_back0_kernel, nt),
        grid=(nt,),
        in_specs=[pl.BlockSpec((TM, d), lambda s: (jnp.maximum(tile(s) - 1, 0), 0)),
                  _full((TM, d)),
                  tok, tok, tok,
                  _full((RG_BLOCKS, RG_BLOCK_W, 2 * RG_BLOCK_W)),
                  _full((1, D_RNN)), _full((1, D_RNN)), _full((1, D_RNN)),
                  _full((D_RNN, d)),
                  modspec(2), _full((1, d)), modspec(3), modspec(4),
                  _full(wr.shape), _full(br.shape)],
        out_specs=[pl.BlockSpec((TM, d), lambda s: (tile(s), 0)),
                   pl.BlockSpec((TM * d // LANES, LANES), lambda s: (tile(s), 0)),
                   pl.BlockSpec((TM, LANES), lambda s: (tile(s), 0))],
        out_shape=_token_out_shapes(t, d),
        scratch_shapes=[pltpu.VMEM((TM, D_RNN), F32)] + _SCAN_SCRATCH(),
        compiler_params=_params(),
        name="back0",
    )(x, ctx, u, gate, hf, wcat, row(b_a), row(b_x), row(lam), w_out_bf, mod, g2.reshape(1, -1),
      mod, mod, wr, br)


def _plan_kernel(n_tiles, nbp, cnt_ref, meta_ref, dest_ref, blk_ref):
    c = cnt_ref[...]
    padded = jnp.floor((c + (TB - 1)) * (1.0 / TB)) * TB
    r = lax.broadcasted_iota(jnp.int32, (N_EXPERTS, N_EXPERTS), 0)
    q = lax.broadcasted_iota(jnp.int32, (N_EXPERTS, N_EXPERTS), 1)
    lower = jnp.where(q <= r, 1.0, 0.0)
    pad_end = jnp.dot(lower, padded, precision=HI, preferred_element_type=F32)
    pad_start = pad_end - padded

    first_row = lax.broadcasted_iota(jnp.int32, (N_EXPERTS, nbp), 1).astype(F32) * TB
    owner = jnp.sum(jnp.where(pad_end[:, 0:1] <= first_row, 1.0, 0.0), axis=0, keepdims=True)
    blk_ref[0:1, :] = jnp.minimum(owner, N_EXPERTS - 1).astype(jnp.int32)
    n_used = pad_end[N_EXPERTS - 1:N_EXPERTS, 0:1] * (1.0 / TB)
    blk_ref[1:2, :] = jnp.broadcast_to(n_used, (1, nbp)).astype(jnp.int32)
    ends = jnp.concatenate([pad_end, jnp.zeros((LANES - N_EXPERTS, LANES), F32)], axis=0).T[0:1, :]
    blk_ref[2:3, :] = jnp.concatenate([ends, jnp.zeros((1, nbp - LANES), F32)], axis=1).astype(jnp.int32)
    blk_ref[3:SUBLANES, :] = jnp.zeros((SUBLANES - 3, nbp), jnp.int32)

    erow = lax.broadcasted_iota(jnp.int32, (N_EXPERTS, TM), 0).astype(F32)

    def body(i, carry):
        ln = pl.ds(pl.multiple_of(i * TM, TM), TM)
        for k in range(2):
            onehot = jnp.where(erow == meta_ref[k:k + 1, ln], 1.0, 0.0)
            start = jnp.sum(onehot * pad_start[:, 0:1], axis=0, keepdims=True)
            dest_ref[k:k + 1, ln] = (start + meta_ref[2 + k:3 + k, ln]).astype(jnp.int32)
        dest_ref[2:SUBLANES, ln] = jnp.zeros((SUBLANES - 2, TM), jnp.int32)
        return carry

    lax.fori_loop(0, n_tiles, body, 0)


def _plan(cnt, meta):
    t = meta.shape[1]
    n_blocks = (2 * t + N_EXPERTS * TB) // TB
    nbp = -(-n_blocks // LANES) * LANES
    vm = pl.BlockSpec(memory_space=pltpu.VMEM)
    dest, blk = pl.pallas_call(
        functools.partial(_plan_kernel, t // TM, nbp),
        in_specs=[vm, vm],
        out_specs=[vm, vm],
        out_shape=[jax.ShapeDtypeStruct((SUBLANES, t), jnp.int32),
                   jax.ShapeDtypeStruct((SUBLANES, nbp), jnp.int32)],
        compiler_params=pltpu.CompilerParams(vmem_limit_bytes=VMEM_LIMIT),
        name="moe_plan",
    )(cnt, meta)
    return dest, blk, n_blocks


def _row_wait(src_ref, dst_ref, d, sem):
    n = TM * d // LANES
    pltpu.make_async_copy(src_ref.at[pl.ds(0, n), :], dst_ref.at[pl.ds(0, n), :], sem).wait()


DISPATCH_RING = 3


def _dispatch_kernel(n_blocks, d, pe_ref, nu_ref, dest_ref, h2_ref, buf_ref, dsm_ref, zero_ref, src_ref,
                     sem_idx, sem_zero, sem_src, sem_rows):
    step = pl.program_id(0)
    nt = pl.num_programs(0)
    blk = TB * d // LANES
    tile = TM * d // LANES

    def src_copy(i):
        slot = i % DISPATCH_RING
        return pltpu.make_async_copy(h2_ref.at[pl.ds(pl.multiple_of(i * tile, tile), tile), :],
                                     src_ref.at[slot], sem_src.at[slot])

    def zero_block(first_row):
        start = pl.multiple_of(first_row * (d // LANES), blk)
        return pltpu.make_async_copy(zero_ref, buf_ref.at[pl.ds(start, blk), :], sem_zero)

    @pl.when(step == 0)
    def _():
        zero_ref[...] = jnp.zeros_like(zero_ref)

        def for_segments(fn):
            for e in range(N_EXPERTS):
                seg_start = pe_ref[e - 1] if e else 0

                @pl.when(pe_ref[e] > seg_start)
                def _():
                    fn(zero_block(pe_ref[e] - TB))

        for_segments(lambda cp: cp.start())
        lax.fori_loop(nu_ref[0], n_blocks, lambda b, c: (zero_block(b * TB).start(), c)[1], 0)
        for_segments(lambda cp: cp.wait())
        lax.fori_loop(nu_ref[0], n_blocks, lambda b, c: (zero_block(b * TB).wait(), c)[1], 0)
        src_copy(0).start()

    @pl.when(step + 1 < nt)
    def _():
        src_copy(step + 1).start()

    cp = pltpu.make_async_copy(dest_ref, dsm_ref, sem_idx)
    cp.start()
    cp.wait()
    src_copy(step).wait()
    src = src_ref.at[step % DISPATCH_RING]
    sem = sem_rows.at[step % 2]

    for r in range(TM):
        for k in range(2):
            pltpu.make_async_copy(_token_tile(src, r, d), _token_tile(buf_ref, dsm_ref[k, r], d),
                                  sem).start(priority=k)

    @pl.when(step > 0)
    def _():
        for k in range(2):
            _row_wait(src, buf_ref, d, sem_rows.at[(step + 1) % 2])

    @pl.when(step == nt - 1)
    def _():
        for k in range(2):
            _row_wait(src, buf_ref, d, sem)


def _dispatch(h2t, dest, pad_end, n_used, n_blocks, d):
    per = d // LANES
    t = h2t.shape[0] // per
    grid_spec = pltpu.PrefetchScalarGridSpec(
        num_scalar_prefetch=2,
        grid=(t // TM,),
        in_specs=[pl.BlockSpec((SUBLANES, TM), lambda i, pe, nu: (0, i)),
                  pl.BlockSpec(memory_space=pl.ANY)],
        out_specs=pl.BlockSpec(memory_space=pl.ANY),
        scratch_shapes=[pltpu.SMEM((SUBLANES, TM), jnp.int32),
                        pltpu.VMEM((TB * per, LANES), F32),
                        pltpu.VMEM((DISPATCH_RING, TM * per, LANES), F32),
                        pltpu.SemaphoreType.DMA, pltpu.SemaphoreType.DMA,
                        pltpu.SemaphoreType.DMA((DISPATCH_RING,)),
                        pltpu.SemaphoreType.DMA((2,))],
    )
    return pl.pallas_call(
        functools.partial(_dispatch_kernel, n_blocks, d),
        grid_spec=grid_spec,
        out_shape=jax.ShapeDtypeStruct((n_blocks * TB * per, LANES), F32),
        compiler_params=_params(),
        name="moe_dispatch",
    )(pad_end, n_used, dest, h2t)


EXPERT_PAIR = 2
EXPERT_RING = 2 * EXPERT_PAIR


def _experts_kernel(be_ref, nu_ref, buf_ref, wg_ref, wu_ref, wd_ref, y_ref, wg_bf, wu_bf, wd_bf,
                    wg_f32, wu_f32, wd_f32, x_ref, st_ref, sems, wsems):
    step = pl.program_id(0)
    n_used = nu_ref[0]
    d = wg_bf.shape[0]
    blk = TB * d // LANES
    first = step * EXPERT_PAIR

    def weight_copies(expert, slot):
        return [pltpu.make_async_copy(src.at[expert], dst.at[slot], wsems.at[slot])
                for src, dst in ((wg_ref, wg_f32), (wu_ref, wu_f32), (wd_ref, wd_f32))]

    @pl.when(step == 0)
    def _():
        st_ref[0] = -1
        st_ref[1] = 0
        for cp in weight_copies(be_ref[0], 0):
            cp.start()

    def block_copy(b):
        slot = b % EXPERT_RING
        return pltpu.make_async_copy(buf_ref.at[pl.ds(pl.multiple_of(b * blk, blk), blk), :],
                                     x_ref.at[slot], sems.at[slot])

    for h in range(EXPERT_PAIR):
        @pl.when(jnp.logical_and(step == 0, h < n_used))
        def _():
            block_copy(h).start()

        @pl.when(first + EXPERT_PAIR + h < n_used)
        def _():
            block_copy(first + EXPERT_PAIR + h).start()

    for h in range(EXPERT_PAIR):
        b = first + h
        used = b < n_used
        expert = be_ref[jnp.clip(b, 0, n_used - 1)]

        @pl.when(jnp.logical_and(used, expert != st_ref[0]))
        def _():
            slot = st_ref[1]
            for cp in weight_copies(expert, slot):
                cp.wait()
            wg_bf[...] = wg_f32[slot].astype(BF16)
            wu_bf[...] = wu_f32[slot].astype(BF16)
            wd_bf[...] = wd_f32[slot].astype(BF16)
            st_ref[0] = expert
            seg_end = lax.while_loop(
                lambda j: jnp.logical_and(j < n_used, be_ref[jnp.minimum(j, n_used - 1)] == expert),
                lambda j: j + 1, b + 1)

            @pl.when(seg_end < n_used)
            def _():
                nxt = be_ref[jnp.minimum(seg_end, n_used - 1)]
                st_ref[1] = 1 - slot
                for cp in weight_copies(nxt, 1 - slot):
                    cp.start()

        @pl.when(used)
        def _():
            block_copy(b).wait()
            xb = _load_token_tiles(x_ref.at[b % EXPERT_RING], TB, d).astype(BF16)
            g = jnp.dot(xb, wg_bf[...], preferred_element_type=F32)
            u = jnp.dot(xb, wu_bf[...], preferred_element_type=F32)
            a = (g * _sigmoid(g)) * u
            _store_token_tiles(y_ref, jnp.dot(a.astype(BF16), wd_bf[...], preferred_element_type=F32),
                               first_row=h * TB)

        @pl.when(jnp.logical_not(used))
        def _():
            y_ref[pl.ds(h * blk, blk), :] = jnp.zeros((blk, LANES), F32)


def _experts(buf, blk_e, n_used, wg, wu, wd, n_blocks):
    d = wg.shape[1]
    blk = TB * d // LANES
    assert n_blocks % EXPERT_PAIR == 0
    hbm = pl.BlockSpec(memory_space=pl.ANY)
    grid_spec = pltpu.PrefetchScalarGridSpec(
        num_scalar_prefetch=2,
        grid=(n_blocks // EXPERT_PAIR,),
        in_specs=[hbm, hbm, hbm, hbm],
        out_specs=pl.BlockSpec((EXPERT_PAIR * blk, LANES), lambda i, be, nu: (i, 0)),
        scratch_shapes=[pltpu.VMEM((d, D_EXPERT), BF16), pltpu.VMEM((d, D_EXPERT), BF16),
                        pltpu.VMEM((D_EXPERT, d), BF16),
                        pltpu.VMEM((2, d, D_EXPERT), F32), pltpu.VMEM((2, d, D_EXPERT), F32),
                        pltpu.VMEM((2, D_EXPERT, d), F32),
                        pltpu.VMEM((EXPERT_RING, blk, LANES), F32),
                        pltpu.SMEM((2,), jnp.int32),
                        pltpu.SemaphoreType.DMA((EXPERT_RING,)),
                        pltpu.SemaphoreType.DMA((2,))],
    )
    return pl.pallas_call(
        _experts_kernel,
        grid_spec=grid_spec,
        out_shape=jax.ShapeDtypeStruct(buf.shape, F32),
        compiler_params=_params(),
        name="moe_experts",
    )(blk_e, n_used, buf, wg, wu, wd)


def _gather_start(dest_ref, dsm_ref, ybuf_ref, rows_ref, slot, d, sem_idx, sem):
    cp = pltpu.make_async_copy(dest_ref, dsm_ref, sem_idx)
    cp.start()
    cp.wait()
    for r in range(TM):
        for k in range(2):
            pltpu.make_async_copy(_token_tile(ybuf_ref, dsm_ref[k, r], d),
                                  _token_tile(rows_ref.at[slot, k], r, d), sem).start(priority=k)


def _gathered_rows(dest_ref, next_ref, dsm_ref, ybuf_ref, rows_ref, d, sem_idx, sems):
    step = pl.program_id(0)
    nt = pl.num_programs(0)
    slot = step % 2

    @pl.when(step == 0)
    def _():
        _gather_start(dest_ref, dsm_ref, ybuf_ref, rows_ref, 0, d, sem_idx, sems.at[0])

    @pl.when(step + 1 < nt)
    def _():
        _gather_start(next_ref, dsm_ref, ybuf_ref, rows_ref, 1 - slot, d, sem_idx, sems.at[1 - slot])

    for k in range(2):
        _row_wait(ybuf_ref, rows_ref.at[slot, k], d, sems.at[slot])
    return [_load_token_tiles(rows_ref.at[slot, k], TM, d) for k in range(2)]


def _token_weights(meta_ref):
    meta = jnp.concatenate([meta_ref[...], jnp.zeros((LANES - SUBLANES, TM), F32)], axis=0)
    mt = meta.T
    return mt[:, 4:5], mt[:, 5:6]


_COMBINE_SCRATCH = lambda d: [pltpu.SMEM((SUBLANES, TM), jnp.int32),
                              pltpu.VMEM((2, 2, TM * d // LANES, LANES), F32),
                              pltpu.SemaphoreType.DMA, pltpu.SemaphoreType.DMA((2,))]


def _next_tile_spec(nt):
    return pl.BlockSpec((SUBLANES, TM), lambda i: (0, jnp.minimum(i + 1, nt - 1)))


def _combine0_kernel(x1_ref, dest_ref, next_ref, meta_ref, gt_ref, ybuf_ref, x2_ref, ctx2_ref,
                     dsm_ref, rows_ref, sem_idx, sems):
    is_ctx = pl.program_id(0) == 0
    y0, y1 = _gathered_rows(dest_ref, next_ref, dsm_ref, ybuf_ref, rows_ref, x1_ref.shape[1],
                            sem_idx, sems)
    w0, w1 = _token_weights(meta_ref)
    out = x1_ref[...] + _mod_row(gt_ref, is_ctx) * (w0 * y0 + w1 * y1)
    x2_ref[...] = out

    @pl.when(is_ctx)
    def _():
        ctx2_ref[...] = out


def _combine0(x1, dest, meta, mod, ybuf):
    t, d = x1.shape
    nt = t // TM
    return pl.pallas_call(
        _combine0_kernel,
        grid=(nt,),
        in_specs=[pl.BlockSpec((TM, d), lambda i: (i, 0)),
                  pl.BlockSpec((SUBLANES, TM), lambda i: (0, i)),
                  _next_tile_spec(nt),
                  pl.BlockSpec((SUBLANES, TM), lambda i: (0, i)),
                  pl.BlockSpec((SUBLANES, d), lambda i: (0, 5)),
                  pl.BlockSpec(memory_space=pl.ANY)],
        out_specs=[pl.BlockSpec((TM, d), lambda i: (jnp.maximum(i - 1, 0), 0)),
                   _full((TM, d))],
        out_shape=[jax.ShapeDtypeStruct((t - TM, d), F32), jax.ShapeDtypeStruct((TM, d), F32)],
        scratch_shapes=_COMBINE_SCRATCH(d),
        compiler_params=_params(),
        name="moe_combine0",
    )(x1, dest, dest, meta, mod, ybuf)


def _combine1_kernel(x1_ref, dest_ref, next_ref, meta_ref, gt_ref, gf_ref, ybuf_ref, o_ref,
                     dsm_ref, rows_ref, sem_idx, sems):
    y0, y1 = _gathered_rows(dest_ref, next_ref, dsm_ref, ybuf_ref, rows_ref, x1_ref.shape[1],
                            sem_idx, sems)
    w0, w1 = _token_weights(meta_ref)
    out = x1_ref[...] + gt_ref[0:1, :] * (w0 * y0 + w1 * y1)
    o_ref[...] = _rms(out, gf_ref[...])


def _combine1(x1, dest, meta, mod, gf, ybuf):
    t, d = x1.shape
    rows = t // GRID_W
    nt = t // TM
    out = pl.pallas_call(
        _combine1_kernel,
        grid=(nt,),
        in_specs=[pl.BlockSpec((TM, d), lambda i: (i, 0)),
                  pl.BlockSpec((SUBLANES, TM), lambda i: (0, i)),
                  _next_tile_spec(nt),
                  pl.BlockSpec((SUBLANES, TM), lambda i: (0, i)),
                  pl.BlockSpec((SUBLANES, d), lambda i: (0, 5)),
                  _full((1, d)),
                  pl.BlockSpec(memory_space=pl.ANY)],
        out_specs=pl.BlockSpec((TM, d), lambda i: (0, i)),
        out_shape=jax.ShapeDtypeStruct((rows, GRID_W * d), F32),
        scratch_shapes=_COMBINE_SCRATCH(d),
        compiler_params=_params(),
        name="moe_combine1",
    )(x1, dest, dest, meta, mod, gf.reshape(1, -1), ybuf)
    return out.reshape(t, d)


def _moe(h2t, meta, cnt, w_gate, w_up, w_down):
    dest, blk, n_blocks = _plan(cnt, meta)
    n_used = blk[1, :1]
    buf = _dispatch(h2t, dest, blk[2, :N_EXPERTS], n_used, n_blocks, w_gate.shape[1])
    ybuf = _experts(buf, blk[0], n_used, w_gate, w_up, w_down, n_blocks)
    return dest, ybuf


def _inproj1_kernel(x_ref, ctx_ref, g_ref, sh_ref, sc_ref, wq_ref, wkt_ref, wv_ref, wo_ref,
                    wg_ref, bg_ref, q_ref, kt_ref, v_ref, o_ref, gcol_ref):
    is_ctx = pl.program_id(0) == 0
    xin = jnp.where(is_ctx, ctx_ref[...], x_ref[...])
    h = _rms(xin, g_ref[...]) * (1.0 + _mod_row(sc_ref, is_ctx)) + _mod_row(sh_ref, is_ctx)
    hb = h.astype(BF16)
    q = jnp.dot(hb, wq_ref[...], preferred_element_type=F32) * (ML_DQK ** -0.5)
    q_ref[...] = q.astype(BF16)
    kt = lax.dot_general(wkt_ref[...], hb, (((1,), (1,)), ((), ())), preferred_element_type=F32)
    kt_ref[...] = kt.astype(BF16)
    v_ref[...] = jnp.dot(hb, wv_ref[...], preferred_element_type=F32).astype(BF16)
    o_ref[...] = jnp.dot(hb, wo_ref[...], preferred_element_type=F32).astype(BF16)
    gcol_ref[...] = _dot_bf16x3(h, wg_ref) + bg_ref[...]


def _inproj1(x, ctx, g1, mod, w_in, b_gates):
    n, d = x.shape
    rows = n // GRID_W
    nt = GRID_W + 1
    t = nt * TM
    ng = 4 * ML_HEADS
    wq = w_in[:, :ML_QK_W].astype(BF16)
    wkt = w_in[:, ML_QK_W:2 * ML_QK_W].T.astype(BF16)
    wv = w_in[:, 2 * ML_QK_W:2 * ML_QK_W + ML_V_W].astype(BF16)
    wo = w_in[:, 2 * ML_QK_W + ML_V_W:2 * ML_QK_W + ML_V_W + d].astype(BF16)
    wg = _split_bf16(jnp.zeros((d, LANES), F32).at[:, :ng].set(w_in[:, -ng:]))
    bg = jnp.zeros((1, LANES), F32).at[0, :ng].set(b_gates)
    tok = lambda w: pl.BlockSpec((TM, w), lambda i: (i, 0))
    return pl.pallas_call(
        _inproj1_kernel,
        grid=(nt,),
        in_specs=[pl.BlockSpec((rows, d), lambda i: (0, jnp.maximum(i - 1, 0))),
                  _full((TM, d)),
                  _full((1, d)),
                  pl.BlockSpec((SUBLANES, d), lambda i: (0, 0)),
                  pl.BlockSpec((SUBLANES, d), lambda i: (0, 1)),
                  _full(wq.shape), _full(wkt.shape), _full(wv.shape), _full(wo.shape),
                  _full(wg.shape), _full(bg.shape)],
        out_specs=[tok(ML_QK_W),
                   pl.BlockSpec((ML_QK_W, TM), lambda i: (0, i)),
                   tok(ML_V_W), tok(d), tok(LANES)],
        out_shape=[jax.ShapeDtypeStruct((t, ML_QK_W), BF16),
                   jax.ShapeDtypeStruct((ML_QK_W, t), BF16),
                   jax.ShapeDtypeStruct((t, ML_V_W), BF16),
                   jax.ShapeDtypeStruct((t, d), BF16),
                   jax.ShapeDtypeStruct((t, LANES), F32)],
        compiler_params=_params(),
        name="inproj1",
    )(x.reshape(rows, GRID_W * d), ctx, g1.reshape(1, -1), mod, mod, wq, wkt, wv, wo, wg, bg)


def _log_sigmoid(x):
    return jnp.minimum(x, 0.0) - jnp.log1p(jnp.exp(-jnp.abs(x)))


LOG2E = 1.4426950408889634


def _mlstm_chunk(reverse, q_ref, kt_ref, v_ref, gcol_ref, h_ref, c_ref, m_ref):
    L = TM
    half = L // 2
    gi = 2 * ML_HEADS if reverse else 0
    gf = gi + ML_HEADS
    end = 0 if reverse else L - 1
    rr = lax.broadcasted_iota(jnp.int32, (L, L), 0)
    cc = lax.broadcasted_iota(jnp.int32, (L, L), 1)
    tri = jnp.where((cc >= rr) if reverse else (cc <= rr), 1.0, 0.0).astype(BF16)
    rh = lax.broadcasted_iota(jnp.int32, (half, half), 0)
    ch = lax.broadcasted_iota(jnp.int32, (half, half), 1)
    diag = (ch >= rh) if reverse else (ch <= rh)

    gates = gcol_ref[...]
    lane = lax.broadcasted_iota(jnp.int32, (L, LANES), 1)
    mine = jnp.logical_and(lane >= gf, lane < gf + ML_HEADS)
    lf = jnp.where(mine, _log_sigmoid(gates), 0.0)
    p0 = lf.astype(BF16)
    r1 = lf - p0.astype(F32)
    p1 = r1.astype(BF16)
    p2 = (r1 - p1.astype(F32)).astype(BF16)
    cum = (jnp.dot(tri, p0, preferred_element_type=F32) + jnp.dot(tri, p1, preferred_element_type=F32)
           + jnp.dot(tri, p2, preferred_element_type=F32))

    r = jnp.where(mine, pltpu.roll(gates, ML_HEADS, 1) - cum, 0.0)
    row = lax.broadcasted_iota(jnp.int32, (L, LANES), 0)
    cm = r
    sh = 1
    while sh < L:
        if sh < SUBLANES:
            if reverse:
                cm = jnp.where(row < L - sh, jnp.maximum(cm, pltpu.roll(cm, L - sh, 0)), cm)
            else:
                cm = jnp.where(row >= sh, jnp.maximum(cm, pltpu.roll(cm, sh, 0)), cm)
        else:
            pad = jnp.full((sh, LANES), NEG_INF, F32)
            moved = (jnp.concatenate([cm[sh:], pad], axis=0) if reverse
                     else jnp.concatenate([pad, cm[:L - sh]], axis=0))
            cm = jnp.maximum(cm, moved)
        sh *= 2
    r8 = r.T[gf:gf + ML_HEADS]

    m_prev = m_ref[0:1, :]
    mm = jnp.maximum(m_prev, cm)
    m_t = cum + mm
    w_inter = jnp.exp(m_prev - mm)
    floor = jnp.exp(-m_t)
    mm2 = mm * LOG2E
    m_new = m_t[end:end + 1, :]
    shift2 = (cum[end:end + 1, :] - m_new) * LOG2E
    decay = jnp.exp(cum[end:end + 1, :] + m_prev - m_new)
    m_ref[0:1, :] = m_new

    ones_col = jnp.where(lax.broadcasted_iota(jnp.int32, (L, ML_DV), 1) == 0, 1.0, 0.0).astype(BF16)
    top, bot = slice(0, half), slice(half, L)
    dot = functools.partial(jnp.dot, preferred_element_type=F32)
    yield

    for hd in range(ML_HEADS):
        ln = gf + hd
        r2_row = r8[hd:hd + 1, :] * LOG2E

        def weights(tq, ks, masked):
            w = jnp.exp2(r2_row[:, ks] - mm2[tq, ln:ln + 1])
            return jnp.where(diag, w, 0.0) if masked else w

        qh = q_ref[:, hd * ML_DQK:(hd + 1) * ML_DQK]
        kth = kt_ref[hd * ML_DQK:(hd + 1) * ML_DQK, :]
        vext = jnp.concatenate([v_ref[:, hd * ML_DV:(hd + 1) * ML_DV], ones_col], axis=1)
        if reverse:
            s_top = dot(qh[top], kth) * jnp.concatenate([weights(top, top, True),
                                                         weights(top, bot, False)], axis=1)
            s_bot = dot(qh[bot], kth[:, bot]) * weights(bot, bot, True)
            intra = jnp.concatenate([dot(s_top.astype(BF16), vext),
                                     dot(s_bot.astype(BF16), vext[bot])], axis=0)
        else:
            s_top = dot(qh[top], kth[:, top]) * weights(top, top, True)
            s_bot = dot(qh[bot], kth) * jnp.concatenate([weights(bot, top, False),
                                                         weights(bot, bot, True)], axis=1)
            intra = jnp.concatenate([dot(s_top.astype(BF16), vext[top]),
                                     dot(s_bot.astype(BF16), vext)], axis=0)
        state = c_ref[hd]
        tot = w_inter[:, ln:ln + 1] * dot(qh, state.astype(BF16)) + intra
        den = tot[:, ML_DV:ML_DV + 1]
        h_ref[:, hd * ML_DV:(hd + 1) * ML_DV] = (
            tot[:, :ML_DV] / jnp.maximum(jnp.abs(den), floor[:, ln:ln + 1])).astype(h_ref.dtype)

        w_state = jnp.exp2(r2_row + shift2[:, ln:ln + 1])
        kw = (kth.astype(F32) * w_state).astype(BF16)
        c_ref[hd] = decay[:, ln:ln + 1] * state + dot(kw, vext)
        yield


def _mlstm_kernel(qf_ref, ktf_ref, vf_ref, gf_ref, qb_ref, ktb_ref, vb_ref, gb_ref, hf_ref, hb_ref,
                  cf_ref, mf_ref, cb_ref, mb_ref):
    @pl.when(pl.program_id(0) == 0)
    def _():
        for ref in (cf_ref, mf_ref, cb_ref, mb_ref):
            ref[...] = jnp.zeros_like(ref)

    fwd = _mlstm_chunk(False, qf_ref, ktf_ref, vf_ref, gf_ref, hf_ref, cf_ref, mf_ref)
    bwd = _mlstm_chunk(True, qb_ref, ktb_ref, vb_ref, gb_ref, hb_ref, cb_ref, mb_ref)
    for _ in zip(fwd, bwd):
        pass


def _mlstm(q, kt, v, gcol):
    t = q.shape[0]
    nt = t // TM
    back = lambda s: _scan_tile(True, s, nt)
    state = [pltpu.VMEM((ML_HEADS, ML_DQK, 2 * ML_DV), F32), pltpu.VMEM((SUBLANES, LANES), F32)]

    def specs(tile):
        return [pl.BlockSpec((TM, ML_QK_W), lambda s: (tile(s), 0)),
                pl.BlockSpec((ML_QK_W, TM), lambda s: (0, tile(s))),
                pl.BlockSpec((TM, ML_V_W), lambda s: (tile(s), 0)),
                pl.BlockSpec((TM, LANES), lambda s: (tile(s), 0))]

    return pl.pallas_call(
        _mlstm_kernel,
        grid=(nt,),
        in_specs=specs(lambda s: s) + specs(back),
        out_specs=[pl.BlockSpec((TM, ML_V_W), lambda s: (s, 0)),
                   pl.BlockSpec((TM, ML_V_W), lambda s: (back(s), 0))],
        out_shape=[jax.ShapeDtypeStruct((t, ML_V_W), BF16)] * 2,
        scratch_shapes=state + state,
        compiler_params=_params(),
        name="mlstm",
    )(q, kt, v, gcol, q, kt, v, gcol)


def _finish1_kernel(x_ref, hf_ref, hb_ref, o_ref, ng_ref, w_ref, gt_ref, g2_ref, sh_ref, sc_ref,
                    wr_ref, br_ref, x1_ref, h2_ref, lg_ref):
    hs = hf_ref[...].astype(F32) + hb_ref[...].astype(F32)
    parts = []
    for hd in range(ML_HEADS):
        blk = hs[:, hd * ML_DV:(hd + 1) * ML_DV]
        parts.append(blk * lax.rsqrt(jnp.mean(blk * blk, axis=-1, keepdims=True) + EPS))
    hn = jnp.concatenate(parts, axis=1) * ng_ref[...]
    y = jnp.dot((hn * _sigmoid(o_ref[...].astype(F32))).astype(BF16), w_ref[...],
                preferred_element_type=F32)
    x1 = x_ref[...] + gt_ref[0:1, :] * y
    x1_ref[...] = x1
    h2 = _rms(x1, g2_ref[...]) * (1.0 + sc_ref[0:1, :]) + sh_ref[0:1, :]
    _store_token_tiles(h2_ref, h2)
    lg_ref[...] = _router_logits(h2, wr_ref, br_ref)


def _finish1(x, hf, hb, o, norm_g, w_out_bf, mod, g2, wr, br):
    n, d = x.shape
    rows = n // GRID_W
    lat = lambda w: pl.BlockSpec((TM, w), lambda i: (i + 1, 0))
    modspec = lambda c: pl.BlockSpec((SUBLANES, d), lambda i: (0, c))
    return pl.pallas_call(
        _finish1_kernel,
        grid=(GRID_W,),
        in_specs=[pl.BlockSpec((rows, d), lambda i: (0, i)),
                  lat(ML_V_W), lat(ML_V_W), lat(d),
                  _full((1, ML_V_W)), _full((ML_V_W, d)),
                  modspec(2), _full((1, d)), modspec(3), modspec(4),
                  _full(wr.shape), _full(br.shape)],
        out_specs=_TOKEN_OUTS(d),
        out_shape=_token_out_shapes(n, d),
        compiler_params=_params(),
        name="finish1",
    )(x.reshape(rows, GRID_W * d), hf, hb, o, norm_g.reshape(1, -1), w_out_bf, mod,
      g2.reshape(1, -1), mod, mod, wr, br)


def kernel(x, c, ctx, c_ctx,
           l0_ada_w, l0_ada_b, l0_norm1_g, l0_norm2_g,
           l0_rg_w_in, l0_rg_conv_w, l0_rg_conv_b, l0_rg_w_a, l0_rg_b_a, l0_rg_w_x, l0_rg_b_x,
           l0_rg_lambda, l0_rg_w_out,
           l0_moe_w_grp, l0_moe_b_grp, l0_moe_w_exp, l0_moe_b_exp, l0_moe_w_gate, l0_moe_w_up,
           l0_moe_w_down,
           l1_ada_w, l1_ada_b, l1_norm1_g, l1_norm2_g,
           l1_ml_w_in, l1_ml_b_gates, l1_ml_norm_g, l1_ml_w_out,
           l1_moe_w_grp, l1_moe_b_grp, l1_moe_w_exp, l1_moe_b_exp, l1_moe_w_gate, l1_moe_w_up,
           l1_moe_w_down,
           final_norm_g):
    assert x.shape[0] == 1 and ctx.shape[1] == TM and x.shape[1] == GRID_W * TM
    xs, cs = x[0], ctx[0]
    d = xs.shape[1]
    cond8 = jnp.zeros((SUBLANES, d), F32).at[0].set(c[0]).at[1].set(c_ctx)

    mod0 = _adaln(cond8, l0_ada_w, l0_ada_b)
    conv_w8 = jnp.zeros((SUBLANES, D_RNN), F32).at[:CONV_W].set(l0_rg_conv_w)
    wcat = [jnp.concatenate([l0_rg_w_a[dr], l0_rg_w_x[dr]], axis=-1).astype(BF16) for dr in range(2)]
    gate, u, hf0 = _front0(xs, cs, l0_norm1_g, mod0, l0_rg_w_in.astype(BF16), conv_w8, l0_rg_conv_b,
                           wcat[0], l0_rg_b_a[0], l0_rg_b_x[0], l0_rg_lambda[0])
    wr0, br0 = _router_weights(l0_moe_w_grp, l0_moe_b_grp, l0_moe_w_exp, l0_moe_b_exp)
    x1, h2, logits = _back0(xs, cs, u, gate, hf0, wcat[1], l0_rg_b_a[1], l0_rg_b_x[1], l0_rg_lambda[1],
                            l0_rg_w_out.astype(BF16), mod0, l0_norm2_g, wr0, br0)
    meta, cnt = _route(logits, ROUTE_TILE_0)
    dest, ybuf = _moe(h2, meta, cnt, l0_moe_w_gate, l0_moe_w_up, l0_moe_w_down)
    x2, ctx2 = _combine0(x1, dest, meta, mod0, ybuf)

    mod1 = _adaln(cond8, l1_ada_w, l1_ada_b)
    q, kt, v, o, gcol = _inproj1(x2, ctx2, l1_norm1_g, mod1, l1_ml_w_in, l1_ml_b_gates)
    hf, hb = _mlstm(q, kt, v, gcol)
    wr1, br1 = _router_weights(l1_moe_w_grp, l1_moe_b_grp, l1_moe_w_exp, l1_moe_b_exp)
    x1, h2, logits = _finish1(x2, hf, hb, o, l1_ml_norm_g, l1_ml_w_out.astype(BF16), mod1,
                              l1_norm2_g, wr1, br1)
    meta, cnt = _route(logits, ROUTE_TILE_1)
    dest, ybuf = _moe(h2, meta, cnt, l1_moe_w_gate, l1_moe_w_up, l1_moe_w_down)
    out = _combine1(x1, dest, meta, mod1, final_norm_g, ybuf)
    return out[None]
```

```python
import functools

import jax
import jax.numpy as jnp
from jax import lax
from jax.experimental import pallas as pl
from jax.experimental.pallas import tpu as pltpu

D_MODEL = 1024
GRID_W = 64
N_MOD = 6
EPS = 1e-6

D_RNN = 1280
RG_BLOCKS = 10
RG_BLOCK_W = D_RNN // RG_BLOCKS
CONV_W = 4
CONV_PAD_L = 2
RG_C = 8.0

ML_HEADS = 8
ML_DQK = D_MODEL // (2 * ML_HEADS)
ML_DV = D_MODEL // ML_HEADS
ML_QK_W = ML_HEADS * ML_DQK
ML_V_W = ML_HEADS * ML_DV

N_GROUPS = 4
EXPERTS_PER_GROUP = 8
N_EXPERTS = N_GROUPS * EXPERTS_PER_GROUP
D_EXPERT = 512

TM = 256
TB = 256
ROUTE_TILE_0 = 1280
ROUTE_TILE_1 = 1024
SUBLANES = 8
LANES = 128
VMEM_LIMIT = 48 * 1024 * 1024

F32 = jnp.float32
BF16 = jnp.bfloat16
HI = lax.Precision.HIGHEST
NEG_INF = float("-inf")


def _params(n_axes=1):
    return pltpu.CompilerParams(dimension_semantics=("arbitrary",) * n_axes,
                                vmem_limit_bytes=VMEM_LIMIT)


def _rms(x, g):
    return x * lax.rsqrt(jnp.mean(x * x, axis=-1, keepdims=True) + EPS) * g


def _sigmoid(x):
    return 1.0 / (1.0 + jnp.exp(-x))


def _softplus(x):
    return jnp.maximum(x, 0.0) + jnp.log1p(jnp.exp(-jnp.abs(x)))


def _gelu_tanh(x):
    return 0.5 * x * (1.0 + jnp.tanh(0.7978845608028654 * (x + 0.044715 * (x * x * x))))


def _full(shape):
    return pl.BlockSpec(shape, lambda *_: (0,) * len(shape))


def _store_token_tiles(ref, x, first_row=0):
    per = x.shape[1] // LANES
    for c in range(per):
        ref[pl.ds(first_row * per + c, x.shape[0], stride=per), :] = x[:, c * LANES:(c + 1) * LANES]


def _load_token_tiles(ref, rows, d):
    per = d // LANES
    return jnp.concatenate([ref[pl.ds(c, rows, stride=per), :] for c in range(per)], axis=1)


def _token_tile(ref, row, d):
    per = d // LANES
    start = row * per if isinstance(row, int) else pl.multiple_of(row * per, per)
    return ref.at[pl.ds(start, per), :]


def _adaln_kernel(cond_ref, w_ref, b_ref, o_ref):
    c = cond_ref[...]
    s = c * _sigmoid(c)
    o_ref[...] = jnp.dot(s, w_ref[...], precision=HI, preferred_element_type=F32) + b_ref[...]


def _adaln(cond8, w, b):
    d = w.shape[0]
    return pl.pallas_call(
        _adaln_kernel,
        grid=(N_MOD,),
        in_specs=[_full((SUBLANES, d)),
                  pl.BlockSpec((d, d), lambda j: (0, j)),
                  pl.BlockSpec((1, d), lambda j: (0, j))],
        out_specs=pl.BlockSpec((SUBLANES, d), lambda j: (0, j)),
        out_shape=jax.ShapeDtypeStruct((SUBLANES, N_MOD * d), F32),
        compiler_params=_params(),
        name="adaln",
    )(cond8, w, b.reshape(1, -1))


def _mod_row(ref, is_ctx):
    return jnp.where(is_ctx, ref[1:2, :], ref[0:1, :])


def _front0_kernel(nt, x_ref, ctx_ref, g_ref, sh_ref, sc_ref, w_ref, cw_ref, cb_ref,
                   wcat_ref, ba_ref, bx_ref, lam_ref, gate_ref, u_ref, hf_ref,
                   ext_ref, ubuf_ref, a_ref, b_ref, o_ref, carry_ref):
    s = pl.program_id(0)
    is_ctx = s == 0

    @pl.when(s == 0)
    def _():
        ext_ref[...] = jnp.zeros_like(ext_ref)
        ubuf_ref[...] = jnp.zeros_like(ubuf_ref)

    @pl.when(s <= 2)
    def _():
        carry_ref[...] = jnp.zeros_like(carry_ref)

    xin = jnp.where(is_ctx, ctx_ref[...], x_ref[...])
    h = _rms(xin, g_ref[...]) * (1.0 + _mod_row(sc_ref, is_ctx)) + _mod_row(sh_ref, is_ctx)
    hb = h.astype(BF16)

    right_valid = jnp.logical_and(s >= 2, s <= nt - 1)
    base = SUBLANES - CONV_PAD_L
    for j in range(RG_BLOCKS):
        ln = slice(j * RG_BLOCK_W, (j + 1) * RG_BLOCK_W)
        _rglru_block(False, j, ubuf_ref.at[s % 2], wcat_ref, ba_ref, bx_ref, lam_ref, hf_ref,
                     a_ref, b_ref, o_ref, carry_ref)
        rc = slice(D_RNN + j * RG_BLOCK_W, D_RNN + (j + 1) * RG_BLOCK_W)
        w_j = jnp.concatenate([w_ref[:, ln], w_ref[:, rc]], axis=1)
        p = jnp.dot(hb, w_j, preferred_element_type=F32)
        gate_ref[:, ln] = p[:, :RG_BLOCK_W]
        rec = p[:, RG_BLOCK_W:]
        ext_ref[j, SUBLANES + TM:, :] = jnp.where(right_valid, rec[0:SUBLANES], 0.0)
        u = cb_ref[:, ln] + ext_ref[j, pl.ds(base, TM), :] * cw_ref[0:1, ln]
        for k in range(1, CONV_W):
            u = u + ext_ref[j, pl.ds(base + k, TM), :] * cw_ref[k:k + 1, ln]
        u_ref[:, ln] = u
        ubuf_ref[(s + 1) % 2, :, ln] = u
        ext_ref[j, 0:SUBLANES, :] = jnp.where(s >= 2, ext_ref[j, TM:TM + SUBLANES, :], 0.0)
        ext_ref[j, SUBLANES:SUBLANES + TM, :] = rec


_SCAN_SLAB = lambda: pltpu.VMEM((RG_BLOCKS, SUBLANES * SCAN_PITCH, LANES), F32)
_SCAN_SCRATCH = lambda: [_SCAN_SLAB(), _SCAN_SLAB(), _SCAN_SLAB(), pltpu.VMEM((SUBLANES, D_RNN), F32)]


def _front0(x, ctx, g1, mod, w_in_bf, conv_w8, conv_b, wcat, b_a, b_x, lam):
    n, d = x.shape
    nx = n // TM
    nt = nx + 1
    t = nt * TM
    row = lambda v: v.reshape(1, -1)
    return pl.pallas_call(
        functools.partial(_front0_kernel, nt),
        grid=(nt + 2,),
        in_specs=[pl.BlockSpec((TM, d), lambda s: (jnp.clip(s - 1, 0, nx - 1), 0)),
                  _full((TM, d)),
                  _full((1, d)),
                  pl.BlockSpec((SUBLANES, d), lambda s: (0, 0)),
                  pl.BlockSpec((SUBLANES, d), lambda s: (0, 1)),
                  _full((d, 2 * D_RNN)),
                  _full((SUBLANES, D_RNN)), _full((1, D_RNN)),
                  _full((RG_BLOCKS, RG_BLOCK_W, 2 * RG_BLOCK_W)),
                  _full((1, D_RNN)), _full((1, D_RNN)), _full((1, D_RNN))],
        out_specs=[pl.BlockSpec((TM, D_RNN), lambda s: (jnp.minimum(s, nt - 1), 0)),
                   pl.BlockSpec((TM, D_RNN), lambda s: (jnp.maximum(s - 1, 0), 0)),
                   pl.BlockSpec((TM, D_RNN), lambda s: (jnp.maximum(s - 2, 0), 0))],
        out_shape=[jax.ShapeDtypeStruct((t, D_RNN), F32),
                   jax.ShapeDtypeStruct((t + TM, D_RNN), F32),
                   jax.ShapeDtypeStruct((t, D_RNN), F32)],
        scratch_shapes=[pltpu.VMEM((RG_BLOCKS, TM + 2 * SUBLANES, LANES), F32),
                        pltpu.VMEM((2, TM, D_RNN), F32)] + _SCAN_SCRATCH(),
        compiler_params=_params(),
        name="front0",
    )(x, ctx, g1.reshape(1, -1), mod, mod, w_in_bf, conv_w8, row(conv_b), wcat, row(b_a), row(b_x),
      row(lam))


def _scan_tile(reverse, s, nt):
    if not reverse:
        return s
    return jnp.where(s == 0, 0, nt - s)


SCAN_CHUNK = TM // SUBLANES
SCAN_PITCH = SCAN_CHUNK + 4


def _rglru_block(reverse, j, u_ref, wcat_ref, ba_ref, bx_ref, lam_ref, h_ref,
                 a_ref, b_ref, o_ref, carry_ref):
    steps = range(SCAN_CHUNK - 1, -1, -1) if reverse else range(SCAN_CHUNK)
    chunks = range(SUBLANES - 1, -1, -1) if reverse else range(SUBLANES)
    ln = slice(j * RG_BLOCK_W, (j + 1) * RG_BLOCK_W)
    u = u_ref[:, ln]
    g = jnp.dot(u.astype(BF16), wcat_ref[j], preferred_element_type=F32)
    half_rate = (-0.5 * RG_C) * _softplus(-lam_ref[:, ln])
    log_a = jnp.tanh(0.5 * (g[:, :RG_BLOCK_W] + ba_ref[:, ln])) * half_rate + half_rate
    ig = 0.5 * jnp.tanh(0.5 * (g[:, RG_BLOCK_W:] + bx_ref[:, ln])) + 0.5
    a = jnp.exp(log_a)
    b = jnp.sqrt(-jnp.tanh(log_a) * (a * a + 1.0)) * ig * u
    for c in range(SUBLANES):
        a_ref[j, pl.ds(c * SCAN_PITCH, SCAN_CHUNK), :] = a[c * SCAN_CHUNK:(c + 1) * SCAN_CHUNK]
        b_ref[j, pl.ds(c * SCAN_PITCH, SCAN_CHUNK), :] = b[c * SCAN_CHUNK:(c + 1) * SCAN_CHUNK]

    row = lambda ref, i: ref[j, pl.ds(i, SUBLANES, stride=SCAN_PITCH), :]
    end = jnp.zeros((SUBLANES, LANES), F32)
    decay = jnp.ones((SUBLANES, LANES), F32)
    for i in steps:
        ai = row(a_ref, i)
        end = ai * end + row(b_ref, i)
        decay = decay * ai

    state = carry_ref[0:1, ln]
    entry = [None] * SUBLANES
    for c in chunks:
        entry[c] = state
        state = decay[c:c + 1] * state + end[c:c + 1]
    carry_ref[0:1, ln] = state

    hcur = jnp.concatenate(entry, axis=0)
    for i in steps:
        hcur = row(a_ref, i) * hcur + row(b_ref, i)
        o_ref[j, pl.ds(i, SUBLANES, stride=SCAN_PITCH), :] = hcur
    for c in range(SUBLANES):
        h_ref[c * SCAN_CHUNK:(c + 1) * SCAN_CHUNK, ln] = o_ref[j, pl.ds(c * SCAN_PITCH, SCAN_CHUNK), :]


def _split_bf16(w):
    hi = w.astype(BF16)
    return jnp.stack([hi, (w - hi.astype(F32)).astype(BF16)])


def _dot_bf16x3(x, w_ref):
    hi = x.astype(BF16)
    lo = (x - hi.astype(F32)).astype(BF16)
    acc = jnp.dot(hi, w_ref[0], preferred_element_type=F32)
    acc = acc + jnp.dot(lo, w_ref[0], preferred_element_type=F32)
    return acc + jnp.dot(hi, w_ref[1], preferred_element_type=F32)


def _router_logits(h2, wr_ref, br_ref):
    return _dot_bf16x3(h2, wr_ref) + br_ref[...]


def _route_kernel(lg_ref, meta_ref, cnt_ref, carry_ref):
    step = pl.program_id(0)
    tm = lg_ref.shape[0]

    @pl.when(step == 0)
    def _():
        carry_ref[...] = jnp.zeros_like(carry_ref)

    logits = jnp.concatenate([lg_ref[i * LANES:(i + 1) * LANES, :].T for i in range(tm // LANES)],
                             axis=1)
    row8 = lax.broadcasted_iota(jnp.int32, (SUBLANES, tm), 0)
    grp_logits = jnp.where(row8 < N_GROUPS, logits[0:SUBLANES], NEG_INF)
    gmax = jnp.max(grp_logits, axis=0, keepdims=True)
    p_sel = 1.0 / jnp.sum(jnp.exp(grp_logits - gmax), axis=0, keepdims=True)
    grp = jnp.min(jnp.where(grp_logits == gmax, row8, SUBLANES), axis=0, keepdims=True)

    in_grp = logits[SUBLANES + (N_GROUPS - 1) * EXPERTS_PER_GROUP:SUBLANES + N_GROUPS * EXPERTS_PER_GROUP]
    for gi in range(N_GROUPS - 2, -1, -1):
        lo = SUBLANES + gi * EXPERTS_PER_GROUP
        in_grp = jnp.where(grp == gi, logits[lo:lo + EXPERTS_PER_GROUP], in_grp)
    v1 = jnp.max(in_grp, axis=0, keepdims=True)
    i1 = jnp.min(jnp.where(in_grp == v1, row8, EXPERTS_PER_GROUP), axis=0, keepdims=True)
    rest = jnp.where(row8 == i1, NEG_INF, in_grp)
    v2 = jnp.max(rest, axis=0, keepdims=True)
    i2 = jnp.min(jnp.where(rest == v2, row8, EXPERTS_PER_GROUP), axis=0, keepdims=True)
    e2 = jnp.exp(v2 - v1)
    w1 = p_sel / (1.0 + e2)
    w2 = p_sel * e2 / (1.0 + e2)
    eid = (grp * EXPERTS_PER_GROUP + i1, grp * EXPERTS_PER_GROUP + i2)

    rr = lax.broadcasted_iota(jnp.int32, (LANES, LANES), 0)
    cc = lax.broadcasted_iota(jnp.int32, (LANES, LANES), 1)
    strict_upper = jnp.where(rr < cc, 1.0, 0.0).astype(BF16)
    erow = lax.broadcasted_iota(jnp.int32, (N_EXPERTS, LANES), 0)
    base = carry_ref[:, 0:1]
    for k in range(2):
        for i in range(tm // LANES):
            ln = slice(i * LANES, (i + 1) * LANES)
            onehot = jnp.where(erow == eid[k][:, ln], 1.0, 0.0)
            pre = jnp.dot(onehot.astype(BF16), strict_upper, preferred_element_type=F32)
            meta_ref[2 + k:3 + k, ln] = jnp.sum(onehot * (base + pre), axis=0, keepdims=True)
            base = base + jnp.sum(onehot, axis=1, keepdims=True)
    carry_ref[...] = jnp.broadcast_to(base, carry_ref.shape)
    cnt_ref[...] = jnp.broadcast_to(base, cnt_ref.shape)

    meta_ref[0:1, :] = eid[0].astype(F32)
    meta_ref[1:2, :] = eid[1].astype(F32)
    meta_ref[4:5, :] = w1
    meta_ref[5:6, :] = w2
    meta_ref[6:8, :] = jnp.zeros((2, tm), F32)


def _router_weights(w_grp, b_grp, w_exp, b_exp):
    d = w_grp.shape[0]
    wr = jnp.zeros((d, LANES), F32).at[:, :N_GROUPS].set(w_grp)
    wr = wr.at[:, SUBLANES:SUBLANES + N_EXPERTS].set(w_exp)
    br = jnp.zeros((1, LANES), F32).at[0, :N_GROUPS].set(b_grp)
    br = br.at[0, SUBLANES:SUBLANES + N_EXPERTS].set(b_exp)
    return _split_bf16(wr), br


def _route(logits, tile):
    t = logits.shape[0]
    return pl.pallas_call(
        _route_kernel,
        grid=(t // tile,),
        in_specs=[pl.BlockSpec((tile, LANES), lambda i: (i, 0))],
        out_specs=[pl.BlockSpec((SUBLANES, tile), lambda i: (0, i)), _full((N_EXPERTS, LANES))],
        out_shape=[jax.ShapeDtypeStruct((SUBLANES, t), F32),
                   jax.ShapeDtypeStruct((N_EXPERTS, LANES), F32)],
        scratch_shapes=[pltpu.VMEM((N_EXPERTS, LANES), F32)],
        compiler_params=_params(),
        name="moe_route",
    )(logits)


def _back0_kernel(nt, x_ref, ctx_ref, u_ref, gate_ref, hf_ref, wcat_ref, ba_ref, bx_ref, lam_ref,
                  w_ref, gt_ref, g2_ref, sh_ref, sc_ref, wr_ref, br_ref, x1_ref, h2_ref, lg_ref,
                  hb_ref, a_ref, b_ref, o_ref, carry_ref):
    s = pl.program_id(0)
    is_ctx = s == 0

    @pl.when(s == 0)
    def _():
        carry_ref[...] = jnp.zeros_like(carry_ref)

    for j in range(RG_BLOCKS):
        _rglru_block(True, j, u_ref, wcat_ref, ba_ref, bx_ref, lam_ref, hb_ref,
                     a_ref, b_ref, o_ref, carry_ref)
    xin = jnp.where(is_ctx, ctx_ref[...], x_ref[...])
    y = _gelu_tanh(gate_ref[...]) * (hf_ref[...] + hb_ref[...])
    y = jnp.dot(y.astype(BF16), w_ref[...], preferred_element_type=F32)
    x1 = xin + _mod_row(gt_ref, is_ctx) * y
    x1_ref[...] = x1
    h2 = _rms(x1, g2_ref[...]) * (1.0 + _mod_row(sc_ref, is_ctx)) + _mod_row(sh_ref, is_ctx)
    _store_token_tiles(h2_ref, h2)
    lg_ref[...] = _router_logits(h2, wr_ref, br_ref)


_TOKEN_OUTS = lambda d: [pl.BlockSpec((TM, d), lambda i: (i, 0)),
                         pl.BlockSpec((TM * d // LANES, LANES), lambda i: (i, 0)),
                         pl.BlockSpec((TM, LANES), lambda i: (i, 0))]


def _token_out_shapes(t, d):
    return [jax.ShapeDtypeStruct((t, d), F32), jax.ShapeDtypeStruct((t * d // LANES, LANES), F32),
            jax.ShapeDtypeStruct((t, LANES), F32)]


def _back0(x, ctx, u, gate, hf, wcat, b_a, b_x, lam, w_out_bf, mod, g2, wr, br):
    n, d = x.shape
    nt = n // TM + 1
    t = nt * TM
    tile = lambda s: _scan_tile(True, s, nt)
    row = lambda v: v.reshape(1, -1)
    tok = pl.BlockSpec((TM, D_RNN), lambda s: (tile(s), 0))
    modspec = lambda c: pl.BlockSpec((SUBLANES, d), lambda s: (0, c))
    return pl.pallas_call(
        functools.partial(_back0_kernel, nt),
        grid=(nt,),
        in_specs=[pl.BlockSpec((TM, d), lambda s: (jnp.maximum(tile(s) - 1, 0), 0)),
                  _full((TM, d)),
                  tok, tok, tok,
                  _full((RG_BLOCKS, RG_BLOCK_W, 2 * RG_BLOCK_W)),
                  _full((1, D_RNN)), _full((1, D_RNN)), _full((1, D_RNN)),
                  _full((D_RNN, d)),
                  modspec(2), _full((1, d)), modspec(3), modspec(4),
                  _full(wr.shape), _full(br.shape)],
        out_specs=[pl.BlockSpec((TM, d), lambda s: (tile(s), 0)),
                   pl.BlockSpec((TM * d // LANES, LANES), lambda s: (tile(s), 0)),
                   pl.BlockSpec((TM, LANES), lambda s: (tile(s), 0))],
        out_shape=_token_out_shapes(t, d),
        scratch_shapes=[pltpu.VMEM((TM, D_RNN), F32)] + _SCAN_SCRATCH(),
        compiler_params=_params(),
        name="back0",
    )(x, ctx, u, gate, hf, wcat, row(b_a), row(b_x), row(lam), w_out_bf, mod, g2.reshape(1, -1),
      mod, mod, wr, br)


def _plan_kernel(n_tiles, nbp, cnt_ref, meta_ref, dest_ref, blk_ref):
    c = cnt_ref[...]
    padded = jnp.floor((c + (TB - 1)) * (1.0 / TB)) * TB
    r = lax.broadcasted_iota(jnp.int32, (N_EXPERTS, N_EXPERTS), 0)
    q = lax.broadcasted_iota(jnp.int32, (N_EXPERTS, N_EXPERTS), 1)
    lower = jnp.where(q <= r, 1.0, 0.0)
    pad_end = jnp.dot(lower, padded, precision=HI, preferred_element_type=F32)
    pad_start = pad_end - padded

    first_row = lax.broadcasted_iota(jnp.int32, (N_EXPERTS, nbp), 1).astype(F32) * TB
    owner = jnp.sum(jnp.where(pad_end[:, 0:1] <= first_row, 1.0, 0.0), axis=0, keepdims=True)
    blk_ref[0:1, :] = jnp.minimum(owner, N_EXPERTS - 1).astype(jnp.int32)
    n_used = pad_end[N_EXPERTS - 1:N_EXPERTS, 0:1] * (1.0 / TB)
    blk_ref[1:2, :] = jnp.broadcast_to(n_used, (1, nbp)).astype(jnp.int32)
    ends = jnp.concatenate([pad_end, jnp.zeros((LANES - N_EXPERTS, LANES), F32)], axis=0).T[0:1, :]
    blk_ref[2:3, :] = jnp.concatenate([ends, jnp.zeros((1, nbp - LANES), F32)], axis=1).astype(jnp.int32)
    blk_ref[3:SUBLANES, :] = jnp.zeros((SUBLANES - 3, nbp), jnp.int32)

    erow = lax.broadcasted_iota(jnp.int32, (N_EXPERTS, TM), 0).astype(F32)

    def body(i, carry):
        ln = pl.ds(pl.multiple_of(i * TM, TM), TM)
        for k in range(2):
            onehot = jnp.where(erow == meta_ref[k:k + 1, ln], 1.0, 0.0)
            start = jnp.sum(onehot * pad_start[:, 0:1], axis=0, keepdims=True)
            dest_ref[k:k + 1, ln] = (start + meta_ref[2 + k:3 + k, ln]).astype(jnp.int32)
        dest_ref[2:SUBLANES, ln] = jnp.zeros((SUBLANES - 2, TM), jnp.int32)
        return carry

    lax.fori_loop(0, n_tiles, body, 0)


def _plan(cnt, meta):
    t = meta.shape[1]
    n_blocks = (2 * t + N_EXPERTS * TB) // TB
    nbp = -(-n_blocks // LANES) * LANES
    vm = pl.BlockSpec(memory_space=pltpu.VMEM)
    dest, blk = pl.pallas_call(
        functools.partial(_plan_kernel, t // TM, nbp),
        in_specs=[vm, vm],
        out_specs=[vm, vm],
        out_shape=[jax.ShapeDtypeStruct((SUBLANES, t), jnp.int32),
                   jax.ShapeDtypeStruct((SUBLANES, nbp), jnp.int32)],
        compiler_params=pltpu.CompilerParams(vmem_limit_bytes=VMEM_LIMIT),
        name="moe_plan",
    )(cnt, meta)
    return dest, blk, n_blocks


def _row_wait(src_ref, dst_ref, d, sem):
    n = TM * d // LANES
    pltpu.make_async_copy(src_ref.at[pl.ds(0, n), :], dst_ref.at[pl.ds(0, n), :], sem).wait()


DISPATCH_RING = 3


def _dispatch_kernel(n_blocks, d, pe_ref, nu_ref, dest_ref, h2_ref, buf_ref, dsm_ref, zero_ref, src_ref,
                     sem_idx, sem_zero, sem_src, sem_rows):
    step = pl.program_id(0)
    nt = pl.num_programs(0)
    blk = TB * d // LANES
    tile = TM * d // LANES

    def src_copy(i):
        slot = i % DISPATCH_RING
        return pltpu.make_async_copy(h2_ref.at[pl.ds(pl.multiple_of(i * tile, tile), tile), :],
                                     src_ref.at[slot], sem_src.at[slot])

    def zero_block(first_row):
        start = pl.multiple_of(first_row * (d // LANES), blk)
        return pltpu.make_async_copy(zero_ref, buf_ref.at[pl.ds(start, blk), :], sem_zero)

    @pl.when(step == 0)
    def _():
        zero_ref[...] = jnp.zeros_like(zero_ref)

        def for_segments(fn):
            for e in range(N_EXPERTS):
                seg_start = pe_ref[e - 1] if e else 0

                @pl.when(pe_ref[e] > seg_start)
                def _():
                    fn(zero_block(pe_ref[e] - TB))

        for_segments(lambda cp: cp.start())
        lax.fori_loop(nu_ref[0], n_blocks, lambda b, c: (zero_block(b * TB).start(), c)[1], 0)
        for_segments(lambda cp: cp.wait())
        lax.fori_loop(nu_ref[0], n_blocks, lambda b, c: (zero_block(b * TB).wait(), c)[1], 0)
        src_copy(0).start()

    @pl.when(step + 1 < nt)
    def _():
        src_copy(step + 1).start()

    cp = pltpu.make_async_copy(dest_ref, dsm_ref, sem_idx)
    cp.start()
    cp.wait()
    src_copy(step).wait()
    src = src_ref.at[step % DISPATCH_RING]
    sem = sem_rows.at[step % 2]

    for r in range(TM):
        for k in range(2):
            pltpu.make_async_copy(_token_tile(src, r, d), _token_tile(buf_ref, dsm_ref[k, r], d),
                                  sem).start(priority=k)

    @pl.when(step > 0)
    def _():
        for k in range(2):
            _row_wait(src, buf_ref, d, sem_rows.at[(step + 1) % 2])

    @pl.when(step == nt - 1)
    def _():
        for k in range(2):
            _row_wait(src, buf_ref, d, sem)


def _dispatch(h2t, dest, pad_end, n_used, n_blocks, d):
    per = d // LANES
    t = h2t.shape[0] // per
    grid_spec = pltpu.PrefetchScalarGridSpec(
        num_scalar_prefetch=2,
        grid=(t // TM,),
        in_specs=[pl.BlockSpec((SUBLANES, TM), lambda i, pe, nu: (0, i)),
                  pl.BlockSpec(memory_space=pl.ANY)],
        out_specs=pl.BlockSpec(memory_space=pl.ANY),
        scratch_shapes=[pltpu.SMEM((SUBLANES, TM), jnp.int32),
                        pltpu.VMEM((TB * per, LANES), F32),
                        pltpu.VMEM((DISPATCH_RING, TM * per, LANES), F32),
                        pltpu.SemaphoreType.DMA, pltpu.SemaphoreType.DMA,
                        pltpu.SemaphoreType.DMA((DISPATCH_RING,)),
                        pltpu.SemaphoreType.DMA((2,))],
    )
    return pl.pallas_call(
        functools.partial(_dispatch_kernel, n_blocks, d),
        grid_spec=grid_spec,
        out_shape=jax.ShapeDtypeStruct((n_blocks * TB * per, LANES), F32),
        compiler_params=_params(),
        name="moe_dispatch",
    )(pad_end, n_used, dest, h2t)


EXPERT_PAIR = 2
EXPERT_RING = 2 * EXPERT_PAIR


def _experts_kernel(be_ref, nu_ref, buf_ref, wg_ref, wu_ref, wd_ref, y_ref, wg_bf, wu_bf, wd_bf,
                    wg_f32, wu_f32, wd_f32, x_ref, st_ref, sems, wsems):
    step = pl.program_id(0)
    n_used = nu_ref[0]
    d = wg_bf.shape[0]
    blk = TB * d // LANES
    first = step * EXPERT_PAIR

    def weight_copies(expert, slot):
        return [pltpu.make_async_copy(src.at[expert], dst.at[slot], wsems.at[slot])
                for src, dst in ((wg_ref, wg_f32), (wu_ref, wu_f32), (wd_ref, wd_f32))]

    @pl.when(step == 0)
    def _():
        st_ref[0] = -1
        st_ref[1] = 0
        for cp in weight_copies(be_ref[0], 0):
            cp.start()

    def block_copy(b):
        slot = b % EXPERT_RING
        return pltpu.make_async_copy(buf_ref.at[pl.ds(pl.multiple_of(b * blk, blk), blk), :],
                                     x_ref.at[slot], sems.at[slot])

    for h in range(EXPERT_PAIR):
        @pl.when(jnp.logical_and(step == 0, h < n_used))
        def _():
            block_copy(h).start()

        @pl.when(first + EXPERT_PAIR + h < n_used)
        def _():
            block_copy(first + EXPERT_PAIR + h).start()

    for h in range(EXPERT_PAIR):
        b = first + h
        used = b < n_used
        expert = be_ref[jnp.clip(b, 0, n_used - 1)]

        @pl.when(jnp.logical_and(used, expert != st_ref[0]))
        def _():
            slot = st_ref[1]
            for cp in weight_copies(expert, slot):
                cp.wait()
            wg_bf[...] = wg_f32[slot].astype(BF16)
            wu_bf[...] = wu_f32[slot].astype(BF16)
            wd_bf[...] = wd_f32[slot].astype(BF16)
            st_ref[0] = expert
            seg_end = lax.while_loop(
                lambda j: jnp.logical_and(j < n_used, be_ref[jnp.minimum(j, n_used - 1)] == expert),
                lambda j: j + 1, b + 1)

            @pl.when(seg_end < n_used)
            def _():
                nxt = be_ref[jnp.minimum(seg_end, n_used - 1)]
                st_ref[1] = 1 - slot
                for cp in weight_copies(nxt, 1 - slot):
                    cp.start()

        @pl.when(used)
        def _():
            block_copy(b).wait()
            xb = _load_token_tiles(x_ref.at[b % EXPERT_RING], TB, d).astype(BF16)
            g = jnp.dot(xb, wg_bf[...], preferred_element_type=F32)
            u = jnp.dot(xb, wu_bf[...], preferred_element_type=F32)
            a = (g * _sigmoid(g)) * u
            _store_token_tiles(y_ref, jnp.dot(a.astype(BF16), wd_bf[...], preferred_element_type=F32),
                               first_row=h * TB)

        @pl.when(jnp.logical_not(used))
        def _():
            y_ref[pl.ds(h * blk, blk), :] = jnp.zeros((blk, LANES), F32)


def _experts(buf, blk_e, n_used, wg, wu, wd, n_blocks):
    d = wg.shape[1]
    blk = TB * d // LANES
    assert n_blocks % EXPERT_PAIR == 0
    hbm = pl.BlockSpec(memory_space=pl.ANY)
    grid_spec = pltpu.PrefetchScalarGridSpec(
        num_scalar_prefetch=2,
        grid=(n_blocks // EXPERT_PAIR,),
        in_specs=[hbm, hbm, hbm, hbm],
        out_specs=pl.BlockSpec((EXPERT_PAIR * blk, LANES), lambda i, be, nu: (i, 0)),
        scratch_shapes=[pltpu.VMEM((d, D_EXPERT), BF16), pltpu.VMEM((d, D_EXPERT), BF16),
                        pltpu.VMEM((D_EXPERT, d), BF16),
                        pltpu.VMEM((2, d, D_EXPERT), F32), pltpu.VMEM((2, d, D_EXPERT), F32),
                        pltpu.VMEM((2, D_EXPERT, d), F32),
                        pltpu.VMEM((EXPERT_RING, blk, LANES), F32),
                        pltpu.SMEM((2,), jnp.int32),
                        pltpu.SemaphoreType.DMA((EXPERT_RING,)),
                        pltpu.SemaphoreType.DMA((2,))],
    )
    return pl.pallas_call(
        _experts_kernel,
        grid_spec=grid_spec,
        out_shape=jax.ShapeDtypeStruct(buf.shape, F32),
        compiler_params=_params(),
        name="moe_experts",
    )(blk_e, n_used, buf, wg, wu, wd)


def _gather_start(dest_ref, dsm_ref, ybuf_ref, rows_ref, slot, d, sem_idx, sem):
    cp = pltpu.make_async_copy(dest_ref, dsm_ref, sem_idx)
    cp.start()
    cp.wait()
    for r in range(TM):
        for k in range(2):
            pltpu.make_async_copy(_token_tile(ybuf_ref, dsm_ref[k, r], d),
                                  _token_tile(rows_ref.at[slot, k], r, d), sem).start(priority=k)


GATHER_AHEAD = 2
GATHER_RING = GATHER_AHEAD + 1


def _gathered_rows(dest_refs, dsm_ref, ybuf_ref, rows_ref, d, sem_idx, sems):
    step = pl.program_id(0)
    nt = pl.num_programs(0)
    slot = step % GATHER_RING

    for j in range(GATHER_AHEAD):
        @pl.when(jnp.logical_and(step == 0, j < nt))
        def _():
            _gather_start(dest_refs[j], dsm_ref, ybuf_ref, rows_ref, j, d, sem_idx, sems.at[j])

    @pl.when(step + GATHER_AHEAD < nt)
    def _():
        ahead = (step + GATHER_AHEAD) % GATHER_RING
        _gather_start(dest_refs[GATHER_AHEAD], dsm_ref, ybuf_ref, rows_ref, ahead, d, sem_idx,
                      sems.at[ahead])

    for k in range(2):
        _row_wait(ybuf_ref, rows_ref.at[slot, k], d, sems.at[slot])
    return [_load_token_tiles(rows_ref.at[slot, k], TM, d) for k in range(2)]


def _token_weights(meta_ref):
    meta = jnp.concatenate([meta_ref[...], jnp.zeros((LANES - SUBLANES, TM), F32)], axis=0)
    mt = meta.T
    return mt[:, 4:5], mt[:, 5:6]


_COMBINE_SCRATCH = lambda d: [pltpu.SMEM((SUBLANES, TM), jnp.int32),
                              pltpu.VMEM((GATHER_RING, 2, TM * d // LANES, LANES), F32),
                              pltpu.SemaphoreType.DMA, pltpu.SemaphoreType.DMA((GATHER_RING,))]


def _dest_specs(nt):
    return [pl.BlockSpec((SUBLANES, TM), functools.partial(lambda j, i: (0, jnp.minimum(i + j, nt - 1)), j))
            for j in range(GATHER_RING)]


def _combine0_kernel(x1_ref, *refs):
    dest_refs = refs[:GATHER_RING]
    meta_ref, gt_ref, ybuf_ref, x2_ref, ctx2_ref, dsm_ref, rows_ref, sem_idx, sems = refs[GATHER_RING:]
    is_ctx = pl.program_id(0) == 0
    y0, y1 = _gathered_rows(dest_refs, dsm_ref, ybuf_ref, rows_ref, x1_ref.shape[1], sem_idx, sems)
    w0, w1 = _token_weights(meta_ref)
    out = x1_ref[...] + _mod_row(gt_ref, is_ctx) * (w0 * y0 + w1 * y1)
    x2_ref[...] = out

    @pl.when(is_ctx)
    def _():
        ctx2_ref[...] = out


def _combine0(x1, dest, meta, mod, ybuf):
    t, d = x1.shape
    nt = t // TM
    return pl.pallas_call(
        _combine0_kernel,
        grid=(nt,),
        in_specs=[pl.BlockSpec((TM, d), lambda i: (i, 0))] + _dest_specs(nt) + [
                  pl.BlockSpec((SUBLANES, TM), lambda i: (0, i)),
                  pl.BlockSpec((SUBLANES, d), lambda i: (0, 5)),
                  pl.BlockSpec(memory_space=pl.ANY)],
        out_specs=[pl.BlockSpec((TM, d), lambda i: (jnp.maximum(i - 1, 0), 0)),
                   _full((TM, d))],
        out_shape=[jax.ShapeDtypeStruct((t - TM, d), F32), jax.ShapeDtypeStruct((TM, d), F32)],
        scratch_shapes=_COMBINE_SCRATCH(d),
        compiler_params=_params(),
        name="moe_combine0",
    )(x1, *([dest] * GATHER_RING), meta, mod, ybuf)


def _combine1_kernel(x1_ref, *refs):
    dest_refs = refs[:GATHER_RING]
    meta_ref, gt_ref, gf_ref, ybuf_ref, o_ref, dsm_ref, rows_ref, sem_idx, sems = refs[GATHER_RING:]
    y0, y1 = _gathered_rows(dest_refs, dsm_ref, ybuf_ref, rows_ref, x1_ref.shape[1], sem_idx, sems)
    w0, w1 = _token_weights(meta_ref)
    out = x1_ref[...] + gt_ref[0:1, :] * (w0 * y0 + w1 * y1)
    o_ref[...] = _rms(out, gf_ref[...])


def _combine1(x1, dest, meta, mod, gf, ybuf):
    t, d = x1.shape
    rows = t // GRID_W
    nt = t // TM
    out = pl.pallas_call(
        _combine1_kernel,
        grid=(nt,),
        in_specs=[pl.BlockSpec((TM, d), lambda i: (i, 0))] + _dest_specs(nt) + [
                  pl.BlockSpec((SUBLANES, TM), lambda i: (0, i)),
                  pl.BlockSpec((SUBLANES, d), lambda i: (0, 5)),
                  _full((1, d)),
                  pl.BlockSpec(memory_space=pl.ANY)],
        out_specs=pl.BlockSpec((TM, d), lambda i: (0, i)),
        out_shape=jax.ShapeDtypeStruct((rows, GRID_W * d), F32),
        scratch_shapes=_COMBINE_SCRATCH(d),
        compiler_params=_params(),
        name="moe_combine1",
    )(x1, *([dest] * GATHER_RING), meta, mod, gf.reshape(1, -1), ybuf)
    return out.reshape(t, d)


def _moe(h2t, meta, cnt, w_gate, w_up, w_down):
    dest, blk, n_blocks = _plan(cnt, meta)
    n_used = blk[1, :1]
    buf = _dispatch(h2t, dest, blk[2, :N_EXPERTS], n_used, n_blocks, w_gate.shape[1])
    ybuf = _experts(buf, blk[0], n_used, w_gate, w_up, w_down, n_blocks)
    return dest, ybuf


def _inproj1_kernel(x_ref, ctx_ref, g_ref, sh_ref, sc_ref, wq_ref, wkt_ref, wv_ref, wo_ref,
                    wg_ref, bg_ref, q_ref, kt_ref, v_ref, o_ref, gcol_ref):
    is_ctx = pl.program_id(0) == 0
    xin = jnp.where(is_ctx, ctx_ref[...], x_ref[...])
    h = _rms(xin, g_ref[...]) * (1.0 + _mod_row(sc_ref, is_ctx)) + _mod_row(sh_ref, is_ctx)
    hb = h.astype(BF16)
    q = jnp.dot(hb, wq_ref[...], preferred_element_type=F32) * (ML_DQK ** -0.5)
    q_ref[...] = q.astype(BF16)
    kt = lax.dot_general(wkt_ref[...], hb, (((1,), (1,)), ((), ())), preferred_element_type=F32)
    kt_ref[...] = kt.astype(BF16)
    v_ref[...] = jnp.dot(hb, wv_ref[...], preferred_element_type=F32).astype(BF16)
    o_ref[...] = jnp.dot(hb, wo_ref[...], preferred_element_type=F32).astype(BF16)
    gcol_ref[...] = _dot_bf16x3(h, wg_ref) + bg_ref[...]


def _inproj1(x, ctx, g1, mod, w_in, b_gates):
    n, d = x.shape
    rows = n // GRID_W
    nt = GRID_W + 1
    t = nt * TM
    ng = 4 * ML_HEADS
    wq = w_in[:, :ML_QK_W].astype(BF16)
    wkt = w_in[:, ML_QK_W:2 * ML_QK_W].T.astype(BF16)
    wv = w_in[:, 2 * ML_QK_W:2 * ML_QK_W + ML_V_W].astype(BF16)
    wo = w_in[:, 2 * ML_QK_W + ML_V_W:2 * ML_QK_W + ML_V_W + d].astype(BF16)
    wg = _split_bf16(jnp.zeros((d, LANES), F32).at[:, :ng].set(w_in[:, -ng:]))
    bg = jnp.zeros((1, LANES), F32).at[0, :ng].set(b_gates)
    tok = lambda w: pl.BlockSpec((TM, w), lambda i: (i, 0))
    return pl.pallas_call(
        _inproj1_kernel,
        grid=(nt,),
        in_specs=[pl.BlockSpec((rows, d), lambda i: (0, jnp.maximum(i - 1, 0))),
                  _full((TM, d)),
                  _full((1, d)),
                  pl.BlockSpec((SUBLANES, d), lambda i: (0, 0)),
                  pl.BlockSpec((SUBLANES, d), lambda i: (0, 1)),
                  _full(wq.shape), _full(wkt.shape), _full(wv.shape), _full(wo.shape),
                  _full(wg.shape), _full(bg.shape)],
        out_specs=[tok(ML_QK_W),
                   pl.BlockSpec((ML_QK_W, TM), lambda i: (0, i)),
                   tok(ML_V_W), tok(d), tok(LANES)],
        out_shape=[jax.ShapeDtypeStruct((t, ML_QK_W), BF16),
                   jax.ShapeDtypeStruct((ML_QK_W, t), BF16),
                   jax.ShapeDtypeStruct((t, ML_V_W), BF16),
                   jax.ShapeDtypeStruct((t, d), BF16),
                   jax.ShapeDtypeStruct((t, LANES), F32)],
        compiler_params=_params(),
        name="inproj1",
    )(x.reshape(rows, GRID_W * d), ctx, g1.reshape(1, -1), mod, mod, wq, wkt, wv, wo, wg, bg)


def _log_sigmoid(x):
    return jnp.minimum(x, 0.0) - jnp.log1p(jnp.exp(-jnp.abs(x)))


LOG2E = 1.4426950408889634


def _mlstm_chunk(reverse, q_ref, kt_ref, v_ref, gcol_ref, h_ref, c_ref, m_ref):
    L = TM
    half = L // 2
    gi = 2 * ML_HEADS if reverse else 0
    gf = gi + ML_HEADS
    end = 0 if reverse else L - 1
    rr = lax.broadcasted_iota(jnp.int32, (L, L), 0)
    cc = lax.broadcasted_iota(jnp.int32, (L, L), 1)
    tri = jnp.where((cc >= rr) if reverse else (cc <= rr), 1.0, 0.0).astype(BF16)
    rh = lax.broadcasted_iota(jnp.int32, (half, half), 0)
    ch = lax.broadcasted_iota(jnp.int32, (half, half), 1)
    diag = (ch >= rh) if reverse else (ch <= rh)

    gates = gcol_ref[...]
    lane = lax.broadcasted_iota(jnp.int32, (L, LANES), 1)
    mine = jnp.logical_and(lane >= gf, lane < gf + ML_HEADS)
    lf = jnp.where(mine, _log_sigmoid(gates), 0.0)
    p0 = lf.astype(BF16)
    r1 = lf - p0.astype(F32)
    p1 = r1.astype(BF16)
    p2 = (r1 - p1.astype(F32)).astype(BF16)
    cum = (jnp.dot(tri, p0, preferred_element_type=F32) + jnp.dot(tri, p1, preferred_element_type=F32)
           + jnp.dot(tri, p2, preferred_element_type=F32))

    r = jnp.where(mine, pltpu.roll(gates, ML_HEADS, 1) - cum, 0.0)
    row = lax.broadcasted_iota(jnp.int32, (L, LANES), 0)
    cm = r
    sh = 1
    while sh < L:
        if sh < SUBLANES:
            if reverse:
                cm = jnp.where(row < L - sh, jnp.maximum(cm, pltpu.roll(cm, L - sh, 0)), cm)
            else:
                cm = jnp.where(row >= sh, jnp.maximum(cm, pltpu.roll(cm, sh, 0)), cm)
        else:
            pad = jnp.full((sh, LANES), NEG_INF, F32)
            moved = (jnp.concatenate([cm[sh:], pad], axis=0) if reverse
                     else jnp.concatenate([pad, cm[:L - sh]], axis=0))
            cm = jnp.maximum(cm, moved)
        sh *= 2
    r8 = r.T[gf:gf + ML_HEADS]

    m_prev = m_ref[0:1, :]
    mm = jnp.maximum(m_prev, cm)
    m_t = cum + mm
    w_inter = jnp.exp(m_prev - mm)
    floor = jnp.exp(-m_t)
    mm2 = mm * LOG2E
    m_new = m_t[end:end + 1, :]
    shift2 = (cum[end:end + 1, :] - m_new) * LOG2E
    decay = jnp.exp(cum[end:end + 1, :] + m_prev - m_new)
    m_ref[0:1, :] = m_new

    ones_col = jnp.where(lax.broadcasted_iota(jnp.int32, (L, ML_DV), 1) == 0, 1.0, 0.0).astype(BF16)
    top, bot = slice(0, half), slice(half, L)
    dot = functools.partial(jnp.dot, preferred_element_type=F32)
    yield

    for hd in range(ML_HEADS):
        ln = gf + hd
        r2_row = r8[hd:hd + 1, :] * LOG2E

        def weights(tq, ks, masked):
            w = jnp.exp2(r2_row[:, ks] - mm2[tq, ln:ln + 1])
            return jnp.where(diag, w, 0.0) if masked else w

        qh = q_ref[:, hd * ML_DQK:(hd + 1) * ML_DQK]
        kth = kt_ref[hd * ML_DQK:(hd + 1) * ML_DQK, :]
        vext = jnp.concatenate([v_ref[:, hd * ML_DV:(hd + 1) * ML_DV], ones_col], axis=1)
        if reverse:
            s_top = dot(qh[top], kth) * jnp.concatenate([weights(top, top, True),
                                                         weights(top, bot, False)], axis=1)
            s_bot = dot(qh[bot], kth[:, bot]) * weights(bot, bot, True)
            intra = jnp.concatenate([dot(s_top.astype(BF16), vext),
                                     dot(s_bot.astype(BF16), vext[bot])], axis=0)
        else:
            s_top = dot(qh[top], kth[:, top]) * weights(top, top, True)
            s_bot = dot(qh[bot], kth) * jnp.concatenate([weights(bot, top, False),
                                                         weights(bot, bot, True)], axis=1)
            intra = jnp.concatenate([dot(s_top.astype(BF16), vext[top]),
                                     dot(s_bot.astype(BF16), vext)], axis=0)
        state = c_ref[hd]
        tot = w_inter[:, ln:ln + 1] * dot(qh, state.astype(BF16)) + intra
        den = tot[:, ML_DV:ML_DV + 1]
        h_ref[:, hd * ML_DV:(hd + 1) * ML_DV] = (
            tot[:, :ML_DV] / jnp.maximum(jnp.abs(den), floor[:, ln:ln + 1])).astype(h_ref.dtype)

        w_state = jnp.exp2(r2_row + shift2[:, ln:ln + 1])
        kw = (kth.astype(F32) * w_state).astype(BF16)
        c_ref[hd] = decay[:, ln:ln + 1] * state + dot(kw, vext)
        yield


def _mlstm_kernel(qf_ref, ktf_ref, vf_ref, gf_ref, qb_ref, ktb_ref, vb_ref, gb_ref, hf_ref, hb_ref,
                  cf_ref, mf_ref, cb_ref, mb_ref):
    @pl.when(pl.program_id(0) == 0)
    def _():
        for ref in (cf_ref, mf_ref, cb_ref, mb_ref):
            ref[...] = jnp.zeros_like(ref)

    fwd = _mlstm_chunk(False, qf_ref, ktf_ref, vf_ref, gf_ref, hf_ref, cf_ref, mf_ref)
    bwd = _mlstm_chunk(True, qb_ref, ktb_ref, vb_ref, gb_ref, hb_ref, cb_ref, mb_ref)
    for _ in zip(fwd, bwd):
        pass


def _mlstm(q, kt, v, gcol):
    t = q.shape[0]
    nt = t // TM
    back = lambda s: _scan_tile(True, s, nt)
    state = [pltpu.VMEM((ML_HEADS, ML_DQK, 2 * ML_DV), F32), pltpu.VMEM((SUBLANES, LANES), F32)]

    def specs(tile):
        return [pl.BlockSpec((TM, ML_QK_W), lambda s: (tile(s), 0)),
                pl.BlockSpec((ML_QK_W, TM), lambda s: (0, tile(s))),
                pl.BlockSpec((TM, ML_V_W), lambda s: (tile(s), 0)),
                pl.BlockSpec((TM, LANES), lambda s: (tile(s), 0))]

    return pl.pallas_call(
        _mlstm_kernel,
        grid=(nt,),
        in_specs=specs(lambda s: s) + specs(back),
        out_specs=[pl.BlockSpec((TM, ML_V_W), lambda s: (s, 0)),
                   pl.BlockSpec((TM, ML_V_W), lambda s: (back(s), 0))],
        out_shape=[jax.ShapeDtypeStruct((t, ML_V_W), BF16)] * 2,
        scratch_shapes=state + state,
        compiler_params=_params(),
        name="mlstm",
    )(q, kt, v, gcol, q, kt, v, gcol)


def _finish1_kernel(x_ref, hf_ref, hb_ref, o_ref, ng_ref, w_ref, gt_ref, g2_ref, sh_ref, sc_ref,
                    wr_ref, br_ref, x1_ref, h2_ref, lg_ref):
    hs = hf_ref[...].astype(F32) + hb_ref[...].astype(F32)
    parts = []
    for hd in range(ML_HEADS):
        blk = hs[:, hd * ML_DV:(hd + 1) * ML_DV]
        parts.append(blk * lax.rsqrt(jnp.mean(blk * blk, axis=-1, keepdims=True) + EPS))
    hn = jnp.concatenate(parts, axis=1) * ng_ref[...]
    y = jnp.dot((hn * _sigmoid(o_ref[...].astype(F32))).astype(BF16), w_ref[...],
                preferred_element_type=F32)
    x1 = x_ref[...] + gt_ref[0:1, :] * y
    x1_ref[...] = x1
    h2 = _rms(x1, g2_ref[...]) * (1.0 + sc_ref[0:1, :]) + sh_ref[0:1, :]
    _store_token_tiles(h2_ref, h2)
    lg_ref[...] = _router_logits(h2, wr_ref, br_ref)


def _finish1(x, hf, hb, o, norm_g, w_out_bf, mod, g2, wr, br):
    n, d = x.shape
    rows = n // GRID_W
    lat = lambda w: pl.BlockSpec((TM, w), lambda i: (i + 1, 0))
    modspec = lambda c: pl.BlockSpec((SUBLANES, d), lambda i: (0, c))
    return pl.pallas_call(
        _finish1_kernel,
        grid=(GRID_W,),
        in_specs=[pl.BlockSpec((rows, d), lambda i: (0, i)),
                  lat(ML_V_W), lat(ML_V_W), lat(d),
                  _full((1, ML_V_W)), _full((ML_V_W, d)),
                  modspec(2), _full((1, d)), modspec(3), modspec(4),
                  _full(wr.shape), _full(br.shape)],
        out_specs=_TOKEN_OUTS(d),
        out_shape=_token_out_shapes(n, d),
        compiler_params=_params(),
        name="finish1",
    )(x.reshape(rows, GRID_W * d), hf, hb, o, norm_g.reshape(1, -1), w_out_bf, mod,
      g2.reshape(1, -1), mod, mod, wr, br)


def kernel(x, c, ctx, c_ctx,
           l0_ada_w, l0_ada_b, l0_norm1_g, l0_norm2_g,
           l0_rg_w_in, l0_rg_conv_w, l0_rg_conv_b, l0_rg_w_a, l0_rg_b_a, l0_rg_w_x, l0_rg_b_x,
           l0_rg_lambda, l0_rg_w_out,
           l0_moe_w_grp, l0_moe_b_grp, l0_moe_w_exp, l0_moe_b_exp, l0_moe_w_gate, l0_moe_w_up,
           l0_moe_w_down,
           l1_ada_w, l1_ada_b, l1_norm1_g, l1_norm2_g,
           l1_ml_w_in, l1_ml_b_gates, l1_ml_norm_g, l1_ml_w_out,
           l1_moe_w_grp, l1_moe_b_grp, l1_moe_w_exp, l1_moe_b_exp, l1_moe_w_gate, l1_moe_w_up,
           l1_moe_w_down,
           final_norm_g):
    assert x.shape[0] == 1 and ctx.shape[1] == TM and x.shape[1] == GRID_W * TM
    xs, cs = x[0], ctx[0]
    d = xs.shape[1]
    cond8 = jnp.zeros((SUBLANES, d), F32).at[0].set(c[0]).at[1].set(c_ctx)

    mod0 = _adaln(cond8, l0_ada_w, l0_ada_b)
    conv_w8 = jnp.zeros((SUBLANES, D_RNN), F32).at[:CONV_W].set(l0_rg_conv_w)
    wcat = [jnp.concatenate([l0_rg_w_a[dr], l0_rg_w_x[dr]], axis=-1).astype(BF16) for dr in range(2)]
    gate, u, hf0 = _front0(xs, cs, l0_norm1_g, mod0, l0_rg_w_in.astype(BF16), conv_w8, l0_rg_conv_b,
                           wcat[0], l0_rg_b_a[0], l0_rg_b_x[0], l0_rg_lambda[0])
    wr0, br0 = _router_weights(l0_moe_w_grp, l0_moe_b_grp, l0_moe_w_exp, l0_moe_b_exp)
    x1, h2, logits = _back0(xs, cs, u, gate, hf0, wcat[1], l0_rg_b_a[1], l0_rg_b_x[1], l0_rg_lambda[1],
                            l0_rg_w_out.astype(BF16), mod0, l0_norm2_g, wr0, br0)
    meta, cnt = _route(logits, ROUTE_TILE_0)
    dest, ybuf = _moe(h2, meta, cnt, l0_moe_w_gate, l0_moe_w_up, l0_moe_w_down)
    x2, ctx2 = _combine0(x1, dest, meta, mod0, ybuf)

    mod1 = _adaln(cond8, l1_ada_w, l1_ada_b)
    q, kt, v, o, gcol = _inproj1(x2, ctx2, l1_norm1_g, mod1, l1_ml_w_in, l1_ml_b_gates)
    hf, hb = _mlstm(q, kt, v, gcol)
    wr1, br1 = _router_weights(l1_moe_w_grp, l1_moe_b_grp, l1_moe_w_exp, l1_moe_b_exp)
    x1, h2, logits = _finish1(x2, hf, hb, o, l1_ml_norm_g, l1_ml_w_out.astype(BF16), mod1,
                              l1_norm2_g, wr1, br1)
    meta, cnt = _route(logits, ROUTE_TILE_1)
    dest, ybuf = _moe(h2, meta, cnt, l1_moe_w_gate, l1_moe_w_up, l1_moe_w_down)
    out = _combine1(x1, dest, meta, mod1, final_norm_g, ybuf)
    return out[None]
```

```python
import functools

import jax
import jax.numpy as jnp
from jax import lax
from jax.experimental import pallas as pl
from jax.experimental.pallas import tpu as pltpu

D_MODEL = 1024
GRID_W = 64
N_MOD = 6
EPS = 1e-6

D_RNN = 1280
RG_BLOCKS = 10
RG_BLOCK_W = D_RNN // RG_BLOCKS
CONV_W = 4
CONV_PAD_L = 2
RG_C = 8.0

ML_HEADS = 8
ML_DQK = D_MODEL // (2 * ML_HEADS)
ML_DV = D_MODEL // ML_HEADS
ML_QK_W = ML_HEADS * ML_DQK
ML_V_W = ML_HEADS * ML_DV

N_GROUPS = 4
EXPERTS_PER_GROUP = 8
N_EXPERTS = N_GROUPS * EXPERTS_PER_GROUP
D_EXPERT = 512

TM = 256
TB = 256
ROUTE_TILE_0 = 1280
ROUTE_TILE_1 = 1024
SUBLANES = 8
LANES = 128
VMEM_LIMIT = 48 * 1024 * 1024

F32 = jnp.float32
BF16 = jnp.bfloat16
HI = lax.Precision.HIGHEST
NEG_INF = float("-inf")


def _params(n_axes=1):
    return pltpu.CompilerParams(dimension_semantics=("arbitrary",) * n_axes,
                                vmem_limit_bytes=VMEM_LIMIT)


def _rms(x, g):
    return x * lax.rsqrt(jnp.mean(x * x, axis=-1, keepdims=True) + EPS) * g


def _sigmoid(x):
    return 1.0 / (1.0 + jnp.exp(-x))


def _softplus(x):
    return jnp.maximum(x, 0.0) + jnp.log1p(jnp.exp(-jnp.abs(x)))


def _gelu_tanh(x):
    return 0.5 * x * (1.0 + jnp.tanh(0.7978845608028654 * (x + 0.044715 * (x * x * x))))


def _full(shape):
    return pl.BlockSpec(shape, lambda *_: (0,) * len(shape))


def _store_token_tiles(ref, x, first_row=0):
    per = x.shape[1] // LANES
    for c in range(per):
        ref[pl.ds(first_row * per + c, x.shape[0], stride=per), :] = x[:, c * LANES:(c + 1) * LANES]


def _load_token_tiles(ref, rows, d):
    per = d // LANES
    return jnp.concatenate([ref[pl.ds(c, rows, stride=per), :] for c in range(per)], axis=1)


def _token_tile(ref, row, d):
    per = d // LANES
    start = row * per if isinstance(row, int) else pl.multiple_of(row * per, per)
    return ref.at[pl.ds(start, per), :]


def _adaln_kernel(cond_ref, w_ref, b_ref, o_ref):
    c = cond_ref[...]
    s = c * _sigmoid(c)
    o_ref[...] = jnp.dot(s, w_ref[...], precision=HI, preferred_element_type=F32) + b_ref[...]


def _adaln(cond8, w, b):
    d = w.shape[0]
    return pl.pallas_call(
        _adaln_kernel,
        grid=(N_MOD,),
        in_specs=[_full((SUBLANES, d)),
                  pl.BlockSpec((d, d), lambda j: (0, j)),
                  pl.BlockSpec((1, d), lambda j: (0, j))],
        out_specs=pl.BlockSpec((SUBLANES, d), lambda j: (0, j)),
        out_shape=jax.ShapeDtypeStruct((SUBLANES, N_MOD * d), F32),
        compiler_params=_params(),
        name="adaln",
    )(cond8, w, b.reshape(1, -1))


def _mod_row(ref, is_ctx):
    return jnp.where(is_ctx, ref[1:2, :], ref[0:1, :])


def _front0_kernel(nt, x_ref, ctx_ref, g_ref, sh_ref, sc_ref, w_ref, cw_ref, cb_ref,
                   wcat_ref, ba_ref, bx_ref, lam_ref, gate_ref, u_ref, hf_ref,
                   ext_ref, ubuf_ref, a_ref, b_ref, o_ref, carry_ref):
    s = pl.program_id(0)
    is_ctx = s == 0

    @pl.when(s == 0)
    def _():
        ext_ref[...] = jnp.zeros_like(ext_ref)
        ubuf_ref[...] = jnp.zeros_like(ubuf_ref)

    @pl.when(s <= 2)
    def _():
        carry_ref[...] = jnp.zeros_like(carry_ref)

    xin = jnp.where(is_ctx, ctx_ref[...], x_ref[...])
    h = _rms(xin, g_ref[...]) * (1.0 + _mod_row(sc_ref, is_ctx)) + _mod_row(sh_ref, is_ctx)
    hb = h.astype(BF16)

    right_valid = jnp.logical_and(s >= 2, s <= nt - 1)
    base = SUBLANES - CONV_PAD_L
    for j in range(RG_BLOCKS):
        ln = slice(j * RG_BLOCK_W, (j + 1) * RG_BLOCK_W)
        _rglru_block(False, j, ubuf_ref.at[s % 2], wcat_ref, ba_ref, bx_ref, lam_ref, hf_ref,
                     a_ref, b_ref, o_ref, carry_ref)
        rc = slice(D_RNN + j * RG_BLOCK_W, D_RNN + (j + 1) * RG_BLOCK_W)
        w_j = jnp.concatenate([w_ref[:, ln], w_ref[:, rc]], axis=1)
        p = jnp.dot(hb, w_j, preferred_element_type=F32)
        gate_ref[:, ln] = p[:, :RG_BLOCK_W]
        rec = p[:, RG_BLOCK_W:]
        ext_ref[j, SUBLANES + TM:, :] = jnp.where(right_valid, rec[0:SUBLANES], 0.0)
        u = cb_ref[:, ln] + ext_ref[j, pl.ds(base, TM), :] * cw_ref[0:1, ln]
        for k in range(1, CONV_W):
            u = u + ext_ref[j, pl.ds(base + k, TM), :] * cw_ref[k:k + 1, ln]
        u_ref[:, ln] = u
        ubuf_ref[(s + 1) % 2, :, ln] = u
        ext_ref[j, 0:SUBLANES, :] = jnp.where(s >= 2, ext_ref[j, TM:TM + SUBLANES, :], 0.0)
        ext_ref[j, SUBLANES:SUBLANES + TM, :] = rec


_SCAN_SLAB = lambda: pltpu.VMEM((RG_BLOCKS, SUBLANES * SCAN_PITCH, LANES), F32)
_SCAN_SCRATCH = lambda: [_SCAN_SLAB(), _SCAN_SLAB(), _SCAN_SLAB(), pltpu.VMEM((SUBLANES, D_RNN), F32)]


def _front0(x, ctx, g1, mod, w_in_bf, conv_w8, conv_b, wcat, b_a, b_x, lam):
    n, d = x.shape
    nx = n // TM
    nt = nx + 1
    t = nt * TM
    row = lambda v: v.reshape(1, -1)
    return pl.pallas_call(
        functools.partial(_front0_kernel, nt),
        grid=(nt + 2,),
        in_specs=[pl.BlockSpec((TM, d), lambda s: (jnp.clip(s - 1, 0, nx - 1), 0)),
                  _full((TM, d)),
                  _full((1, d)),
                  pl.BlockSpec((SUBLANES, d), lambda s: (0, 0)),
                  pl.BlockSpec((SUBLANES, d), lambda s: (0, 1)),
                  _full((d, 2 * D_RNN)),
                  _full((SUBLANES, D_RNN)), _full((1, D_RNN)),
                  _full((RG_BLOCKS, RG_BLOCK_W, 2 * RG_BLOCK_W)),
                  _full((1, D_RNN)), _full((1, D_RNN)), _full((1, D_RNN))],
        out_specs=[pl.BlockSpec((TM, D_RNN), lambda s: (jnp.minimum(s, nt - 1), 0)),
                   pl.BlockSpec((TM, D_RNN), lambda s: (jnp.maximum(s - 1, 0), 0)),
                   pl.BlockSpec((TM, D_RNN), lambda s: (jnp.maximum(s - 2, 0), 0))],
        out_shape=[jax.ShapeDtypeStruct((t, D_RNN), F32),
                   jax.ShapeDtypeStruct((t + TM, D_RNN), F32),
                   jax.ShapeDtypeStruct((t, D_RNN), F32)],
        scratch_shapes=[pltpu.VMEM((RG_BLOCKS, TM + 2 * SUBLANES, LANES), F32),
                        pltpu.VMEM((2, TM, D_RNN), F32)] + _SCAN_SCRATCH(),
        compiler_params=_params(),
        name="front0",
    )(x, ctx, g1.reshape(1, -1), mod, mod, w_in_bf, conv_w8, row(conv_b), wcat, row(b_a), row(b_x),
      row(lam))


def _scan_tile(reverse, s, nt):
    if not reverse:
        return s
    return jnp.where(s == 0, 0, nt - s)


SCAN_CHUNK = TM // SUBLANES
SCAN_PITCH = SCAN_CHUNK + 4


def _rglru_block(reverse, j, u_ref, wcat_ref, ba_ref, bx_ref, lam_ref, h_ref,
                 a_ref, b_ref, o_ref, carry_ref):
    steps = range(SCAN_CHUNK - 1, -1, -1) if reverse else range(SCAN_CHUNK)
    chunks = range(SUBLANES - 1, -1, -1) if reverse else range(SUBLANES)
    ln = slice(j * RG_BLOCK_W, (j + 1) * RG_BLOCK_W)
    u = u_ref[:, ln]
    g = jnp.dot(u.astype(BF16), wcat_ref[j], preferred_element_type=F32)
    half_rate = (-0.5 * RG_C) * _softplus(-lam_ref[:, ln])
    log_a = jnp.tanh(0.5 * (g[:, :RG_BLOCK_W] + ba_ref[:, ln])) * half_rate + half_rate
    ig = 0.5 * jnp.tanh(0.5 * (g[:, RG_BLOCK_W:] + bx_ref[:, ln])) + 0.5
    a = jnp.exp(log_a)
    b = jnp.sqrt(-jnp.tanh(log_a) * (a * a + 1.0)) * ig * u
    for c in range(SUBLANES):
        a_ref[j, pl.ds(c * SCAN_PITCH, SCAN_CHUNK), :] = a[c * SCAN_CHUNK:(c + 1) * SCAN_CHUNK]
        b_ref[j, pl.ds(c * SCAN_PITCH, SCAN_CHUNK), :] = b[c * SCAN_CHUNK:(c + 1) * SCAN_CHUNK]

    row = lambda ref, i: ref[j, pl.ds(i, SUBLANES, stride=SCAN_PITCH), :]
    end = jnp.zeros((SUBLANES, LANES), F32)
    decay = jnp.ones((SUBLANES, LANES), F32)
    for i in steps:
        ai = row(a_ref, i)
        end = ai * end + row(b_ref, i)
        decay = decay * ai

    state = carry_ref[0:1, ln]
    entry = [None] * SUBLANES
    for c in chunks:
        entry[c] = state
        state = decay[c:c + 1] * state + end[c:c + 1]
    carry_ref[0:1, ln] = state

    hcur = jnp.concatenate(entry, axis=0)
    for i in steps:
        hcur = row(a_ref, i) * hcur + row(b_ref, i)
        o_ref[j, pl.ds(i, SUBLANES, stride=SCAN_PITCH), :] = hcur
    for c in range(SUBLANES):
        h_ref[c * SCAN_CHUNK:(c + 1) * SCAN_CHUNK, ln] = o_ref[j, pl.ds(c * SCAN_PITCH, SCAN_CHUNK), :]


def _split_bf16(w):
    hi = w.astype(BF16)
    return jnp.stack([hi, (w - hi.astype(F32)).astype(BF16)])


def _dot_bf16x3(x, w_ref):
    hi = x.astype(BF16)
    lo = (x - hi.astype(F32)).astype(BF16)
    acc = jnp.dot(hi, w_ref[0], preferred_element_type=F32)
    acc = acc + jnp.dot(lo, w_ref[0], preferred_element_type=F32)
    return acc + jnp.dot(hi, w_ref[1], preferred_element_type=F32)


def _router_logits(h2, wr_ref, br_ref):
    return _dot_bf16x3(h2, wr_ref) + br_ref[...]


def _route_kernel(lg_ref, meta_ref, cnt_ref, carry_ref):
    step = pl.program_id(0)
    tm = lg_ref.shape[0]

    @pl.when(step == 0)
    def _():
        carry_ref[...] = jnp.zeros_like(carry_ref)

    logits = jnp.concatenate([lg_ref[i * LANES:(i + 1) * LANES, :].T for i in range(tm // LANES)],
                             axis=1)
    row8 = lax.broadcasted_iota(jnp.int32, (SUBLANES, tm), 0)
    grp_logits = jnp.where(row8 < N_GROUPS, logits[0:SUBLANES], NEG_INF)
    gmax = jnp.max(grp_logits, axis=0, keepdims=True)
    p_sel = 1.0 / jnp.sum(jnp.exp(grp_logits - gmax), axis=0, keepdims=True)
    grp = jnp.min(jnp.where(grp_logits == gmax, row8, SUBLANES), axis=0, keepdims=True)

    in_grp = logits[SUBLANES + (N_GROUPS - 1) * EXPERTS_PER_GROUP:SUBLANES + N_GROUPS * EXPERTS_PER_GROUP]
    for gi in range(N_GROUPS - 2, -1, -1):
        lo = SUBLANES + gi * EXPERTS_PER_GROUP
        in_grp = jnp.where(grp == gi, logits[lo:lo + EXPERTS_PER_GROUP], in_grp)
    v1 = jnp.max(in_grp, axis=0, keepdims=True)
    i1 = jnp.min(jnp.where(in_grp == v1, row8, EXPERTS_PER_GROUP), axis=0, keepdims=True)
    rest = jnp.where(row8 == i1, NEG_INF, in_grp)
    v2 = jnp.max(rest, axis=0, keepdims=True)
    i2 = jnp.min(jnp.where(rest == v2, row8, EXPERTS_PER_GROUP), axis=0, keepdims=True)
    e2 = jnp.exp(v2 - v1)
    w1 = p_sel / (1.0 + e2)
    w2 = p_sel * e2 / (1.0 + e2)
    eid = (grp * EXPERTS_PER_GROUP + i1, grp * EXPERTS_PER_GROUP + i2)

    rr = lax.broadcasted_iota(jnp.int32, (LANES, LANES), 0)
    cc = lax.broadcasted_iota(jnp.int32, (LANES, LANES), 1)
    strict_upper = jnp.where(rr < cc, 1.0, 0.0).astype(BF16)
    erow = lax.broadcasted_iota(jnp.int32, (N_EXPERTS, LANES), 0)
    base = carry_ref[:, 0:1]
    for k in range(2):
        for i in range(tm // LANES):
            ln = slice(i * LANES, (i + 1) * LANES)
            onehot = jnp.where(erow == eid[k][:, ln], 1.0, 0.0)
            pre = jnp.dot(onehot.astype(BF16), strict_upper, preferred_element_type=F32)
            meta_ref[2 + k:3 + k, ln] = jnp.sum(onehot * (base + pre), axis=0, keepdims=True)
            base = base + jnp.sum(onehot, axis=1, keepdims=True)
    carry_ref[...] = jnp.broadcast_to(base, carry_ref.shape)
    cnt_ref[...] = jnp.broadcast_to(base, cnt_ref.shape)

    meta_ref[0:1, :] = eid[0].astype(F32)
    meta_ref[1:2, :] = eid[1].astype(F32)
    meta_ref[4:5, :] = w1
    meta_ref[5:6, :] = w2
    meta_ref[6:8, :] = jnp.zeros((2, tm), F32)


def _router_weights(w_grp, b_grp, w_exp, b_exp):
    d = w_grp.shape[0]
    wr = jnp.zeros((d, LANES), F32).at[:, :N_GROUPS].set(w_grp)
    wr = wr.at[:, SUBLANES:SUBLANES + N_EXPERTS].set(w_exp)
    br = jnp.zeros((1, LANES), F32).at[0, :N_GROUPS].set(b_grp)
    br = br.at[0, SUBLANES:SUBLANES + N_EXPERTS].set(b_exp)
    return _split_bf16(wr), br


def _route(logits, tile):
    t = logits.shape[0]
    return pl.pallas_call(
        _route_kernel,
        grid=(t // tile,),
        in_specs=[pl.BlockSpec((tile, LANES), lambda i: (i, 0))],
        out_specs=[pl.BlockSpec((SUBLANES, tile), lambda i: (0, i)), _full((N_EXPERTS, LANES))],
        out_shape=[jax.ShapeDtypeStruct((SUBLANES, t), F32),
                   jax.ShapeDtypeStruct((N_EXPERTS, LANES), F32)],
        scratch_shapes=[pltpu.VMEM((N_EXPERTS, LANES), F32)],
        compiler_params=_params(),
        name="moe_route",
    )(logits)


def _back0_kernel(nt, x_ref, ctx_ref, u_ref, gate_ref, hf_ref, wcat_ref, ba_ref, bx_ref, lam_ref,
                  w_ref, gt_ref, g2_ref, sh_ref, sc_ref, wr_ref, br_ref, x1_ref, h2_ref, lg_ref,
                  hb_ref, a_ref, b_ref, o_ref, carry_ref):
    s = pl.program_id(0)
    is_ctx = s == 0

    @pl.when(s == 0)
    def _():
        carry_ref[...] = jnp.zeros_like(carry_ref)

    for j in range(RG_BLOCKS):
        _rglru_block(True, j, u_ref, wcat_ref, ba_ref, bx_ref, lam_ref, hb_ref,
                     a_ref, b_ref, o_ref, carry_ref)
    xin = jnp.where(is_ctx, ctx_ref[...], x_ref[...])
    y = _gelu_tanh(gate_ref[...]) * (hf_ref[...] + hb_ref[...])
    y = jnp.dot(y.astype(BF16), w_ref[...], preferred_element_type=F32)
    x1 = xin + _mod_row(gt_ref, is_ctx) * y
    x1_ref[...] = x1
    h2 = _rms(x1, g2_ref[...]) * (1.0 + _mod_row(sc_ref, is_ctx)) + _mod_row(sh_ref, is_ctx)
    _store_token_tiles(h2_ref, h2)
    lg_ref[...] = _router_logits(h2, wr_ref, br_ref)


_TOKEN_OUTS = lambda d: [pl.BlockSpec((TM, d), lambda i: (i, 0)),
                         pl.BlockSpec((TM * d // LANES, LANES), lambda i: (i, 0)),
                         pl.BlockSpec((TM, LANES), lambda i: (i, 0))]


def _token_out_shapes(t, d):
    return [jax.ShapeDtypeStruct((t, d), F32), jax.ShapeDtypeStruct((t * d // LANES, LANES), F32),
            jax.ShapeDtypeStruct((t, LANES), F32)]


def _back0(x, ctx, u, gate, hf, wcat, b_a, b_x, lam, w_out_bf, mod, g2, wr, br):
    n, d = x.shape
    nt = n // TM + 1
    t = nt * TM
    tile = lambda s: _scan_tile(True, s, nt)
    row = lambda v: v.reshape(1, -1)
    tok = pl.BlockSpec((TM, D_RNN), lambda s: (tile(s), 0))
    modspec = lambda c: pl.BlockSpec((SUBLANES, d), lambda s: (0, c))
    return pl.pallas_call(
        functools.partial(_back0_kernel, nt),
        grid=(nt,),
        in_specs=[pl.BlockSpec((TM, d), lambda s: (jnp.maximum(tile(s) - 1, 0), 0)),
                  _full((TM, d)),
                  tok, tok, tok,
                  _full((RG_BLOCKS, RG_BLOCK_W, 2 * RG_BLOCK_W)),
                  _full((1, D_RNN)), _full((1, D_RNN)), _full((1, D_RNN)),
                  _full((D_RNN, d)),
                  modspec(2), _full((1, d)), modspec(3), modspec(4),
                  _full(wr.shape), _full(br.shape)],
        out_specs=[pl.BlockSpec((TM, d), lambda s: (tile(s), 0)),
                   pl.BlockSpec((TM * d // LANES, LANES), lambda s: (tile(s), 0)),
                   pl.BlockSpec((TM, LANES), lambda s: (tile(s), 0))],
        out_shape=_token_out_shapes(t, d),
        scratch_shapes=[pltpu.VMEM((TM, D_RNN), F32)] + _SCAN_SCRATCH(),
        compiler_params=_params(),
        name="back0",
    )(x, ctx, u, gate, hf, wcat, row(b_a), row(b_x), row(lam), w_out_bf, mod, g2.reshape(1, -1),
      mod, mod, wr, br)


def _plan_kernel(n_tiles, nbp, cnt_ref, meta_ref, dest_ref, blk_ref):
    c = cnt_ref[...]
    padded = jnp.floor((c + (TB - 1)) * (1.0 / TB)) * TB
    r = lax.broadcasted_iota(jnp.int32, (N_EXPERTS, N_EXPERTS), 0)
    q = lax.broadcasted_iota(jnp.int32, (N_EXPERTS, N_EXPERTS), 1)
    lower = jnp.where(q <= r, 1.0, 0.0)
    pad_end = jnp.dot(lower, padded, precision=HI, preferred_element_type=F32)
    pad_start = pad_end - padded

    first_row = lax.broadcasted_iota(jnp.int32, (N_EXPERTS, nbp), 1).astype(F32) * TB
    owner = jnp.sum(jnp.where(pad_end[:, 0:1] <= first_row, 1.0, 0.0), axis=0, keepdims=True)
    blk_ref[0:1, :] = jnp.minimum(owner, N_EXPERTS - 1).astype(jnp.int32)
    n_used = pad_end[N_EXPERTS - 1:N_EXPERTS, 0:1] * (1.0 / TB)
    blk_ref[1:2, :] = jnp.broadcast_to(n_used, (1, nbp)).astype(jnp.int32)
    ends = jnp.concatenate([pad_end, jnp.zeros((LANES - N_EXPERTS, LANES), F32)], axis=0).T[0:1, :]
    blk_ref[2:3, :] = jnp.concatenate([ends, jnp.zeros((1, nbp - LANES), F32)], axis=1).astype(jnp.int32)
    blk_ref[3:SUBLANES, :] = jnp.zeros((SUBLANES - 3, nbp), jnp.int32)

    erow = lax.broadcasted_iota(jnp.int32, (N_EXPERTS, TM), 0).astype(F32)

    def body(i, carry):
        ln = pl.ds(pl.multiple_of(i * TM, TM), TM)
        for k in range(2):
            onehot = jnp.where(erow == meta_ref[k:k + 1, ln], 1.0, 0.0)
            start = jnp.sum(onehot * pad_start[:, 0:1], axis=0, keepdims=True)
            dest_ref[k:k + 1, ln] = (start + meta_ref[2 + k:3 + k, ln]).astype(jnp.int32)
        dest_ref[2:SUBLANES, ln] = jnp.zeros((SUBLANES - 2, TM), jnp.int32)
        return carry

    lax.fori_loop(0, n_tiles, body, 0)


def _plan(cnt, meta):
    t = meta.shape[1]
    n_blocks = (2 * t + N_EXPERTS * TB) // TB
    n_blocks = -(-n_blocks // EXPERT_PAIR) * EXPERT_PAIR
    nbp = -(-n_blocks // LANES) * LANES
    vm = pl.BlockSpec(memory_space=pltpu.VMEM)
    dest, blk = pl.pallas_call(
        functools.partial(_plan_kernel, t // TM, nbp),
        in_specs=[vm, vm],
        out_specs=[vm, vm],
        out_shape=[jax.ShapeDtypeStruct((SUBLANES, t), jnp.int32),
                   jax.ShapeDtypeStruct((SUBLANES, nbp), jnp.int32)],
        compiler_params=pltpu.CompilerParams(vmem_limit_bytes=VMEM_LIMIT),
        name="moe_plan",
    )(cnt, meta)
    return dest, blk, n_blocks


def _row_wait(src_ref, dst_ref, d, sem):
    n = TM * d // LANES
    pltpu.make_async_copy(src_ref.at[pl.ds(0, n), :], dst_ref.at[pl.ds(0, n), :], sem).wait()


DISPATCH_RING = 3


def _dispatch_kernel(n_blocks, d, pe_ref, nu_ref, dest_ref, h2_ref, buf_ref, dsm_ref, zero_ref, src_ref,
                     sem_idx, sem_zero, sem_src, sem_rows):
    step = pl.program_id(0)
    nt = pl.num_programs(0)
    blk = TB * d // LANES
    tile = TM * d // LANES

    def src_copy(i):
        slot = i % DISPATCH_RING
        return pltpu.make_async_copy(h2_ref.at[pl.ds(pl.multiple_of(i * tile, tile), tile), :],
                                     src_ref.at[slot], sem_src.at[slot])

    def zero_block(first_row):
        start = pl.multiple_of(first_row * (d // LANES), blk)
        return pltpu.make_async_copy(zero_ref, buf_ref.at[pl.ds(start, blk), :], sem_zero)

    @pl.when(step == 0)
    def _():
        zero_ref[...] = jnp.zeros_like(zero_ref)

        def for_segments(fn):
            for e in range(N_EXPERTS):
                seg_start = pe_ref[e - 1] if e else 0

                @pl.when(pe_ref[e] > seg_start)
                def _():
                    fn(zero_block(pe_ref[e] - TB))

        for_segments(lambda cp: cp.start())
        lax.fori_loop(nu_ref[0], n_blocks, lambda b, c: (zero_block(b * TB).start(), c)[1], 0)
        for_segments(lambda cp: cp.wait())
        lax.fori_loop(nu_ref[0], n_blocks, lambda b, c: (zero_block(b * TB).wait(), c)[1], 0)
        src_copy(0).start()

    @pl.when(step + 1 < nt)
    def _():
        src_copy(step + 1).start()

    cp = pltpu.make_async_copy(dest_ref, dsm_ref, sem_idx)
    cp.start()
    cp.wait()
    src_copy(step).wait()
    src = src_ref.at[step % DISPATCH_RING]
    sem = sem_rows.at[step % 2]

    for r in range(TM):
        for k in range(2):
            pltpu.make_async_copy(_token_tile(src, r, d), _token_tile(buf_ref, dsm_ref[k, r], d),
                                  sem).start(priority=k)

    @pl.when(step > 0)
    def _():
        for k in range(2):
            _row_wait(src, buf_ref, d, sem_rows.at[(step + 1) % 2])

    @pl.when(step == nt - 1)
    def _():
        for k in range(2):
            _row_wait(src, buf_ref, d, sem)


def _dispatch(h2t, dest, pad_end, n_used, n_blocks, d):
    per = d // LANES
    t = h2t.shape[0] // per
    grid_spec = pltpu.PrefetchScalarGridSpec(
        num_scalar_prefetch=2,
        grid=(t // TM,),
        in_specs=[pl.BlockSpec((SUBLANES, TM), lambda i, pe, nu: (0, i)),
                  pl.BlockSpec(memory_space=pl.ANY)],
        out_specs=pl.BlockSpec(memory_space=pl.ANY),
        scratch_shapes=[pltpu.SMEM((SUBLANES, TM), jnp.int32),
                        pltpu.VMEM((TB * per, LANES), F32),
                        pltpu.VMEM((DISPATCH_RING, TM * per, LANES), F32),
                        pltpu.SemaphoreType.DMA, pltpu.SemaphoreType.DMA,
                        pltpu.SemaphoreType.DMA((DISPATCH_RING,)),
                        pltpu.SemaphoreType.DMA((2,))],
    )
    return pl.pallas_call(
        functools.partial(_dispatch_kernel, n_blocks, d),
        grid_spec=grid_spec,
        out_shape=jax.ShapeDtypeStruct((n_blocks * TB * per, LANES), F32),
        compiler_params=_params(),
        name="moe_dispatch",
    )(pad_end, n_used, dest, h2t)


EXPERT_PAIR = 4
EXPERT_RING = 2 * EXPERT_PAIR


def _experts_kernel(be_ref, nu_ref, buf_ref, wg_ref, wu_ref, wd_ref, y_ref, wg_bf, wu_bf, wd_bf,
                    wg_f32, wu_f32, wd_f32, x_ref, st_ref, sems, wsems):
    step = pl.program_id(0)
    n_used = nu_ref[0]
    d = wg_bf.shape[0]
    blk = TB * d // LANES
    first = step * EXPERT_PAIR

    def weight_copies(expert, slot):
        return [pltpu.make_async_copy(src.at[expert], dst.at[slot], wsems.at[slot])
                for src, dst in ((wg_ref, wg_f32), (wu_ref, wu_f32), (wd_ref, wd_f32))]

    @pl.when(step == 0)
    def _():
        st_ref[0] = -1
        st_ref[1] = 0
        for cp in weight_copies(be_ref[0], 0):
            cp.start()

    def block_copy(b):
        slot = b % EXPERT_RING
        return pltpu.make_async_copy(buf_ref.at[pl.ds(pl.multiple_of(b * blk, blk), blk), :],
                                     x_ref.at[slot], sems.at[slot])

    for h in range(EXPERT_PAIR):
        @pl.when(jnp.logical_and(step == 0, h < n_used))
        def _():
            block_copy(h).start()

        @pl.when(first + EXPERT_PAIR + h < n_used)
        def _():
            block_copy(first + EXPERT_PAIR + h).start()

    for h in range(EXPERT_PAIR):
        b = first + h
        used = b < n_used
        expert = be_ref[jnp.clip(b, 0, n_used - 1)]

        @pl.when(jnp.logical_and(used, expert != st_ref[0]))
        def _():
            slot = st_ref[1]
            for cp in weight_copies(expert, slot):
                cp.wait()
            wg_bf[...] = wg_f32[slot].astype(BF16)
            wu_bf[...] = wu_f32[slot].astype(BF16)
            wd_bf[...] = wd_f32[slot].astype(BF16)
            st_ref[0] = expert
            seg_end = lax.while_loop(
                lambda j: jnp.logical_and(j < n_used, be_ref[jnp.minimum(j, n_used - 1)] == expert),
                lambda j: j + 1, b + 1)

            @pl.when(seg_end < n_used)
            def _():
                nxt = be_ref[jnp.minimum(seg_end, n_used - 1)]
                st_ref[1] = 1 - slot
                for cp in weight_copies(nxt, 1 - slot):
                    cp.start()

        @pl.when(used)
        def _():
            block_copy(b).wait()
            xb = _load_token_tiles(x_ref.at[b % EXPERT_RING], TB, d).astype(BF16)
            g = jnp.dot(xb, wg_bf[...], preferred_element_type=F32)
            u = jnp.dot(xb, wu_bf[...], preferred_element_type=F32)
            a = (g * _sigmoid(g)) * u
            _store_token_tiles(y_ref, jnp.dot(a.astype(BF16), wd_bf[...], preferred_element_type=F32),
                               first_row=h * TB)

        @pl.when(jnp.logical_not(used))
        def _():
            y_ref[pl.ds(h * blk, blk), :] = jnp.zeros((blk, LANES), F32)


def _experts(buf, blk_e, n_used, wg, wu, wd, n_blocks):
    d = wg.shape[1]
    blk = TB * d // LANES
    assert n_blocks % EXPERT_PAIR == 0
    hbm = pl.BlockSpec(memory_space=pl.ANY)
    grid_spec = pltpu.PrefetchScalarGridSpec(
        num_scalar_prefetch=2,
        grid=(n_blocks // EXPERT_PAIR,),
        in_specs=[hbm, hbm, hbm, hbm],
        out_specs=pl.BlockSpec((EXPERT_PAIR * blk, LANES), lambda i, be, nu: (i, 0)),
        scratch_shapes=[pltpu.VMEM((d, D_EXPERT), BF16), pltpu.VMEM((d, D_EXPERT), BF16),
                        pltpu.VMEM((D_EXPERT, d), BF16),
                        pltpu.VMEM((2, d, D_EXPERT), F32), pltpu.VMEM((2, d, D_EXPERT), F32),
                        pltpu.VMEM((2, D_EXPERT, d), F32),
                        pltpu.VMEM((EXPERT_RING, blk, LANES), F32),
                        pltpu.SMEM((2,), jnp.int32),
                        pltpu.SemaphoreType.DMA((EXPERT_RING,)),
                        pltpu.SemaphoreType.DMA((2,))],
    )
    return pl.pallas_call(
        _experts_kernel,
        grid_spec=grid_spec,
        out_shape=jax.ShapeDtypeStruct(buf.shape, F32),
        compiler_params=_params(),
        name="moe_experts",
    )(blk_e, n_used, buf, wg, wu, wd)


def _gather_start(dest_ref, dsm_ref, ybuf_ref, rows_ref, slot, d, sem_idx, sem):
    cp = pltpu.make_async_copy(dest_ref, dsm_ref, sem_idx)
    cp.start()
    cp.wait()
    for r in range(TM):
        for k in range(2):
            pltpu.make_async_copy(_token_tile(ybuf_ref, dsm_ref[k, r], d),
                                  _token_tile(rows_ref.at[slot, k], r, d), sem).start(priority=k)


def _gathered_rows(dest_ref, next_ref, dsm_ref, ybuf_ref, rows_ref, d, sem_idx, sems):
    step = pl.program_id(0)
    nt = pl.num_programs(0)
    slot = step % 2

    @pl.when(step == 0)
    def _():
        _gather_start(dest_ref, dsm_ref, ybuf_ref, rows_ref, 0, d, sem_idx, sems.at[0])

    @pl.when(step + 1 < nt)
    def _():
        _gather_start(next_ref, dsm_ref, ybuf_ref, rows_ref, 1 - slot, d, sem_idx, sems.at[1 - slot])

    for k in range(2):
        _row_wait(ybuf_ref, rows_ref.at[slot, k], d, sems.at[slot])
    return [_load_token_tiles(rows_ref.at[slot, k], TM, d) for k in range(2)]


def _token_weights(meta_ref):
    meta = jnp.concatenate([meta_ref[...], jnp.zeros((LANES - SUBLANES, TM), F32)], axis=0)
    mt = meta.T
    return mt[:, 4:5], mt[:, 5:6]


_COMBINE_SCRATCH = lambda d: [pltpu.SMEM((SUBLANES, TM), jnp.int32),
                              pltpu.VMEM((2, 2, TM * d // LANES, LANES), F32),
                              pltpu.SemaphoreType.DMA, pltpu.SemaphoreType.DMA((2,))]


def _next_tile_spec(nt):
    return pl.BlockSpec((SUBLANES, TM), lambda i: (0, jnp.minimum(i + 1, nt - 1)))


def _combine0_kernel(x1_ref, dest_ref, next_ref, meta_ref, gt_ref, ybuf_ref, x2_ref, ctx2_ref,
                     dsm_ref, rows_ref, sem_idx, sems):
    is_ctx = pl.program_id(0) == 0
    y0, y1 = _gathered_rows(dest_ref, next_ref, dsm_ref, ybuf_ref, rows_ref, x1_ref.shape[1],
                            sem_idx, sems)
    w0, w1 = _token_weights(meta_ref)
    out = x1_ref[...] + _mod_row(gt_ref, is_ctx) * (w0 * y0 + w1 * y1)
    x2_ref[...] = out

    @pl.when(is_ctx)
    def _():
        ctx2_ref[...] = out


def _combine0(x1, dest, meta, mod, ybuf):
    t, d = x1.shape
    nt = t // TM
    return pl.pallas_call(
        _combine0_kernel,
        grid=(nt,),
        in_specs=[pl.BlockSpec((TM, d), lambda i: (i, 0)),
                  pl.BlockSpec((SUBLANES, TM), lambda i: (0, i)),
                  _next_tile_spec(nt),
                  pl.BlockSpec((SUBLANES, TM), lambda i: (0, i)),
                  pl.BlockSpec((SUBLANES, d), lambda i: (0, 5)),
                  pl.BlockSpec(memory_space=pl.ANY)],
        out_specs=[pl.BlockSpec((TM, d), lambda i: (jnp.maximum(i - 1, 0), 0)),
                   _full((TM, d))],
        out_shape=[jax.ShapeDtypeStruct((t - TM, d), F32), jax.ShapeDtypeStruct((TM, d), F32)],
        scratch_shapes=_COMBINE_SCRATCH(d),
        compiler_params=_params(),
        name="moe_combine0",
    )(x1, dest, dest, meta, mod, ybuf)


def _combine1_kernel(x1_ref, dest_ref, next_ref, meta_ref, gt_ref, gf_ref, ybuf_ref, o_ref,
                     dsm_ref, rows_ref, sem_idx, sems):
    y0, y1 = _gathered_rows(dest_ref, next_ref, dsm_ref, ybuf_ref, rows_ref, x1_ref.shape[1],
                            sem_idx, sems)
    w0, w1 = _token_weights(meta_ref)
    out = x1_ref[...] + gt_ref[0:1, :] * (w0 * y0 + w1 * y1)
    o_ref[...] = _rms(out, gf_ref[...])


def _combine1(x1, dest, meta, mod, gf, ybuf):
    t, d = x1.shape
    rows = t // GRID_W
    nt = t // TM
    out = pl.pallas_call(
        _combine1_kernel,
        grid=(nt,),
        in_specs=[pl.BlockSpec((TM, d), lambda i: (i, 0)),
                  pl.BlockSpec((SUBLANES, TM), lambda i: (0, i)),
                  _next_tile_spec(nt),
                  pl.BlockSpec((SUBLANES, TM), lambda i: (0, i)),
                  pl.BlockSpec((SUBLANES, d), lambda i: (0, 5)),
                  _full((1, d)),
                  pl.BlockSpec(memory_space=pl.ANY)],
        out_specs=pl.BlockSpec((TM, d), lambda i: (0, i)),
        out_shape=jax.ShapeDtypeStruct((rows, GRID_W * d), F32),
        scratch_shapes=_COMBINE_SCRATCH(d),
        compiler_params=_params(),
        name="moe_combine1",
    )(x1, dest, dest, meta, mod, gf.reshape(1, -1), ybuf)
    return out.reshape(t, d)


def _moe(h2t, meta, cnt, w_gate, w_up, w_down):
    dest, blk, n_blocks = _plan(cnt, meta)
    n_used = blk[1, :1]
    buf = _dispatch(h2t, dest, blk[2, :N_EXPERTS], n_used, n_blocks, w_gate.shape[1])
    ybuf = _experts(buf, blk[0], n_used, w_gate, w_up, w_down, n_blocks)
    return dest, ybuf


def _inproj1_kernel(x_ref, ctx_ref, g_ref, sh_ref, sc_ref, wq_ref, wkt_ref, wv_ref, wo_ref,
                    wg_ref, bg_ref, q_ref, kt_ref, v_ref, o_ref, gcol_ref):
    is_ctx = pl.program_id(0) == 0
    xin = jnp.where(is_ctx, ctx_ref[...], x_ref[...])
    h = _rms(xin, g_ref[...]) * (1.0 + _mod_row(sc_ref, is_ctx)) + _mod_row(sh_ref, is_ctx)
    hb = h.astype(BF16)
    q = jnp.dot(hb, wq_ref[...], preferred_element_type=F32) * (ML_DQK ** -0.5)
    q_ref[...] = q.astype(BF16)
    kt = lax.dot_general(wkt_ref[...], hb, (((1,), (1,)), ((), ())), preferred_element_type=F32)
    kt_ref[...] = kt.astype(BF16)
    v_ref[...] = jnp.dot(hb, wv_ref[...], preferred_element_type=F32).astype(BF16)
    o_ref[...] = jnp.dot(hb, wo_ref[...], preferred_element_type=F32).astype(BF16)
    gcol_ref[...] = _dot_bf16x3(h, wg_ref) + bg_ref[...]


def _inproj1(x, ctx, g1, mod, w_in, b_gates):
    n, d = x.shape
    rows = n // GRID_W
    nt = GRID_W + 1
    t = nt * TM
    ng = 4 * ML_HEADS
    wq = w_in[:, :ML_QK_W].astype(BF16)
    wkt = w_in[:, ML_QK_W:2 * ML_QK_W].T.astype(BF16)
    wv = w_in[:, 2 * ML_QK_W:2 * ML_QK_W + ML_V_W].astype(BF16)
    wo = w_in[:, 2 * ML_QK_W + ML_V_W:2 * ML_QK_W + ML_V_W + d].astype(BF16)
    wg = _split_bf16(jnp.zeros((d, LANES), F32).at[:, :ng].set(w_in[:, -ng:]))
    bg = jnp.zeros((1, LANES), F32).at[0, :ng].set(b_gates)
    tok = lambda w: pl.BlockSpec((TM, w), lambda i: (i, 0))
    return pl.pallas_call(
        _inproj1_kernel,
        grid=(nt,),
        in_specs=[pl.BlockSpec((rows, d), lambda i: (0, jnp.maximum(i - 1, 0))),
                  _full((TM, d)),
                  _full((1, d)),
                  pl.BlockSpec((SUBLANES, d), lambda i: (0, 0)),
                  pl.BlockSpec((SUBLANES, d), lambda i: (0, 1)),
                  _full(wq.shape), _full(wkt.shape), _full(wv.shape), _full(wo.shape),
                  _full(wg.shape), _full(bg.shape)],
        out_specs=[tok(ML_QK_W),
                   pl.BlockSpec((ML_QK_W, TM), lambda i: (0, i)),
                   tok(ML_V_W), tok(d), tok(LANES)],
        out_shape=[jax.ShapeDtypeStruct((t, ML_QK_W), BF16),
                   jax.ShapeDtypeStruct((ML_QK_W, t), BF16),
                   jax.ShapeDtypeStruct((t, ML_V_W), BF16),
                   jax.ShapeDtypeStruct((t, d), BF16),
                   jax.ShapeDtypeStruct((t, LANES), F32)],
        compiler_params=_params(),
        name="inproj1",
    )(x.reshape(rows, GRID_W * d), ctx, g1.reshape(1, -1), mod, mod, wq, wkt, wv, wo, wg, bg)


def _log_sigmoid(x):
    return jnp.minimum(x, 0.0) - jnp.log1p(jnp.exp(-jnp.abs(x)))


LOG2E = 1.4426950408889634


def _mlstm_chunk(reverse, q_ref, kt_ref, v_ref, gcol_ref, h_ref, c_ref, m_ref):
    L = TM
    half = L // 2
    gi = 2 * ML_HEADS if reverse else 0
    gf = gi + ML_HEADS
    end = 0 if reverse else L - 1
    rr = lax.broadcasted_iota(jnp.int32, (L, L), 0)
    cc = lax.broadcasted_iota(jnp.int32, (L, L), 1)
    tri = jnp.where((cc >= rr) if reverse else (cc <= rr), 1.0, 0.0).astype(BF16)
    rh = lax.broadcasted_iota(jnp.int32, (half, half), 0)
    ch = lax.broadcasted_iota(jnp.int32, (half, half), 1)
    diag = (ch >= rh) if reverse else (ch <= rh)

    gates = gcol_ref[...]
    lane = lax.broadcasted_iota(jnp.int32, (L, LANES), 1)
    mine = jnp.logical_and(lane >= gf, lane < gf + ML_HEADS)
    lf = jnp.where(mine, _log_sigmoid(gates), 0.0)
    p0 = lf.astype(BF16)
    r1 = lf - p0.astype(F32)
    p1 = r1.astype(BF16)
    p2 = (r1 - p1.astype(F32)).astype(BF16)
    cum = (jnp.dot(tri, p0, preferred_element_type=F32) + jnp.dot(tri, p1, preferred_element_type=F32)
           + jnp.dot(tri, p2, preferred_element_type=F32))

    r = jnp.where(mine, pltpu.roll(gates, ML_HEADS, 1) - cum, 0.0)
    row = lax.broadcasted_iota(jnp.int32, (L, LANES), 0)
    cm = r
    sh = 1
    while sh < L:
        if sh < SUBLANES:
            if reverse:
                cm = jnp.where(row < L - sh, jnp.maximum(cm, pltpu.roll(cm, L - sh, 0)), cm)
            else:
                cm = jnp.where(row >= sh, jnp.maximum(cm, pltpu.roll(cm, sh, 0)), cm)
        else:
            pad = jnp.full((sh, LANES), NEG_INF, F32)
            moved = (jnp.concatenate([cm[sh:], pad], axis=0) if reverse
                     else jnp.concatenate([pad, cm[:L - sh]], axis=0))
            cm = jnp.maximum(cm, moved)
        sh *= 2
    r8 = r.T[gf:gf + ML_HEADS]

    m_prev = m_ref[0:1, :]
    mm = jnp.maximum(m_prev, cm)
    m_t = cum + mm
    w_inter = jnp.exp(m_prev - mm)
    floor = jnp.exp(-m_t)
    mm2 = mm * LOG2E
    m_new = m_t[end:end + 1, :]
    shift2 = (cum[end:end + 1, :] - m_new) * LOG2E
    decay = jnp.exp(cum[end:end + 1, :] + m_prev - m_new)
    m_ref[0:1, :] = m_new

    ones_col = jnp.where(lax.broadcasted_iota(jnp.int32, (L, ML_DV), 1) == 0, 1.0, 0.0).astype(BF16)
    top, bot = slice(0, half), slice(half, L)
    dot = functools.partial(jnp.dot, preferred_element_type=F32)
    yield

    for hd in range(ML_HEADS):
        ln = gf + hd
        r2_row = r8[hd:hd + 1, :] * LOG2E

        def weights(tq, ks, masked):
            w = jnp.exp2(r2_row[:, ks] - mm2[tq, ln:ln + 1])
            return jnp.where(diag, w, 0.0) if masked else w

        qh = q_ref[:, hd * ML_DQK:(hd + 1) * ML_DQK]
        kth = kt_ref[hd * ML_DQK:(hd + 1) * ML_DQK, :]
        vext = jnp.concatenate([v_ref[:, hd * ML_DV:(hd + 1) * ML_DV], ones_col], axis=1)
        if reverse:
            s_top = dot(qh[top], kth) * jnp.concatenate([weights(top, top, True),
                                                         weights(top, bot, False)], axis=1)
            s_bot = dot(qh[bot], kth[:, bot]) * weights(bot, bot, True)
            intra = jnp.concatenate([dot(s_top.astype(BF16), vext),
                                     dot(s_bot.astype(BF16), vext[bot])], axis=0)
        else:
            s_top = dot(qh[top], kth[:, top]) * weights(top, top, True)
            s_bot = dot(qh[bot], kth) * jnp.concatenate([weights(bot, top, False),
                                                         weights(bot, bot, True)], axis=1)
            intra = jnp.concatenate([dot(s_top.astype(BF16), vext[top]),
                                     dot(s_bot.astype(BF16), vext)], axis=0)
        state = c_ref[hd]
        tot = w_inter[:, ln:ln + 1] * dot(qh, state.astype(BF16)) + intra
        den = tot[:, ML_DV:ML_DV + 1]
        h_ref[:, hd * ML_DV:(hd + 1) * ML_DV] = (
            tot[:, :ML_DV] / jnp.maximum(jnp.abs(den), floor[:, ln:ln + 1])).astype(h_ref.dtype)

        w_state = jnp.exp2(r2_row + shift2[:, ln:ln + 1])
        kw = (kth.astype(F32) * w_state).astype(BF16)
        c_ref[hd] = decay[:, ln:ln + 1] * state + dot(kw, vext)
        yield


def _mlstm_kernel(qf_ref, ktf_ref, vf_ref, gf_ref, qb_ref, ktb_ref, vb_ref, gb_ref, hf_ref, hb_ref,
                  cf_ref, mf_ref, cb_ref, mb_ref):
    @pl.when(pl.program_id(0) == 0)
    def _():
        for ref in (cf_ref, mf_ref, cb_ref, mb_ref):
            ref[...] = jnp.zeros_like(ref)

    fwd = _mlstm_chunk(False, qf_ref, ktf_ref, vf_ref, gf_ref, hf_ref, cf_ref, mf_ref)
    bwd = _mlstm_chunk(True, qb_ref, ktb_ref, vb_ref, gb_ref, hb_ref, cb_ref, mb_ref)
    for _ in zip(fwd, bwd):
        pass


def _mlstm(q, kt, v, gcol):
    t = q.shape[0]
    nt = t // TM
    back = lambda s: _scan_tile(True, s, nt)
    state = [pltpu.VMEM((ML_HEADS, ML_DQK, 2 * ML_DV), F32), pltpu.VMEM((SUBLANES, LANES), F32)]

    def specs(tile):
        return [pl.BlockSpec((TM, ML_QK_W), lambda s: (tile(s), 0)),
                pl.BlockSpec((ML_QK_W, TM), lambda s: (0, tile(s))),
                pl.BlockSpec((TM, ML_V_W), lambda s: (tile(s), 0)),
                pl.BlockSpec((TM, LANES), lambda s: (tile(s), 0))]

    return pl.pallas_call(
        _mlstm_kernel,
        grid=(nt,),
        in_specs=specs(lambda s: s) + specs(back),
        out_specs=[pl.BlockSpec((TM, ML_V_W), lambda s: (s, 0)),
                   pl.BlockSpec((TM, ML_V_W), lambda s: (back(s), 0))],
        out_shape=[jax.ShapeDtypeStruct((t, ML_V_W), BF16)] * 2,
        scratch_shapes=state + state,
        compiler_params=_params(),
        name="mlstm",
    )(q, kt, v, gcol, q, kt, v, gcol)


def _finish1_kernel(x_ref, hf_ref, hb_ref, o_ref, ng_ref, w_ref, gt_ref, g2_ref, sh_ref, sc_ref,
                    wr_ref, br_ref, x1_ref, h2_ref, lg_ref):
    hs = hf_ref[...].astype(F32) + hb_ref[...].astype(F32)
    parts = []
    for hd in range(ML_HEADS):
        blk = hs[:, hd * ML_DV:(hd + 1) * ML_DV]
        parts.append(blk * lax.rsqrt(jnp.mean(blk * blk, axis=-1, keepdims=True) + EPS))
    hn = jnp.concatenate(parts, axis=1) * ng_ref[...]
    y = jnp.dot((hn * _sigmoid(o_ref[...].astype(F32))).astype(BF16), w_ref[...],
                preferred_element_type=F32)
    x1 = x_ref[...] + gt_ref[0:1, :] * y
    x1_ref[...] = x1
    h2 = _rms(x1, g2_ref[...]) * (1.0 + sc_ref[0:1, :]) + sh_ref[0:1, :]
    _store_token_tiles(h2_ref, h2)
    lg_ref[...] = _router_logits(h2, wr_ref, br_ref)


def _finish1(x, hf, hb, o, norm_g, w_out_bf, mod, g2, wr, br):
    n, d = x.shape
    rows = n // GRID_W
    lat = lambda w: pl.BlockSpec((TM, w), lambda i: (i + 1, 0))
    modspec = lambda c: pl.BlockSpec((SUBLANES, d), lambda i: (0, c))
    return pl.pallas_call(
        _finish1_kernel,
        grid=(GRID_W,),
        in_specs=[pl.BlockSpec((rows, d), lambda i: (0, i)),
                  lat(ML_V_W), lat(ML_V_W), lat(d),
                  _full((1, ML_V_W)), _full((ML_V_W, d)),
                  modspec(2), _full((1, d)), modspec(3), modspec(4),
                  _full(wr.shape), _full(br.shape)],
        out_specs=_TOKEN_OUTS(d),
        out_shape=_token_out_shapes(n, d),
        compiler_params=_params(),
        name="finish1",
    )(x.reshape(rows, GRID_W * d), hf, hb, o, norm_g.reshape(1, -1), w_out_bf, mod,
      g2.reshape(1, -1), mod, mod, wr, br)


def kernel(x, c, ctx, c_ctx,
           l0_ada_w, l0_ada_b, l0_norm1_g, l0_norm2_g,
           l0_rg_w_in, l0_rg_conv_w, l0_rg_conv_b, l0_rg_w_a, l0_rg_b_a, l0_rg_w_x, l0_rg_b_x,
           l0_rg_lambda, l0_rg_w_out,
           l0_moe_w_grp, l0_moe_b_grp, l0_moe_w_exp, l0_moe_b_exp, l0_moe_w_gate, l0_moe_w_up,
           l0_moe_w_down,
           l1_ada_w, l1_ada_b, l1_norm1_g, l1_norm2_g,
           l1_ml_w_in, l1_ml_b_gates, l1_ml_norm_g, l1_ml_w_out,
           l1_moe_w_grp, l1_moe_b_grp, l1_moe_w_exp, l1_moe_b_exp, l1_moe_w_gate, l1_moe_w_up,
           l1_moe_w_down,
           final_norm_g):
    assert x.shape[0] == 1 and ctx.shape[1] == TM and x.shape[1] == GRID_W * TM
    xs, cs = x[0], ctx[0]
    d = xs.shape[1]
    cond8 = jnp.zeros((SUBLANES, d), F32).at[0].set(c[0]).at[1].set(c_ctx)

    mod0 = _adaln(cond8, l0_ada_w, l0_ada_b)
    conv_w8 = jnp.zeros((SUBLANES, D_RNN), F32).at[:CONV_W].set(l0_rg_conv_w)
    wcat = [jnp.concatenate([l0_rg_w_a[dr], l0_rg_w_x[dr]], axis=-1).astype(BF16) for dr in range(2)]
    gate, u, hf0 = _front0(xs, cs, l0_norm1_g, mod0, l0_rg_w_in.astype(BF16), conv_w8, l0_rg_conv_b,
                           wcat[0], l0_rg_b_a[0], l0_rg_b_x[0], l0_rg_lambda[0])
    wr0, br0 = _router_weights(l0_moe_w_grp, l0_moe_b_grp, l0_moe_w_exp, l0_moe_b_exp)
    x1, h2, logits = _back0(xs, cs, u, gate, hf0, wcat[1], l0_rg_b_a[1], l0_rg_b_x[1], l0_rg_lambda[1],
                            l0_rg_w_out.astype(BF16), mod0, l0_norm2_g, wr0, br0)
    meta, cnt = _route(logits, ROUTE_TILE_0)
    dest, ybuf = _moe(h2, meta, cnt, l0_moe_w_gate, l0_moe_w_up, l0_moe_w_down)
    x2, ctx2 = _combine0(x1, dest, meta, mod0, ybuf)

    mod1 = _adaln(cond8, l1_ada_w, l1_ada_b)
    q, kt, v, o, gcol = _inproj1(x2, ctx2, l1_norm1_g, mod1, l1_ml_w_in, l1_ml_b_gates)
    hf, hb = _mlstm(q, kt, v, gcol)
    wr1, br1 = _router_weights(l1_moe_w_grp, l1_moe_b_grp, l1_moe_w_exp, l1_moe_b_exp)
    x1, h2, logits = _finish1(x2, hf, hb, o, l1_ml_norm_g, l1_ml_w_out.astype(BF16), mod1,
                              l1_norm2_g, wr1, br1)
    meta, cnt = _route(logits, ROUTE_TILE_1)
    dest, ybuf = _moe(h2, meta, cnt, l1_moe_w_gate, l1_moe_w_up, l1_moe_w_down)
    out = _combine1(x1, dest, meta, mod1, final_norm_g, ybuf)
    return out[None]
```
